```python
import jax
import jax.numpy as jnp
from jax import lax
import numpy as np

D_MODEL = 1024
BATCH = 4
SEQ = 4096
DEPTH = 1
DEC_BATCH = 128
DEC_SEQ = 1
PAST_LEN = 8192
PAGE_SIZE = 128

N_HEADS = 8
HEAD_DIM = 64
KV_HEADS = 2
GQA_GROUP = N_HEADS // KV_HEADS
D_ATTN = N_HEADS * HEAD_DIM
D_CONV = D_MODEL - D_ATTN
D_KV = KV_HEADS * HEAD_DIM
CMP_BLOCK = 32
CMP_STRIDE = 16
CMP_RATIO = CMP_BLOCK // CMP_STRIDE
CMP_HIDDEN = HEAD_DIM
SLC_BLOCK = 64
SLC_TOPN = 16
N_LOCAL_BLOCKS = 2
WINDOW = 512
FORCED_SCORE = 1e4
CONV_WIDTH = 31
N_EXPERT_GROUPS = 4
EXPERTS_PER_GROUP = 8
N_EXPERTS = N_EXPERT_GROUPS * EXPERTS_PER_GROUP
D_EXPERT = 256
MOE_TOPK = 2
Q_BLOCK = 128
EPS = 1e-6
NEG_INF = -1e30
D_IN = D_ATTN + 6 * D_KV + 3 * N_HEADS + 2 * D_CONV

kernel_name = 'hymba_nsa_conformer_hmoe_step'


def rmsnorm(x, g):
    x32 = x.astype(jnp.float32)
    y = x32 * lax.rsqrt(jnp.mean(x32 * x32, axis=-1, keepdims=True) + EPS)
    return (y * g.astype(jnp.float32)).astype(x.dtype)


def layernorm(x, g, b):
    x32 = x.astype(jnp.float32)
    xc = x32 - jnp.mean(x32, axis=-1, keepdims=True)
    y = xc * lax.rsqrt(jnp.mean(xc * xc, axis=-1, keepdims=True) + EPS)
    return (y * g.astype(jnp.float32) + b.astype(jnp.float32)).astype(x.dtype)


def masked_softmax(s, mask):
    p = jax.nn.softmax(jnp.where(mask, s, NEG_INF), axis=-1)
    return jnp.where(mask, p, 0.0)


def adaln(c, w_ada, b_ada):
    m = (jax.nn.silu(c) @ w_ada + b_ada).reshape(c.shape[0], 6, D_MODEL)
    return [m[:, i, None, :] for i in range(6)]


def split_proj(p):
    b, s = p.shape[:2]
    q = p[..., :D_ATTN].reshape(b, s, N_HEADS, HEAD_DIM)
    o = D_ATTN
    kv = p[..., o:o + 6 * D_KV].reshape(b, s, 3, 2, KV_HEADS, HEAD_DIM)
    o += 6 * D_KV
    gl = p[..., o:o + 3 * N_HEADS].reshape(b, s, N_HEADS, 3)
    o += 3 * N_HEADS
    u = p[..., o:o + D_CONV] * jax.nn.sigmoid(p[..., o + D_CONV:])
    return q, kv[:, :, 0], kv[:, :, 1], kv[:, :, 2], gl, u


def mixer_input(x, mods, norm_g, w_in):
    h = rmsnorm(x, norm_g) * (1.0 + mods[1]) + mods[0]
    return split_proj(h @ w_in)


def chunk_project(rows, w_cmp1):
    b, t = rows.shape[:2]
    n = t // CMP_STRIDE
    chunks = rows[:, :n * CMP_STRIDE].reshape(b, n, CMP_STRIDE, 2, KV_HEADS, HEAD_DIM)
    w1 = w_cmp1.reshape(2, CMP_RATIO, CMP_STRIDE, HEAD_DIM, CMP_HIDDEN)
    return jnp.einsum('bnsckd,crsdf->bnrckf', chunks, w1)


def compress_chunks(part, w_cmp1, pos_cmp, w_cmp2):
    nb = part.shape[1] - CMP_RATIO + 1
    pre = part[:, 0:nb, 0]
    for r in range(1, CMP_RATIO):
        pre = pre + part[:, r:r + nb, r]
    pre = pre + jnp.einsum('cld,cldf->cf', pos_cmp, w_cmp1)[:, None, :]
    out = jnp.einsum('bnckf,cfd->bnckd', jax.nn.silu(pre), w_cmp2)
    return out[:, :, 0], out[:, :, 1]


def overlap_matrix(nb_cmp, nb_slc):
    c_start = jnp.arange(nb_cmp) * CMP_STRIDE
    s_start = jnp.arange(nb_slc) * SLC_BLOCK
    hit = (c_start[:, None] <= s_start[None, :] + SLC_BLOCK - 1) & (c_start[:, None] + CMP_BLOCK - 1 >= s_start[None, :])
    return hit.astype(jnp.float32)


def nsa_attend(q, q_pos, kc, vc, gather_sel, n_slc, kw, vw, kw_pos, gate_logits):
    f32 = jnp.float32
    b, nq = q.shape[:2]
    scale = HEAD_DIM ** -0.5
    qg = q.reshape(b, nq, KV_HEADS, GQA_GROUP, HEAD_DIM).transpose(0, 2, 3, 1, 4)
    nb = kc.shape[1]
    c_mask = (jnp.arange(nb) * CMP_STRIDE + CMP_BLOCK - 1)[None, :] <= q_pos[:, None]
    s_c = jnp.einsum('bkgqd,bnkd->bkgqn', qg, kc).astype(f32) * scale
    p_c = masked_softmax(s_c, c_mask)
    o_c = jnp.einsum('bkgqn,bnkd->bkgqd', p_c.astype(vc.dtype), vc)
    imp = jnp.einsum('bkgqn,nj->bkqj', p_c, overlap_matrix(nb, n_slc))
    blk = jnp.arange(n_slc)[None, :]
    cur = (q_pos // SLC_BLOCK)[:, None]
    valid = blk <= cur
    forced = (blk == 0) | ((cur - blk >= 0) & (cur - blk < N_LOCAL_BLOCKS))
    imp = jnp.where(valid & forced, FORCED_SCORE, jnp.where(valid, imp, -1.0))
    top_val, top_idx = lax.top_k(imp, min(SLC_TOPN, n_slc))
    kv_sel = gather_sel(top_idx)
    nsel = top_idx.shape[-1]
    key_pos = top_idx[..., None] * SLC_BLOCK + jnp.arange(SLC_BLOCK)
    sel_mask = (top_val >= 0.0)[..., None] & (key_pos <= q_pos[None, None, :, None, None])
    s_s = jnp.einsum('bkgqd,bkqnld->bkgqnl', qg, kv_sel[..., 0, :]).astype(f32) * scale
    p_s = masked_softmax(s_s.reshape(b, KV_HEADS, GQA_GROUP, nq, nsel * SLC_BLOCK),
                         sel_mask.reshape(b, KV_HEADS, 1, nq, nsel * SLC_BLOCK))
    p_s = p_s.reshape(b, KV_HEADS, GQA_GROUP, nq, nsel, SLC_BLOCK)
    o_s = jnp.einsum('bkgqnl,bkqnld->bkgqd', p_s.astype(kv_sel.dtype), kv_sel[..., 1, :])
    w_mask = (kw_pos[None, :] <= q_pos[:, None]) & (kw_pos[None, :] >= q_pos[:, None] - WINDOW) & (kw_pos[None, :] >= 0)
    s_w = jnp.einsum('bkgqd,btkd->bkgqt', qg, kw).astype(f32) * scale
    p_w = masked_softmax(s_w, w_mask)
    o_w = jnp.einsum('bkgqt,btkd->bkgqd', p_w.astype(vw.dtype), vw)
    g = jax.nn.sigmoid(gate_logits.astype(f32)).reshape(b, nq, KV_HEADS, GQA_GROUP, 3).transpose(0, 2, 3, 1, 4)
    o = g[..., 0:1] * o_c + g[..., 1:2] * o_s + g[..., 2:3] * o_w
    return o.transpose(0, 3, 1, 2, 4).reshape(b, nq, D_ATTN).astype(q.dtype)


def conformer_conv(u_ext, w_dw, b_dw, ln_g, ln_b):
    y = lax.conv_general_dilated(u_ext, w_dw[:, None, :], window_strides=(1,), padding='VALID',
                                 dimension_numbers=('NWC', 'WIO', 'NWC'), feature_group_count=D_CONV)
    return jax.nn.silu(layernorm(y + b_dw, ln_g, ln_b))


def hier_moe(h, w_group, b_group, w_router, b_router, w_gate, w_up, w_down):
    f32 = jnp.float32
    shp = h.shape
    hf = h.reshape(-1, D_MODEL)
    p_group = jax.nn.softmax((hf @ w_group).astype(f32) + b_group, axis=-1)
    p_top, g_idx = lax.top_k(p_group, 1)
    e_logits = ((hf @ w_router).astype(f32) + b_router).reshape(-1, N_EXPERT_GROUPS, EXPERTS_PER_GROUP)
    e_in = jnp.take_along_axis(e_logits, g_idx[:, :, None], axis=1)[:, 0]
    w_top, e_idx = lax.top_k(jax.nn.softmax(e_in, axis=-1), MOE_TOPK)
    w_top = w_top / jnp.sum(w_top, axis=-1, keepdims=True) * p_top
    eid = g_idx * EXPERTS_PER_GROUP + e_idx
    combine = jnp.sum(jax.nn.one_hot(eid, N_EXPERTS, dtype=f32) * w_top[..., None], axis=1)
    hid = jax.nn.silu(jnp.einsum('nd,edf->nef', hf, w_gate)) * jnp.einsum('nd,edf->nef', hf, w_up)
    y = jnp.einsum('nef,ne,efd->nd', hid, combine.astype(hid.dtype), w_down)
    return y.reshape(shp)


def finish_layer(x, mods, o_attn, o_conv, g_attn_out, g_conv_out, w_out, norm2_g,
                 w_group, b_group, w_router, b_router, w_gate, w_up, w_down):
    mix = jnp.concatenate([rmsnorm(o_attn, g_attn_out), rmsnorm(o_conv, g_conv_out)], axis=-1) @ w_out
    x = x + mods[2] * mix
    h = rmsnorm(x, norm2_g) * (1.0 + mods[4]) + mods[3]
    return x + mods[5] * hier_moe(h, w_group, b_group, w_router, b_router, w_gate, w_up, w_down)


def setup_inputs(seed: int = 0) -> dict:
    key = jax.random.key(seed)
    ks = iter(jax.random.split(key, 48))
    f32 = jnp.float32

    def nrm(shape, scale=1.0):
        return jax.random.normal(next(ks), shape, f32) * scale

    n_pages = PAST_LEN // PAGE_SIZE
    n_used = DEC_BATCH * n_pages
    n_phys = n_used + n_used // 4
    win_buf = min(WINDOW, PAST_LEN)
    d_in = D_MODEL ** -0.5
    return {
        'x_prompt': nrm((BATCH, SEQ, D_MODEL)),
        'x_sample': nrm((DEC_BATCH, DEC_SEQ, D_MODEL)),
        'cache_cmp_kv': nrm((DEPTH, n_phys, PAGE_SIZE, 2, KV_HEADS, HEAD_DIM)),
        'cache_slc_kv': nrm((DEPTH, n_phys, PAGE_SIZE, 2, KV_HEADS, HEAD_DIM)),
        'state_win_kv': nrm((DEPTH, DEC_BATCH, win_buf, 2, KV_HEADS, HEAD_DIM)),
        'state_conv': nrm((DEPTH, DEC_BATCH, CONV_WIDTH - 1, D_CONV), 0.5),
        'page_table': jax.random.permutation(next(ks), n_phys)[:n_used].reshape(DEC_BATCH, n_pages).astype(jnp.int32),
        'c_prompt': nrm((BATCH, D_MODEL)),
        'c_sample': nrm((DEC_BATCH, D_MODEL)),
        'norm1_g': 1.0 + nrm((DEPTH, D_MODEL), 0.02),
        'w_ada': nrm((DEPTH, D_MODEL, 6 * D_MODEL), 0.5 * d_in),
        'b_ada': nrm((DEPTH, 6 * D_MODEL), 0.02),
        'w_in': nrm((DEPTH, D_MODEL, D_IN), d_in),
        'w_cmp1': nrm((DEPTH, 2, CMP_BLOCK, HEAD_DIM, CMP_HIDDEN), (CMP_BLOCK * HEAD_DIM) ** -0.5),
        'pos_cmp': nrm((DEPTH, 2, CMP_BLOCK, HEAD_DIM), 0.1),
        'w_cmp2': nrm((DEPTH, 2, CMP_HIDDEN, HEAD_DIM), CMP_HIDDEN ** -0.5),
        'w_dw': nrm((DEPTH, CONV_WIDTH, D_CONV), CONV_WIDTH ** -0.5),
        'b_dw': nrm((DEPTH, D_CONV), 0.02),
        'conv_ln_g': 1.0 + nrm((DEPTH, D_CONV), 0.02),
        'conv_ln_b': nrm((DEPTH, D_CONV), 0.02),
        'g_attn_out': 1.0 + nrm((DEPTH, D_ATTN), 0.02),
        'g_conv_out': 1.0 + nrm((DEPTH, D_CONV), 0.02),
        'w_out': nrm((DEPTH, D_MODEL, D_MODEL), d_in),
        'norm2_g': 1.0 + nrm((DEPTH, D_MODEL), 0.02),
        'w_group': nrm((DEPTH, D_MODEL, N_EXPERT_GROUPS), d_in),
        'b_group': nrm((DEPTH, N_EXPERT_GROUPS), 0.01),
        'w_router': nrm((DEPTH, D_MODEL, N_EXPERTS), d_in),
        'b_router': nrm((DEPTH, N_EXPERTS), 0.01),
        'w_gate': nrm((DEPTH, N_EXPERTS, D_MODEL, D_EXPERT), d_in),
        'w_up': nrm((DEPTH, N_EXPERTS, D_MODEL, D_EXPERT), d_in),
        'w_down': nrm((DEPTH, N_EXPERTS, D_EXPERT, D_MODEL), D_EXPERT ** -0.5),
        'final_g': 1.0 + nrm((D_MODEL,), 0.02),
    }


def reference(x_prompt, x_sample, cache_cmp_kv, cache_slc_kv, state_win_kv, state_conv, page_table,
              c_prompt, c_sample, norm1_g, w_ada, b_ada, w_in, w_cmp1, pos_cmp, w_cmp2, w_dw, b_dw,
              conv_ln_g, conv_ln_b, g_attn_out, g_conv_out, w_out, norm2_g, w_group, b_group,
              w_router, b_router, w_gate, w_up, w_down, final_g):
    n_past_slc = PAST_LEN // SLC_BLOCK
    sub = PAGE_SIZE // SLC_BLOCK
    n_new_slc = -(-DEC_SEQ // SLC_BLOCK)
    win_buf = state_win_kv.shape[2]
    win_keep_p = min(WINDOW, SEQ)
    bi_p = jnp.arange(BATCH)[:, None, None, None]
    bi_s = jnp.arange(DEC_BATCH)[:, None, None, None]
    ki = jnp.arange(KV_HEADS)[None, :, None, None]
    q_pos_s = PAST_LEN + jnp.arange(DEC_SEQ)
    kw_pos_s = PAST_LEN - win_buf + jnp.arange(win_buf + DEC_SEQ)

    xp, xs = x_prompt, x_sample
    cmp_p, slc_p, win_p, conv_p = [], [], [], []
    cmp_s, slc_s, win_s, conv_s = [], [], [], []
    for l in range(DEPTH):
        mods_p = adaln(c_prompt, w_ada[l], b_ada[l])
        mods_s = adaln(c_sample, w_ada[l], b_ada[l])

        q, kv_c, kv_s, kv_w, gl, u = mixer_input(xp, mods_p, norm1_g[l], w_in[l])
        kc, vc = compress_chunks(chunk_project(kv_c, w_cmp1[l]), w_cmp1[l], pos_cmp[l], w_cmp2[l])
        slc_blocks = kv_s.reshape(BATCH, SEQ // SLC_BLOCK, SLC_BLOCK, 2, KV_HEADS, HEAD_DIM)
        win_pad = jnp.pad(kv_w, ((0, 0), (WINDOW, 0), (0, 0), (0, 0), (0, 0)))

        def gather_p(idx):
            return slc_blocks[bi_p, idx, :, :, ki]

        def block_fn(i):
            start = i * Q_BLOCK
            q_b = lax.dynamic_slice_in_dim(q, start, Q_BLOCK, axis=1)
            gl_b = lax.dynamic_slice_in_dim(gl, start, Q_BLOCK, axis=1)
            w_b = lax.dynamic_slice_in_dim(win_pad, start, WINDOW + Q_BLOCK, axis=1)
            q_pos = start + jnp.arange(Q_BLOCK)
            kw_pos = start - WINDOW + jnp.arange(WINDOW + Q_BLOCK)
            return nsa_attend(q_b, q_pos, kc, vc, gather_p, SEQ // SLC_BLOCK,
                              w_b[:, :, 0], w_b[:, :, 1], kw_pos, gl_b)

        o_blocks = lax.map(block_fn, jnp.arange(SEQ // Q_BLOCK))
        o_attn = jnp.transpose(o_blocks, (1, 0, 2, 3)).reshape(BATCH, SEQ, D_ATTN)
        u_ext = jnp.pad(u, ((0, 0), (CONV_WIDTH - 1, 0), (0, 0)))
        o_conv = conformer_conv(u_ext, w_dw[l], b_dw[l], conv_ln_g[l], conv_ln_b[l])
        cmp_p.append(kv_c)
        slc_p.append(kv_s)
        win_p.append(kv_w[:, SEQ - win_keep_p:])
        conv_p.append(u_ext[:, -(CONV_WIDTH - 1):])
        xp = finish_layer(xp, mods_p, o_attn, o_conv, g_attn_out[l], g_conv_out[l], w_out[l], norm2_g[l],
                          w_group[l], b_group[l], w_router[l], b_router[l], w_gate[l], w_up[l], w_down[l])

        q, kv_c, kv_s, kv_w, gl, u = mixer_input(xs, mods_s, norm1_g[l], w_in[l])
        past_c = cache_cmp_kv[l, page_table].reshape(DEC_BATCH, PAST_LEN, 2, KV_HEADS, HEAD_DIM)
        part = chunk_project(past_c, w_cmp1[l])
        if DEC_SEQ >= CMP_STRIDE:
            part = jnp.concatenate([part, chunk_project(kv_c, w_cmp1[l])], axis=1)
        kc, vc = compress_chunks(part, w_cmp1[l], pos_cmp[l], w_cmp2[l])
        new_blocks = jnp.pad(kv_s, ((0, 0), (0, n_new_slc * SLC_BLOCK - DEC_SEQ), (0, 0), (0, 0), (0, 0)))
        new_blocks = new_blocks.reshape(DEC_BATCH, n_new_slc, SLC_BLOCK, 2, KV_HEADS, HEAD_DIM)

        def gather_s(idx):
            jp = jnp.minimum(idx, n_past_slc - 1)
            pg = page_table[bi_s, jp // sub]
            rw = (jp % sub)[..., None] * SLC_BLOCK + jnp.arange(SLC_BLOCK)
            from_pool = cache_slc_kv[l, pg[..., None], rw, :, ki[..., None]]
            jn = jnp.clip(idx - n_past_slc, 0, n_new_slc - 1)
            from_new = new_blocks[bi_s, jn, :, :, ki]
            return jnp.where((idx < n_past_slc)[..., None, None, None], from_pool, from_new)

        kw_all = jnp.concatenate([state_win_kv[l], kv_w], axis=1)
        o_attn = nsa_attend(q, q_pos_s, kc, vc, gather_s, n_past_slc + n_new_slc,
                            kw_all[:, :, 0], kw_all[:, :, 1], kw_pos_s, gl)
        u_ext = jnp.concatenate([state_conv[l], u], axis=1)
        o_conv = conformer_conv(u_ext, w_dw[l], b_dw[l], conv_ln_g[l], conv_ln_b[l])
        cmp_s.append(kv_c)
        slc_s.append(kv_s)
        win_s.append(kw_all[:, -win_buf:])
        conv_s.append(u_ext[:, -(CONV_WIDTH - 1):])
        xs = finish_layer(xs, mods_s, o_attn, o_conv, g_attn_out[l], g_conv_out[l], w_out[l], norm2_g[l],
                          w_group[l], b_group[l], w_router[l], b_router[l], w_gate[l], w_up[l], w_down[l])

    y_prompt = rmsnorm(xp, final_g)
    y_sample = rmsnorm(xs, final_g)
    new_cmp_prompt = jnp.stack(cmp_p, axis=0)
    new_slc_prompt = jnp.stack(slc_p, axis=0)
    new_win_prompt = jnp.stack(win_p, axis=0)
    new_conv_prompt = jnp.stack(conv_p, axis=0)
    new_cmp_sample = jnp.stack(cmp_s, axis=0)
    new_slc_sample = jnp.stack(slc_s, axis=0)
    new_win_sample = jnp.stack(win_s, axis=0)
    new_conv_sample = jnp.stack(conv_s, axis=0)
    return (y_prompt, y_sample, new_cmp_prompt, new_slc_prompt, new_win_prompt, new_conv_prompt,
            new_cmp_sample, new_slc_sample, new_win_sample, new_conv_sample)
```

```python
import functools

import jax
import jax.numpy as jnp
from jax import lax
from jax.experimental import pallas as pl
from jax.experimental.pallas import tpu as pltpu

F32 = jnp.float32
BF16 = jnp.bfloat16

D_MODEL = 1024
N_HEADS = 8
HEAD_DIM = 64
KV_HEADS = 2
GQA = N_HEADS // KV_HEADS
D_ATTN = N_HEADS * HEAD_DIM
D_CONV = D_MODEL - D_ATTN
D_KV = KV_HEADS * HEAD_DIM
CMP_BLOCK = 32
CMP_STRIDE = 16
SLC_BLOCK = 64
SLC_TOPN = 16
N_LOCAL_BLOCKS = 2
WINDOW = 512
FORCED_SCORE = 1e4
CONV_WIDTH = 31
N_GROUPS = 4
EPG = 8
N_EXPERTS = N_GROUPS * EPG
D_EXPERT = 256
PAGE_SIZE = 128
EPS = 1e-6
NEG_INF = -1e30
SCALE = HEAD_DIM ** -0.5
ROW_W = 2 * D_KV
CHUNK_W = CMP_STRIDE * ROW_W
Q_TILE = 128
K_TILE = 256
VMEM_LIMIT = 48 * 1024 * 1024


def _cparams(*sem):
    return pltpu.CompilerParams(dimension_semantics=sem, vmem_limit_bytes=VMEM_LIMIT)


def _rms(x, g):
    return x * lax.rsqrt(jnp.mean(x * x, axis=-1, keepdims=True) + EPS) * g


def _silu(x):
    return x * jax.nn.sigmoid(x)


def _mm(a, b):
    return jnp.dot(a, b, preferred_element_type=F32)


def _mm_nt(a, b):
    return lax.dot_general(a, b, (((1,), (1,)), ((), ())), preferred_element_type=F32)


def _split3(x):
    hi = x.astype(BF16)
    r = x - hi.astype(F32)
    mid = r.astype(BF16)
    lo = (r - mid.astype(F32)).astype(BF16)
    return hi, mid, lo


def _softmax_masked(s, mask, axis):
    s = jnp.where(mask, s, NEG_INF)
    m = jnp.max(s, axis=axis, keepdims=True)
    e = jnp.exp(s - m)
    p = e / jnp.sum(e, axis=axis, keepdims=True)
    return jnp.where(mask, p, 0.0)


def _adaln_kernel(c_ref, w_ref, b_ref, o_ref):
    s = _silu(c_ref[...]).astype(BF16)
    o_ref[...] = _mm(s, w_ref[...].astype(BF16)) + b_ref[...]


def _adaln(c_all, w_ada, b_ada):
    n, d = c_all.shape
    nout = w_ada.shape[1]
    tn = 1024
    return pl.pallas_call(
        _adaln_kernel,
        grid=(nout // tn,),
        in_specs=[pl.BlockSpec((n, d), lambda j: (0, 0)),
                  pl.BlockSpec((d, tn), lambda j: (0, j)),
                  pl.BlockSpec((1, tn), lambda j: (0, j))],
        out_specs=pl.BlockSpec((n, tn), lambda j: (0, j)),
        out_shape=jax.ShapeDtypeStruct((n, nout), F32),
        compiler_params=_cparams("arbitrary"),
        name="adaln",
    )(c_all, w_ada, b_ada.reshape(1, nout))


def _mixer_kernel(x_ref, shift_ref, scale_ref, g_ref, wqkv_ref, wgl_ref, wu_ref, *outs, transposed):
    x = x_ref[0]
    h = _rms(x, g_ref[...]) * (1.0 + scale_ref[0]) + shift_ref[0]
    hb = h.astype(BF16)
    p = _mm(hb, wqkv_ref[...])
    gl = _mm(hb, wgl_ref[...])
    pu = _mm(hb, wu_ref[...])
    u = pu[:, :D_CONV] * jax.nn.sigmoid(pu[:, D_CONV:])
    q = p[:, :D_ATTN] * SCALE
    o = D_ATTN
    kvc = p[:, o:o + ROW_W]
    kvs = p[:, o + ROW_W:o + 2 * ROW_W]
    kvw = p[:, o + 2 * ROW_W:o + 3 * ROW_W]
    if transposed:
        qt_ref, kvc_ref, kvs_ref, kvw_ref, ks_ref, vts_ref, kw_ref, vtw_ref, gt_ref, u_ref = outs
        qt_ref[0] = q.T.astype(BF16)
        ks_ref[0] = kvs[:, :D_KV].astype(BF16)
        vts_ref[0] = kvs[:, D_KV:].T.astype(BF16)
        kw_ref[0] = kvw[:, :D_KV].astype(BF16)
        vtw_ref[0] = kvw[:, D_KV:].T.astype(BF16)
        gt_ref[0] = jax.nn.sigmoid(gl).T[:32, :]
    else:
        q_ref, kvc_ref, kvs_ref, kvw_ref, gl_ref, u_ref = outs
        q_ref[0] = q
        gl_ref[0] = gl
    kvc_ref[0] = kvc
    kvs_ref[0] = kvs
    kvw_ref[0] = kvw
    u_ref[0] = u


def _mixer(x, shift, scale, g, wqkv, wgl, wu, *, tm, transposed):
    b, s, d = x.shape
    r = shift.shape[1]
    mod_block = (1, 1, d) if r == 1 else (1, tm, d)
    mod_map = (lambda i, j: (i, 0, 0)) if r == 1 else (lambda i, j: (i, j, 0))
    row = lambda w: pl.BlockSpec((1, tm, w), lambda i, j: (i, j, 0))
    col = lambda w: pl.BlockSpec((1, w, tm), lambda i, j: (i, 0, j))
    const = lambda a: pl.BlockSpec(a.shape, lambda i, j: (0,) * a.ndim)
    if transposed:
        out_specs = [col(D_ATTN), row(ROW_W), row(ROW_W), row(ROW_W), row(D_KV), col(D_KV), row(D_KV), col(D_KV),
                     col(32), row(D_CONV)]
        out_shape = [jax.ShapeDtypeStruct((b, D_ATTN, s), BF16)] + [jax.ShapeDtypeStruct((b, s, ROW_W), F32)] * 3 + [
            jax.ShapeDtypeStruct((b, s, D_KV), BF16), jax.ShapeDtypeStruct((b, D_KV, s), BF16),
            jax.ShapeDtypeStruct((b, s, D_KV), BF16), jax.ShapeDtypeStruct((b, D_KV, s), BF16),
            jax.ShapeDtypeStruct((b, 32, s), F32), jax.ShapeDtypeStruct((b, s, D_CONV), F32)]
    else:
        out_specs = [row(D_ATTN), row(ROW_W), row(ROW_W), row(ROW_W), row(128), row(D_CONV)]
        out_shape = [jax.ShapeDtypeStruct((b, s, D_ATTN), F32)] + [jax.ShapeDtypeStruct((b, s, ROW_W), F32)] * 3 + [
            jax.ShapeDtypeStruct((b, s, 128), F32), jax.ShapeDtypeStruct((b, s, D_CONV), F32)]
    return pl.pallas_call(
        functools.partial(_mixer_kernel, transposed=transposed),
        grid=(b, s // tm),
        in_specs=[row(d), pl.BlockSpec(mod_block, mod_map), pl.BlockSpec(mod_block, mod_map),
                  const(g), const(wqkv), const(wgl), const(wu)],
        out_specs=out_specs,
        out_shape=out_shape,
        compiler_params=_cparams("arbitrary", "arbitrary"),
        name="mixer_t" if transposed else "mixer_s",
    )(x, shift, scale, g, wqkv, wgl, wu)


def _chunk_part(load_lanes, w1_c, c):
    xc = jnp.concatenate([load_lanes(s * ROW_W + c * D_KV, D_KV) for s in range(CMP_STRIDE)], axis=1)
    return _mm(xc.astype(BF16), w1_c)


def _compress_finish(part, posb, w2_c):
    n = part.shape[0]
    nxt = pltpu.roll(part[:, D_KV:], n - 1, 0)
    pre = part[:, :D_KV] + nxt + posb[0:1, :D_KV] + posb[1:2, D_KV:]
    return _mm(_silu(pre).astype(BF16), w2_c)


def _compress_prompt_kernel(x_ref, w1_ref, w2_ref, pos_ref, kc_ref, vct_ref):
    for c in range(2):
        part = _chunk_part(lambda o, w: x_ref[0, :, o:o + w], w1_ref[c], c)
        posb = _mm(pos_ref[c], w1_ref[c])
        out = _compress_finish(part, posb, w2_ref[c])
        if c == 0:
            kc_ref[0] = out.astype(BF16)
        else:
            vct_ref[0] = out.T.astype(BF16)


def _compress_prompt(kvc, w1big, w2big, posrows):
    b, s, _ = kvc.shape
    n = s // CMP_STRIDE
    x = kvc.reshape(b, n, CHUNK_W)
    const = lambda a: pl.BlockSpec(a.shape, lambda i: (0,) * a.ndim)
    return pl.pallas_call(
        _compress_prompt_kernel,
        grid=(b,),
        in_specs=[pl.BlockSpec((1, n, CHUNK_W), lambda i: (i, 0, 0)), const(w1big), const(w2big), const(posrows)],
        out_specs=[pl.BlockSpec((1, n, D_KV), lambda i: (i, 0, 0)), pl.BlockSpec((1, D_KV, n), lambda i: (i, 0, 0))],
        out_shape=[jax.ShapeDtypeStruct((b, n, D_KV), BF16), jax.ShapeDtypeStruct((b, D_KV, n), BF16)],
        compiler_params=_cparams("arbitrary"),
        name="compress_prompt",
    )(x, w1big, w2big, posrows)


def _compress_pages_kernel(pt_ref, *refs, pages):
    x_refs = refs[:pages]
    w1_ref, part_ref = refs[pages], refs[pages + 1]
    for c in range(2):
        def load(o, w):
            return jnp.concatenate([r[0, :, o:o + w] for r in x_refs], axis=0)
        part_ref[0, :, c * ROW_W:(c + 1) * ROW_W] = _chunk_part(load, w1_ref[c], c)


def _compress_pages(cache, page_table, w1big, *, pages):
    n_phys = cache.shape[0]
    nb, n_pages = page_table.shape
    cpp = PAGE_SIZE // CMP_STRIDE
    x = cache.reshape(n_phys, cpp, CHUNK_W)
    page_spec = lambda k: pl.BlockSpec((1, cpp, CHUNK_W), lambda i, j, pt: (pt[i * n_pages + j * pages + k], 0, 0))
    return pl.pallas_call(
        functools.partial(_compress_pages_kernel, pages=pages),
        grid_spec=pltpu.PrefetchScalarGridSpec(
            num_scalar_prefetch=1,
            grid=(nb, n_pages // pages),
            in_specs=[page_spec(k) for k in range(pages)] + [pl.BlockSpec(w1big.shape, lambda i, j, pt: (0, 0, 0))],
            out_specs=pl.BlockSpec((1, pages * cpp, 2 * ROW_W), lambda i, j, pt: (i, j, 0)),
        ),
        out_shape=jax.ShapeDtypeStruct((nb, n_pages * cpp, 2 * ROW_W), F32),
        compiler_params=_cparams("arbitrary", "arbitrary"),
        name="compress_pages",
    )(page_table.reshape(-1), *([x] * pages), w1big)


def _compress_sample_finish_kernel(part_ref, w1_ref, w2_ref, pos_ref, kc_ref, vc_ref):
    for c in range(2):
        posb = _mm(pos_ref[c], w1_ref[c])
        for i in range(part_ref.shape[0]):
            out = _compress_finish(part_ref[i, :, c * ROW_W:(c + 1) * ROW_W], posb, w2_ref[c])
            (kc_ref if c == 0 else vc_ref)[i] = out.astype(BF16)


def _compress_sample_finish(part, w1big, w2big, posrows, *, sb):
    nb, n, _ = part.shape
    const = lambda a: pl.BlockSpec(a.shape, lambda i: (0,) * a.ndim)
    return pl.pallas_call(
        _compress_sample_finish_kernel,
        grid=(nb // sb,),
        in_specs=[pl.BlockSpec((sb, n, 2 * ROW_W), lambda i: (i, 0, 0)), const(w1big), const(w2big), const(posrows)],
        out_specs=[pl.BlockSpec((sb, n, D_KV), lambda i: (i, 0, 0))] * 2,
        out_shape=[jax.ShapeDtypeStruct((nb, n, D_KV), BF16)] * 2,
        compiler_params=_cparams("arbitrary"),
        name="compress_sample_finish",
    )(part, w1big, w2big, posrows)


def _overlap(cmp_idx, slc_idx):
    lo = cmp_idx * CMP_STRIDE
    so = slc_idx * SLC_BLOCK
    return (lo <= so + SLC_BLOCK - 1) & (lo + CMP_BLOCK - 1 >= so)


def _forced_importance(imp, blk, cur, n_slc):
    valid = (blk <= cur) & (blk < n_slc)
    forced = (blk == 0) | ((cur - blk >= 0) & (cur - blk < N_LOCAL_BLOCKS))
    return jnp.where(valid & forced, FORCED_SCORE, jnp.where(valid, imp, -1.0))


def _attn_prompt_kernel(qt_ref, kc_ref, vct_ref, ks_ref, vts_ref, kw_ref, vtw_ref, gt_ref, o_ref, sel_ref):
    i = pl.program_id(1)
    seq = ks_ref.shape[1]
    nbp = kc_ref.shape[1]
    n_slc = seq // SLC_BLOCK
    lanes = GQA * Q_TILE
    lane = lax.broadcasted_iota(jnp.int32, (1, lanes), 1)
    qpos = i * Q_TILE + (lane & (Q_TILE - 1))
    qpos_q = qpos[:, :Q_TILE]
    qt = qt_ref[0]
    gt = gt_ref[0]
    win_keys = min(WINDOW + Q_TILE, seq)
    blocks_per_tile = K_TILE // SLC_BLOCK

    ovl = _overlap(lax.broadcasted_iota(jnp.int32, (n_slc, nbp), 1), lax.broadcasted_iota(jnp.int32, (n_slc, nbp), 0))
    ovl = jnp.where(ovl, 1.0, 0.0).astype(BF16)
    blk = lax.broadcasted_iota(jnp.int32, (n_slc, Q_TILE), 0)

    for kvh in range(KV_HEADS):
        hd = slice(kvh * HEAD_DIM, (kvh + 1) * HEAD_DIM)
        qk = jnp.concatenate([qt[(kvh * GQA + g) * HEAD_DIM:(kvh * GQA + g + 1) * HEAD_DIM, :] for g in range(GQA)],
                             axis=1)
        zero = jnp.zeros_like(qk)
        qpad = jnp.concatenate([qk, zero] if kvh == 0 else [zero, qk], axis=0)

        s = _mm(kc_ref[0], qpad)
        cpos = lax.broadcasted_iota(jnp.int32, s.shape, 0) * CMP_STRIDE + (CMP_BLOCK - 1)
        p = _softmax_masked(s, cpos <= qpos, 0)
        o_c = _mm(vct_ref[0, hd, :], p.astype(BF16))

        psum = p[:, 0:Q_TILE]
        for g in range(1, GQA):
            psum = psum + p[:, g * Q_TILE:(g + 1) * Q_TILE]
        imp = sum(_mm(ovl, t) for t in _split3(psum))
        impf = _forced_importance(imp, blk, qpos_q // SLC_BLOCK, n_slc)
        rank = jnp.zeros_like(impf)
        for k in range(n_slc):
            rk = impf[k:k + 1, :]
            rank = rank + jnp.where(rk > impf, 1.0, 0.0) + jnp.where(rk == impf, jnp.where(blk > k, 1.0, 0.0), 0.0)
        sel = jnp.where(rank < SLC_TOPN, jnp.where(impf >= 0.0, 1.0, 0.0), 0.0)
        sel_ref[...] = jnp.concatenate([sel] * GQA, axis=1)

        def slc_step(t, carry):
            m, l, acc = carry
            k0 = pl.multiple_of(t * K_TILE, K_TILE)
            sk = _mm(ks_ref[0, pl.ds(k0, K_TILE), :], qpad)
            rows = [jnp.broadcast_to(sel_ref[pl.ds(t * blocks_per_tile + j, 1), :], (SLC_BLOCK, lanes))
                    for j in range(blocks_per_tile)]
            chosen = jnp.concatenate(rows, axis=0)
            kpos = k0 + lax.broadcasted_iota(jnp.int32, sk.shape, 0)
            mask = jnp.where(kpos <= qpos, chosen, 0.0) > 0.5
            sk = jnp.where(mask, sk, NEG_INF)
            m_new = jnp.maximum(m, jnp.max(sk, axis=0, keepdims=True))
            alpha = jnp.exp(m - m_new)
            e = jnp.where(mask, jnp.exp(sk - m_new), 0.0)
            l_new = alpha * l + jnp.sum(e, axis=0, keepdims=True)
            pv = _mm(vts_ref[0, hd, pl.ds(k0, K_TILE)], e.astype(BF16))
            return m_new, l_new, alpha * acc + pv

        n_steps = ((i + 1) * Q_TILE + K_TILE - 1) // K_TILE
        init = (jnp.full((1, lanes), NEG_INF, F32), jnp.zeros((1, lanes), F32), jnp.zeros((HEAD_DIM, lanes), F32))
        _, l_s, acc_s = lax.fori_loop(0, n_steps, slc_step, init)
        o_s = acc_s / l_s

        w0 = pl.multiple_of(jnp.clip(i * Q_TILE - WINDOW, 0, seq - win_keys), Q_TILE)
        sw = _mm(kw_ref[0, pl.ds(w0, win_keys), :], qpad)
        wpos = w0 + lax.broadcasted_iota(jnp.int32, sw.shape, 0)
        pw = _softmax_masked(sw, (wpos <= qpos) & (wpos >= qpos - WINDOW), 0)
        o_w = _mm(vtw_ref[0, hd, pl.ds(w0, win_keys)], pw.astype(BF16))

        def gate(r):
            return jnp.concatenate([gt[r * N_HEADS + kvh * GQA + g:r * N_HEADS + kvh * GQA + g + 1, :]
                                    for g in range(GQA)], axis=1)
        o_t = gate(0) * o_c + gate(1) * o_s + gate(2) * o_w
        for pair in range(GQA // 2):
            two = jnp.concatenate([o_t[:, (2 * pair) * Q_TILE:(2 * pair + 1) * Q_TILE],
                                   o_t[:, (2 * pair + 1) * Q_TILE:(2 * pair + 2) * Q_TILE]], axis=0)
            c0 = kvh * GQA * HEAD_DIM + pair * 2 * HEAD_DIM
            o_ref[0, :, c0:c0 + 2 * HEAD_DIM] = two.T


def _attn_prompt(qt, kc, vct, ks, vts, kw, vtw, gt):
    b, _, s = qt.shape
    nbp = kc.shape[1]
    per_b = lambda shape: pl.BlockSpec((1,) + shape, lambda i, j: (i, 0, 0))
    return pl.pallas_call(
        _attn_prompt_kernel,
        grid=(b, s // Q_TILE),
        in_specs=[pl.BlockSpec((1, D_ATTN, Q_TILE), lambda i, j: (i, 0, j)),
                  per_b((nbp, D_KV)), per_b((D_KV, nbp)),
                  per_b((s, D_KV)), per_b((D_KV, s)), per_b((s, D_KV)), per_b((D_KV, s)),
                  pl.BlockSpec((1, 32, Q_TILE), lambda i, j: (i, 0, j))],
        out_specs=pl.BlockSpec((1, Q_TILE, D_ATTN), lambda i, j: (i, j, 0)),
        out_shape=jax.ShapeDtypeStruct((b, s, D_ATTN), F32),
        scratch_shapes=[pltpu.VMEM((s // SLC_BLOCK, GQA * Q_TILE), F32)],
        compiler_params=_cparams("arbitrary", "arbitrary"),
        name="attn_prompt",
    )(qt, kc, vct, ks, vts, kw, vtw, gt)


def _attn_sample_cmp_kernel(q_ref, kc_ref, vc_ref, oc_ref, idx_ref, *, q_pos, n_slc):
    q = q_ref[0].astype(BF16)
    nb = kc_ref.shape[1]
    s = _mm_nt(q, kc_ref[0])
    cpos = lax.broadcasted_iota(jnp.int32, s.shape, 1) * CMP_STRIDE + (CMP_BLOCK - 1)
    p = _softmax_masked(s, cpos <= q_pos, 1)
    oc_ref[0] = _mm(p.astype(BF16), vc_ref[0])

    nsp = idx_ref.shape[2] * 2
    group_sums = [jnp.sum(p[k * GQA:(k + 1) * GQA, :], axis=0, keepdims=True) for k in range(KV_HEADS)]
    psum = jnp.concatenate(group_sums + [jnp.zeros((N_HEADS - KV_HEADS, nb), F32)], axis=0)
    ovl = _overlap(lax.broadcasted_iota(jnp.int32, (nb, nsp), 0), lax.broadcasted_iota(jnp.int32, (nb, nsp), 1))
    ovl = jnp.where(ovl, 1.0, 0.0).astype(BF16)
    imp = sum(_mm(t, ovl) for t in _split3(psum))
    blk = lax.broadcasted_iota(jnp.int32, imp.shape, 1)
    impf = _forced_importance(imp, blk, q_pos // SLC_BLOCK, n_slc)
    rank = jnp.zeros_like(impf)
    for k in range(n_slc):
        ck = impf[:, k:k + 1]
        rank = rank + jnp.where(ck > impf, 1.0, 0.0) + jnp.where(ck == impf, jnp.where(blk > k, 1.0, 0.0), 0.0)
    blk_f = blk.astype(F32)
    slot = lax.broadcasted_iota(jnp.int32, (N_HEADS, idx_ref.shape[2]), 1)
    idx = jnp.zeros((N_HEADS, idx_ref.shape[2]), F32)
    for t in range(SLC_TOPN):
        chosen = jnp.sum(jnp.where(rank == float(t), blk_f, 0.0), axis=1, keepdims=True)
        idx = idx + jnp.where(slot == t, chosen, 0.0)
    idx_ref[0] = idx.astype(jnp.int32)


def _attn_sample_cmp(qpad, kc, vc, *, q_pos, n_slc):
    nb = qpad.shape[0]
    n = kc.shape[1]
    per_b = lambda shape: pl.BlockSpec((1,) + shape, lambda i: (i, 0, 0))
    return pl.pallas_call(
        functools.partial(_attn_sample_cmp_kernel, q_pos=q_pos, n_slc=n_slc),
        grid=(nb,),
        in_specs=[per_b((N_HEADS, D_KV)), per_b((n, D_KV)), per_b((n, D_KV))],
        out_specs=[per_b((N_HEADS, D_KV)), per_b((N_HEADS, 128))],
        out_shape=[jax.ShapeDtypeStruct((nb, N_HEADS, D_KV), F32), jax.ShapeDtypeStruct((nb, N_HEADS, 128), jnp.int32)],
        compiler_params=_cparams("arbitrary"),
        name="attn_sample_cmp",
    )(qpad, kc, vc)


def _attn_sample_kernel(idx_ref, pt_ref, *refs, n_past_slc):
    n_sel = KV_HEADS * SLC_TOPN
    blk_refs = refs[:n_sel]
    q_ref, oc_ref, gl_ref, win_ref, new_s_ref, new_w_ref, o_ref = refs[n_sel:]
    b = pl.program_id(0)
    q = q_ref[0]
    qb = q.astype(BF16)
    head_kvh = lax.broadcasted_iota(jnp.int32, (N_HEADS, 1), 0) // GQA

    def attend(keys_bf, valid, new_row, new_on):
        s = _mm_nt(qb, keys_bf)
        s_new = jnp.where(new_on, jnp.sum(q * new_row, axis=1, keepdims=True), NEG_INF)
        s = jnp.where(valid, s, NEG_INF)
        m = jnp.maximum(jnp.max(s, axis=1, keepdims=True), s_new)
        e = jnp.where(valid, jnp.exp(s - m), 0.0)
        e_new = jnp.exp(s_new - m)
        den = jnp.sum(e, axis=1, keepdims=True) + e_new
        o = _mm(e.astype(BF16), keys_bf) + e_new * new_row
        return o / den

    new_s = new_s_ref[0]
    o_s = jnp.zeros((N_HEADS, ROW_W), F32)
    lane_slot = lax.broadcasted_iota(jnp.int32, (1, SLC_TOPN * SLC_BLOCK), 1) // SLC_BLOCK
    for kvh in range(KV_HEADS):
        keys = jnp.concatenate([blk_refs[kvh * SLC_TOPN + t][0].astype(BF16) for t in range(SLC_TOPN)], axis=0)
        from_pool = jnp.zeros(lane_slot.shape, F32)
        n_new = jnp.int32(0)
        for t in range(SLC_TOPN):
            past = idx_ref[(b * KV_HEADS + kvh) * SLC_TOPN + t] < n_past_slc
            from_pool = from_pool + jnp.where(lane_slot == t, jnp.where(past, 1.0, 0.0), 0.0)
            n_new = n_new + jnp.where(past, 0, 1)
        o_k = attend(keys, from_pool > 0.5, new_s, n_new > 0)
        o_s = jnp.where(head_kvh == kvh, o_k, o_s)

    win = win_ref[0, 0].astype(BF16)
    o_w = attend(win, jnp.full((1, win.shape[0]), True), new_w_ref[0], True)

    g = jax.nn.sigmoid(gl_ref[0])
    o_ref[0] = (g[:, 0:1] * oc_ref[0] + g[:, 1:2] * o_s[:, D_KV:] + g[:, 2:3] * o_w[:, D_KV:])


def _attn_sample(idx, page_table, cache_slc, qpad, o_c, gl, win, new_s, new_w, *, n_past_slc):
    nb = qpad.shape[0]
    sub = PAGE_SIZE // SLC_BLOCK
    half_pages = cache_slc.reshape(cache_slc.shape[0] * sub, SLC_BLOCK, ROW_W)
    win_buf = win.shape[2]
    n_pages = page_table.shape[1]

    def blk_spec(kvh, t):
        def index(i, idx_ref, pt_ref):
            j = jnp.minimum(idx_ref[(i * KV_HEADS + kvh) * SLC_TOPN + t], n_past_slc - 1)
            return (pt_ref[i * n_pages + j // sub] * sub + j % sub, 0, 0)
        return pl.BlockSpec((1, SLC_BLOCK, ROW_W), index)

    per_b = lambda shape: pl.BlockSpec((1,) + shape, lambda i, a, c: (i,) + (0,) * len(shape))
    n_sel = KV_HEADS * SLC_TOPN
    return pl.pallas_call(
        functools.partial(_attn_sample_kernel, n_past_slc=n_past_slc),
        grid_spec=pltpu.PrefetchScalarGridSpec(
            num_scalar_prefetch=2,
            grid=(nb,),
            in_specs=[blk_spec(k, t) for k in range(KV_HEADS) for t in range(SLC_TOPN)] + [
                per_b((N_HEADS, ROW_W)), per_b((N_HEADS, D_KV)), per_b((N_HEADS, 128)),
                pl.BlockSpec((1, 1, win_buf, ROW_W), lambda i, a, c: (0, i, 0, 0)),
                per_b((1, ROW_W)), per_b((1, ROW_W))],
            out_specs=per_b((N_HEADS, D_KV)),
        ),
        out_shape=jax.ShapeDtypeStruct((nb, N_HEADS, D_KV), F32),
        compiler_params=_cparams("arbitrary"),
        name="attn_sample",
    )(idx.reshape(-1), page_table.reshape(-1), *([half_pages] * n_sel), qpad, o_c, gl, win, new_s, new_w)


def _conv_tail(y, b_ref, lg_ref, lb_ref):
    y = y + b_ref[...]
    yc = y - jnp.mean(y, axis=-1, keepdims=True)
    yn = yc * lax.rsqrt(jnp.mean(yc * yc, axis=-1, keepdims=True) + EPS)
    return _silu(yn * lg_ref[...] + lb_ref[...])


def _conv_prompt_kernel(u_ref, halo_ref, w_ref, b_ref, lg_ref, lb_ref, o_ref, buf_ref, *, halo):
    tm = u_ref.shape[1]
    j = pl.program_id(1)
    buf_ref[0:halo, :] = jnp.where(j > 0, halo_ref[0], 0.0)
    buf_ref[halo:, :] = u_ref[0]
    lead = halo - (CONV_WIDTH - 1)
    y = buf_ref[lead:lead + tm, :] * w_ref[0:1, :]
    for w in range(1, CONV_WIDTH):
        y = y + buf_ref[lead + w:lead + w + tm, :] * w_ref[w:w + 1, :]
    o_ref[0] = _conv_tail(y, b_ref, lg_ref, lb_ref)


def _conv_prompt(u, w_dw, b_dw, ln_g, ln_b, *, tm):
    b, s, dc = u.shape
    halo = 32
    per = tm // halo
    const = lambda a: pl.BlockSpec(a.shape, lambda i, j: (0,) * a.ndim)
    return pl.pallas_call(
        functools.partial(_conv_prompt_kernel, halo=halo),
        grid=(b, s // tm),
        in_specs=[pl.BlockSpec((1, tm, dc), lambda i, j: (i, j, 0)),
                  pl.BlockSpec((1, halo, dc), lambda i, j: (i, jnp.maximum(j * per - 1, 0), 0)),
                  const(w_dw), const(b_dw), const(ln_g), const(ln_b)],
        out_specs=pl.BlockSpec((1, tm, dc), lambda i, j: (i, j, 0)),
        out_shape=jax.ShapeDtypeStruct((b, s, dc), F32),
        scratch_shapes=[pltpu.VMEM((tm + halo, dc), F32)],
        compiler_params=_cparams("arbitrary", "arbitrary"),
        name="conv_prompt",
    )(u, u, w_dw, b_dw, ln_g, ln_b)


def _conv_sample_kernel(st_ref, u_ref, w_ref, b_ref, lg_ref, lb_ref, o_ref):
    hist = CONV_WIDTH - 1
    rows = [jnp.sum(st_ref[0, i] * w_ref[0:hist, :], axis=0, keepdims=True) for i in range(st_ref.shape[1])]
    y = jnp.concatenate(rows, axis=0) + u_ref[...] * w_ref[hist:hist + 1, :]
    o_ref[...] = _conv_tail(y, b_ref, lg_ref, lb_ref)


def _conv_sample(state_conv, u, w_dw, b_dw, ln_g, ln_b, *, sb):
    nb, dc = u.shape
    hist = state_conv.shape[2]
    const = lambda a: pl.BlockSpec(a.shape, lambda i: (0,) * a.ndim)
    return pl.pallas_call(
        _conv_sample_kernel,
        grid=(nb // sb,),
        in_specs=[pl.BlockSpec((1, sb, hist, dc), lambda i: (0, i, 0, 0)), pl.BlockSpec((sb, dc), lambda i: (i, 0)),
                  const(w_dw), const(b_dw), const(ln_g), const(ln_b)],
        out_specs=pl.BlockSpec((sb, dc), lambda i: (i, 0)),
        out_shape=jax.ShapeDtypeStruct((nb, dc), F32),
        compiler_params=_cparams("arbitrary"),
        name="conv_sample",
    )(state_conv, u, w_dw, b_dw, ln_g, ln_b)


def _finish_kernel(x_ref, oa_ref, ocv_ref, gate1_ref, shift2_ref, scale2_ref, ga_ref, gc_ref, wout_ref, g2_ref,
                   wr_ref, br_ref, x1_ref, h2_ref, comb_ref):
    mixed = jnp.concatenate([_rms(oa_ref[0], ga_ref[...]), _rms(ocv_ref[0], gc_ref[...])], axis=1)
    mix = _mm(mixed.astype(BF16), wout_ref[...])
    x1 = x_ref[0] + gate1_ref[0] * mix
    x1_ref[0] = x1
    h2 = _rms(x1, g2_ref[...]) * (1.0 + scale2_ref[0]) + shift2_ref[0]
    h2_ref[0] = h2.astype(BF16)

    h_hi, h_mid, _ = _split3(h2)
    w_hi, w_mid, _ = _split3(wr_ref[...])
    lg = (_mm(h_hi, w_hi) + _mm(h_hi, w_mid)
          + _mm(h_mid, w_hi)) + br_ref[...]
    lane = lax.broadcasted_iota(jnp.int32, lg.shape, 1)
    lane_f = lane.astype(F32)
    is_group = (lane >= N_EXPERTS) & (lane < N_EXPERTS + N_GROUPS)
    gl = jnp.where(is_group, lg, NEG_INF)
    g_max = jnp.max(gl, axis=1, keepdims=True)
    p_top = 1.0 / jnp.sum(jnp.where(is_group, jnp.exp(gl - g_max), 0.0), axis=1, keepdims=True)
    g_lane = jnp.min(jnp.where(gl == g_max, lane_f, 1e9), axis=1, keepdims=True)
    in_group = (lane < N_EXPERTS) & ((lane // EPG).astype(F32) == g_lane - N_EXPERTS)
    el = jnp.where(in_group, lg, NEG_INF)
    l1 = jnp.max(el, axis=1, keepdims=True)
    i1 = jnp.min(jnp.where(el == l1, lane_f, 1e9), axis=1, keepdims=True)
    el2 = jnp.where(lane_f == i1, NEG_INF, el)
    l2 = jnp.max(el2, axis=1, keepdims=True)
    i2 = jnp.min(jnp.where(el2 == l2, lane_f, 1e9), axis=1, keepdims=True)
    r = jnp.exp(l2 - l1)
    w1 = p_top / (1.0 + r)
    w2 = p_top * r / (1.0 + r)
    comb_ref[0] = jnp.where(lane_f == i1, w1, jnp.where(lane_f == i2, w2, 0.0))


def _finish(x, o_attn, o_conv, gate1, shift2, scale2, ga, gc, w_out, g2, w_route, b_route, *, tm):
    b, s, d = x.shape
    r = gate1.shape[1]
    mod_block = (1, 1, d) if r == 1 else (1, tm, d)
    mod_map = (lambda i, j: (i, 0, 0)) if r == 1 else (lambda i, j: (i, j, 0))
    mod = pl.BlockSpec(mod_block, mod_map)
    row = lambda w: pl.BlockSpec((1, tm, w), lambda i, j: (i, j, 0))
    const = lambda a: pl.BlockSpec(a.shape, lambda i, j: (0,) * a.ndim)
    return pl.pallas_call(
        _finish_kernel,
        grid=(b, s // tm),
        in_specs=[row(d), row(D_ATTN), row(D_CONV), mod, mod, mod, const(ga), const(gc), const(w_out), const(g2),
                  const(w_route), const(b_route)],
        out_specs=[row(d), row(d), row(128)],
        out_shape=[jax.ShapeDtypeStruct((b, s, d), F32), jax.ShapeDtypeStruct((b, s, d), BF16),
                   jax.ShapeDtypeStruct((b, s, 128), F32)],
        compiler_params=_cparams("arbitrary", "arbitrary"),
        name="finish",
    )(x, o_attn, o_conv, gate1, shift2, scale2, ga, gc, w_out, g2, w_route, b_route)


def _moe_kernel(h_ref, comb_ref, x1_ref, gate2_ref, wg_ref, wu_ref, wd_ref, gf_ref, y_ref, acc_ref):
    e = pl.program_id(2)

    @pl.when(e == 0)
    def _():
        acc_ref[...] = jnp.zeros_like(acc_ref)

    h = h_ref[0]
    a = _mm(h, wg_ref[0].astype(BF16))
    u = _mm(h, wu_ref[0].astype(BF16))
    comb = comb_ref[0]
    lane = lax.broadcasted_iota(jnp.int32, comb.shape, 1)
    cw = jnp.sum(jnp.where(lane == e, comb, 0.0), axis=1, keepdims=True)
    hid = (_silu(a) * u * cw).astype(BF16)
    acc_ref[...] += _mm(hid, wd_ref[0].astype(BF16))

    @pl.when(e == pl.num_programs(2) - 1)
    def _():
        y = x1_ref[0] + gate2_ref[0] * acc_ref[...]
        y_ref[0] = _rms(y, gf_ref[...])


def _moe(h2, comb, x1, gate2, w_gate, w_up, w_down, final_g, *, tm):
    b, s, d = x1.shape
    r = gate2.shape[1]
    mod_block = (1, 1, d) if r == 1 else (1, tm, d)
    mod_map = (lambda i, j, e: (i, 0, 0)) if r == 1 else (lambda i, j, e: (i, j, 0))
    row = lambda w: pl.BlockSpec((1, tm, w), lambda i, j, e: (i, j, 0))
    ne, _, de = w_gate.shape
    return pl.pallas_call(
        _moe_kernel,
        grid=(b, s // tm, ne),
        in_specs=[row(d), row(128), row(d), pl.BlockSpec(mod_block, mod_map),
                  pl.BlockSpec((1, d, de), lambda i, j, e: (e, 0, 0)),
                  pl.BlockSpec((1, d, de), lambda i, j, e: (e, 0, 0)),
                  pl.BlockSpec((1, de, d), lambda i, j, e: (e, 0, 0)),
                  pl.BlockSpec((1, d), lambda i, j, e: (0, 0))],
        out_specs=row(d),
        out_shape=jax.ShapeDtypeStruct((b, s, d), F32),
        scratch_shapes=[pltpu.VMEM((tm, d), F32)],
        compiler_params=_cparams("arbitrary", "arbitrary", "arbitrary"),
        name="moe",
    )(h2, comb, x1, gate2, w_gate, w_up, w_down, final_g)


def _prep_w_in(w_in):
    o_gl = D_ATTN + 3 * ROW_W
    wqkv = w_in[:, :o_gl].astype(BF16)
    wgl = w_in[:, o_gl:o_gl + 3 * N_HEADS]
    pad = jnp.zeros((w_in.shape[0], 128 - 3 * N_HEADS), w_in.dtype)
    wgl_head_major = jnp.concatenate([wgl, pad], axis=1).astype(BF16)
    wgl_branch_major = jnp.concatenate(
        [wgl.reshape(-1, N_HEADS, 3).transpose(0, 2, 1).reshape(-1, 3 * N_HEADS), pad], axis=1).astype(BF16)
    wu = w_in[:, o_gl + 3 * N_HEADS:].astype(BF16)
    return wqkv, wgl_head_major, wgl_branch_major, wu


def _prep_compress(w_cmp1, pos_cmp, w_cmp2):
    ratio = CMP_BLOCK // CMP_STRIDE
    eye = jnp.eye(KV_HEADS, dtype=w_cmp1.dtype)
    w1 = w_cmp1.reshape(2, ratio, CMP_STRIDE, HEAD_DIM, HEAD_DIM)
    w1big = jnp.einsum('crsdf,kj->cskdrjf', w1, eye).reshape(2, CMP_STRIDE * D_KV, ratio * D_KV).astype(BF16)
    w2big = jnp.einsum('cfd,kj->ckfjd', w_cmp2, eye).reshape(2, D_KV, D_KV).astype(BF16)
    pos = pos_cmp.reshape(2, ratio, CMP_STRIDE, 1, HEAD_DIM)
    pos = jnp.broadcast_to(pos, (2, ratio, CMP_STRIDE, KV_HEADS, HEAD_DIM)).reshape(2, ratio, CMP_STRIDE * D_KV)
    posrows = jnp.concatenate([pos, jnp.zeros((2, 8 - ratio, CMP_STRIDE * D_KV), pos.dtype)], axis=1).astype(BF16)
    return w1big, w2big, posrows


def _prep_router(w_group, b_group, w_router, b_router):
    d = w_group.shape[0]
    pad = 128 - N_EXPERTS - N_GROUPS
    w = jnp.concatenate([w_router, w_group, jnp.zeros((d, pad), w_group.dtype)], axis=1)
    b = jnp.concatenate([b_router, b_group, jnp.zeros((pad,), b_group.dtype)]).reshape(1, 128)
    return w, b


def kernel(x_prompt, x_sample, cache_cmp_kv, cache_slc_kv, state_win_kv, state_conv, page_table, c_prompt, c_sample,
           norm1_g, w_ada, b_ada, w_in, w_cmp1, pos_cmp, w_cmp2, w_dw, b_dw, conv_ln_g, conv_ln_b, g_attn_out,
           g_conv_out, w_out, norm2_g, w_group, b_group, w_router, b_router, w_gate, w_up, w_down, final_g):
    depth = norm1_g.shape[0]
    assert depth == 1 and x_sample.shape[1] == 1
    bp, seq, d = x_prompt.shape
    nb = x_sample.shape[0]
    n_pages = page_table.shape[1]
    past_len = n_pages * PAGE_SIZE
    n_past_slc = past_len // SLC_BLOCK
    win_buf = state_win_kv.shape[2]
    assert win_buf == WINDOW and seq % K_TILE == 0 and seq >= WINDOW + Q_TILE
    l = 0
    row2 = lambda a: a.reshape(1, -1)

    mods = _adaln(jnp.concatenate([c_prompt, c_sample], axis=0), w_ada[l], b_ada[l]).reshape(bp + nb, 6, d)
    mods_p = [mods[:bp, i][:, None, :] for i in range(6)]
    mods_s = [mods[bp:, i][None, :, :] for i in range(6)]

    wqkv, wgl_head_major, wgl_branch_major, wu = _prep_w_in(w_in[l])
    w1big, w2big, posrows = _prep_compress(w_cmp1[l], pos_cmp[l], w_cmp2[l])
    w_route, b_route = _prep_router(w_group[l], b_group[l], w_router[l], b_router[l])
    w_out_b = w_out[l].astype(BF16)
    conv_args = (w_dw[l], row2(b_dw[l]), row2(conv_ln_g[l]), row2(conv_ln_b[l]))

    qt, kvc, kvs, kvw, ks, vts, kw, vtw, gt, u = _mixer(
        x_prompt, mods_p[0], mods_p[1], row2(norm1_g[l]), wqkv, wgl_branch_major, wu, tm=512, transposed=True)
    kc, vct = _compress_prompt(kvc, w1big, w2big, posrows)
    o_attn = _attn_prompt(qt, kc, vct, ks, vts, kw, vtw, gt)
    o_conv = _conv_prompt(u, *conv_args, tm=512)
    x1, h2, comb = _finish(x_prompt, o_attn, o_conv, mods_p[2], mods_p[3], mods_p[4], row2(g_attn_out[l]),
                           row2(g_conv_out[l]), w_out_b, row2(norm2_g[l]), w_route, b_route, tm=512)
    y_prompt = _moe(h2, comb, x1, mods_p[5], w_gate[l], w_up[l], w_down[l], row2(final_g), tm=1024)

    kv_shape = (1, bp, seq, 2, KV_HEADS, HEAD_DIM)
    new_cmp_prompt = kvc.reshape(kv_shape)
    new_slc_prompt = kvs.reshape(kv_shape)
    new_win_prompt = kvw[:, seq - WINDOW:].reshape(1, bp, WINDOW, 2, KV_HEADS, HEAD_DIM)
    new_conv_prompt = u[:, seq - (CONV_WIDTH - 1):][None]

    xs = x_sample.reshape(1, nb, d)
    q_s, kvc_s, kvs_s, kvw_s, gl_s, u_s = _mixer(
        xs, mods_s[0], mods_s[1], row2(norm1_g[l]), wqkv, wgl_head_major, wu, tm=nb, transposed=False)
    part = _compress_pages(cache_cmp_kv[l], page_table, w1big, pages=16)
    kc_s, vc_s = _compress_sample_finish(part, w1big, w2big, posrows, sb=4)

    q4 = q_s.reshape(nb, KV_HEADS, GQA, HEAD_DIM)
    zq = jnp.zeros_like(q4)
    kvh_id = jnp.arange(KV_HEADS).reshape(1, KV_HEADS, 1, 1)
    qpad = jnp.concatenate([jnp.where(kvh_id == 0, q4, zq), jnp.where(kvh_id == 1, q4, zq), zq, zq],
                           axis=-1).reshape(nb, N_HEADS, ROW_W)
    o_c, idx = _attn_sample_cmp(qpad[:, :, :D_KV], kc_s, vc_s, q_pos=past_len, n_slc=n_past_slc + 1)
    idx = idx[:, :KV_HEADS, :SLC_TOPN]
    gl3 = jnp.pad(gl_s.reshape(nb, 128)[:, :3 * N_HEADS].reshape(nb, N_HEADS, 3), ((0, 0), (0, 0), (0, 125)))
    o_heads = _attn_sample(idx, page_table, cache_slc_kv[l], qpad, o_c, gl3,
                           state_win_kv[l:l + 1].reshape(1, nb, win_buf, ROW_W),
                           kvs_s.reshape(nb, 1, ROW_W), kvw_s.reshape(nb, 1, ROW_W), n_past_slc=n_past_slc)
    o4 = o_heads.reshape(nb, KV_HEADS, GQA, KV_HEADS, HEAD_DIM)
    o_attn_s = jnp.concatenate([o4[:, k, :, k, :] for k in range(KV_HEADS)], axis=1).reshape(1, nb, D_ATTN)
    o_conv_s = _conv_sample(state_conv[l:l + 1], u_s.reshape(nb, D_CONV), *conv_args, sb=8).reshape(1, nb, D_CONV)
    x1_s, h2_s, comb_s = _finish(xs, o_attn_s, o_conv_s, mods_s[2], mods_s[3], mods_s[4], row2(g_attn_out[l]),
                                 row2(g_conv_out[l]), w_out_b, row2(norm2_g[l]), w_route, b_route, tm=nb)
    y_sample = _moe(h2_s, comb_s, x1_s, mods_s[5], w_gate[l], w_up[l], w_down[l], row2(final_g), tm=nb)

    row_shape = (1, nb, 1, 2, KV_HEADS, HEAD_DIM)
    new_cmp_sample = kvc_s.reshape(row_shape)
    new_slc_sample = kvs_s.reshape(row_shape)
    new_win_sample = jnp.concatenate([state_win_kv[l, :, 1:], kvw_s.reshape(nb, 1, 2, KV_HEADS, HEAD_DIM)],
                                     axis=1)[None]
    new_conv_sample = jnp.concatenate([state_conv[l, :, 1:], u_s.reshape(nb, 1, D_CONV)], axis=1)[None]

    return (y_prompt, y_sample.reshape(nb, 1, d), new_cmp_prompt, new_slc_prompt, new_win_prompt, new_conv_prompt,
            new_cmp_sample, new_slc_sample, new_win_sample, new_conv_sample)
```

```python
import functools

import jax
import jax.numpy as jnp
from jax import lax
from jax.experimental import pallas as pl
from jax.experimental.pallas import tpu as pltpu

F32 = jnp.float32
BF16 = jnp.bfloat16

D_MODEL = 1024
N_HEADS = 8
HEAD_DIM = 64
KV_HEADS = 2
GQA = N_HEADS // KV_HEADS
D_ATTN = N_HEADS * HEAD_DIM
D_CONV = D_MODEL - D_ATTN
D_KV = KV_HEADS * HEAD_DIM
CMP_BLOCK = 32
CMP_STRIDE = 16
SLC_BLOCK = 64
SLC_TOPN = 16
N_LOCAL_BLOCKS = 2
WINDOW = 512
FORCED_SCORE = 1e4
CONV_WIDTH = 31
N_GROUPS = 4
EPG = 8
N_EXPERTS = N_GROUPS * EPG
D_EXPERT = 256
PAGE_SIZE = 128
EPS = 1e-6
NEG_INF = -1e30
SCALE = HEAD_DIM ** -0.5
ROW_W = 2 * D_KV
CHUNK_W = CMP_STRIDE * ROW_W
Q_TILE = 128
K_TILE = 256
VMEM_LIMIT = 48 * 1024 * 1024


def _cparams(*sem):
    return pltpu.CompilerParams(dimension_semantics=sem, vmem_limit_bytes=VMEM_LIMIT)


def _rms(x, g):
    return x * lax.rsqrt(jnp.mean(x * x, axis=-1, keepdims=True) + EPS) * g


def _silu(x):
    return x * jax.nn.sigmoid(x)


def _mm(a, b):
    return jnp.dot(a, b, preferred_element_type=F32)


def _mm_nt(a, b):
    return lax.dot_general(a, b, (((1,), (1,)), ((), ())), preferred_element_type=F32)


def _split3(x):
    hi = x.astype(BF16)
    r = x - hi.astype(F32)
    mid = r.astype(BF16)
    lo = (r - mid.astype(F32)).astype(BF16)
    return hi, mid, lo


def _softmax_masked(s, mask, axis):
    s = jnp.where(mask, s, NEG_INF)
    m = jnp.max(s, axis=axis, keepdims=True)
    e = jnp.exp(s - m)
    p = e / jnp.sum(e, axis=axis, keepdims=True)
    return jnp.where(mask, p, 0.0)


def _adaln_kernel(c_ref, w_ref, b_ref, o_ref):
    s = _silu(c_ref[...]).astype(BF16)
    o_ref[...] = _mm(s, w_ref[...].astype(BF16)) + b_ref[...]


def _adaln(c_all, w_ada, b_ada):
    n, d = c_all.shape
    nout = w_ada.shape[1]
    tn = 1024
    return pl.pallas_call(
        _adaln_kernel,
        grid=(nout // tn,),
        in_specs=[pl.BlockSpec((n, d), lambda j: (0, 0)),
                  pl.BlockSpec((d, tn), lambda j: (0, j)),
                  pl.BlockSpec((1, tn), lambda j: (0, j))],
        out_specs=pl.BlockSpec((n, tn), lambda j: (0, j)),
        out_shape=jax.ShapeDtypeStruct((n, nout), F32),
        compiler_params=_cparams("arbitrary"),
        name="adaln",
    )(c_all, w_ada, b_ada.reshape(1, nout))


def _mixer_kernel(x_ref, shift_ref, scale_ref, g_ref, wqkv_ref, wgl_ref, wu_ref, *outs, prompt):
    x = x_ref[0]
    h = _rms(x, g_ref[...]) * (1.0 + scale_ref[0]) + shift_ref[0]
    hb = h.astype(BF16)
    p = _mm(hb, wqkv_ref[...])
    gl = _mm(hb, wgl_ref[...])
    pu = _mm(hb, wu_ref[...])
    u = pu[:, :D_CONV] * jax.nn.sigmoid(pu[:, D_CONV:])
    q = p[:, :D_ATTN] * SCALE
    o = D_ATTN
    kvc = p[:, o:o + ROW_W]
    kvs = p[:, o + ROW_W:o + 2 * ROW_W]
    kvw = p[:, o + 2 * ROW_W:o + 3 * ROW_W]
    kvs_t = kvs.T
    kvw_t = kvw.T
    if prompt:
        q_ref, kvc_ref, kvct_ref, kvst_ref, kvwt_ref, ks_ref, vts_ref, kw_ref, vtw_ref, gt_ref, u_ref = outs
        q_ref[0] = q.T.astype(BF16)
        kvc_ref[0] = kvc
        ks_ref[0] = kvs[:, :D_KV].astype(BF16)
        vts_ref[0] = kvs_t[D_KV:, :].astype(BF16)
        kw_ref[0] = kvw[:, :D_KV].astype(BF16)
        vtw_ref[0] = kvw_t[D_KV:, :].astype(BF16)
    else:
        q_ref, kvct_ref, kvst_ref, kvwt_ref, gt_ref, u_ref = outs
        q_ref[0] = q
    kvct_ref[0] = kvc.T
    kvst_ref[0] = kvs_t
    kvwt_ref[0] = kvw_t
    gt_ref[0] = jax.nn.sigmoid(gl).T[:32, :]
    u_ref[0] = u


def _mixer(x, shift, scale, g, wqkv, wgl, wu, *, tm, prompt):
    b, s, d = x.shape
    r = shift.shape[1]
    mod_block = (1, 1, d) if r == 1 else (1, tm, d)
    mod_map = (lambda i, j: (i, 0, 0)) if r == 1 else (lambda i, j: (i, j, 0))
    row = lambda w: pl.BlockSpec((1, tm, w), lambda i, j: (i, j, 0))
    col = lambda w: pl.BlockSpec((1, w, tm), lambda i, j: (i, 0, j))
    const = lambda a: pl.BlockSpec(a.shape, lambda i, j: (0,) * a.ndim)
    rows = lambda w, dt: jax.ShapeDtypeStruct((b, s, w), dt)
    cols = lambda w, dt: jax.ShapeDtypeStruct((b, w, s), dt)
    if prompt:
        out_specs = [col(D_ATTN), row(ROW_W), col(ROW_W), col(ROW_W), col(ROW_W), row(D_KV), col(D_KV), row(D_KV),
                     col(D_KV), col(32), row(D_CONV)]
        out_shape = [cols(D_ATTN, BF16), rows(ROW_W, F32), cols(ROW_W, F32), cols(ROW_W, F32), cols(ROW_W, F32),
                     rows(D_KV, BF16), cols(D_KV, BF16), rows(D_KV, BF16), cols(D_KV, BF16), cols(32, F32),
                     rows(D_CONV, F32)]
    else:
        out_specs = [row(D_ATTN), col(ROW_W), col(ROW_W), col(ROW_W), col(32), row(D_CONV)]
        out_shape = [rows(D_ATTN, F32), cols(ROW_W, F32), cols(ROW_W, F32), cols(ROW_W, F32), cols(32, F32),
                     rows(D_CONV, F32)]
    return pl.pallas_call(
        functools.partial(_mixer_kernel, prompt=prompt),
        grid=(b, s // tm),
        in_specs=[row(d), pl.BlockSpec(mod_block, mod_map), pl.BlockSpec(mod_block, mod_map),
                  const(g), const(wqkv), const(wgl), const(wu)],
        out_specs=out_specs,
        out_shape=out_shape,
        compiler_params=_cparams("arbitrary", "arbitrary"),
        name="mixer_prompt" if prompt else "mixer_sample",
    )(x, shift, scale, g, wqkv, wgl, wu)


def _chunk_part(load_offset, w1_c):
    xc = jnp.concatenate([load_offset(s) for s in range(CMP_STRIDE)], axis=1)
    return _mm(xc.astype(BF16), w1_c)


def _compress_finish(part, posb, w2_c):
    n = part.shape[0]
    nxt = pltpu.roll(part[:, D_KV:], n - 1, 0)
    pre = part[:, :D_KV] + nxt + posb[0:1, :D_KV] + posb[1:2, D_KV:]
    return _mm(_silu(pre).astype(BF16), w2_c)


def _compress_prompt_kernel(x_ref, w1_ref, w2_ref, pos_ref, kc_ref, vct_ref):
    for c in range(2):
        part = _chunk_part(lambda s: x_ref[0, :, s * ROW_W + c * D_KV:s * ROW_W + (c + 1) * D_KV], w1_ref[c])
        posb = _mm(pos_ref[c], w1_ref[c])
        out = _compress_finish(part, posb, w2_ref[c])
        if c == 0:
            kc_ref[0] = out.astype(BF16)
        else:
            vct_ref[0] = out.T.astype(BF16)


def _compress_prompt(kvc, w1big, w2big, posrows):
    b, s, _ = kvc.shape
    n = s // CMP_STRIDE
    x = kvc.reshape(b, n, CHUNK_W)
    const = lambda a: pl.BlockSpec(a.shape, lambda i: (0,) * a.ndim)
    return pl.pallas_call(
        _compress_prompt_kernel,
        grid=(b,),
        in_specs=[pl.BlockSpec((1, n, CHUNK_W), lambda i: (i, 0, 0)), const(w1big), const(w2big), const(posrows)],
        out_specs=[pl.BlockSpec((1, n, D_KV), lambda i: (i, 0, 0)), pl.BlockSpec((1, D_KV, n), lambda i: (i, 0, 0))],
        out_shape=[jax.ShapeDtypeStruct((b, n, D_KV), BF16), jax.ShapeDtypeStruct((b, D_KV, n), BF16)],
        compiler_params=_cparams("arbitrary"),
        name="compress_prompt",
    )(x, w1big, w2big, posrows)


def _compress_sample_kernel(pt_ref, *refs, pages):
    x_refs = refs[:pages]
    w1_ref, w2_ref, pos_ref, kc_ref, vc_ref, rows_ref = refs[pages:]
    for k in range(pages):
        for c in range(2):
            rows_ref[c, k * PAGE_SIZE:(k + 1) * PAGE_SIZE, :] = x_refs[k][0, c * D_KV:(c + 1) * D_KV, :].T
    n = pages * (PAGE_SIZE // CMP_STRIDE)
    for c in range(2):
        part = _chunk_part(lambda s: rows_ref[c, pl.ds(s, n, stride=CMP_STRIDE), :], w1_ref[c])
        posb = _mm(pos_ref[c], w1_ref[c])
        out = _compress_finish(part, posb, w2_ref[c])
        (kc_ref if c == 0 else vc_ref)[0] = out.astype(BF16)


def _compress_sample(cache_t, page_table, w1big, w2big, posrows):
    nb, n_pages = page_table.shape
    n = n_pages * (PAGE_SIZE // CMP_STRIDE)
    page_spec = lambda k: pl.BlockSpec((1, ROW_W, PAGE_SIZE), lambda i, pt: (pt[i * n_pages + k], 0, 0))
    const = lambda a: pl.BlockSpec(a.shape, lambda i, pt: (0,) * a.ndim)
    return pl.pallas_call(
        functools.partial(_compress_sample_kernel, pages=n_pages),
        grid_spec=pltpu.PrefetchScalarGridSpec(
            num_scalar_prefetch=1,
            grid=(nb,),
            in_specs=[page_spec(k) for k in range(n_pages)] + [const(w1big), const(w2big), const(posrows)],
            out_specs=[pl.BlockSpec((1, n, D_KV), lambda i, pt: (i, 0, 0))] * 2,
            scratch_shapes=[pltpu.VMEM((2, n_pages * PAGE_SIZE, D_KV), F32)],
        ),
        out_shape=[jax.ShapeDtypeStruct((nb, n, D_KV), BF16)] * 2,
        compiler_params=_cparams("arbitrary"),
        name="compress_sample",
    )(page_table.reshape(-1), *([cache_t] * n_pages), w1big, w2big, posrows)


def _overlap(cmp_idx, slc_idx):
    lo = cmp_idx * CMP_STRIDE
    so = slc_idx * SLC_BLOCK
    return (lo <= so + SLC_BLOCK - 1) & (lo + CMP_BLOCK - 1 >= so)


def _forced_importance(imp, blk, cur, n_slc):
    valid = (blk <= cur) & (blk < n_slc)
    forced = (blk == 0) | ((cur - blk >= 0) & (cur - blk < N_LOCAL_BLOCKS))
    return jnp.where(valid & forced, FORCED_SCORE, jnp.where(valid, imp, -1.0))


def _attn_prompt_kernel(qt_ref, kc_ref, vct_ref, ks_ref, vts_ref, kw_ref, vtw_ref, gt_ref, o_ref, sel_ref):
    i = pl.program_id(1)
    seq = ks_ref.shape[1]
    nbp = kc_ref.shape[1]
    n_slc = seq // SLC_BLOCK
    lanes = GQA * Q_TILE
    lane = lax.broadcasted_iota(jnp.int32, (1, lanes), 1)
    qpos = i * Q_TILE + (lane & (Q_TILE - 1))
    qpos_q = qpos[:, :Q_TILE]
    qt = qt_ref[0]
    gt = gt_ref[0]
    win_keys = min(WINDOW + Q_TILE, seq)
    blocks_per_tile = K_TILE // SLC_BLOCK

    ovl = _overlap(lax.broadcasted_iota(jnp.int32, (n_slc, nbp), 1), lax.broadcasted_iota(jnp.int32, (n_slc, nbp), 0))
    ovl = jnp.where(ovl, 1.0, 0.0).astype(BF16)
    blk = lax.broadcasted_iota(jnp.int32, (n_slc, Q_TILE), 0)

    for kvh in range(KV_HEADS):
        hd = slice(kvh * HEAD_DIM, (kvh + 1) * HEAD_DIM)
        qk = jnp.concatenate([qt[(kvh * GQA + g) * HEAD_DIM:(kvh * GQA + g + 1) * HEAD_DIM, :] for g in range(GQA)],
                             axis=1)
        zero = jnp.zeros_like(qk)
        qpad = jnp.concatenate([qk, zero] if kvh == 0 else [zero, qk], axis=0)

        s = _mm(kc_ref[0], qpad)
        cpos = lax.broadcasted_iota(jnp.int32, s.shape, 0) * CMP_STRIDE + (CMP_BLOCK - 1)
        p = _softmax_masked(s, cpos <= qpos, 0)
        o_c = _mm(vct_ref[0, hd, :], p.astype(BF16))

        psum = p[:, 0:Q_TILE]
        for g in range(1, GQA):
            psum = psum + p[:, g * Q_TILE:(g + 1) * Q_TILE]
        imp = sum(_mm(ovl, t) for t in _split3(psum))
        impf = _forced_importance(imp, blk, qpos_q // SLC_BLOCK, n_slc)
        rank = jnp.zeros_like(impf)
        for k in range(n_slc):
            rk = impf[k:k + 1, :]
            rank = rank + jnp.where(rk > impf, 1.0, 0.0) + jnp.where(rk == impf, jnp.where(blk > k, 1.0, 0.0), 0.0)
        sel = jnp.where(rank < SLC_TOPN, jnp.where(impf >= 0.0, 1.0, 0.0), 0.0)
        sel_ref[...] = jnp.concatenate([sel] * GQA, axis=1)

        def slc_step(t, carry):
            m, l, acc = carry
            k0 = pl.multiple_of(t * K_TILE, K_TILE)
            sk = _mm(ks_ref[0, pl.ds(k0, K_TILE), :], qpad)
            rows = [jnp.broadcast_to(sel_ref[pl.ds(t * blocks_per_tile + j, 1), :], (SLC_BLOCK, lanes))
                    for j in range(blocks_per_tile)]
            chosen = jnp.concatenate(rows, axis=0)
            kpos = k0 + lax.broadcasted_iota(jnp.int32, sk.shape, 0)
            mask = jnp.where(kpos <= qpos, chosen, 0.0) > 0.5
            sk = jnp.where(mask, sk, NEG_INF)
            m_new = jnp.maximum(m, jnp.max(sk, axis=0, keepdims=True))
            alpha = jnp.exp(m - m_new)
            e = jnp.where(mask, jnp.exp(sk - m_new), 0.0)
            l_new = alpha * l + jnp.sum(e, axis=0, keepdims=True)
            pv = _mm(vts_ref[0, hd, pl.ds(k0, K_TILE)], e.astype(BF16))
            return m_new, l_new, alpha * acc + pv

        n_steps = ((i + 1) * Q_TILE + K_TILE - 1) // K_TILE
        init = (jnp.full((1, lanes), NEG_INF, F32), jnp.zeros((1, lanes), F32), jnp.zeros((HEAD_DIM, lanes), F32))
        _, l_s, acc_s = lax.fori_loop(0, n_steps, slc_step, init)
        o_s = acc_s / l_s

        w0 = pl.multiple_of(jnp.clip(i * Q_TILE - WINDOW, 0, seq - win_keys), Q_TILE)
        sw = _mm(kw_ref[0, pl.ds(w0, win_keys), :], qpad)
        wpos = w0 + lax.broadcasted_iota(jnp.int32, sw.shape, 0)
        pw = _softmax_masked(sw, (wpos <= qpos) & (wpos >= qpos - WINDOW), 0)
        o_w = _mm(vtw_ref[0, hd, pl.ds(w0, win_keys)], pw.astype(BF16))

        def gate(r):
            return jnp.concatenate([gt[r * N_HEADS + kvh * GQA + g:r * N_HEADS + kvh * GQA + g + 1, :]
                                    for g in range(GQA)], axis=1)
        o_t = gate(0) * o_c + gate(1) * o_s + gate(2) * o_w
        for pair in range(GQA // 2):
            two = jnp.concatenate([o_t[:, (2 * pair) * Q_TILE:(2 * pair + 1) * Q_TILE],
                                   o_t[:, (2 * pair + 1) * Q_TILE:(2 * pair + 2) * Q_TILE]], axis=0)
            c0 = kvh * GQA * HEAD_DIM + pair * 2 * HEAD_DIM
            o_ref[0, :, c0:c0 + 2 * HEAD_DIM] = two.T


def _attn_prompt(qt, kc, vct, ks, vts, kw, vtw, gt):
    b, _, s = qt.shape
    nbp = kc.shape[1]
    per_b = lambda shape: pl.BlockSpec((1,) + shape, lambda i, j: (i, 0, 0))
    return pl.pallas_call(
        _attn_prompt_kernel,
        grid=(b, s // Q_TILE),
        in_specs=[pl.BlockSpec((1, D_ATTN, Q_TILE), lambda i, j: (i, 0, j)),
                  per_b((nbp, D_KV)), per_b((D_KV, nbp)),
                  per_b((s, D_KV)), per_b((D_KV, s)), per_b((s, D_KV)), per_b((D_KV, s)),
                  pl.BlockSpec((1, 32, Q_TILE), lambda i, j: (i, 0, j))],
        out_specs=pl.BlockSpec((1, Q_TILE, D_ATTN), lambda i, j: (i, j, 0)),
        out_shape=jax.ShapeDtypeStruct((b, s, D_ATTN), F32),
        scratch_shapes=[pltpu.VMEM((s // SLC_BLOCK, GQA * Q_TILE), F32)],
        compiler_params=_cparams("arbitrary", "arbitrary"),
        name="attn_prompt",
    )(qt, kc, vct, ks, vts, kw, vtw, gt)


def _attn_sample_cmp_kernel(q_ref, kc_ref, vc_ref, oc_ref, idx_ref, *, q_pos, n_slc):
    q = q_ref[0].astype(BF16)
    nb = kc_ref.shape[1]
    s = _mm_nt(q, kc_ref[0])
    cpos = lax.broadcasted_iota(jnp.int32, s.shape, 1) * CMP_STRIDE + (CMP_BLOCK - 1)
    p = _softmax_masked(s, cpos <= q_pos, 1)
    oc_ref[0] = _mm(p.astype(BF16), vc_ref[0])

    nsp = idx_ref.shape[2] * 2
    group_sums = [jnp.sum(p[k * GQA:(k + 1) * GQA, :], axis=0, keepdims=True) for k in range(KV_HEADS)]
    psum = jnp.concatenate(group_sums + [jnp.zeros((N_HEADS - KV_HEADS, nb), F32)], axis=0)
    ovl = _overlap(lax.broadcasted_iota(jnp.int32, (nb, nsp), 0), lax.broadcasted_iota(jnp.int32, (nb, nsp), 1))
    ovl = jnp.where(ovl, 1.0, 0.0).astype(BF16)
    imp = sum(_mm(t, ovl) for t in _split3(psum))
    blk = lax.broadcasted_iota(jnp.int32, imp.shape, 1)
    impf = _forced_importance(imp, blk, q_pos // SLC_BLOCK, n_slc)
    rank = jnp.zeros_like(impf)
    for k in range(n_slc):
        ck = impf[:, k:k + 1]
        rank = rank + jnp.where(ck > impf, 1.0, 0.0) + jnp.where(ck == impf, jnp.where(blk > k, 1.0, 0.0), 0.0)
    blk_f = blk.astype(F32)
    slot = lax.broadcasted_iota(jnp.int32, (N_HEADS, idx_ref.shape[2]), 1)
    idx = jnp.zeros((N_HEADS, idx_ref.shape[2]), F32)
    for t in range(SLC_TOPN):
        chosen = jnp.sum(jnp.where(rank == float(t), blk_f, 0.0), axis=1, keepdims=True)
        idx = idx + jnp.where(slot == t, chosen, 0.0)
    idx_ref[0] = idx.astype(jnp.int32)


def _attn_sample_cmp(qpad, kc, vc, *, q_pos, n_slc):
    nb = qpad.shape[0]
    n = kc.shape[1]
    per_b = lambda shape: pl.BlockSpec((1,) + shape, lambda i: (i, 0, 0))
    return pl.pallas_call(
        functools.partial(_attn_sample_cmp_kernel, q_pos=q_pos, n_slc=n_slc),
        grid=(nb,),
        in_specs=[per_b((N_HEADS, D_KV)), per_b((n, D_KV)), per_b((n, D_KV))],
        out_specs=[per_b((N_HEADS, D_KV)), per_b((N_HEADS, 128))],
        out_shape=[jax.ShapeDtypeStruct((nb, N_HEADS, D_KV), F32), jax.ShapeDtypeStruct((nb, N_HEADS, 128), jnp.int32)],
        compiler_params=_cparams("arbitrary"),
        name="attn_sample_cmp",
    )(qpad, kc, vc)


def _attn_sample_kernel(idx_ref, pt_ref, *refs, n_past_slc):
    n_sel = KV_HEADS * SLC_TOPN
    blk_refs = refs[:n_sel]
    q_ref, oc_ref, gt_ref, kvst_ref, kvwt_ref, win_ref, o_ref, nwin_ref = refs[n_sel:]
    b = pl.program_id(0)
    nb = kvst_ref.shape[1]
    q = q_ref[0].astype(BF16)
    head_kvh = lax.broadcasted_iota(jnp.int32, (N_HEADS, 1), 0) // GQA
    mine = lax.broadcasted_iota(jnp.int32, (1, nb), 1) == b
    lane_half = lax.broadcasted_iota(jnp.int32, (1, PAGE_SIZE), 1) // SLC_BLOCK

    def attend(kt, vt, valid):
        s = jnp.where(valid, _mm(q, kt), NEG_INF)
        e = jnp.where(valid, jnp.exp(s - jnp.max(s, axis=1, keepdims=True)), 0.0)
        return _mm_nt(e.astype(BF16), vt) / jnp.sum(e, axis=1, keepdims=True)

    kvst = kvst_ref[...]
    kvwt = kvwt_ref[...]
    o_s = jnp.zeros((N_HEADS, HEAD_DIM), F32)
    o_w = jnp.zeros((N_HEADS, HEAD_DIM), F32)
    win = win_ref[0]
    for kvh in range(KV_HEADS):
        k_rows = slice(kvh * HEAD_DIM, (kvh + 1) * HEAD_DIM)
        v_rows = slice(D_KV + kvh * HEAD_DIM, D_KV + (kvh + 1) * HEAD_DIM)
        valid = []
        n_new = jnp.int32(0)
        for t in range(SLC_TOPN):
            j = idx_ref[(b * KV_HEADS + kvh) * SLC_TOPN + t]
            past = j < n_past_slc
            half = jnp.minimum(j, n_past_slc - 1) % (PAGE_SIZE // SLC_BLOCK)
            valid.append((lane_half == half) & past)
            n_new = n_new + jnp.where(past, 0, 1)
        valid.append(mine & (n_new > 0))
        kt = jnp.concatenate([blk_refs[kvh * SLC_TOPN + t][0] for t in range(SLC_TOPN)] + [kvst[k_rows, :]], axis=1)
        vt = jnp.concatenate([blk_refs[kvh * SLC_TOPN + t][1] for t in range(SLC_TOPN)] + [kvst[v_rows, :]], axis=1)
        o_k = attend(kt.astype(BF16), vt.astype(BF16), jnp.concatenate(valid, axis=1))
        o_s = jnp.where(head_kvh == kvh, o_k, o_s)
        kt = jnp.concatenate([win[k_rows, :], kvwt[k_rows, :]], axis=1)
        vt = jnp.concatenate([win[v_rows, :], kvwt[v_rows, :]], axis=1)
        valid_w = jnp.concatenate([jnp.full((1, win.shape[1]), True), mine], axis=1)
        o_k = attend(kt.astype(BF16), vt.astype(BF16), valid_w)
        o_w = jnp.where(head_kvh == kvh, o_k, o_w)

    oc = oc_ref[0]
    o_c = jnp.where(head_kvh == 0, oc[:, :HEAD_DIM], oc[:, HEAD_DIM:])
    gates = jnp.sum(jnp.where(mine, gt_ref[...], 0.0), axis=1, keepdims=True)
    o_ref[0] = (gates[0:N_HEADS] * o_c + gates[N_HEADS:2 * N_HEADS] * o_s + gates[2 * N_HEADS:3 * N_HEADS] * o_w)

    new_col = jnp.sum(jnp.where(mine, kvwt, 0.0), axis=1, keepdims=True)
    last = lax.broadcasted_iota(jnp.int32, win.shape, 1) == win.shape[1] - 1
    nwin_ref[0] = jnp.where(last, new_col, pltpu.roll(win, win.shape[1] - 1, 1))


def _attn_sample(idx, page_table, cache_t, q, o_c, gt, kvst, kvwt, win_t, *, n_past_slc):
    nb = q.shape[0]
    sub = PAGE_SIZE // SLC_BLOCK
    win_buf = win_t.shape[2]
    n_pages = page_table.shape[1]

    def blk_spec(kvh, t):
        def index(i, idx_ref, pt_ref):
            j = jnp.minimum(idx_ref[(i * KV_HEADS + kvh) * SLC_TOPN + t], n_past_slc - 1)
            return (pt_ref[i * n_pages + j // sub], 0, kvh, 0, 0)
        return pl.BlockSpec((None, 2, None, HEAD_DIM, PAGE_SIZE), index)

    per_b = lambda shape: pl.BlockSpec((1,) + shape, lambda i, a, c: (i,) + (0,) * len(shape))
    const = lambda a: pl.BlockSpec(a.shape, lambda i, x, c: (0,) * a.ndim)
    n_sel = KV_HEADS * SLC_TOPN
    return pl.pallas_call(
        functools.partial(_attn_sample_kernel, n_past_slc=n_past_slc),
        grid_spec=pltpu.PrefetchScalarGridSpec(
            num_scalar_prefetch=2,
            grid=(nb,),
            in_specs=[blk_spec(k, t) for k in range(KV_HEADS) for t in range(SLC_TOPN)] + [
                per_b((N_HEADS, HEAD_DIM)), per_b((N_HEADS, D_KV)), const(gt), const(kvst), const(kvwt),
                per_b((ROW_W, win_buf))],
            out_specs=[per_b((N_HEADS, HEAD_DIM)), per_b((ROW_W, win_buf))],
        ),
        out_shape=[jax.ShapeDtypeStruct((nb, N_HEADS, HEAD_DIM), F32), jax.ShapeDtypeStruct(win_t.shape, F32)],
        compiler_params=_cparams("arbitrary"),
        name="attn_sample",
    )(idx.reshape(-1), page_table.reshape(-1), *([cache_t] * n_sel), q, o_c, gt, kvst, kvwt, win_t)


def _conv_tail(y, b_ref, lg_ref, lb_ref):
    y = y + b_ref[...]
    yc = y - jnp.mean(y, axis=-1, keepdims=True)
    yn = yc * lax.rsqrt(jnp.mean(yc * yc, axis=-1, keepdims=True) + EPS)
    return _silu(yn * lg_ref[...] + lb_ref[...])


def _conv_prompt_kernel(u_ref, halo_ref, w_ref, b_ref, lg_ref, lb_ref, o_ref, buf_ref, *, halo):
    tm = u_ref.shape[1]
    j = pl.program_id(1)
    buf_ref[0:halo, :] = jnp.where(j > 0, halo_ref[0], 0.0)
    buf_ref[halo:, :] = u_ref[0]
    lead = halo - (CONV_WIDTH - 1)
    y = buf_ref[lead:lead + tm, :] * w_ref[0:1, :]
    for w in range(1, CONV_WIDTH):
        y = y + buf_ref[lead + w:lead + w + tm, :] * w_ref[w:w + 1, :]
    o_ref[0] = _conv_tail(y, b_ref, lg_ref, lb_ref)


def _conv_prompt(u, w_dw, b_dw, ln_g, ln_b, *, tm):
    b, s, dc = u.shape
    halo = 32
    per = tm // halo
    const = lambda a: pl.BlockSpec(a.shape, lambda i, j: (0,) * a.ndim)
    return pl.pallas_call(
        functools.partial(_conv_prompt_kernel, halo=halo),
        grid=(b, s // tm),
        in_specs=[pl.BlockSpec((1, tm, dc), lambda i, j: (i, j, 0)),
                  pl.BlockSpec((1, halo, dc), lambda i, j: (i, jnp.maximum(j * per - 1, 0), 0)),
                  const(w_dw), const(b_dw), const(ln_g), const(ln_b)],
        out_specs=pl.BlockSpec((1, tm, dc), lambda i, j: (i, j, 0)),
        out_shape=jax.ShapeDtypeStruct((b, s, dc), F32),
        scratch_shapes=[pltpu.VMEM((tm + halo, dc), F32)],
        compiler_params=_cparams("arbitrary", "arbitrary"),
        name="conv_prompt",
    )(u, u, w_dw, b_dw, ln_g, ln_b)


def _conv_sample_kernel(st_ref, u_ref, w_ref, b_ref, lg_ref, lb_ref, o_ref):
    hist = CONV_WIDTH - 1
    y = u_ref[...] * w_ref[hist:hist + 1, :]
    for w in range(hist):
        y = y + st_ref[w] * w_ref[w:w + 1, :]
    o_ref[...] = _conv_tail(y, b_ref, lg_ref, lb_ref)


def _conv_sample(state_t, u, w_dw, b_dw, ln_g, ln_b, *, sb):
    nb, dc = u.shape
    hist = state_t.shape[0]
    const = lambda a: pl.BlockSpec(a.shape, lambda i: (0,) * a.ndim)
    return pl.pallas_call(
        _conv_sample_kernel,
        grid=(nb // sb,),
        in_specs=[pl.BlockSpec((hist, sb, dc), lambda i: (0, i, 0)), pl.BlockSpec((sb, dc), lambda i: (i, 0)),
                  const(w_dw), const(b_dw), const(ln_g), const(ln_b)],
        out_specs=pl.BlockSpec((sb, dc), lambda i: (i, 0)),
        out_shape=jax.ShapeDtypeStruct((nb, dc), F32),
        compiler_params=_cparams("arbitrary"),
        name="conv_sample",
    )(state_t, u, w_dw, b_dw, ln_g, ln_b)


def _finish_kernel(x_ref, oa_ref, ocv_ref, gate1_ref, shift2_ref, scale2_ref, ga_ref, gc_ref, wout_ref, g2_ref,
                   wr_ref, br_ref, x1_ref, h2_ref, comb_ref):
    mixed = jnp.concatenate([_rms(oa_ref[0], ga_ref[...]), _rms(ocv_ref[0], gc_ref[...])], axis=1)
    mix = _mm(mixed.astype(BF16), wout_ref[...])
    x1 = x_ref[0] + gate1_ref[0] * mix
    x1_ref[0] = x1
    h2 = _rms(x1, g2_ref[...]) * (1.0 + scale2_ref[0]) + shift2_ref[0]
    h2_ref[0] = h2.astype(BF16)

    h_hi, h_mid, _ = _split3(h2)
    w_hi, w_mid, _ = _split3(wr_ref[...])
    lg = (_mm(h_hi, w_hi) + _mm(h_hi, w_mid)
          + _mm(h_mid, w_hi)) + br_ref[...]
    lane = lax.broadcasted_iota(jnp.int32, lg.shape, 1)
    lane_f = lane.astype(F32)
    is_group = (lane >= N_EXPERTS) & (lane < N_EXPERTS + N_GROUPS)
    gl = jnp.where(is_group, lg, NEG_INF)
    g_max = jnp.max(gl, axis=1, keepdims=True)
    p_top = 1.0 / jnp.sum(jnp.where(is_group, jnp.exp(gl - g_max), 0.0), axis=1, keepdims=True)
    g_lane = jnp.min(jnp.where(gl == g_max, lane_f, 1e9), axis=1, keepdims=True)
    in_group = (lane < N_EXPERTS) & ((lane // EPG).astype(F32) == g_lane - N_EXPERTS)
    el = jnp.where(in_group, lg, NEG_INF)
    l1 = jnp.max(el, axis=1, keepdims=True)
    i1 = jnp.min(jnp.where(el == l1, lane_f, 1e9), axis=1, keepdims=True)
    el2 = jnp.where(lane_f == i1, NEG_INF, el)
    l2 = jnp.max(el2, axis=1, keepdims=True)
    i2 = jnp.min(jnp.where(el2 == l2, lane_f, 1e9), axis=1, keepdims=True)
    r = jnp.exp(l2 - l1)
    w1 = p_top / (1.0 + r)
    w2 = p_top * r / (1.0 + r)
    comb_ref[0] = jnp.where(lane_f == i1, w1, jnp.where(lane_f == i2, w2, 0.0))


def _finish(x, o_attn, o_conv, gate1, shift2, scale2, ga, gc, w_out, g2, w_route, b_route, *, tm):
    b, s, d = x.shape
    r = gate1.shape[1]
    mod_block = (1, 1, d) if r == 1 else (1, tm, d)
    mod_map = (lambda i, j: (i, 0, 0)) if r == 1 else (lambda i, j: (i, j, 0))
    mod = pl.BlockSpec(mod_block, mod_map)
    row = lambda w: pl.BlockSpec((1, tm, w), lambda i, j: (i, j, 0))
    const = lambda a: pl.BlockSpec(a.shape, lambda i, j: (0,) * a.ndim)
    return pl.pallas_call(
        _finish_kernel,
        grid=(b, s // tm),
        in_specs=[row(d), row(D_ATTN), row(D_CONV), mod, mod, mod, const(ga), const(gc), const(w_out), const(g2),
                  const(w_route), const(b_route)],
        out_specs=[row(d), row(d), row(128)],
        out_shape=[jax.ShapeDtypeStruct((b, s, d), F32), jax.ShapeDtypeStruct((b, s, d), BF16),
                   jax.ShapeDtypeStruct((b, s, 128), F32)],
        compiler_params=_cparams("arbitrary", "arbitrary"),
        name="finish",
    )(x, o_attn, o_conv, gate1, shift2, scale2, ga, gc, w_out, g2, w_route, b_route)


def _moe_kernel(h_ref, comb_ref, x1_ref, gate2_ref, wg_ref, wu_ref, wd_ref, gf_ref, y_ref, acc_ref):
    e = pl.program_id(2)

    @pl.when(e == 0)
    def _():
        acc_ref[...] = jnp.zeros_like(acc_ref)

    h = h_ref[0]
    a = _mm(h, wg_ref[0].astype(BF16))
    u = _mm(h, wu_ref[0].astype(BF16))
    comb = comb_ref[0]
    lane = lax.broadcasted_iota(jnp.int32, comb.shape, 1)
    cw = jnp.sum(jnp.where(lane == e, comb, 0.0), axis=1, keepdims=True)
    hid = (_silu(a) * u * cw).astype(BF16)
    acc_ref[...] += _mm(hid, wd_ref[0].astype(BF16))

    @pl.when(e == pl.num_programs(2) - 1)
    def _():
        y = x1_ref[0] + gate2_ref[0] * acc_ref[...]
        y_ref[0] = _rms(y, gf_ref[...])


def _moe(h2, comb, x1, gate2, w_gate, w_up, w_down, final_g, *, tm):
    b, s, d = x1.shape
    r = gate2.shape[1]
    mod_block = (1, 1, d) if r == 1 else (1, tm, d)
    mod_map = (lambda i, j, e: (i, 0, 0)) if r == 1 else (lambda i, j, e: (i, j, 0))
    row = lambda w: pl.BlockSpec((1, tm, w), lambda i, j, e: (i, j, 0))
    ne, _, de = w_gate.shape
    return pl.pallas_call(
        _moe_kernel,
        grid=(b, s // tm, ne),
        in_specs=[row(d), row(128), row(d), pl.BlockSpec(mod_block, mod_map),
                  pl.BlockSpec((1, d, de), lambda i, j, e: (e, 0, 0)),
                  pl.BlockSpec((1, d, de), lambda i, j, e: (e, 0, 0)),
                  pl.BlockSpec((1, de, d), lambda i, j, e: (e, 0, 0)),
                  pl.BlockSpec((1, d), lambda i, j, e: (0, 0))],
        out_specs=row(d),
        out_shape=jax.ShapeDtypeStruct((b, s, d), F32),
        scratch_shapes=[pltpu.VMEM((tm, d), F32)],
        compiler_params=_cparams("arbitrary", "arbitrary", "arbitrary"),
        name="moe",
    )(h2, comb, x1, gate2, w_gate, w_up, w_down, final_g)


def _prep_w_in(w_in):
    o_gl = D_ATTN + 3 * ROW_W
    wqkv = w_in[:, :o_gl].astype(BF16)
    wgl = w_in[:, o_gl:o_gl + 3 * N_HEADS]
    pad = jnp.zeros((w_in.shape[0], 128 - 3 * N_HEADS), w_in.dtype)
    wgl = jnp.concatenate([wgl.reshape(-1, N_HEADS, 3).transpose(0, 2, 1).reshape(-1, 3 * N_HEADS), pad], axis=1)
    wu = w_in[:, o_gl + 3 * N_HEADS:].astype(BF16)
    return wqkv, wgl.astype(BF16), wu


def _prep_compress(w_cmp1, pos_cmp, w_cmp2):
    ratio = CMP_BLOCK // CMP_STRIDE
    eye = jnp.eye(KV_HEADS, dtype=w_cmp1.dtype)
    w1 = w_cmp1.reshape(2, ratio, CMP_STRIDE, HEAD_DIM, HEAD_DIM)
    w1big = jnp.einsum('crsdf,kj->cskdrjf', w1, eye).reshape(2, CMP_STRIDE * D_KV, ratio * D_KV).astype(BF16)
    w2big = jnp.einsum('cfd,kj->ckfjd', w_cmp2, eye).reshape(2, D_KV, D_KV).astype(BF16)
    pos = pos_cmp.reshape(2, ratio, CMP_STRIDE, 1, HEAD_DIM)
    pos = jnp.broadcast_to(pos, (2, ratio, CMP_STRIDE, KV_HEADS, HEAD_DIM)).reshape(2, ratio, CMP_STRIDE * D_KV)
    posrows = jnp.concatenate([pos, jnp.zeros((2, 8 - ratio, CMP_STRIDE * D_KV), pos.dtype)], axis=1).astype(BF16)
    return w1big, w2big, posrows


def _prep_router(w_group, b_group, w_router, b_router):
    d = w_group.shape[0]
    pad = 128 - N_EXPERTS - N_GROUPS
    w = jnp.concatenate([w_router, w_group, jnp.zeros((d, pad), w_group.dtype)], axis=1)
    b = jnp.concatenate([b_router, b_group, jnp.zeros((pad,), b_group.dtype)]).reshape(1, 128)
    return w, b


def kernel(x_prompt, x_sample, cache_cmp_kv, cache_slc_kv, state_win_kv, state_conv, page_table, c_prompt, c_sample,
           norm1_g, w_ada, b_ada, w_in, w_cmp1, pos_cmp, w_cmp2, w_dw, b_dw, conv_ln_g, conv_ln_b, g_attn_out,
           g_conv_out, w_out, norm2_g, w_group, b_group, w_router, b_router, w_gate, w_up, w_down, final_g):
    depth = norm1_g.shape[0]
    assert depth == 1 and x_sample.shape[1] == 1
    bp, seq, d = x_prompt.shape
    nb = x_sample.shape[0]
    n_pages = page_table.shape[1]
    past_len = n_pages * PAGE_SIZE
    n_past_slc = past_len // SLC_BLOCK
    win_buf = state_win_kv.shape[2]
    assert win_buf == WINDOW and seq % K_TILE == 0 and seq >= WINDOW + Q_TILE
    l = 0
    row2 = lambda a: a.reshape(1, -1)

    mods = _adaln(jnp.concatenate([c_prompt, c_sample], axis=0), w_ada[l], b_ada[l]).reshape(bp + nb, 6, d)
    mods_p = [mods[:bp, i][:, None, :] for i in range(6)]
    mods_s = [mods[bp:, i][None, :, :] for i in range(6)]

    wqkv, wgl, wu = _prep_w_in(w_in[l])
    w1big, w2big, posrows = _prep_compress(w_cmp1[l], pos_cmp[l], w_cmp2[l])
    w_route, b_route = _prep_router(w_group[l], b_group[l], w_router[l], b_router[l])
    w_out_b = w_out[l].astype(BF16)
    conv_args = (w_dw[l], row2(b_dw[l]), row2(conv_ln_g[l]), row2(conv_ln_b[l]))

    def kv_rows_from_t(a_t):
        n, _, t = a_t.shape
        return a_t.reshape(n, 2, KV_HEADS, HEAD_DIM, t).transpose(0, 4, 1, 2, 3)

    def kv_rows_to_t(a):
        return a.transpose(0, 2, 3, 4, 1)

    qt, kvc, kvct, kvst, kvwt, ks, vts, kw, vtw, gt, u = _mixer(
        x_prompt, mods_p[0], mods_p[1], row2(norm1_g[l]), wqkv, wgl, wu, tm=512, prompt=True)
    kc, vct = _compress_prompt(kvc, w1big, w2big, posrows)
    o_attn = _attn_prompt(qt, kc, vct, ks, vts, kw, vtw, gt)
    o_conv = _conv_prompt(u, *conv_args, tm=512)
    x1, h2, comb = _finish(x_prompt, o_attn, o_conv, mods_p[2], mods_p[3], mods_p[4], row2(g_attn_out[l]),
                           row2(g_conv_out[l]), w_out_b, row2(norm2_g[l]), w_route, b_route, tm=512)
    y_prompt = _moe(h2, comb, x1, mods_p[5], w_gate[l], w_up[l], w_down[l], row2(final_g), tm=1024)

    new_cmp_prompt = kv_rows_from_t(kvct)[None]
    new_slc_prompt = kv_rows_from_t(kvst)[None]
    new_win_prompt = kv_rows_from_t(kvwt[:, :, seq - WINDOW:])[None]
    new_conv_prompt = u[:, seq - (CONV_WIDTH - 1):][None]

    xs = x_sample.reshape(1, nb, d)
    q_s, kvct_s, kvst_s, kvwt_s, gt_s, u_s = _mixer(
        xs, mods_s[0], mods_s[1], row2(norm1_g[l]), wqkv, wgl, wu, tm=nb, prompt=False)
    cmp_t = kv_rows_to_t(cache_cmp_kv[l])
    kc_s, vc_s = _compress_sample(cmp_t.reshape(cmp_t.shape[0], ROW_W, PAGE_SIZE), page_table, w1big, w2big, posrows)

    q4 = q_s.reshape(nb, KV_HEADS, GQA, HEAD_DIM)
    zq = jnp.zeros_like(q4)
    kvh_id = jnp.arange(KV_HEADS).reshape(1, KV_HEADS, 1, 1)
    qpad = jnp.concatenate([jnp.where(kvh_id == 0, q4, zq), jnp.where(kvh_id == 1, q4, zq)],
                           axis=-1).reshape(nb, N_HEADS, D_KV)
    o_c, idx = _attn_sample_cmp(qpad, kc_s, vc_s, q_pos=past_len, n_slc=n_past_slc + 1)
    idx = idx[:, :KV_HEADS, :SLC_TOPN]
    win_t = kv_rows_to_t(state_win_kv[l]).reshape(nb, ROW_W, win_buf)
    o_heads, new_win_t = _attn_sample(idx, page_table, kv_rows_to_t(cache_slc_kv[l]), q_s.reshape(nb, N_HEADS, HEAD_DIM),
                                      o_c, gt_s[0], kvst_s[0], kvwt_s[0], win_t, n_past_slc=n_past_slc)
    o_attn_s = o_heads.reshape(1, nb, D_ATTN)
    state_t = state_conv[l].transpose(1, 0, 2)
    u_rows = u_s.reshape(nb, D_CONV)
    o_conv_s = _conv_sample(state_t, u_rows, *conv_args, sb=8).reshape(1, nb, D_CONV)
    x1_s, h2_s, comb_s = _finish(xs, o_attn_s, o_conv_s, mods_s[2], mods_s[3], mods_s[4], row2(g_attn_out[l]),
                                 row2(g_conv_out[l]), w_out_b, row2(norm2_g[l]), w_route, b_route, tm=nb)
    y_sample = _moe(h2_s, comb_s, x1_s, mods_s[5], w_gate[l], w_up[l], w_down[l], row2(final_g), tm=nb)

    row_shape = (1, nb, 1, 2, KV_HEADS, HEAD_DIM)
    new_cmp_sample = kvct_s[0].T.reshape(row_shape)
    new_slc_sample = kvst_s[0].T.reshape(row_shape)
    new_win_sample = kv_rows_from_t(new_win_t)[None]
    new_conv_sample = jnp.concatenate([state_t[1:], u_rows[None]], axis=0).transpose(1, 0, 2)[None]

    return (y_prompt, y_sample.reshape(nb, 1, d), new_cmp_prompt, new_slc_prompt, new_win_prompt, new_conv_prompt,
            new_cmp_sample, new_slc_sample, new_win_sample, new_conv_sample)
```

```python
import functools

import jax
import jax.numpy as jnp
from jax import lax
from jax.experimental import pallas as pl
from jax.experimental.pallas import tpu as pltpu

F32 = jnp.float32
BF16 = jnp.bfloat16

D_MODEL = 1024
N_HEADS = 8
HEAD_DIM = 64
KV_HEADS = 2
GQA = N_HEADS // KV_HEADS
D_ATTN = N_HEADS * HEAD_DIM
D_CONV = D_MODEL - D_ATTN
D_KV = KV_HEADS * HEAD_DIM
CMP_BLOCK = 32
CMP_STRIDE = 16
SLC_BLOCK = 64
SLC_TOPN = 16
N_LOCAL_BLOCKS = 2
WINDOW = 512
FORCED_SCORE = 1e4
CONV_WIDTH = 31
N_GROUPS = 4
EPG = 8
N_EXPERTS = N_GROUPS * EPG
D_EXPERT = 256
PAGE_SIZE = 128
EPS = 1e-6
NEG_INF = -1e30
SCALE = HEAD_DIM ** -0.5
LOG2E = 1.4426950408889634
ROW_W = 2 * D_KV
CHUNK_W = CMP_STRIDE * ROW_W
CHUNK_PITCH = 20
Q_TILE = 128
K_TILE = 512
VMEM_LIMIT = 48 * 1024 * 1024


def _cparams(*sem):
    return pltpu.CompilerParams(dimension_semantics=sem, vmem_limit_bytes=VMEM_LIMIT)


def _rms(x, g):
    return x * lax.rsqrt(jnp.mean(x * x, axis=-1, keepdims=True) + EPS) * g


def _silu(x):
    return x * jax.nn.sigmoid(x)


def _mm(a, b):
    return jnp.dot(a, b, preferred_element_type=F32)


def _mm_nt(a, b):
    return lax.dot_general(a, b, (((1,), (1,)), ((), ())), preferred_element_type=F32)


def _split3(x):
    hi = x.astype(BF16)
    r = x - hi.astype(F32)
    mid = r.astype(BF16)
    lo = (r - mid.astype(F32)).astype(BF16)
    return hi, mid, lo


def _softmax_masked(s, mask, axis):
    s = jnp.where(mask, s, NEG_INF)
    m = jnp.max(s, axis=axis, keepdims=True)
    e = jnp.exp(s - m)
    p = e / jnp.sum(e, axis=axis, keepdims=True)
    return jnp.where(mask, p, 0.0)


def _adaln_kernel(c_ref, w_ref, b_ref, o_ref):
    s = _silu(c_ref[...]).astype(BF16)
    o_ref[...] = _mm(s, w_ref[...].astype(BF16)) + b_ref[...]


def _adaln(c_all, w_ada, b_ada):
    n, d = c_all.shape
    nout = w_ada.shape[1]
    tn = 1024
    return pl.pallas_call(
        _adaln_kernel,
        grid=(nout // tn,),
        in_specs=[pl.BlockSpec((n, d), lambda j: (0, 0)),
                  pl.BlockSpec((d, tn), lambda j: (0, j)),
                  pl.BlockSpec((1, tn), lambda j: (0, j))],
        out_specs=pl.BlockSpec((n, tn), lambda j: (0, j)),
        out_shape=jax.ShapeDtypeStruct((n, nout), F32),
        compiler_params=_cparams("arbitrary"),
        name="adaln",
    )(c_all, w_ada, b_ada.reshape(1, nout))


def _mixer_kernel(x_ref, shift_ref, scale_ref, g_ref, wqkv_ref, wgl_ref, wu_ref, *outs, prompt):
    x = x_ref[0]
    h = _rms(x, g_ref[...]) * (1.0 + scale_ref[0]) + shift_ref[0]
    hb = h.astype(BF16)
    p = _mm(hb, wqkv_ref[...])
    gl = _mm(hb, wgl_ref[...])
    pu = _mm(hb, wu_ref[...])
    u = pu[:, :D_CONV] * jax.nn.sigmoid(pu[:, D_CONV:])
    q = p[:, :D_ATTN] * (SCALE * LOG2E if prompt else SCALE)
    o = D_ATTN
    kvc = p[:, o:o + ROW_W]
    kvs = p[:, o + ROW_W:o + 2 * ROW_W]
    kvw = p[:, o + 2 * ROW_W:o + 3 * ROW_W]
    kvs_t = kvs.T
    kvw_t = kvw.T
    if prompt:
        q_ref, kvc_ref, kvct_ref, kvst_ref, kvwt_ref, ks_ref, vts_ref, kw_ref, vtw_ref, gt_ref, u_ref = outs
        q_ref[0] = q.T.astype(BF16)
        kvc_ref[0] = kvc
        ks_ref[0] = kvs[:, :D_KV].astype(BF16)
        vts_ref[0] = kvs_t[D_KV:, :].astype(BF16)
        kw_ref[0] = kvw[:, :D_KV].astype(BF16)
        vtw_ref[0] = kvw_t[D_KV:, :].astype(BF16)
    else:
        q_ref, kvct_ref, kvst_ref, kvwt_ref, gt_ref, u_ref = outs
        q_ref[0] = q
    kvct_ref[0] = kvc.T
    kvst_ref[0] = kvs_t
    kvwt_ref[0] = kvw_t
    gt_ref[0] = jax.nn.sigmoid(gl).T[:32, :]
    u_ref[0] = u


def _mixer(x, shift, scale, g, wqkv, wgl, wu, *, tm, prompt):
    b, s, d = x.shape
    r = shift.shape[1]
    mod_block = (1, 1, d) if r == 1 else (1, tm, d)
    mod_map = (lambda i, j: (i, 0, 0)) if r == 1 else (lambda i, j: (i, j, 0))
    row = lambda w: pl.BlockSpec((1, tm, w), lambda i, j: (i, j, 0))
    col = lambda w: pl.BlockSpec((1, w, tm), lambda i, j: (i, 0, j))
    const = lambda a: pl.BlockSpec(a.shape, lambda i, j: (0,) * a.ndim)
    rows = lambda w, dt: jax.ShapeDtypeStruct((b, s, w), dt)
    cols = lambda w, dt: jax.ShapeDtypeStruct((b, w, s), dt)
    if prompt:
        out_specs = [col(D_ATTN), row(ROW_W), col(ROW_W), col(ROW_W), col(ROW_W), row(D_KV), col(D_KV), row(D_KV),
                     col(D_KV), col(32), row(D_CONV)]
        out_shape = [cols(D_ATTN, BF16), rows(ROW_W, F32), cols(ROW_W, F32), cols(ROW_W, F32), cols(ROW_W, F32),
                     rows(D_KV, BF16), cols(D_KV, BF16), rows(D_KV, BF16), cols(D_KV, BF16), cols(32, F32),
                     rows(D_CONV, F32)]
    else:
        out_specs = [row(D_ATTN), col(ROW_W), col(ROW_W), col(ROW_W), col(32), row(D_CONV)]
        out_shape = [rows(D_ATTN, F32), cols(ROW_W, F32), cols(ROW_W, F32), cols(ROW_W, F32), cols(32, F32),
                     rows(D_CONV, F32)]
    return pl.pallas_call(
        functools.partial(_mixer_kernel, prompt=prompt),
        grid=(b, s // tm),
        in_specs=[row(d), pl.BlockSpec(mod_block, mod_map), pl.BlockSpec(mod_block, mod_map),
                  const(g), const(wqkv), const(wgl), const(wu)],
        out_specs=out_specs,
        out_shape=out_shape,
        compiler_params=_cparams("arbitrary", "arbitrary"),
        name="mixer_prompt" if prompt else "mixer_sample",
    )(x, shift, scale, g, wqkv, wgl, wu)


def _chunk_part(load_offset, w1_c):
    xc = jnp.concatenate([load_offset(s) for s in range(CMP_STRIDE)], axis=1)
    return _mm(xc.astype(BF16), w1_c)


def _compress_finish(part, posb, w2_c):
    n = part.shape[0]
    nxt = pltpu.roll(part[:, D_KV:], n - 1, 0)
    pre = part[:, :D_KV] + nxt + posb[0:1, :D_KV] + posb[1:2, D_KV:]
    return _mm(_silu(pre).astype(BF16), w2_c)


def _compress_prompt_kernel(x_ref, w1_ref, w2_ref, pos_ref, kc_ref, vct_ref):
    for c in range(2):
        part = _chunk_part(lambda s: x_ref[0, :, s * ROW_W + c * D_KV:s * ROW_W + (c + 1) * D_KV], w1_ref[c])
        posb = _mm(pos_ref[c], w1_ref[c])
        out = _compress_finish(part, posb, w2_ref[c])
        if c == 0:
            kc_ref[0] = out.astype(BF16)
        else:
            vct_ref[0] = out.T.astype(BF16)


def _compress_prompt(kvc, w1big, w2big, posrows):
    b, s, _ = kvc.shape
    n = s // CMP_STRIDE
    x = kvc.reshape(b, n, CHUNK_W)
    const = lambda a: pl.BlockSpec(a.shape, lambda i: (0,) * a.ndim)
    return pl.pallas_call(
        _compress_prompt_kernel,
        grid=(b,),
        in_specs=[pl.BlockSpec((1, n, CHUNK_W), lambda i: (i, 0, 0)), const(w1big), const(w2big), const(posrows)],
        out_specs=[pl.BlockSpec((1, n, D_KV), lambda i: (i, 0, 0)), pl.BlockSpec((1, D_KV, n), lambda i: (i, 0, 0))],
        out_shape=[jax.ShapeDtypeStruct((b, n, D_KV), BF16), jax.ShapeDtypeStruct((b, D_KV, n), BF16)],
        compiler_params=_cparams("arbitrary"),
        name="compress_prompt",
    )(x, w1big, w2big, posrows)


def _compress_sample_kernel(pt_ref, *refs, pages):
    x_refs = refs[:pages]
    w1_ref, w2_ref, pos_ref, kc_ref, vc_ref = refs[pages:pages + 5]
    row_refs = refs[pages + 5:]
    cpp = PAGE_SIZE // CMP_STRIDE
    n = pages * cpp
    for c in range(2):
        for k in range(pages):
            rows = x_refs[k][0, c * D_KV:(c + 1) * D_KV, :].T
            for j in range(cpp):
                r0 = (k * cpp + j) * CHUNK_PITCH
                row_refs[c][r0:r0 + CMP_STRIDE, :] = rows[j * CMP_STRIDE:(j + 1) * CMP_STRIDE, :]
        part = _chunk_part(lambda s: row_refs[c][pl.ds(s, n, stride=CHUNK_PITCH), :], w1_ref[c])
        posb = _mm(pos_ref[c], w1_ref[c])
        out = _compress_finish(part, posb, w2_ref[c])
        (kc_ref if c == 0 else vc_ref)[0] = out.astype(BF16)


def _compress_sample(cache_t, page_table, w1big, w2big, posrows):
    nb, n_pages = page_table.shape
    n = n_pages * (PAGE_SIZE // CMP_STRIDE)
    page_spec = lambda k: pl.BlockSpec((1, ROW_W, PAGE_SIZE), lambda i, pt: (pt[i * n_pages + k], 0, 0))
    const = lambda a: pl.BlockSpec(a.shape, lambda i, pt: (0,) * a.ndim)
    return pl.pallas_call(
        functools.partial(_compress_sample_kernel, pages=n_pages),
        grid_spec=pltpu.PrefetchScalarGridSpec(
            num_scalar_prefetch=1,
            grid=(nb,),
            in_specs=[page_spec(k) for k in range(n_pages)] + [const(w1big), const(w2big), const(posrows)],
            out_specs=[pl.BlockSpec((1, n, D_KV), lambda i, pt: (i, 0, 0))] * 2,
            scratch_shapes=[pltpu.VMEM((n * CHUNK_PITCH, D_KV), F32)] * 2,
        ),
        out_shape=[jax.ShapeDtypeStruct((nb, n, D_KV), BF16)] * 2,
        compiler_params=_cparams("arbitrary"),
        name="compress_sample",
    )(page_table.reshape(-1), *([cache_t] * n_pages), w1big, w2big, posrows)


def _overlap(cmp_idx, slc_idx):
    lo = cmp_idx * CMP_STRIDE
    so = slc_idx * SLC_BLOCK
    return (lo <= so + SLC_BLOCK - 1) & (lo + CMP_BLOCK - 1 >= so)


def _forced_importance(imp, blk, cur, n_slc):
    valid = (blk <= cur) & (blk < n_slc)
    forced = (blk == 0) | ((cur - blk >= 0) & (cur - blk < N_LOCAL_BLOCKS))
    return jnp.where(valid & forced, FORCED_SCORE, jnp.where(valid, imp, -1.0))


def _attn_prompt_kernel(qt_ref, kc_ref, vct_ref, ks_ref, vts_ref, kw_ref, vtw_ref, gt_ref, o_ref, sel_ref):
    i = pl.program_id(1)
    seq = ks_ref.shape[1]
    nbp = kc_ref.shape[1]
    n_slc = seq // SLC_BLOCK
    lanes = GQA * Q_TILE
    lane = lax.broadcasted_iota(jnp.int32, (1, lanes), 1)
    qpos = i * Q_TILE + (lane & (Q_TILE - 1))
    qpos_q = qpos[:, :Q_TILE]
    qt = qt_ref[0]
    gt = gt_ref[0]
    win_keys = min(WINDOW + Q_TILE, seq)
    blocks_per_tile = K_TILE // SLC_BLOCK

    ovl = _overlap(lax.broadcasted_iota(jnp.int32, (n_slc, nbp), 1), lax.broadcasted_iota(jnp.int32, (n_slc, nbp), 0))
    ovl = jnp.where(ovl, 1.0, 0.0).astype(BF16)
    blk = lax.broadcasted_iota(jnp.int32, (n_slc, Q_TILE), 0)

    cpos = lax.broadcasted_iota(jnp.int32, (nbp, lanes), 0) * CMP_STRIDE + (CMP_BLOCK - 1)
    cmp_bias = jnp.where(cpos <= qpos, 0.0, NEG_INF)
    any_cmp = jnp.where(qpos >= CMP_BLOCK - 1, 1.0, 0.0)
    w0 = pl.multiple_of(jnp.clip(i * Q_TILE - WINDOW, 0, seq - win_keys), Q_TILE)
    wpos = w0 + lax.broadcasted_iota(jnp.int32, (win_keys, lanes), 0)
    win_bias = jnp.where(wpos <= qpos, jnp.where(wpos >= qpos - WINDOW, 0.0, NEG_INF), NEG_INF)
    n_steps = ((i + 1) * Q_TILE + K_TILE - 1) // K_TILE
    last0 = pl.multiple_of((n_steps - 1) * K_TILE, K_TILE)
    causal_bias = jnp.where(last0 + lax.broadcasted_iota(jnp.int32, (K_TILE, lanes), 0) <= qpos, 0.0, NEG_INF)

    heads = range(KV_HEADS)
    hd = [slice(kvh * HEAD_DIM, (kvh + 1) * HEAD_DIM) for kvh in heads]
    qpad, o_c = [], []
    for kvh in heads:
        qk = jnp.concatenate([qt[(kvh * GQA + g) * HEAD_DIM:(kvh * GQA + g + 1) * HEAD_DIM, :] for g in range(GQA)],
                             axis=1)
        zero = jnp.zeros_like(qk)
        qpad.append(jnp.concatenate([qk, zero] if kvh == 0 else [zero, qk], axis=0))

        s = _mm(kc_ref[0], qpad[kvh]) + cmp_bias
        e = jnp.exp2(s - jnp.max(s, axis=0, keepdims=True))
        p = e * (any_cmp / jnp.sum(e, axis=0, keepdims=True))
        o_c.append(_mm(vct_ref[0, hd[kvh], :], p.astype(BF16)))

        psum = p[:, 0:Q_TILE]
        for g in range(1, GQA):
            psum = psum + p[:, g * Q_TILE:(g + 1) * Q_TILE]
        imp = sum(_mm(ovl, t) for t in _split3(psum))
        impf = _forced_importance(imp, blk, qpos_q // SLC_BLOCK, n_slc)
        rank = jnp.zeros_like(impf)
        for k in range(n_slc):
            rk = impf[k:k + 1, :]
            rank = rank + jnp.where(rk > impf, 1.0, 0.0) + jnp.where(rk == impf, jnp.where(blk > k, 1.0, 0.0), 0.0)
        sel = jnp.where(rank < SLC_TOPN, jnp.where(impf >= 0.0, 0.0, NEG_INF), NEG_INF)
        sel_ref[kvh] = jnp.concatenate([sel] * GQA, axis=1)

    def slc_step(t, carry, extra_bias=None):
        k0 = pl.multiple_of(t * K_TILE, K_TILE)
        keys = ks_ref[0, pl.ds(k0, K_TILE), :]
        out = []
        for kvh in heads:
            m, l, acc = carry[kvh]
            sk = _mm(keys, qpad[kvh])
            sk = jnp.concatenate(
                [sk[j * SLC_BLOCK:(j + 1) * SLC_BLOCK, :] + sel_ref[kvh, pl.ds(t * blocks_per_tile + j, 1), :]
                 for j in range(blocks_per_tile)], axis=0)
            if extra_bias is not None:
                sk = sk + extra_bias
            m_new = jnp.maximum(m, jnp.max(sk, axis=0, keepdims=True))
            alpha = jnp.exp2(m - m_new)
            e = jnp.exp2(sk - m_new)
            l_new = alpha * l + jnp.sum(e, axis=0, keepdims=True)
            pv = _mm(vts_ref[0, hd[kvh], pl.ds(k0, K_TILE)], e.astype(BF16))
            out.append((m_new, l_new, alpha * acc + pv))
        return tuple(out)

    init = (jnp.full((1, lanes), NEG_INF, F32), jnp.zeros((1, lanes), F32), jnp.zeros((HEAD_DIM, lanes), F32))
    carry = lax.fori_loop(0, n_steps - 1, slc_step, (init,) * KV_HEADS)
    carry = slc_step(n_steps - 1, carry, causal_bias)

    for kvh in heads:
        _, l_s, acc_s = carry[kvh]
        o_s = acc_s * (1.0 / l_s)

        sw = _mm(kw_ref[0, pl.ds(w0, win_keys), :], qpad[kvh]) + win_bias
        ew = jnp.exp2(sw - jnp.max(sw, axis=0, keepdims=True))
        o_w = _mm(vtw_ref[0, hd[kvh], pl.ds(w0, win_keys)], ew.astype(BF16)) * (
            1.0 / jnp.sum(ew, axis=0, keepdims=True))

        def gate(r):
            return jnp.concatenate([gt[r * N_HEADS + kvh * GQA + g:r * N_HEADS + kvh * GQA + g + 1, :]
                                    for g in range(GQA)], axis=1)
        o_t = gate(0) * o_c[kvh] + gate(1) * o_s + gate(2) * o_w
        for pair in range(GQA // 2):
            two = jnp.concatenate([o_t[:, (2 * pair) * Q_TILE:(2 * pair + 1) * Q_TILE],
                                   o_t[:, (2 * pair + 1) * Q_TILE:(2 * pair + 2) * Q_TILE]], axis=0)
            c0 = kvh * GQA * HEAD_DIM + pair * 2 * HEAD_DIM
            o_ref[0, :, c0:c0 + 2 * HEAD_DIM] = two.T


def _attn_prompt(qt, kc, vct, ks, vts, kw, vtw, gt):
    b, _, s = qt.shape
    nbp = kc.shape[1]
    per_b = lambda shape: pl.BlockSpec((1,) + shape, lambda i, j: (i, 0, 0))
    return pl.pallas_call(
        _attn_prompt_kernel,
        grid=(b, s // Q_TILE),
        in_specs=[pl.BlockSpec((1, D_ATTN, Q_TILE), lambda i, j: (i, 0, j)),
                  per_b((nbp, D_KV)), per_b((D_KV, nbp)),
                  per_b((s, D_KV)), per_b((D_KV, s)), per_b((s, D_KV)), per_b((D_KV, s)),
                  pl.BlockSpec((1, 32, Q_TILE), lambda i, j: (i, 0, j))],
        out_specs=pl.BlockSpec((1, Q_TILE, D_ATTN), lambda i, j: (i, j, 0)),
        out_shape=jax.ShapeDtypeStruct((b, s, D_ATTN), F32),
        scratch_shapes=[pltpu.VMEM((KV_HEADS, s // SLC_BLOCK, GQA * Q_TILE), F32)],
        compiler_params=_cparams("arbitrary", "arbitrary"),
        name="attn_prompt",
    )(qt, kc, vct, ks, vts, kw, vtw, gt)


def _attn_sample_cmp_kernel(q_ref, kc_ref, vc_ref, oc_ref, idx_ref, *, q_pos, n_slc):
    q = q_ref[0].astype(BF16)
    nb = kc_ref.shape[1]
    s = _mm_nt(q, kc_ref[0])
    cpos = lax.broadcasted_iota(jnp.int32, s.shape, 1) * CMP_STRIDE + (CMP_BLOCK - 1)
    p = _softmax_masked(s, cpos <= q_pos, 1)
    oc_ref[0] = _mm(p.astype(BF16), vc_ref[0])

    nsp = idx_ref.shape[2] * 2
    group_sums = [jnp.sum(p[k * GQA:(k + 1) * GQA, :], axis=0, keepdims=True) for k in range(KV_HEADS)]
    psum = jnp.concatenate(group_sums + [jnp.zeros((N_HEADS - KV_HEADS, nb), F32)], axis=0)
    ovl = _overlap(lax.broadcasted_iota(jnp.int32, (nb, nsp), 0), lax.broadcasted_iota(jnp.int32, (nb, nsp), 1))
    ovl = jnp.where(ovl, 1.0, 0.0).astype(BF16)
    imp = sum(_mm(t, ovl) for t in _split3(psum))
    blk = lax.broadcasted_iota(jnp.int32, imp.shape, 1)
    impf = _forced_importance(imp, blk, q_pos // SLC_BLOCK, n_slc)
    rank = jnp.zeros_like(impf)
    for k in range(n_slc):
        ck = impf[:, k:k + 1]
        rank = rank + jnp.where(ck > impf, 1.0, 0.0) + jnp.where(ck == impf, jnp.where(blk > k, 1.0, 0.0), 0.0)
    blk_f = blk.astype(F32)
    slot = lax.broadcasted_iota(jnp.int32, (N_HEADS, idx_ref.shape[2]), 1)
    idx = jnp.zeros((N_HEADS, idx_ref.shape[2]), F32)
    for t in range(SLC_TOPN):
        chosen = jnp.sum(jnp.where(rank == float(t), blk_f, 0.0), axis=1, keepdims=True)
        idx = idx + jnp.where(slot == t, chosen, 0.0)
    idx_ref[0] = idx.astype(jnp.int32)


def _attn_sample_cmp(qpad, kc, vc, *, q_pos, n_slc):
    nb = qpad.shape[0]
    n = kc.shape[1]
    per_b = lambda shape: pl.BlockSpec((1,) + shape, lambda i: (i, 0, 0))
    return pl.pallas_call(
        functools.partial(_attn_sample_cmp_kernel, q_pos=q_pos, n_slc=n_slc),
        grid=(nb,),
        in_specs=[per_b((N_HEADS, D_KV)), per_b((n, D_KV)), per_b((n, D_KV))],
        out_specs=[per_b((N_HEADS, D_KV)), per_b((N_HEADS, 128))],
        out_shape=[jax.ShapeDtypeStruct((nb, N_HEADS, D_KV), F32), jax.ShapeDtypeStruct((nb, N_HEADS, 128), jnp.int32)],
        compiler_params=_cparams("arbitrary"),
        name="attn_sample_cmp",
    )(qpad, kc, vc)


def _attn_sample_kernel(idx_ref, pt_ref, *refs, n_past_slc):
    n_sel = KV_HEADS * SLC_TOPN
    blk_refs = refs[:n_sel]
    q_ref, oc_ref, gt_ref, kvst_ref, kvwt_ref, win_ref, o_ref, nwin_ref = refs[n_sel:]
    b = pl.program_id(0)
    nb = kvst_ref.shape[1]
    q = q_ref[0].astype(BF16)
    head_kvh = lax.broadcasted_iota(jnp.int32, (N_HEADS, 1), 0) // GQA
    mine = lax.broadcasted_iota(jnp.int32, (1, nb), 1) == b
    lane_half = lax.broadcasted_iota(jnp.int32, (1, PAGE_SIZE), 1) // SLC_BLOCK

    def attend(kt, vt, valid):
        s = jnp.where(valid, _mm(q, kt), NEG_INF)
        e = jnp.where(valid, jnp.exp(s - jnp.max(s, axis=1, keepdims=True)), 0.0)
        return _mm_nt(e.astype(BF16), vt) / jnp.sum(e, axis=1, keepdims=True)

    kvst = kvst_ref[...]
    kvwt = kvwt_ref[...]
    o_s = jnp.zeros((N_HEADS, HEAD_DIM), F32)
    o_w = jnp.zeros((N_HEADS, HEAD_DIM), F32)
    win = win_ref[0]
    for kvh in range(KV_HEADS):
        k_rows = slice(kvh * HEAD_DIM, (kvh + 1) * HEAD_DIM)
        v_rows = slice(D_KV + kvh * HEAD_DIM, D_KV + (kvh + 1) * HEAD_DIM)
        valid = []
        n_new = jnp.int32(0)
        for t in range(SLC_TOPN):
            j = idx_ref[(b * KV_HEADS + kvh) * SLC_TOPN + t]
            past = j < n_past_slc
            half = jnp.minimum(j, n_past_slc - 1) % (PAGE_SIZE // SLC_BLOCK)
            valid.append((lane_half == half) & past)
            n_new = n_new + jnp.where(past, 0, 1)
        valid.append(mine & (n_new > 0))
        kt = jnp.concatenate([blk_refs[kvh * SLC_TOPN + t][0] for t in range(SLC_TOPN)] + [kvst[k_rows, :]], axis=1)
        vt = jnp.concatenate([blk_refs[kvh * SLC_TOPN + t][1] for t in range(SLC_TOPN)] + [kvst[v_rows, :]], axis=1)
        o_k = attend(kt.astype(BF16), vt.astype(BF16), jnp.concatenate(valid, axis=1))
        o_s = jnp.where(head_kvh == kvh, o_k, o_s)
        kt = jnp.concatenate([win[k_rows, :], kvwt[k_rows, :]], axis=1)
        vt = jnp.concatenate([win[v_rows, :], kvwt[v_rows, :]], axis=1)
        valid_w = jnp.concatenate([jnp.full((1, win.shape[1]), True), mine], axis=1)
        o_k = attend(kt.astype(BF16), vt.astype(BF16), valid_w)
        o_w = jnp.where(head_kvh == kvh, o_k, o_w)

    oc = oc_ref[0]
    o_c = jnp.where(head_kvh == 0, oc[:, :HEAD_DIM], oc[:, HEAD_DIM:])
    gates = jnp.sum(jnp.where(mine, gt_ref[...], 0.0), axis=1, keepdims=True)
    o_ref[0] = (gates[0:N_HEADS] * o_c + gates[N_HEADS:2 * N_HEADS] * o_s + gates[2 * N_HEADS:3 * N_HEADS] * o_w)

    new_col = jnp.sum(jnp.where(mine, kvwt, 0.0), axis=1, keepdims=True)
    last = lax.broadcasted_iota(jnp.int32, win.shape, 1) == win.shape[1] - 1
    nwin_ref[0] = jnp.where(last, new_col, pltpu.roll(win, win.shape[1] - 1, 1))


def _attn_sample(idx, page_table, cache_t, q, o_c, gt, kvst, kvwt, win_t, *, n_past_slc):
    nb = q.shape[0]
    sub = PAGE_SIZE // SLC_BLOCK
    win_buf = win_t.shape[2]
    n_pages = page_table.shape[1]

    def blk_spec(kvh, t):
        def index(i, idx_ref, pt_ref):
            j = jnp.minimum(idx_ref[(i * KV_HEADS + kvh) * SLC_TOPN + t], n_past_slc - 1)
            return (pt_ref[i * n_pages + j // sub], 0, kvh, 0, 0)
        return pl.BlockSpec((None, 2, None, HEAD_DIM, PAGE_SIZE), index)

    per_b = lambda shape: pl.BlockSpec((1,) + shape, lambda i, a, c: (i,) + (0,) * len(shape))
    const = lambda a: pl.BlockSpec(a.shape, lambda i, x, c: (0,) * a.ndim)
    n_sel = KV_HEADS * SLC_TOPN
    return pl.pallas_call(
        functools.partial(_attn_sample_kernel, n_past_slc=n_past_slc),
        grid_spec=pltpu.PrefetchScalarGridSpec(
            num_scalar_prefetch=2,
            grid=(nb,),
            in_specs=[blk_spec(k, t) for k in range(KV_HEADS) for t in range(SLC_TOPN)] + [
                per_b((N_HEADS, HEAD_DIM)), per_b((N_HEADS, D_KV)), const(gt), const(kvst), const(kvwt),
                per_b((ROW_W, win_buf))],
            out_specs=[per_b((N_HEADS, HEAD_DIM)), per_b((ROW_W, win_buf))],
        ),
        out_shape=[jax.ShapeDtypeStruct((nb, N_HEADS, HEAD_DIM), F32), jax.ShapeDtypeStruct(win_t.shape, F32)],
        compiler_params=_cparams("arbitrary"),
        name="attn_sample",
    )(idx.reshape(-1), page_table.reshape(-1), *([cache_t] * n_sel), q, o_c, gt, kvst, kvwt, win_t)


def _conv_tail(y, b_ref, lg_ref, lb_ref):
    y = y + b_ref[...]
    yc = y - jnp.mean(y, axis=-1, keepdims=True)
    yn = yc * lax.rsqrt(jnp.mean(yc * yc, axis=-1, keepdims=True) + EPS)
    return _silu(yn * lg_ref[...] + lb_ref[...])


def _conv_prompt_kernel(u_ref, halo_ref, w_ref, b_ref, lg_ref, lb_ref, o_ref, buf_ref, *, halo):
    tm = u_ref.shape[1]
    j = pl.program_id(1)
    buf_ref[0:halo, :] = jnp.where(j > 0, halo_ref[0], 0.0)
    buf_ref[halo:, :] = u_ref[0]
    lead = halo - (CONV_WIDTH - 1)
    y = buf_ref[lead:lead + tm, :] * w_ref[0:1, :]
    for w in range(1, CONV_WIDTH):
        y = y + buf_ref[lead + w:lead + w + tm, :] * w_ref[w:w + 1, :]
    o_ref[0] = _conv_tail(y, b_ref, lg_ref, lb_ref)


def _conv_prompt(u, w_dw, b_dw, ln_g, ln_b, *, tm):
    b, s, dc = u.shape
    halo = 32
    per = tm // halo
    const = lambda a: pl.BlockSpec(a.shape, lambda i, j: (0,) * a.ndim)
    return pl.pallas_call(
        functools.partial(_conv_prompt_kernel, halo=halo),
        grid=(b, s // tm),
        in_specs=[pl.BlockSpec((1, tm, dc), lambda i, j: (i, j, 0)),
                  pl.BlockSpec((1, halo, dc), lambda i, j: (i, jnp.maximum(j * per - 1, 0), 0)),
                  const(w_dw), const(b_dw), const(ln_g), const(ln_b)],
        out_specs=pl.BlockSpec((1, tm, dc), lambda i, j: (i, j, 0)),
        out_shape=jax.ShapeDtypeStruct((b, s, dc), F32),
        scratch_shapes=[pltpu.VMEM((tm + halo, dc), F32)],
        compiler_params=_cparams("arbitrary", "arbitrary"),
        name="conv_prompt",
    )(u, u, w_dw, b_dw, ln_g, ln_b)


def _conv_sample_kernel(st_ref, u_ref, w_ref, b_ref, lg_ref, lb_ref, o_ref):
    hist = CONV_WIDTH - 1
    y = u_ref[...] * w_ref[hist:hist + 1, :]
    for w in range(hist):
        y = y + st_ref[w] * w_ref[w:w + 1, :]
    o_ref[...] = _conv_tail(y, b_ref, lg_ref, lb_ref)


def _conv_sample(state_t, u, w_dw, b_dw, ln_g, ln_b, *, sb):
    nb, dc = u.shape
    hist = state_t.shape[0]
    const = lambda a: pl.BlockSpec(a.shape, lambda i: (0,) * a.ndim)
    return pl.pallas_call(
        _conv_sample_kernel,
        grid=(nb // sb,),
        in_specs=[pl.BlockSpec((hist, sb, dc), lambda i: (0, i, 0)), pl.BlockSpec((sb, dc), lambda i: (i, 0)),
                  const(w_dw), const(b_dw), const(ln_g), const(ln_b)],
        out_specs=pl.BlockSpec((sb, dc), lambda i: (i, 0)),
        out_shape=jax.ShapeDtypeStruct((nb, dc), F32),
        compiler_params=_cparams("arbitrary"),
        name="conv_sample",
    )(state_t, u, w_dw, b_dw, ln_g, ln_b)


def _finish_kernel(x_ref, oa_ref, ocv_ref, gate1_ref, shift2_ref, scale2_ref, ga_ref, gc_ref, wout_ref, g2_ref,
                   wr_ref, br_ref, x1_ref, h2_ref, comb_ref):
    mixed = jnp.concatenate([_rms(oa_ref[0], ga_ref[...]), _rms(ocv_ref[0], gc_ref[...])], axis=1)
    mix = _mm(mixed.astype(BF16), wout_ref[...])
    x1 = x_ref[0] + gate1_ref[0] * mix
    x1_ref[0] = x1
    h2 = _rms(x1, g2_ref[...]) * (1.0 + scale2_ref[0]) + shift2_ref[0]
    h2_ref[0] = h2.astype(BF16)

    h_hi, h_mid, _ = _split3(h2)
    w_hi, w_mid, _ = _split3(wr_ref[...])
    lg = (_mm(h_hi, w_hi) + _mm(h_hi, w_mid)
          + _mm(h_mid, w_hi)) + br_ref[...]
    lane = lax.broadcasted_iota(jnp.int32, lg.shape, 1)
    lane_f = lane.astype(F32)
    is_group = (lane >= N_EXPERTS) & (lane < N_EXPERTS + N_GROUPS)
    gl = jnp.where(is_group, lg, NEG_INF)
    g_max = jnp.max(gl, axis=1, keepdims=True)
    p_top = 1.0 / jnp.sum(jnp.where(is_group, jnp.exp(gl - g_max), 0.0), axis=1, keepdims=True)
    g_lane = jnp.min(jnp.where(gl == g_max, lane_f, 1e9), axis=1, keepdims=True)
    in_group = (lane < N_EXPERTS) & ((lane // EPG).astype(F32) == g_lane - N_EXPERTS)
    el = jnp.where(in_group, lg, NEG_INF)
    l1 = jnp.max(el, axis=1, keepdims=True)
    i1 = jnp.min(jnp.where(el == l1, lane_f, 1e9), axis=1, keepdims=True)
    el2 = jnp.where(lane_f == i1, NEG_INF, el)
    l2 = jnp.max(el2, axis=1, keepdims=True)
    i2 = jnp.min(jnp.where(el2 == l2, lane_f, 1e9), axis=1, keepdims=True)
    r = jnp.exp(l2 - l1)
    w1 = p_top / (1.0 + r)
    w2 = p_top * r / (1.0 + r)
    comb_ref[0] = jnp.where(lane_f == i1, w1, jnp.where(lane_f == i2, w2, 0.0))


def _finish(x, o_attn, o_conv, gate1, shift2, scale2, ga, gc, w_out, g2, w_route, b_route, *, tm):
    b, s, d = x.shape
    r = gate1.shape[1]
    mod_block = (1, 1, d) if r == 1 else (1, tm, d)
    mod_map = (lambda i, j: (i, 0, 0)) if r == 1 else (lambda i, j: (i, j, 0))
    mod = pl.BlockSpec(mod_block, mod_map)
    row = lambda w: pl.BlockSpec((1, tm, w), lambda i, j: (i, j, 0))
    const = lambda a: pl.BlockSpec(a.shape, lambda i, j: (0,) * a.ndim)
    return pl.pallas_call(
        _finish_kernel,
        grid=(b, s // tm),
        in_specs=[row(d), row(D_ATTN), row(D_CONV), mod, mod, mod, const(ga), const(gc), const(w_out), const(g2),
                  const(w_route), const(b_route)],
        out_specs=[row(d), row(d), row(128)],
        out_shape=[jax.ShapeDtypeStruct((b, s, d), F32), jax.ShapeDtypeStruct((b, s, d), BF16),
                   jax.ShapeDtypeStruct((b, s, 128), F32)],
        compiler_params=_cparams("arbitrary", "arbitrary"),
        name="finish",
    )(x, o_attn, o_conv, gate1, shift2, scale2, ga, gc, w_out, g2, w_route, b_route)


def _moe_kernel(h_ref, comb_ref, x1_ref, gate2_ref, wg_ref, wu_ref, wd_ref, gf_ref, y_ref, acc_ref):
    e = pl.program_id(2)

    @pl.when(e == 0)
    def _():
        acc_ref[...] = jnp.zeros_like(acc_ref)

    h = h_ref[0]
    a = _mm(h, wg_ref[0].astype(BF16))
    u = _mm(h, wu_ref[0].astype(BF16))
    comb = comb_ref[0]
    lane = lax.broadcasted_iota(jnp.int32, comb.shape, 1)
    cw = jnp.sum(jnp.where(lane == e, comb, 0.0), axis=1, keepdims=True)
    hid = (_silu(a) * u * cw).astype(BF16)
    acc_ref[...] += _mm(hid, wd_ref[0].astype(BF16))

    @pl.when(e == pl.num_programs(2) - 1)
    def _():
        y = x1_ref[0] + gate2_ref[0] * acc_ref[...]
        y_ref[0] = _rms(y, gf_ref[...])


def _moe(h2, comb, x1, gate2, w_gate, w_up, w_down, final_g, *, tm):
    b, s, d = x1.shape
    r = gate2.shape[1]
    mod_block = (1, 1, d) if r == 1 else (1, tm, d)
    mod_map = (lambda i, j, e: (i, 0, 0)) if r == 1 else (lambda i, j, e: (i, j, 0))
    row = lambda w: pl.BlockSpec((1, tm, w), lambda i, j, e: (i, j, 0))
    ne, _, de = w_gate.shape
    return pl.pallas_call(
        _moe_kernel,
        grid=(b, s // tm, ne),
        in_specs=[row(d), row(128), row(d), pl.BlockSpec(mod_block, mod_map),
                  pl.BlockSpec((1, d, de), lambda i, j, e: (e, 0, 0)),
                  pl.BlockSpec((1, d, de), lambda i, j, e: (e, 0, 0)),
                  pl.BlockSpec((1, de, d), lambda i, j, e: (e, 0, 0)),
                  pl.BlockSpec((1, d), lambda i, j, e: (0, 0))],
        out_specs=row(d),
        out_shape=jax.ShapeDtypeStruct((b, s, d), F32),
        scratch_shapes=[pltpu.VMEM((tm, d), F32)],
        compiler_params=_cparams("arbitrary", "arbitrary", "arbitrary"),
        name="moe",
    )(h2, comb, x1, gate2, w_gate, w_up, w_down, final_g)


def _prep_w_in(w_in):
    o_gl = D_ATTN + 3 * ROW_W
    wqkv = w_in[:, :o_gl].astype(BF16)
    wgl = w_in[:, o_gl:o_gl + 3 * N_HEADS]
    pad = jnp.zeros((w_in.shape[0], 128 - 3 * N_HEADS), w_in.dtype)
    wgl = jnp.concatenate([wgl.reshape(-1, N_HEADS, 3).transpose(0, 2, 1).reshape(-1, 3 * N_HEADS), pad], axis=1)
    wu = w_in[:, o_gl + 3 * N_HEADS:].astype(BF16)
    return wqkv, wgl.astype(BF16), wu


def _prep_compress(w_cmp1, pos_cmp, w_cmp2):
    ratio = CMP_BLOCK // CMP_STRIDE
    eye = jnp.eye(KV_HEADS, dtype=w_cmp1.dtype)
    w1 = w_cmp1.reshape(2, ratio, CMP_STRIDE, HEAD_DIM, HEAD_DIM)
    w1big = jnp.einsum('crsdf,kj->cskdrjf', w1, eye).reshape(2, CMP_STRIDE * D_KV, ratio * D_KV).astype(BF16)
    w2big = jnp.einsum('cfd,kj->ckfjd', w_cmp2, eye).reshape(2, D_KV, D_KV).astype(BF16)
    pos = pos_cmp.reshape(2, ratio, CMP_STRIDE, 1, HEAD_DIM)
    pos = jnp.broadcast_to(pos, (2, ratio, CMP_STRIDE, KV_HEADS, HEAD_DIM)).reshape(2, ratio, CMP_STRIDE * D_KV)
    posrows = jnp.concatenate([pos, jnp.zeros((2, 8 - ratio, CMP_STRIDE * D_KV), pos.dtype)], axis=1).astype(BF16)
    return w1big, w2big, posrows


def _prep_router(w_group, b_group, w_router, b_router):
    d = w_group.shape[0]
    pad = 128 - N_EXPERTS - N_GROUPS
    w = jnp.concatenate([w_router, w_group, jnp.zeros((d, pad), w_group.dtype)], axis=1)
    b = jnp.concatenate([b_router, b_group, jnp.zeros((pad,), b_group.dtype)]).reshape(1, 128)
    return w, b


def kernel(x_prompt, x_sample, cache_cmp_kv, cache_slc_kv, state_win_kv, state_conv, page_table, c_prompt, c_sample,
           norm1_g, w_ada, b_ada, w_in, w_cmp1, pos_cmp, w_cmp2, w_dw, b_dw, conv_ln_g, conv_ln_b, g_attn_out,
           g_conv_out, w_out, norm2_g, w_group, b_group, w_router, b_router, w_gate, w_up, w_down, final_g):
    depth = norm1_g.shape[0]
    assert depth == 1 and x_sample.shape[1] == 1
    bp, seq, d = x_prompt.shape
    nb = x_sample.shape[0]
    n_pages = page_table.shape[1]
    past_len = n_pages * PAGE_SIZE
    n_past_slc = past_len // SLC_BLOCK
    win_buf = state_win_kv.shape[2]
    assert win_buf == WINDOW and seq % K_TILE == 0 and seq >= WINDOW + Q_TILE
    l = 0
    row2 = lambda a: a.reshape(1, -1)

    mods = _adaln(jnp.concatenate([c_prompt, c_sample], axis=0), w_ada[l], b_ada[l]).reshape(bp + nb, 6, d)
    mods_p = [mods[:bp, i][:, None, :] for i in range(6)]
    mods_s = [mods[bp:, i][None, :, :] for i in range(6)]

    wqkv, wgl, wu = _prep_w_in(w_in[l])
    w1big, w2big, posrows = _prep_compress(w_cmp1[l], pos_cmp[l], w_cmp2[l])
    w_route, b_route = _prep_router(w_group[l], b_group[l], w_router[l], b_router[l])
    w_out_b = w_out[l].astype(BF16)
    conv_args = (w_dw[l], row2(b_dw[l]), row2(conv_ln_g[l]), row2(conv_ln_b[l]))

    def kv_rows_from_t(a_t):
        n, _, t = a_t.shape
        return a_t.reshape(n, 2, KV_HEADS, HEAD_DIM, t).transpose(0, 4, 1, 2, 3)

    def kv_rows_to_t(a):
        return a.transpose(0, 2, 3, 4, 1)

    qt, kvc, kvct, kvst, kvwt, ks, vts, kw, vtw, gt, u = _mixer(
        x_prompt, mods_p[0], mods_p[1], row2(norm1_g[l]), wqkv, wgl, wu, tm=512, prompt=True)
    kc, vct = _compress_prompt(kvc, w1big, w2big, posrows)
    o_attn = _attn_prompt(qt, kc, vct, ks, vts, kw, vtw, gt)
    o_conv = _conv_prompt(u, *conv_args, tm=512)
    x1, h2, comb = _finish(x_prompt, o_attn, o_conv, mods_p[2], mods_p[3], mods_p[4], row2(g_attn_out[l]),
                           row2(g_conv_out[l]), w_out_b, row2(norm2_g[l]), w_route, b_route, tm=512)
    y_prompt = _moe(h2, comb, x1, mods_p[5], w_gate[l], w_up[l], w_down[l], row2(final_g), tm=1024)

    new_cmp_prompt = kv_rows_from_t(kvct)[None]
    new_slc_prompt = kv_rows_from_t(kvst)[None]
    new_win_prompt = kv_rows_from_t(kvwt[:, :, seq - WINDOW:])[None]
    new_conv_prompt = u[:, seq - (CONV_WIDTH - 1):][None]

    xs = x_sample.reshape(1, nb, d)
    q_s, kvct_s, kvst_s, kvwt_s, gt_s, u_s = _mixer(
        xs, mods_s[0], mods_s[1], row2(norm1_g[l]), wqkv, wgl, wu, tm=nb, prompt=False)
    cmp_t = kv_rows_to_t(cache_cmp_kv[l])
    kc_s, vc_s = _compress_sample(cmp_t.reshape(cmp_t.shape[0], ROW_W, PAGE_SIZE), page_table, w1big, w2big, posrows)

    q4 = q_s.reshape(nb, KV_HEADS, GQA, HEAD_DIM)
    zq = jnp.zeros_like(q4)
    kvh_id = jnp.arange(KV_HEADS).reshape(1, KV_HEADS, 1, 1)
    qpad = jnp.concatenate([jnp.where(kvh_id == 0, q4, zq), jnp.where(kvh_id == 1, q4, zq)],
                           axis=-1).reshape(nb, N_HEADS, D_KV)
    o_c, idx = _attn_sample_cmp(qpad, kc_s, vc_s, q_pos=past_len, n_slc=n_past_slc + 1)
    idx = idx[:, :KV_HEADS, :SLC_TOPN]
    win_t = kv_rows_to_t(state_win_kv[l]).reshape(nb, ROW_W, win_buf)
    o_heads, new_win_t = _attn_sample(idx, page_table, kv_rows_to_t(cache_slc_kv[l]), q_s.reshape(nb, N_HEADS, HEAD_DIM),
                                      o_c, gt_s[0], kvst_s[0], kvwt_s[0], win_t, n_past_slc=n_past_slc)
    o_attn_s = o_heads.reshape(1, nb, D_ATTN)
    state_t = state_conv[l].transpose(1, 0, 2)
    u_rows = u_s.reshape(nb, D_CONV)
    o_conv_s = _conv_sample(state_t, u_rows, *conv_args, sb=8).reshape(1, nb, D_CONV)
    x1_s, h2_s, comb_s = _finish(xs, o_attn_s, o_conv_s, mods_s[2], mods_s[3], mods_s[4], row2(g_attn_out[l]),
                                 row2(g_conv_out[l]), w_out_b, row2(norm2_g[l]), w_route, b_route, tm=nb)
    y_sample = _moe(h2_s, comb_s, x1_s, mods_s[5], w_gate[l], w_up[l], w_down[l], row2(final_g), tm=nb)

    row_shape = (1, nb, 1, 2, KV_HEADS, HEAD_DIM)
    new_cmp_sample = kvct_s[0].T.reshape(row_shape)
    new_slc_sample = kvst_s[0].T.reshape(row_shape)
    new_win_sample = kv_rows_from_t(new_win_t)[None]
    new_conv_sample = jnp.concatenate([state_t[1:], u_rows[None]], axis=0).transpose(1, 0, 2)[None]

    return (y_prompt, y_sample.reshape(nb, 1, d), new_cmp_prompt, new_slc_prompt, new_win_prompt, new_conv_prompt,
            new_cmp_sample, new_slc_sample, new_win_sample, new_conv_sample)
```

```python
import functools

import jax
import jax.numpy as jnp
from jax import lax
from jax.experimental import pallas as pl
from jax.experimental.pallas import tpu as pltpu

F32 = jnp.float32
BF16 = jnp.bfloat16

D_MODEL = 1024
N_HEADS = 8
HEAD_DIM = 64
KV_HEADS = 2
GQA = N_HEADS // KV_HEADS
D_ATTN = N_HEADS * HEAD_DIM
D_CONV = D_MODEL - D_ATTN
D_KV = KV_HEADS * HEAD_DIM
CMP_BLOCK = 32
CMP_STRIDE = 16
SLC_BLOCK = 64
SLC_TOPN = 16
N_LOCAL_BLOCKS = 2
WINDOW = 512
FORCED_SCORE = 1e4
CONV_WIDTH = 31
N_GROUPS = 4
EPG = 8
N_EXPERTS = N_GROUPS * EPG
D_EXPERT = 256
PAGE_SIZE = 128
EPS = 1e-6
NEG_INF = -1e30
SCALE = HEAD_DIM ** -0.5
LOG2E = 1.4426950408889634
ROW_W = 2 * D_KV
CHUNK_W = CMP_STRIDE * ROW_W
CHUNK_PITCH = 20
MOE_TILE = 256
Q_TILE = 128
K_TILE = 512
VMEM_LIMIT = 48 * 1024 * 1024


def _cparams(*sem):
    return pltpu.CompilerParams(dimension_semantics=sem, vmem_limit_bytes=VMEM_LIMIT)


def _rms(x, g):
    return x * lax.rsqrt(jnp.mean(x * x, axis=-1, keepdims=True) + EPS) * g


def _silu(x):
    return x * jax.nn.sigmoid(x)


def _mm(a, b):
    return jnp.dot(a, b, preferred_element_type=F32)


def _mm_nt(a, b):
    return lax.dot_general(a, b, (((1,), (1,)), ((), ())), preferred_element_type=F32)


def _split3(x):
    hi = x.astype(BF16)
    r = x - hi.astype(F32)
    mid = r.astype(BF16)
    lo = (r - mid.astype(F32)).astype(BF16)
    return hi, mid, lo


def _softmax_masked(s, mask, axis):
    s = jnp.where(mask, s, NEG_INF)
    m = jnp.max(s, axis=axis, keepdims=True)
    e = jnp.exp(s - m)
    p = e / jnp.sum(e, axis=axis, keepdims=True)
    return jnp.where(mask, p, 0.0)


def _adaln_kernel(c_ref, w_ref, b_ref, o_ref):
    s = _silu(c_ref[...]).astype(BF16)
    o_ref[...] = _mm(s, w_ref[...].astype(BF16)) + b_ref[...]


def _adaln(c_all, w_ada, b_ada):
    n, d = c_all.shape
    nout = w_ada.shape[1]
    tn = 1024
    return pl.pallas_call(
        _adaln_kernel,
        grid=(nout // tn,),
        in_specs=[pl.BlockSpec((n, d), lambda j: (0, 0)),
                  pl.BlockSpec((d, tn), lambda j: (0, j)),
                  pl.BlockSpec((1, tn), lambda j: (0, j))],
        out_specs=pl.BlockSpec((n, tn), lambda j: (0, j)),
        out_shape=jax.ShapeDtypeStruct((n, nout), F32),
        compiler_params=_cparams("arbitrary"),
        name="adaln",
    )(c_all, w_ada, b_ada.reshape(1, nout))


def _mixer_kernel(x_ref, shift_ref, scale_ref, g_ref, wqkv_ref, wgl_ref, wu_ref, *outs, prompt):
    x = x_ref[0]
    h = _rms(x, g_ref[...]) * (1.0 + scale_ref[0]) + shift_ref[0]
    hb = h.astype(BF16)
    p = _mm(hb, wqkv_ref[...])
    gl = _mm(hb, wgl_ref[...])
    pu = _mm(hb, wu_ref[...])
    u = pu[:, :D_CONV] * jax.nn.sigmoid(pu[:, D_CONV:])
    q = p[:, :D_ATTN] * (SCALE * LOG2E if prompt else SCALE)
    o = D_ATTN
    kvc = p[:, o:o + ROW_W]
    kvs = p[:, o + ROW_W:o + 2 * ROW_W]
    kvw = p[:, o + 2 * ROW_W:o + 3 * ROW_W]
    kvs_t = kvs.T
    kvw_t = kvw.T
    if prompt:
        q_ref, kvc_ref, kvct_ref, kvst_ref, kvwt_ref, ks_ref, vts_ref, kw_ref, vtw_ref, gt_ref, u_ref = outs
        q_ref[0] = q.T.astype(BF16)
        kvc_ref[0] = kvc
        ks_ref[0] = kvs[:, :D_KV].astype(BF16)
        vts_ref[0] = kvs_t[D_KV:, :].astype(BF16)
        kw_ref[0] = kvw[:, :D_KV].astype(BF16)
        vtw_ref[0] = kvw_t[D_KV:, :].astype(BF16)
    else:
        q_ref, kvct_ref, kvst_ref, kvwt_ref, gt_ref, u_ref = outs
        q_ref[0] = q
    kvct_ref[0] = kvc.T
    kvst_ref[0] = kvs_t
    kvwt_ref[0] = kvw_t
    gt_ref[0] = jax.nn.sigmoid(gl).T[:32, :]
    u_ref[0] = u


def _mixer(x, shift, scale, g, wqkv, wgl, wu, *, tm, prompt):
    b, s, d = x.shape
    r = shift.shape[1]
    mod_block = (1, 1, d) if r == 1 else (1, tm, d)
    mod_map = (lambda i, j: (i, 0, 0)) if r == 1 else (lambda i, j: (i, j, 0))
    row = lambda w: pl.BlockSpec((1, tm, w), lambda i, j: (i, j, 0))
    col = lambda w: pl.BlockSpec((1, w, tm), lambda i, j: (i, 0, j))
    const = lambda a: pl.BlockSpec(a.shape, lambda i, j: (0,) * a.ndim)
    rows = lambda w, dt: jax.ShapeDtypeStruct((b, s, w), dt)
    cols = lambda w, dt: jax.ShapeDtypeStruct((b, w, s), dt)
    if prompt:
        out_specs = [col(D_ATTN), row(ROW_W), col(ROW_W), col(ROW_W), col(ROW_W), row(D_KV), col(D_KV), row(D_KV),
                     col(D_KV), col(32), row(D_CONV)]
        out_shape = [cols(D_ATTN, BF16), rows(ROW_W, F32), cols(ROW_W, F32), cols(ROW_W, F32), cols(ROW_W, F32),
                     rows(D_KV, BF16), cols(D_KV, BF16), rows(D_KV, BF16), cols(D_KV, BF16), cols(32, F32),
                     rows(D_CONV, F32)]
    else:
        out_specs = [row(D_ATTN), col(ROW_W), col(ROW_W), col(ROW_W), col(32), row(D_CONV)]
        out_shape = [rows(D_ATTN, F32), cols(ROW_W, F32), cols(ROW_W, F32), cols(ROW_W, F32), cols(32, F32),
                     rows(D_CONV, F32)]
    return pl.pallas_call(
        functools.partial(_mixer_kernel, prompt=prompt),
        grid=(b, s // tm),
        in_specs=[row(d), pl.BlockSpec(mod_block, mod_map), pl.BlockSpec(mod_block, mod_map),
                  const(g), const(wqkv), const(wgl), const(wu)],
        out_specs=out_specs,
        out_shape=out_shape,
        compiler_params=_cparams("arbitrary", "arbitrary"),
        name="mixer_prompt" if prompt else "mixer_sample",
    )(x, shift, scale, g, wqkv, wgl, wu)


def _chunk_part(load_offset, w1_c):
    xc = jnp.concatenate([load_offset(s) for s in range(CMP_STRIDE)], axis=1)
    return _mm(xc.astype(BF16), w1_c)


def _compress_finish(part, posb, w2_c):
    n = part.shape[0]
    nxt = pltpu.roll(part[:, D_KV:], n - 1, 0)
    pre = part[:, :D_KV] + nxt + posb[0:1, :D_KV] + posb[1:2, D_KV:]
    return _mm(_silu(pre).astype(BF16), w2_c)


def _compress_prompt_kernel(x_ref, w1_ref, w2_ref, pos_ref, kc_ref, vct_ref):
    for c in range(2):
        part = _chunk_part(lambda s: x_ref[0, :, s * ROW_W + c * D_KV:s * ROW_W + (c + 1) * D_KV], w1_ref[c])
        posb = _mm(pos_ref[c], w1_ref[c])
        out = _compress_finish(part, posb, w2_ref[c])
        if c == 0:
            kc_ref[0] = out.astype(BF16)
        else:
            vct_ref[0] = out.T.astype(BF16)


def _compress_prompt(kvc, w1big, w2big, posrows):
    b, s, _ = kvc.shape
    n = s // CMP_STRIDE
    x = kvc.reshape(b, n, CHUNK_W)
    const = lambda a: pl.BlockSpec(a.shape, lambda i: (0,) * a.ndim)
    return pl.pallas_call(
        _compress_prompt_kernel,
        grid=(b,),
        in_specs=[pl.BlockSpec((1, n, CHUNK_W), lambda i: (i, 0, 0)), const(w1big), const(w2big), const(posrows)],
        out_specs=[pl.BlockSpec((1, n, D_KV), lambda i: (i, 0, 0)), pl.BlockSpec((1, D_KV, n), lambda i: (i, 0, 0))],
        out_shape=[jax.ShapeDtypeStruct((b, n, D_KV), BF16), jax.ShapeDtypeStruct((b, D_KV, n), BF16)],
        compiler_params=_cparams("arbitrary"),
        name="compress_prompt",
    )(x, w1big, w2big, posrows)


def _compress_sample_kernel(pt_ref, *refs, pages):
    x_refs = refs[:pages]
    w1_ref, w2_ref, pos_ref, kc_ref, vc_ref = refs[pages:pages + 5]
    row_refs = refs[pages + 5:]
    cpp = PAGE_SIZE // CMP_STRIDE
    n = pages * cpp
    for c in range(2):
        for k in range(pages):
            rows = x_refs[k][0, c * D_KV:(c + 1) * D_KV, :].T
            for j in range(cpp):
                r0 = (k * cpp + j) * CHUNK_PITCH
                row_refs[c][r0:r0 + CMP_STRIDE, :] = rows[j * CMP_STRIDE:(j + 1) * CMP_STRIDE, :]
        part = _chunk_part(lambda s: row_refs[c][pl.ds(s, n, stride=CHUNK_PITCH), :], w1_ref[c])
        posb = _mm(pos_ref[c], w1_ref[c])
        out = _compress_finish(part, posb, w2_ref[c])
        (kc_ref if c == 0 else vc_ref)[0] = out.astype(BF16)


def _compress_sample(cache_t, page_table, w1big, w2big, posrows):
    nb, n_pages = page_table.shape
    n = n_pages * (PAGE_SIZE // CMP_STRIDE)
    page_spec = lambda k: pl.BlockSpec((1, ROW_W, PAGE_SIZE), lambda i, pt: (pt[i * n_pages + k], 0, 0))
    const = lambda a: pl.BlockSpec(a.shape, lambda i, pt: (0,) * a.ndim)
    return pl.pallas_call(
        functools.partial(_compress_sample_kernel, pages=n_pages),
        grid_spec=pltpu.PrefetchScalarGridSpec(
            num_scalar_prefetch=1,
            grid=(nb,),
            in_specs=[page_spec(k) for k in range(n_pages)] + [const(w1big), const(w2big), const(posrows)],
            out_specs=[pl.BlockSpec((1, n, D_KV), lambda i, pt: (i, 0, 0))] * 2,
            scratch_shapes=[pltpu.VMEM((n * CHUNK_PITCH, D_KV), F32)] * 2,
        ),
        out_shape=[jax.ShapeDtypeStruct((nb, n, D_KV), BF16)] * 2,
        compiler_params=_cparams("arbitrary"),
        name="compress_sample",
    )(page_table.reshape(-1), *([cache_t] * n_pages), w1big, w2big, posrows)


def _overlap(cmp_idx, slc_idx):
    lo = cmp_idx * CMP_STRIDE
    so = slc_idx * SLC_BLOCK
    return (lo <= so + SLC_BLOCK - 1) & (lo + CMP_BLOCK - 1 >= so)


def _forced_importance(imp, blk, cur, n_slc):
    valid = (blk <= cur) & (blk < n_slc)
    forced = (blk == 0) | ((cur - blk >= 0) & (cur - blk < N_LOCAL_BLOCKS))
    return jnp.where(valid & forced, FORCED_SCORE, jnp.where(valid, imp, -1.0))


def _attn_prompt_kernel(qt_ref, kc_ref, vct_ref, ks_ref, vts_ref, kw_ref, vtw_ref, gt_ref, o_ref, sel_ref):
    i = pl.program_id(1)
    seq = ks_ref.shape[1]
    nbp = kc_ref.shape[1]
    n_slc = seq // SLC_BLOCK
    lanes = GQA * Q_TILE
    lane = lax.broadcasted_iota(jnp.int32, (1, lanes), 1)
    qpos = i * Q_TILE + (lane & (Q_TILE - 1))
    qpos_q = qpos[:, :Q_TILE]
    qt = qt_ref[0]
    gt = gt_ref[0]
    win_keys = min(WINDOW + Q_TILE, seq)
    blocks_per_tile = K_TILE // SLC_BLOCK

    ovl = _overlap(lax.broadcasted_iota(jnp.int32, (n_slc, nbp), 1), lax.broadcasted_iota(jnp.int32, (n_slc, nbp), 0))
    ovl = jnp.where(ovl, 1.0, 0.0).astype(BF16)
    blk = lax.broadcasted_iota(jnp.int32, (n_slc, Q_TILE), 0)

    cpos = lax.broadcasted_iota(jnp.int32, (nbp, lanes), 0) * CMP_STRIDE + (CMP_BLOCK - 1)
    cmp_bias = jnp.where(cpos <= qpos, 0.0, NEG_INF)
    any_cmp = jnp.where(qpos >= CMP_BLOCK - 1, 1.0, 0.0)
    w0 = pl.multiple_of(jnp.clip(i * Q_TILE - WINDOW, 0, seq - win_keys), Q_TILE)
    wpos = w0 + lax.broadcasted_iota(jnp.int32, (win_keys, lanes), 0)
    win_bias = jnp.where(wpos <= qpos, jnp.where(wpos >= qpos - WINDOW, 0.0, NEG_INF), NEG_INF)
    n_steps = ((i + 1) * Q_TILE + K_TILE - 1) // K_TILE
    last0 = pl.multiple_of((n_steps - 1) * K_TILE, K_TILE)
    causal_bias = jnp.where(last0 + lax.broadcasted_iota(jnp.int32, (K_TILE, lanes), 0) <= qpos, 0.0, NEG_INF)

    heads = range(KV_HEADS)
    hd = [slice(kvh * HEAD_DIM, (kvh + 1) * HEAD_DIM) for kvh in heads]
    qpad, o_c = [], []
    for kvh in heads:
        qk = jnp.concatenate([qt[(kvh * GQA + g) * HEAD_DIM:(kvh * GQA + g + 1) * HEAD_DIM, :] for g in range(GQA)],
                             axis=1)
        zero = jnp.zeros_like(qk)
        qpad.append(jnp.concatenate([qk, zero] if kvh == 0 else [zero, qk], axis=0))

        s = _mm(kc_ref[0], qpad[kvh]) + cmp_bias
        e = jnp.exp2(s - jnp.max(s, axis=0, keepdims=True))
        p = e * (any_cmp / jnp.sum(e, axis=0, keepdims=True))
        o_c.append(_mm(vct_ref[0, hd[kvh], :], p.astype(BF16)))

        psum = p[:, 0:Q_TILE]
        for g in range(1, GQA):
            psum = psum + p[:, g * Q_TILE:(g + 1) * Q_TILE]
        imp = sum(_mm(ovl, t) for t in _split3(psum))
        impf = _forced_importance(imp, blk, qpos_q // SLC_BLOCK, n_slc)
        rank = jnp.zeros_like(impf)
        for k in range(n_slc):
            rk = impf[k:k + 1, :]
            rank = rank + jnp.where(rk > impf, 1.0, 0.0) + jnp.where(rk == impf, jnp.where(blk > k, 1.0, 0.0), 0.0)
        sel = jnp.where(rank < SLC_TOPN, jnp.where(impf >= 0.0, 0.0, NEG_INF), NEG_INF)
        sel_ref[kvh] = jnp.concatenate([sel] * GQA, axis=1)

    def slc_step(t, carry, extra_bias=None):
        k0 = pl.multiple_of(t * K_TILE, K_TILE)
        keys = ks_ref[0, pl.ds(k0, K_TILE), :]
        out = []
        for kvh in heads:
            m, l, acc = carry[kvh]
            sk = _mm(keys, qpad[kvh])
            sk = jnp.concatenate(
                [sk[j * SLC_BLOCK:(j + 1) * SLC_BLOCK, :] + sel_ref[kvh, pl.ds(t * blocks_per_tile + j, 1), :]
                 for j in range(blocks_per_tile)], axis=0)
            if extra_bias is not None:
                sk = sk + extra_bias
            m_new = jnp.maximum(m, jnp.max(sk, axis=0, keepdims=True))
            alpha = jnp.exp2(m - m_new)
            e = jnp.exp2(sk - m_new)
            l_new = alpha * l + jnp.sum(e, axis=0, keepdims=True)
            pv = _mm(vts_ref[0, hd[kvh], pl.ds(k0, K_TILE)], e.astype(BF16))
            out.append((m_new, l_new, alpha * acc + pv))
        return tuple(out)

    init = (jnp.full((1, lanes), NEG_INF, F32), jnp.zeros((1, lanes), F32), jnp.zeros((HEAD_DIM, lanes), F32))
    carry = lax.fori_loop(0, n_steps - 1, slc_step, (init,) * KV_HEADS)
    carry = slc_step(n_steps - 1, carry, causal_bias)

    for kvh in heads:
        _, l_s, acc_s = carry[kvh]
        o_s = acc_s * (1.0 / l_s)

        sw = _mm(kw_ref[0, pl.ds(w0, win_keys), :], qpad[kvh]) + win_bias
        ew = jnp.exp2(sw - jnp.max(sw, axis=0, keepdims=True))
        o_w = _mm(vtw_ref[0, hd[kvh], pl.ds(w0, win_keys)], ew.astype(BF16)) * (
            1.0 / jnp.sum(ew, axis=0, keepdims=True))

        def gate(r):
            return jnp.concatenate([gt[r * N_HEADS + kvh * GQA + g:r * N_HEADS + kvh * GQA + g + 1, :]
                                    for g in range(GQA)], axis=1)
        o_t = gate(0) * o_c[kvh] + gate(1) * o_s + gate(2) * o_w
        for pair in range(GQA // 2):
            two = jnp.concatenate([o_t[:, (2 * pair) * Q_TILE:(2 * pair + 1) * Q_TILE],
                                   o_t[:, (2 * pair + 1) * Q_TILE:(2 * pair + 2) * Q_TILE]], axis=0)
            c0 = kvh * GQA * HEAD_DIM + pair * 2 * HEAD_DIM
            o_ref[0, :, c0:c0 + 2 * HEAD_DIM] = two.T


def _attn_prompt(qt, kc, vct, ks, vts, kw, vtw, gt):
    b, _, s = qt.shape
    nbp = kc.shape[1]
    per_b = lambda shape: pl.BlockSpec((1,) + shape, lambda i, j: (i, 0, 0))
    return pl.pallas_call(
        _attn_prompt_kernel,
        grid=(b, s // Q_TILE),
        in_specs=[pl.BlockSpec((1, D_ATTN, Q_TILE), lambda i, j: (i, 0, j)),
                  per_b((nbp, D_KV)), per_b((D_KV, nbp)),
                  per_b((s, D_KV)), per_b((D_KV, s)), per_b((s, D_KV)), per_b((D_KV, s)),
                  pl.BlockSpec((1, 32, Q_TILE), lambda i, j: (i, 0, j))],
        out_specs=pl.BlockSpec((1, Q_TILE, D_ATTN), lambda i, j: (i, j, 0)),
        out_shape=jax.ShapeDtypeStruct((b, s, D_ATTN), F32),
        scratch_shapes=[pltpu.VMEM((KV_HEADS, s // SLC_BLOCK, GQA * Q_TILE), F32)],
        compiler_params=_cparams("arbitrary", "arbitrary"),
        name="attn_prompt",
    )(qt, kc, vct, ks, vts, kw, vtw, gt)


def _attn_sample_cmp_kernel(q_ref, kc_ref, vc_ref, oc_ref, idx_ref, *, q_pos, n_slc):
    q = q_ref[0].astype(BF16)
    nb = kc_ref.shape[1]
    s = _mm_nt(q, kc_ref[0])
    cpos = lax.broadcasted_iota(jnp.int32, s.shape, 1) * CMP_STRIDE + (CMP_BLOCK - 1)
    p = _softmax_masked(s, cpos <= q_pos, 1)
    oc_ref[0] = _mm(p.astype(BF16), vc_ref[0])

    nsp = idx_ref.shape[2] * 2
    group_sums = [jnp.sum(p[k * GQA:(k + 1) * GQA, :], axis=0, keepdims=True) for k in range(KV_HEADS)]
    psum = jnp.concatenate(group_sums + [jnp.zeros((N_HEADS - KV_HEADS, nb), F32)], axis=0)
    ovl = _overlap(lax.broadcasted_iota(jnp.int32, (nb, nsp), 0), lax.broadcasted_iota(jnp.int32, (nb, nsp), 1))
    ovl = jnp.where(ovl, 1.0, 0.0).astype(BF16)
    imp = sum(_mm(t, ovl) for t in _split3(psum))
    blk = lax.broadcasted_iota(jnp.int32, imp.shape, 1)
    impf = _forced_importance(imp, blk, q_pos // SLC_BLOCK, n_slc)
    rank = jnp.zeros_like(impf)
    for k in range(n_slc):
        ck = impf[:, k:k + 1]
        rank = rank + jnp.where(ck > impf, 1.0, 0.0) + jnp.where(ck == impf, jnp.where(blk > k, 1.0, 0.0), 0.0)
    blk_f = blk.astype(F32)
    slot = lax.broadcasted_iota(jnp.int32, (N_HEADS, idx_ref.shape[2]), 1)
    idx = jnp.zeros((N_HEADS, idx_ref.shape[2]), F32)
    for t in range(SLC_TOPN):
        chosen = jnp.sum(jnp.where(rank == float(t), blk_f, 0.0), axis=1, keepdims=True)
        idx = idx + jnp.where(slot == t, chosen, 0.0)
    idx_ref[0] = idx.astype(jnp.int32)


def _attn_sample_cmp(qpad, kc, vc, *, q_pos, n_slc):
    nb = qpad.shape[0]
    n = kc.shape[1]
    per_b = lambda shape: pl.BlockSpec((1,) + shape, lambda i: (i, 0, 0))
    return pl.pallas_call(
        functools.partial(_attn_sample_cmp_kernel, q_pos=q_pos, n_slc=n_slc),
        grid=(nb,),
        in_specs=[per_b((N_HEADS, D_KV)), per_b((n, D_KV)), per_b((n, D_KV))],
        out_specs=[per_b((N_HEADS, D_KV)), per_b((N_HEADS, 128))],
        out_shape=[jax.ShapeDtypeStruct((nb, N_HEADS, D_KV), F32), jax.ShapeDtypeStruct((nb, N_HEADS, 128), jnp.int32)],
        compiler_params=_cparams("arbitrary"),
        name="attn_sample_cmp",
    )(qpad, kc, vc)


def _attn_sample_kernel(idx_ref, pt_ref, *refs, n_past_slc):
    n_sel = KV_HEADS * SLC_TOPN
    blk_refs = refs[:n_sel]
    q_ref, oc_ref, gt_ref, kvst_ref, kvwt_ref, win_ref, o_ref, nwin_ref = refs[n_sel:]
    b = pl.program_id(0)
    nb = kvst_ref.shape[1]
    q = q_ref[0].astype(BF16)
    head_kvh = lax.broadcasted_iota(jnp.int32, (N_HEADS, 1), 0) // GQA
    mine = lax.broadcasted_iota(jnp.int32, (1, nb), 1) == b
    lane_half = lax.broadcasted_iota(jnp.int32, (1, PAGE_SIZE), 1) // SLC_BLOCK

    def attend(kt, vt, valid):
        s = jnp.where(valid, _mm(q, kt), NEG_INF)
        e = jnp.where(valid, jnp.exp(s - jnp.max(s, axis=1, keepdims=True)), 0.0)
        return _mm_nt(e.astype(BF16), vt) / jnp.sum(e, axis=1, keepdims=True)

    kvst = kvst_ref[...]
    kvwt = kvwt_ref[...]
    o_s = jnp.zeros((N_HEADS, HEAD_DIM), F32)
    o_w = jnp.zeros((N_HEADS, HEAD_DIM), F32)
    win = win_ref[0]
    for kvh in range(KV_HEADS):
        k_rows = slice(kvh * HEAD_DIM, (kvh + 1) * HEAD_DIM)
        v_rows = slice(D_KV + kvh * HEAD_DIM, D_KV + (kvh + 1) * HEAD_DIM)
        valid = []
        n_new = jnp.int32(0)
        for t in range(SLC_TOPN):
            j = idx_ref[(b * KV_HEADS + kvh) * SLC_TOPN + t]
            past = j < n_past_slc
            half = jnp.minimum(j, n_past_slc - 1) % (PAGE_SIZE // SLC_BLOCK)
            valid.append((lane_half == half) & past)
            n_new = n_new + jnp.where(past, 0, 1)
        valid.append(mine & (n_new > 0))
        kt = jnp.concatenate([blk_refs[kvh * SLC_TOPN + t][0] for t in range(SLC_TOPN)] + [kvst[k_rows, :]], axis=1)
        vt = jnp.concatenate([blk_refs[kvh * SLC_TOPN + t][1] for t in range(SLC_TOPN)] + [kvst[v_rows, :]], axis=1)
        o_k = attend(kt.astype(BF16), vt.astype(BF16), jnp.concatenate(valid, axis=1))
        o_s = jnp.where(head_kvh == kvh, o_k, o_s)
        kt = jnp.concatenate([win[k_rows, :], kvwt[k_rows, :]], axis=1)
        vt = jnp.concatenate([win[v_rows, :], kvwt[v_rows, :]], axis=1)
        valid_w = jnp.concatenate([jnp.full((1, win.shape[1]), True), mine], axis=1)
        o_k = attend(kt.astype(BF16), vt.astype(BF16), valid_w)
        o_w = jnp.where(head_kvh == kvh, o_k, o_w)

    oc = oc_ref[0]
    o_c = jnp.where(head_kvh == 0, oc[:, :HEAD_DIM], oc[:, HEAD_DIM:])
    gates = jnp.sum(jnp.where(mine, gt_ref[...], 0.0), axis=1, keepdims=True)
    o_ref[0] = (gates[0:N_HEADS] * o_c + gates[N_HEADS:2 * N_HEADS] * o_s + gates[2 * N_HEADS:3 * N_HEADS] * o_w)

    new_col = jnp.sum(jnp.where(mine, kvwt, 0.0), axis=1, keepdims=True)
    last = lax.broadcasted_iota(jnp.int32, win.shape, 1) == win.shape[1] - 1
    nwin_ref[0] = jnp.where(last, new_col, pltpu.roll(win, win.shape[1] - 1, 1))


def _attn_sample(idx, page_table, cache_t, q, o_c, gt, kvst, kvwt, win_t, *, n_past_slc):
    nb = q.shape[0]
    sub = PAGE_SIZE // SLC_BLOCK
    win_buf = win_t.shape[2]
    n_pages = page_table.shape[1]

    def blk_spec(kvh, t):
        def index(i, idx_ref, pt_ref):
            j = jnp.minimum(idx_ref[(i * KV_HEADS + kvh) * SLC_TOPN + t], n_past_slc - 1)
            return (pt_ref[i * n_pages + j // sub], 0, kvh, 0, 0)
        return pl.BlockSpec((None, 2, None, HEAD_DIM, PAGE_SIZE), index)

    per_b = lambda shape: pl.BlockSpec((1,) + shape, lambda i, a, c: (i,) + (0,) * len(shape))
    const = lambda a: pl.BlockSpec(a.shape, lambda i, x, c: (0,) * a.ndim)
    n_sel = KV_HEADS * SLC_TOPN
    return pl.pallas_call(
        functools.partial(_attn_sample_kernel, n_past_slc=n_past_slc),
        grid_spec=pltpu.PrefetchScalarGridSpec(
            num_scalar_prefetch=2,
            grid=(nb,),
            in_specs=[blk_spec(k, t) for k in range(KV_HEADS) for t in range(SLC_TOPN)] + [
                per_b((N_HEADS, HEAD_DIM)), per_b((N_HEADS, D_KV)), const(gt), const(kvst), const(kvwt),
                per_b((ROW_W, win_buf))],
            out_specs=[per_b((N_HEADS, HEAD_DIM)), per_b((ROW_W, win_buf))],
        ),
        out_shape=[jax.ShapeDtypeStruct((nb, N_HEADS, HEAD_DIM), F32), jax.ShapeDtypeStruct(win_t.shape, F32)],
        compiler_params=_cparams("arbitrary"),
        name="attn_sample",
    )(idx.reshape(-1), page_table.reshape(-1), *([cache_t] * n_sel), q, o_c, gt, kvst, kvwt, win_t)


def _conv_tail(y, b_ref, lg_ref, lb_ref):
    y = y + b_ref[...]
    yc = y - jnp.mean(y, axis=-1, keepdims=True)
    yn = yc * lax.rsqrt(jnp.mean(yc * yc, axis=-1, keepdims=True) + EPS)
    return _silu(yn * lg_ref[...] + lb_ref[...])


def _conv_prompt_kernel(u_ref, halo_ref, w_ref, b_ref, lg_ref, lb_ref, o_ref, buf_ref, *, halo):
    tm = u_ref.shape[1]
    j = pl.program_id(1)
    buf_ref[0:halo, :] = jnp.where(j > 0, halo_ref[0], 0.0)
    buf_ref[halo:, :] = u_ref[0]
    lead = halo - (CONV_WIDTH - 1)
    y = buf_ref[lead:lead + tm, :] * w_ref[0:1, :]
    for w in range(1, CONV_WIDTH):
        y = y + buf_ref[lead + w:lead + w + tm, :] * w_ref[w:w + 1, :]
    o_ref[0] = _conv_tail(y, b_ref, lg_ref, lb_ref)


def _conv_prompt(u, w_dw, b_dw, ln_g, ln_b, *, tm):
    b, s, dc = u.shape
    halo = 32
    per = tm // halo
    const = lambda a: pl.BlockSpec(a.shape, lambda i, j: (0,) * a.ndim)
    return pl.pallas_call(
        functools.partial(_conv_prompt_kernel, halo=halo),
        grid=(b, s // tm),
        in_specs=[pl.BlockSpec((1, tm, dc), lambda i, j: (i, j, 0)),
                  pl.BlockSpec((1, halo, dc), lambda i, j: (i, jnp.maximum(j * per - 1, 0), 0)),
                  const(w_dw), const(b_dw), const(ln_g), const(ln_b)],
        out_specs=pl.BlockSpec((1, tm, dc), lambda i, j: (i, j, 0)),
        out_shape=jax.ShapeDtypeStruct((b, s, dc), F32),
        scratch_shapes=[pltpu.VMEM((tm + halo, dc), F32)],
        compiler_params=_cparams("arbitrary", "arbitrary"),
        name="conv_prompt",
    )(u, u, w_dw, b_dw, ln_g, ln_b)


def _conv_sample_kernel(st_ref, u_ref, w_ref, b_ref, lg_ref, lb_ref, o_ref):
    hist = CONV_WIDTH - 1
    y = u_ref[...] * w_ref[hist:hist + 1, :]
    for w in range(hist):
        y = y + st_ref[w] * w_ref[w:w + 1, :]
    o_ref[...] = _conv_tail(y, b_ref, lg_ref, lb_ref)


def _conv_sample(state_t, u, w_dw, b_dw, ln_g, ln_b, *, sb):
    nb, dc = u.shape
    hist = state_t.shape[0]
    const = lambda a: pl.BlockSpec(a.shape, lambda i: (0,) * a.ndim)
    return pl.pallas_call(
        _conv_sample_kernel,
        grid=(nb // sb,),
        in_specs=[pl.BlockSpec((hist, sb, dc), lambda i: (0, i, 0)), pl.BlockSpec((sb, dc), lambda i: (i, 0)),
                  const(w_dw), const(b_dw), const(ln_g), const(ln_b)],
        out_specs=pl.BlockSpec((sb, dc), lambda i: (i, 0)),
        out_shape=jax.ShapeDtypeStruct((nb, dc), F32),
        compiler_params=_cparams("arbitrary"),
        name="conv_sample",
    )(state_t, u, w_dw, b_dw, ln_g, ln_b)


def _finish_kernel(x_ref, oa_ref, ocv_ref, gate1_ref, shift2_ref, scale2_ref, ga_ref, gc_ref, wout_ref, g2_ref,
                   wr_ref, br_ref, x1_ref, h2_ref, comb_ref, route_ref, packed_ref, count_ref):
    mixed = jnp.concatenate([_rms(oa_ref[0], ga_ref[...]), _rms(ocv_ref[0], gc_ref[...])], axis=1)
    mix = _mm(mixed.astype(BF16), wout_ref[...])
    x1 = x_ref[0] + gate1_ref[0] * mix
    x1_ref[0] = x1
    h2 = _rms(x1, g2_ref[...]) * (1.0 + scale2_ref[0]) + shift2_ref[0]
    h2b = h2.astype(BF16)
    h2_ref[0] = h2b
    half = h2.shape[1] // 2
    bits = pltpu.bitcast(h2b.astype(F32), jnp.uint32)
    packed_ref[0] = (bits[:, half:] & jnp.uint32(0xFFFF0000)) | (bits[:, :half] >> 16)

    h_hi, h_mid, _ = _split3(h2)
    w_hi, w_mid, _ = _split3(wr_ref[...])
    lg = (_mm(h_hi, w_hi) + _mm(h_hi, w_mid)
          + _mm(h_mid, w_hi)) + br_ref[...]
    lane = lax.broadcasted_iota(jnp.int32, lg.shape, 1)
    lane_f = lane.astype(F32)
    is_group = (lane >= N_EXPERTS) & (lane < N_EXPERTS + N_GROUPS)
    gl = jnp.where(is_group, lg, NEG_INF)
    g_max = jnp.max(gl, axis=1, keepdims=True)
    p_top = 1.0 / jnp.sum(jnp.where(is_group, jnp.exp(gl - g_max), 0.0), axis=1, keepdims=True)
    g_lane = jnp.min(jnp.where(gl == g_max, lane_f, 1e9), axis=1, keepdims=True)
    in_group = (lane < N_EXPERTS) & ((lane // EPG).astype(F32) == g_lane - N_EXPERTS)
    el = jnp.where(in_group, lg, NEG_INF)
    l1 = jnp.max(el, axis=1, keepdims=True)
    i1 = jnp.min(jnp.where(el == l1, lane_f, 1e9), axis=1, keepdims=True)
    el2 = jnp.where(lane_f == i1, NEG_INF, el)
    l2 = jnp.max(el2, axis=1, keepdims=True)
    i2 = jnp.min(jnp.where(el2 == l2, lane_f, 1e9), axis=1, keepdims=True)
    r = jnp.exp(l2 - l1)
    w1 = p_top / (1.0 + r)
    w2 = p_top * r / (1.0 + r)
    comb_ref[0] = jnp.where(lane_f == i1, w1, jnp.where(lane_f == i2, w2, 0.0))
    route_ref[0] = jnp.where(lane == 0, i1, jnp.where(lane == 1, i2, jnp.where(lane == 2, w1, jnp.where(
        lane == 3, w2, 0.0))))

    @pl.when((pl.program_id(0) == 0) & (pl.program_id(1) == 0))
    def _():
        count_ref[...] = jnp.zeros_like(count_ref)

    picks = jnp.where(lane_f == i1, 1.0, 0.0) + jnp.where(lane_f == i2, 1.0, 0.0)
    count_ref[...] += jnp.sum(picks, axis=0, keepdims=True)


def _finish(x, o_attn, o_conv, gate1, shift2, scale2, ga, gc, w_out, g2, w_route, b_route, *, tm):
    b, s, d = x.shape
    r = gate1.shape[1]
    mod_block = (1, 1, d) if r == 1 else (1, tm, d)
    mod_map = (lambda i, j: (i, 0, 0)) if r == 1 else (lambda i, j: (i, j, 0))
    mod = pl.BlockSpec(mod_block, mod_map)
    row = lambda w: pl.BlockSpec((1, tm, w), lambda i, j: (i, j, 0))
    const = lambda a: pl.BlockSpec(a.shape, lambda i, j: (0,) * a.ndim)
    return pl.pallas_call(
        _finish_kernel,
        grid=(b, s // tm),
        in_specs=[row(d), row(D_ATTN), row(D_CONV), mod, mod, mod, const(ga), const(gc), const(w_out), const(g2),
                  const(w_route), const(b_route)],
        out_specs=[row(d), row(d), row(128), row(128), row(d // 2), pl.BlockSpec((8, 128), lambda i, j: (0, 0))],
        out_shape=[jax.ShapeDtypeStruct((b, s, d), F32), jax.ShapeDtypeStruct((b, s, d), BF16),
                   jax.ShapeDtypeStruct((b, s, 128), F32), jax.ShapeDtypeStruct((b, s, 128), F32),
                   jax.ShapeDtypeStruct((b, s, d // 2), jnp.uint32), jax.ShapeDtypeStruct((8, 128), F32)],
        compiler_params=_cparams("arbitrary", "arbitrary"),
        name="finish",
    )(x, o_attn, o_conv, gate1, shift2, scale2, ga, gc, w_out, g2, w_route, b_route)


def _moe_kernel(h_ref, comb_ref, x1_ref, gate2_ref, wg_ref, wu_ref, wd_ref, gf_ref, y_ref, acc_ref):
    e = pl.program_id(2)

    @pl.when(e == 0)
    def _():
        acc_ref[...] = jnp.zeros_like(acc_ref)

    h = h_ref[0]
    a = _mm(h, wg_ref[0].astype(BF16))
    u = _mm(h, wu_ref[0].astype(BF16))
    comb = comb_ref[0]
    lane = lax.broadcasted_iota(jnp.int32, comb.shape, 1)
    cw = jnp.sum(jnp.where(lane == e, comb, 0.0), axis=1, keepdims=True)
    hid = (_silu(a) * u * cw).astype(BF16)
    acc_ref[...] += _mm(hid, wd_ref[0].astype(BF16))

    @pl.when(e == pl.num_programs(2) - 1)
    def _():
        y = x1_ref[0] + gate2_ref[0] * acc_ref[...]
        y_ref[0] = _rms(y, gf_ref[...])


def _moe(h2, comb, x1, gate2, w_gate, w_up, w_down, final_g, *, tm):
    b, s, d = x1.shape
    r = gate2.shape[1]
    mod_block = (1, 1, d) if r == 1 else (1, tm, d)
    mod_map = (lambda i, j, e: (i, 0, 0)) if r == 1 else (lambda i, j, e: (i, j, 0))
    row = lambda w: pl.BlockSpec((1, tm, w), lambda i, j, e: (i, j, 0))
    ne, _, de = w_gate.shape
    return pl.pallas_call(
        _moe_kernel,
        grid=(b, s // tm, ne),
        in_specs=[row(d), row(128), row(d), pl.BlockSpec(mod_block, mod_map),
                  pl.BlockSpec((1, d, de), lambda i, j, e: (e, 0, 0)),
                  pl.BlockSpec((1, d, de), lambda i, j, e: (e, 0, 0)),
                  pl.BlockSpec((1, de, d), lambda i, j, e: (e, 0, 0)),
                  pl.BlockSpec((1, d), lambda i, j, e: (0, 0))],
        out_specs=row(d),
        out_shape=jax.ShapeDtypeStruct((b, s, d), F32),
        scratch_shapes=[pltpu.VMEM((tm, d), F32)],
        compiler_params=_cparams("arbitrary", "arbitrary", "arbitrary"),
        name="moe",
    )(h2, comb, x1, gate2, w_gate, w_up, w_down, final_g)


def _moe_slots_kernel(route_ref, base_ref, slot_ref, run_ref):
    @pl.when(pl.program_id(0) == 0)
    def _():
        run_ref[...] = jnp.zeros_like(run_ref)

    route = route_ref[...]
    tm = route.shape[0]
    lane = lax.broadcasted_iota(jnp.int32, route.shape, 1).astype(F32)
    first = jnp.where(lane == route[:, 0:1], 1.0, 0.0)
    second = jnp.where(lane == route[:, 1:2], 1.0, 0.0)
    picks = first + second
    earlier = lax.broadcasted_iota(jnp.int32, (tm, tm), 1) < lax.broadcasted_iota(jnp.int32, (tm, tm), 0)
    seen = _mm(jnp.where(earlier, 1.0, 0.0).astype(BF16), picks.astype(BF16)) + (run_ref[...] + base_ref[...])
    slot0 = jnp.sum(first * seen, axis=1, keepdims=True)
    slot1 = jnp.sum(second * seen, axis=1, keepdims=True)
    slot_ref[...] = jnp.where(lane == 0.0, slot0, jnp.where(lane == 1.0, slot1, 0.0)).astype(jnp.int32)
    run_ref[...] += jnp.sum(picks, axis=0, keepdims=True)


def _moe_slots(route, base, *, tm):
    n = route.shape[0]
    return pl.pallas_call(
        _moe_slots_kernel,
        grid=(n // tm,),
        in_specs=[pl.BlockSpec((tm, 128), lambda j: (j, 0)), pl.BlockSpec((1, 128), lambda j: (0, 0))],
        out_specs=pl.BlockSpec((tm, 128), lambda j: (j, 0)),
        out_shape=jax.ShapeDtypeStruct((n, 128), jnp.int32),
        scratch_shapes=[pltpu.VMEM((1, 128), F32)],
        compiler_params=_cparams("arbitrary"),
        name="moe_slots",
    )(route, base)


def _row_copy(src_ref, src_row, dst_ref, dst_row, sem):
    return pltpu.make_async_copy(src_ref.at[pl.ds(src_row, 1), :], dst_ref.at[pl.ds(dst_row, 1), :], sem)


def _moe_scatter_kernel(s0_ref, s1_ref, h_ref, init_ref, sorted_ref, sem):
    del init_ref
    j = pl.program_id(0)
    tm = h_ref.shape[0]

    def start(r, carry):
        _row_copy(h_ref, r, sorted_ref, s0_ref[j * tm + r], sem).start()
        _row_copy(h_ref, r, sorted_ref, s1_ref[j * tm + r], sem).start()
        return carry

    def wait(r, carry):
        _row_copy(h_ref, 0, sorted_ref, 0, sem).wait()
        _row_copy(h_ref, 0, sorted_ref, 0, sem).wait()
        return carry

    lax.fori_loop(0, tm, start, 0, unroll=8)
    lax.fori_loop(0, tm, wait, 0, unroll=8)


def _moe_scatter(slot0, slot1, packed, n_slots, *, tm):
    n, w = packed.shape
    return pl.pallas_call(
        _moe_scatter_kernel,
        grid_spec=pltpu.PrefetchScalarGridSpec(
            num_scalar_prefetch=2,
            grid=(n // tm,),
            in_specs=[pl.BlockSpec((tm, w), lambda j, a, b: (j, 0)), pl.BlockSpec(memory_space=pl.ANY)],
            out_specs=pl.BlockSpec(memory_space=pl.ANY),
            scratch_shapes=[pltpu.SemaphoreType.DMA(())],
        ),
        out_shape=jax.ShapeDtypeStruct((n_slots, w), packed.dtype),
        input_output_aliases={3: 0},
        compiler_params=pltpu.CompilerParams(dimension_semantics=("arbitrary",), vmem_limit_bytes=VMEM_LIMIT,
                                             disable_bounds_checks=True),
        name="moe_scatter",
    )(slot0, slot1, packed, jnp.zeros((n_slots, w), packed.dtype))


def _moe_experts_kernel(te_ref, na_ref, x_ref, wg_ref, wu_ref, wd_ref, o_ref):
    t = pl.program_id(0)

    @pl.when(t < na_ref[0])
    def _():
        x = x_ref[...]
        lo = pltpu.bitcast(x << 16, F32).astype(BF16)
        hi = pltpu.bitcast(x & jnp.uint32(0xFFFF0000), F32).astype(BF16)
        xb = jnp.concatenate([lo, hi], axis=1)
        a = _mm(xb, wg_ref[0].astype(BF16))
        u = _mm(xb, wu_ref[0].astype(BF16))
        o_ref[...] = _mm((_silu(a) * u).astype(BF16), wd_ref[0].astype(BF16))

    @pl.when(t >= na_ref[0])
    def _():
        o_ref[...] = jnp.zeros_like(o_ref)


def _moe_experts(tile_expert, n_active, sorted_x, w_gate, w_up, w_down):
    n_slots, w = sorted_x.shape
    ne, d, de = w_gate.shape
    x_map = lambda t, te, na: (jnp.minimum(t, na[0] - 1), 0)
    w_map = lambda t, te, na: (te[t], 0, 0)
    return pl.pallas_call(
        _moe_experts_kernel,
        grid_spec=pltpu.PrefetchScalarGridSpec(
            num_scalar_prefetch=2,
            grid=(n_slots // MOE_TILE,),
            in_specs=[pl.BlockSpec((MOE_TILE, w), x_map), pl.BlockSpec((1, d, de), w_map),
                      pl.BlockSpec((1, d, de), w_map), pl.BlockSpec((1, de, d), w_map)],
            out_specs=pl.BlockSpec((MOE_TILE, d), lambda t, te, na: (t, 0)),
        ),
        out_shape=jax.ShapeDtypeStruct((n_slots, d), F32),
        compiler_params=_cparams("arbitrary"),
        name="moe_experts",
    )(tile_expert, n_active, sorted_x, w_gate, w_up, w_down)


def _moe_combine_kernel(s0_ref, s1_ref, y_hbm, route_ref, x1_ref, gate2_ref, gf_ref, o_ref, a_ref, b_ref, sem):
    j = pl.program_id(0)
    tm = x1_ref.shape[0]

    def issue(step, slot):
        def body(r, carry):
            _row_copy(y_hbm, s0_ref[step * tm + r], a_ref.at[slot], r, sem.at[slot]).start()
            _row_copy(y_hbm, s1_ref[step * tm + r], b_ref.at[slot], r, sem.at[slot]).start()
            return carry
        lax.fori_loop(0, tm, body, 0, unroll=8)

    @pl.when(j == 0)
    def _():
        issue(0, 0)

    @pl.when(j + 1 < pl.num_programs(0))
    def _():
        issue(j + 1, (j + 1) % 2)

    slot = j % 2

    def wait(r, carry):
        _row_copy(y_hbm, 0, a_ref.at[slot], 0, sem.at[slot]).wait()
        _row_copy(y_hbm, 0, b_ref.at[slot], 0, sem.at[slot]).wait()
        return carry
    lax.fori_loop(0, tm, wait, 0, unroll=8)

    route = route_ref[...]
    moe = route[:, 2:3] * a_ref[slot] + route[:, 3:4] * b_ref[slot]
    o_ref[...] = _rms(x1_ref[...] + gate2_ref[0] * moe, gf_ref[...])


def _moe_combine(slot0, slot1, y_sorted, route, x1, gate2, final_g, *, tm, rows_per_mod):
    n, d = x1.shape
    return pl.pallas_call(
        _moe_combine_kernel,
        grid_spec=pltpu.PrefetchScalarGridSpec(
            num_scalar_prefetch=2,
            grid=(n // tm,),
            in_specs=[pl.BlockSpec(memory_space=pl.ANY), pl.BlockSpec((tm, 128), lambda j, a, b: (j, 0)),
                      pl.BlockSpec((tm, d), lambda j, a, b: (j, 0)),
                      pl.BlockSpec((1, 1, d), lambda j, a, b: (j * tm // rows_per_mod, 0, 0)),
                      pl.BlockSpec((1, d), lambda j, a, b: (0, 0))],
            out_specs=pl.BlockSpec((tm, d), lambda j, a, b: (j, 0)),
            scratch_shapes=[pltpu.VMEM((2, tm, d), F32), pltpu.VMEM((2, tm, d), F32), pltpu.SemaphoreType.DMA((2,))],
        ),
        out_shape=jax.ShapeDtypeStruct((n, d), F32),
        compiler_params=pltpu.CompilerParams(dimension_semantics=("arbitrary",), vmem_limit_bytes=VMEM_LIMIT,
                                             disable_bounds_checks=True),
        name="moe_combine",
    )(slot0, slot1, y_sorted, route, x1, gate2, final_g)


def _moe_sorted(route, counts, packed, x1, gate2, w_gate, w_up, w_down, final_g):
    b, s, d = x1.shape
    n = b * s
    ne = w_gate.shape[0]
    n_tiles = 2 * n // MOE_TILE + ne
    cnt = counts[0, :ne].astype(jnp.int32)
    padded = (cnt + MOE_TILE - 1) // MOE_TILE * MOE_TILE
    ends = jnp.cumsum(padded)
    base = jnp.zeros((1, 128), F32).at[0, :ne].set((ends - padded).astype(F32))
    n_active = (ends[-1] // MOE_TILE).reshape(1)
    tile_expert = jnp.minimum(jnp.searchsorted(ends, jnp.arange(n_tiles) * MOE_TILE, side='right'), ne - 1)
    tile_expert = tile_expert.astype(jnp.int32)

    route2 = route.reshape(n, 128)
    slots = _moe_slots(route2, base, tm=512)
    slot0, slot1 = slots[:, 0], slots[:, 1]
    sorted_x = _moe_scatter(slot0, slot1, packed.reshape(n, d // 2), n_tiles * MOE_TILE, tm=1024)
    y_sorted = _moe_experts(tile_expert, n_active, sorted_x, w_gate, w_up, w_down)
    y = _moe_combine(slot0, slot1, y_sorted, route2, x1.reshape(n, d), gate2, final_g, tm=256, rows_per_mod=s)
    return y.reshape(b, s, d)


def _prep_w_in(w_in):
    o_gl = D_ATTN + 3 * ROW_W
    wqkv = w_in[:, :o_gl].astype(BF16)
    wgl = w_in[:, o_gl:o_gl + 3 * N_HEADS]
    pad = jnp.zeros((w_in.shape[0], 128 - 3 * N_HEADS), w_in.dtype)
    wgl = jnp.concatenate([wgl.reshape(-1, N_HEADS, 3).transpose(0, 2, 1).reshape(-1, 3 * N_HEADS), pad], axis=1)
    wu = w_in[:, o_gl + 3 * N_HEADS:].astype(BF16)
    return wqkv, wgl.astype(BF16), wu


def _prep_compress(w_cmp1, pos_cmp, w_cmp2):
    ratio = CMP_BLOCK // CMP_STRIDE
    eye = jnp.eye(KV_HEADS, dtype=w_cmp1.dtype)
    w1 = w_cmp1.reshape(2, ratio, CMP_STRIDE, HEAD_DIM, HEAD_DIM)
    w1big = jnp.einsum('crsdf,kj->cskdrjf', w1, eye).reshape(2, CMP_STRIDE * D_KV, ratio * D_KV).astype(BF16)
    w2big = jnp.einsum('cfd,kj->ckfjd', w_cmp2, eye).reshape(2, D_KV, D_KV).astype(BF16)
    pos = pos_cmp.reshape(2, ratio, CMP_STRIDE, 1, HEAD_DIM)
    pos = jnp.broadcast_to(pos, (2, ratio, CMP_STRIDE, KV_HEADS, HEAD_DIM)).reshape(2, ratio, CMP_STRIDE * D_KV)
    posrows = jnp.concatenate([pos, jnp.zeros((2, 8 - ratio, CMP_STRIDE * D_KV), pos.dtype)], axis=1).astype(BF16)
    return w1big, w2big, posrows


def _prep_router(w_group, b_group, w_router, b_router):
    d = w_group.shape[0]
    pad = 128 - N_EXPERTS - N_GROUPS
    w = jnp.concatenate([w_router, w_group, jnp.zeros((d, pad), w_group.dtype)], axis=1)
    b = jnp.concatenate([b_router, b_group, jnp.zeros((pad,), b_group.dtype)]).reshape(1, 128)
    return w, b


def kernel(x_prompt, x_sample, cache_cmp_kv, cache_slc_kv, state_win_kv, state_conv, page_table, c_prompt, c_sample,
           norm1_g, w_ada, b_ada, w_in, w_cmp1, pos_cmp, w_cmp2, w_dw, b_dw, conv_ln_g, conv_ln_b, g_attn_out,
           g_conv_out, w_out, norm2_g, w_group, b_group, w_router, b_router, w_gate, w_up, w_down, final_g):
    depth = norm1_g.shape[0]
    assert depth == 1 and x_sample.shape[1] == 1
    bp, seq, d = x_prompt.shape
    nb = x_sample.shape[0]
    n_pages = page_table.shape[1]
    past_len = n_pages * PAGE_SIZE
    n_past_slc = past_len // SLC_BLOCK
    win_buf = state_win_kv.shape[2]
    assert win_buf == WINDOW and seq % K_TILE == 0 and seq >= WINDOW + Q_TILE
    l = 0
    row2 = lambda a: a.reshape(1, -1)

    mods = _adaln(jnp.concatenate([c_prompt, c_sample], axis=0), w_ada[l], b_ada[l]).reshape(bp + nb, 6, d)
    mods_p = [mods[:bp, i][:, None, :] for i in range(6)]
    mods_s = [mods[bp:, i][None, :, :] for i in range(6)]

    wqkv, wgl, wu = _prep_w_in(w_in[l])
    w1big, w2big, posrows = _prep_compress(w_cmp1[l], pos_cmp[l], w_cmp2[l])
    w_route, b_route = _prep_router(w_group[l], b_group[l], w_router[l], b_router[l])
    w_out_b = w_out[l].astype(BF16)
    conv_args = (w_dw[l], row2(b_dw[l]), row2(conv_ln_g[l]), row2(conv_ln_b[l]))

    def kv_rows_from_t(a_t):
        n, _, t = a_t.shape
        return a_t.reshape(n, 2, KV_HEADS, HEAD_DIM, t).transpose(0, 4, 1, 2, 3)

    def kv_rows_to_t(a):
        return a.transpose(0, 2, 3, 4, 1)

    qt, kvc, kvct, kvst, kvwt, ks, vts, kw, vtw, gt, u = _mixer(
        x_prompt, mods_p[0], mods_p[1], row2(norm1_g[l]), wqkv, wgl, wu, tm=512, prompt=True)
    kc, vct = _compress_prompt(kvc, w1big, w2big, posrows)
    o_attn = _attn_prompt(qt, kc, vct, ks, vts, kw, vtw, gt)
    o_conv = _conv_prompt(u, *conv_args, tm=512)
    x1, _, _, route, packed, counts = _finish(x_prompt, o_attn, o_conv, mods_p[2], mods_p[3], mods_p[4],
                                             row2(g_attn_out[l]), row2(g_conv_out[l]), w_out_b, row2(norm2_g[l]),
                                             w_route, b_route, tm=512)
    y_prompt = _moe_sorted(route, counts, packed, x1, mods_p[5], w_gate[l], w_up[l], w_down[l], row2(final_g))

    new_cmp_prompt = kv_rows_from_t(kvct)[None]
    new_slc_prompt = kv_rows_from_t(kvst)[None]
    new_win_prompt = kv_rows_from_t(kvwt[:, :, seq - WINDOW:])[None]
    new_conv_prompt = u[:, seq - (CONV_WIDTH - 1):][None]

    xs = x_sample.reshape(1, nb, d)
    q_s, kvct_s, kvst_s, kvwt_s, gt_s, u_s = _mixer(
        xs, mods_s[0], mods_s[1], row2(norm1_g[l]), wqkv, wgl, wu, tm=nb, prompt=False)
    cmp_t = kv_rows_to_t(cache_cmp_kv[l])
    kc_s, vc_s = _compress_sample(cmp_t.reshape(cmp_t.shape[0], ROW_W, PAGE_SIZE), page_table, w1big, w2big, posrows)

    q4 = q_s.reshape(nb, KV_HEADS, GQA, HEAD_DIM)
    zq = jnp.zeros_like(q4)
    kvh_id = jnp.arange(KV_HEADS).reshape(1, KV_HEADS, 1, 1)
    qpad = jnp.concatenate([jnp.where(kvh_id == 0, q4, zq), jnp.where(kvh_id == 1, q4, zq)],
                           axis=-1).reshape(nb, N_HEADS, D_KV)
    o_c, idx = _attn_sample_cmp(qpad, kc_s, vc_s, q_pos=past_len, n_slc=n_past_slc + 1)
    idx = idx[:, :KV_HEADS, :SLC_TOPN]
    win_t = kv_rows_to_t(state_win_kv[l]).reshape(nb, ROW_W, win_buf)
    o_heads, new_win_t = _attn_sample(idx, page_table, kv_rows_to_t(cache_slc_kv[l]), q_s.reshape(nb, N_HEADS, HEAD_DIM),
                                      o_c, gt_s[0], kvst_s[0], kvwt_s[0], win_t, n_past_slc=n_past_slc)
    o_attn_s = o_heads.reshape(1, nb, D_ATTN)
    state_t = state_conv[l].transpose(1, 0, 2)
    u_rows = u_s.reshape(nb, D_CONV)
    o_conv_s = _conv_sample(state_t, u_rows, *conv_args, sb=8).reshape(1, nb, D_CONV)
    x1_s, h2_s, comb_s, _, _, _ = _finish(xs, o_attn_s, o_conv_s, mods_s[2], mods_s[3], mods_s[4],
                                          row2(g_attn_out[l]), row2(g_conv_out[l]), w_out_b, row2(norm2_g[l]),
                                          w_route, b_route, tm=nb)
    y_sample = _moe(h2_s, comb_s, x1_s, mods_s[5], w_gate[l], w_up[l], w_down[l], row2(final_g), tm=nb)

    row_shape = (1, nb, 1, 2, KV_HEADS, HEAD_DIM)
    new_cmp_sample = kvct_s[0].T.reshape(row_shape)
    new_slc_sample = kvst_s[0].T.reshape(row_shape)
    new_win_sample = kv_rows_from_t(new_win_t)[None]
    new_conv_sample = jnp.concatenate([state_t[1:], u_rows[None]], axis=0).transpose(1, 0, 2)[None]

    return (y_prompt, y_sample.reshape(nb, 1, d), new_cmp_prompt, new_slc_prompt, new_win_prompt, new_conv_prompt,
            new_cmp_sample, new_slc_sample, new_win_sample, new_conv_sample)
```

```python
import functools

import jax
import jax.numpy as jnp
from jax import lax
from jax.experimental import pallas as pl
from jax.experimental.pallas import tpu as pltpu

F32 = jnp.float32
BF16 = jnp.bfloat16

D_MODEL = 1024
N_HEADS = 8
HEAD_DIM = 64
KV_HEADS = 2
GQA = N_HEADS // KV_HEADS
D_ATTN = N_HEADS * HEAD_DIM
D_CONV = D_MODEL - D_ATTN
D_KV = KV_HEADS * HEAD_DIM
CMP_BLOCK = 32
CMP_STRIDE = 16
SLC_BLOCK = 64
SLC_TOPN = 16
N_LOCAL_BLOCKS = 2
WINDOW = 512
FORCED_SCORE = 1e4
CONV_WIDTH = 31
N_GROUPS = 4
EPG = 8
N_EXPERTS = N_GROUPS * EPG
D_EXPERT = 256
PAGE_SIZE = 128
EPS = 1e-6
NEG_INF = -1e30
SCALE = HEAD_DIM ** -0.5
LOG2E = 1.4426950408889634
ROW_W = 2 * D_KV
CHUNK_W = CMP_STRIDE * ROW_W
CHUNK_PITCH = 20
MOE_TILE = 256
CONV_ROWS = 32
Q_TILE = 128
K_TILE = 512
VMEM_LIMIT = 48 * 1024 * 1024


def _cparams(*sem):
    return pltpu.CompilerParams(dimension_semantics=sem, vmem_limit_bytes=VMEM_LIMIT)


def _rms(x, g):
    return x * lax.rsqrt(jnp.mean(x * x, axis=-1, keepdims=True) + EPS) * g


def _silu(x):
    return x * jax.nn.sigmoid(x)


def _mm(a, b):
    return jnp.dot(a, b, preferred_element_type=F32)


def _mm_nt(a, b):
    return lax.dot_general(a, b, (((1,), (1,)), ((), ())), preferred_element_type=F32)


def _split3(x):
    hi = x.astype(BF16)
    r = x - hi.astype(F32)
    mid = r.astype(BF16)
    lo = (r - mid.astype(F32)).astype(BF16)
    return hi, mid, lo


def _softmax_masked(s, mask, axis):
    s = jnp.where(mask, s, NEG_INF)
    m = jnp.max(s, axis=axis, keepdims=True)
    e = jnp.exp(s - m)
    p = e / jnp.sum(e, axis=axis, keepdims=True)
    return jnp.where(mask, p, 0.0)


def _adaln_kernel(c_ref, w_ref, b_ref, o_ref):
    s = _silu(c_ref[...]).astype(BF16)
    o_ref[...] = _mm(s, w_ref[...].astype(BF16)) + b_ref[...]


def _adaln(c_all, w_ada, b_ada):
    n, d = c_all.shape
    nout = w_ada.shape[1]
    tn = 1024
    return pl.pallas_call(
        _adaln_kernel,
        grid=(nout // tn,),
        in_specs=[pl.BlockSpec((n, d), lambda j: (0, 0)),
                  pl.BlockSpec((d, tn), lambda j: (0, j)),
                  pl.BlockSpec((1, tn), lambda j: (0, j))],
        out_specs=pl.BlockSpec((n, tn), lambda j: (0, j)),
        out_shape=jax.ShapeDtypeStruct((n, nout), F32),
        compiler_params=_cparams("arbitrary"),
        name="adaln",
    )(c_all, w_ada, b_ada.reshape(1, nout))


def _mixer_kernel(x_ref, shift_ref, scale_ref, g_ref, wqkv_ref, wgl_ref, wu_ref, *outs, prompt):
    x = x_ref[0]
    h = _rms(x, g_ref[...]) * (1.0 + scale_ref[0]) + shift_ref[0]
    hb = h.astype(BF16)
    p = _mm(hb, wqkv_ref[...])
    gl = _mm(hb, wgl_ref[...])
    pu = _mm(hb, wu_ref[...])
    u = pu[:, :D_CONV] * jax.nn.sigmoid(pu[:, D_CONV:])
    q = p[:, :D_ATTN] * (SCALE * LOG2E if prompt else SCALE)
    o = D_ATTN
    kvc = p[:, o:o + ROW_W]
    kvs = p[:, o + ROW_W:o + 2 * ROW_W]
    kvw = p[:, o + 2 * ROW_W:o + 3 * ROW_W]
    kvs_t = kvs.T
    kvw_t = kvw.T
    if prompt:
        q_ref, kvc_ref, kvct_ref, kvst_ref, kvwt_ref, ks_ref, vts_ref, kw_ref, vtw_ref, gt_ref, u_ref = outs
        q_ref[0] = q.T.astype(BF16)
        kvc_ref[0] = kvc
        ks_ref[0] = kvs[:, :D_KV].astype(BF16)
        vts_ref[0] = kvs_t[D_KV:, :].astype(BF16)
        kw_ref[0] = kvw[:, :D_KV].astype(BF16)
        vtw_ref[0] = kvw_t[D_KV:, :].astype(BF16)
    else:
        q_ref, kvct_ref, kvst_ref, kvwt_ref, gt_ref, u_ref = outs
        q_ref[0] = q
    kvct_ref[0] = kvc.T
    kvst_ref[0] = kvs_t
    kvwt_ref[0] = kvw_t
    gt_ref[0] = jax.nn.sigmoid(gl).T[:32, :]
    u_ref[0] = u


def _mixer(x, shift, scale, g, wqkv, wgl, wu, *, tm, prompt):
    b, s, d = x.shape
    r = shift.shape[1]
    mod_block = (1, 1, d) if r == 1 else (1, tm, d)
    mod_map = (lambda i, j: (i, 0, 0)) if r == 1 else (lambda i, j: (i, j, 0))
    row = lambda w: pl.BlockSpec((1, tm, w), lambda i, j: (i, j, 0))
    col = lambda w: pl.BlockSpec((1, w, tm), lambda i, j: (i, 0, j))
    const = lambda a: pl.BlockSpec(a.shape, lambda i, j: (0,) * a.ndim)
    rows = lambda w, dt: jax.ShapeDtypeStruct((b, s, w), dt)
    cols = lambda w, dt: jax.ShapeDtypeStruct((b, w, s), dt)
    if prompt:
        out_specs = [col(D_ATTN), row(ROW_W), col(ROW_W), col(ROW_W), col(ROW_W), row(D_KV), col(D_KV), row(D_KV),
                     col(D_KV), col(32), row(D_CONV)]
        out_shape = [cols(D_ATTN, BF16), rows(ROW_W, F32), cols(ROW_W, F32), cols(ROW_W, F32), cols(ROW_W, F32),
                     rows(D_KV, BF16), cols(D_KV, BF16), rows(D_KV, BF16), cols(D_KV, BF16), cols(32, F32),
                     rows(D_CONV, F32)]
    else:
        out_specs = [row(D_ATTN), col(ROW_W), col(ROW_W), col(ROW_W), col(32), row(D_CONV)]
        out_shape = [rows(D_ATTN, F32), cols(ROW_W, F32), cols(ROW_W, F32), cols(ROW_W, F32), cols(32, F32),
                     rows(D_CONV, F32)]
    return pl.pallas_call(
        functools.partial(_mixer_kernel, prompt=prompt),
        grid=(b, s // tm),
        in_specs=[row(d), pl.BlockSpec(mod_block, mod_map), pl.BlockSpec(mod_block, mod_map),
                  const(g), const(wqkv), const(wgl), const(wu)],
        out_specs=out_specs,
        out_shape=out_shape,
        compiler_params=_cparams("arbitrary", "arbitrary"),
        name="mixer_prompt" if prompt else "mixer_sample",
    )(x, shift, scale, g, wqkv, wgl, wu)


def _chunk_part(load_offset, w1_c):
    xc = jnp.concatenate([load_offset(s) for s in range(CMP_STRIDE)], axis=1)
    return _mm(xc.astype(BF16), w1_c)


def _compress_finish(part, posb, w2_c):
    n = part.shape[0]
    nxt = pltpu.roll(part[:, D_KV:], n - 1, 0)
    pre = part[:, :D_KV] + nxt + posb[0:1, :D_KV] + posb[1:2, D_KV:]
    return _mm(_silu(pre).astype(BF16), w2_c)


def _compress_prompt_kernel(x_ref, w1_ref, w2_ref, pos_ref, kc_ref, vct_ref):
    for c in range(2):
        part = _chunk_part(lambda s: x_ref[0, :, s * ROW_W + c * D_KV:s * ROW_W + (c + 1) * D_KV], w1_ref[c])
        posb = _mm(pos_ref[c], w1_ref[c])
        out = _compress_finish(part, posb, w2_ref[c])
        if c == 0:
            kc_ref[0] = out.astype(BF16)
        else:
            vct_ref[0] = out.T.astype(BF16)


def _compress_prompt(kvc, w1big, w2big, posrows):
    b, s, _ = kvc.shape
    n = s // CMP_STRIDE
    x = kvc.reshape(b, n, CHUNK_W)
    const = lambda a: pl.BlockSpec(a.shape, lambda i: (0,) * a.ndim)
    return pl.pallas_call(
        _compress_prompt_kernel,
        grid=(b,),
        in_specs=[pl.BlockSpec((1, n, CHUNK_W), lambda i: (i, 0, 0)), const(w1big), const(w2big), const(posrows)],
        out_specs=[pl.BlockSpec((1, n, D_KV), lambda i: (i, 0, 0)), pl.BlockSpec((1, D_KV, n), lambda i: (i, 0, 0))],
        out_shape=[jax.ShapeDtypeStruct((b, n, D_KV), BF16), jax.ShapeDtypeStruct((b, D_KV, n), BF16)],
        compiler_params=_cparams("arbitrary"),
        name="compress_prompt",
    )(x, w1big, w2big, posrows)


def _compress_sample_kernel(pt_ref, *refs, pages):
    x_refs = refs[:pages]
    w1_ref, w2_ref, pos_ref, kc_ref, vc_ref = refs[pages:pages + 5]
    row_refs = refs[pages + 5:]
    cpp = PAGE_SIZE // CMP_STRIDE
    n = pages * cpp
    for c in range(2):
        for k in range(pages):
            rows = x_refs[k][0, c * D_KV:(c + 1) * D_KV, :].T
            for j in range(cpp):
                r0 = (k * cpp + j) * CHUNK_PITCH
                row_refs[c][r0:r0 + CMP_STRIDE, :] = rows[j * CMP_STRIDE:(j + 1) * CMP_STRIDE, :]
        part = _chunk_part(lambda s: row_refs[c][pl.ds(s, n, stride=CHUNK_PITCH), :], w1_ref[c])
        posb = _mm(pos_ref[c], w1_ref[c])
        out = _compress_finish(part, posb, w2_ref[c])
        (kc_ref if c == 0 else vc_ref)[0] = out.astype(BF16)


def _compress_sample(cache_t, page_table, w1big, w2big, posrows):
    nb, n_pages = page_table.shape
    n = n_pages * (PAGE_SIZE // CMP_STRIDE)
    page_spec = lambda k: pl.BlockSpec((1, ROW_W, PAGE_SIZE), lambda i, pt: (pt[i * n_pages + k], 0, 0))
    const = lambda a: pl.BlockSpec(a.shape, lambda i, pt: (0,) * a.ndim)
    return pl.pallas_call(
        functools.partial(_compress_sample_kernel, pages=n_pages),
        grid_spec=pltpu.PrefetchScalarGridSpec(
            num_scalar_prefetch=1,
            grid=(nb,),
            in_specs=[page_spec(k) for k in range(n_pages)] + [const(w1big), const(w2big), const(posrows)],
            out_specs=[pl.BlockSpec((1, n, D_KV), lambda i, pt: (i, 0, 0))] * 2,
            scratch_shapes=[pltpu.VMEM((n * CHUNK_PITCH, D_KV), F32)] * 2,
        ),
        out_shape=[jax.ShapeDtypeStruct((nb, n, D_KV), BF16)] * 2,
        compiler_params=_cparams("arbitrary"),
        name="compress_sample",
    )(page_table.reshape(-1), *([cache_t] * n_pages), w1big, w2big, posrows)


def _overlap(cmp_idx, slc_idx):
    lo = cmp_idx * CMP_STRIDE
    so = slc_idx * SLC_BLOCK
    return (lo <= so + SLC_BLOCK - 1) & (lo + CMP_BLOCK - 1 >= so)


def _forced_importance(imp, blk, cur, n_slc):
    valid = (blk <= cur) & (blk < n_slc)
    forced = (blk == 0) | ((cur - blk >= 0) & (cur - blk < N_LOCAL_BLOCKS))
    return jnp.where(valid & forced, FORCED_SCORE, jnp.where(valid, imp, -1.0))


def _attn_prompt_kernel(qt_ref, kc_ref, vct_ref, ks_ref, vts_ref, kw_ref, vtw_ref, gt_ref, o_ref, sel_ref):
    i = pl.program_id(1)
    seq = ks_ref.shape[1]
    nbp = kc_ref.shape[1]
    n_slc = seq // SLC_BLOCK
    lanes = GQA * Q_TILE
    lane = lax.broadcasted_iota(jnp.int32, (1, lanes), 1)
    qpos = i * Q_TILE + (lane & (Q_TILE - 1))
    qpos_q = qpos[:, :Q_TILE]
    qt = qt_ref[0]
    gt = gt_ref[0]
    win_keys = min(WINDOW + Q_TILE, seq)
    blocks_per_tile = K_TILE // SLC_BLOCK

    ovl = _overlap(lax.broadcasted_iota(jnp.int32, (n_slc, nbp), 1), lax.broadcasted_iota(jnp.int32, (n_slc, nbp), 0))
    ovl = jnp.where(ovl, 1.0, 0.0).astype(BF16)
    blk = lax.broadcasted_iota(jnp.int32, (n_slc, Q_TILE), 0)

    cpos = lax.broadcasted_iota(jnp.int32, (nbp, lanes), 0) * CMP_STRIDE + (CMP_BLOCK - 1)
    cmp_bias = jnp.where(cpos <= qpos, 0.0, NEG_INF)
    any_cmp = jnp.where(qpos >= CMP_BLOCK - 1, 1.0, 0.0)
    w0 = pl.multiple_of(jnp.clip(i * Q_TILE - WINDOW, 0, seq - win_keys), Q_TILE)
    wpos = w0 + lax.broadcasted_iota(jnp.int32, (win_keys, lanes), 0)
    win_bias = jnp.where(wpos <= qpos, jnp.where(wpos >= qpos - WINDOW, 0.0, NEG_INF), NEG_INF)
    n_steps = ((i + 1) * Q_TILE + K_TILE - 1) // K_TILE
    last0 = pl.multiple_of((n_steps - 1) * K_TILE, K_TILE)
    causal_bias = jnp.where(last0 + lax.broadcasted_iota(jnp.int32, (K_TILE, lanes), 0) <= qpos, 0.0, NEG_INF)

    heads = range(KV_HEADS)
    hd = [slice(kvh * HEAD_DIM, (kvh + 1) * HEAD_DIM) for kvh in heads]
    qpad, o_c = [], []
    for kvh in heads:
        qk = jnp.concatenate([qt[(kvh * GQA + g) * HEAD_DIM:(kvh * GQA + g + 1) * HEAD_DIM, :] for g in range(GQA)],
                             axis=1)
        zero = jnp.zeros_like(qk)
        qpad.append(jnp.concatenate([qk, zero] if kvh == 0 else [zero, qk], axis=0))

        s = _mm(kc_ref[0], qpad[kvh]) + cmp_bias
        e = jnp.exp2(s - jnp.max(s, axis=0, keepdims=True))
        p = e * (any_cmp / jnp.sum(e, axis=0, keepdims=True))
        o_c.append(_mm(vct_ref[0, hd[kvh], :], p.astype(BF16)))

        psum = p[:, 0:Q_TILE]
        for g in range(1, GQA):
            psum = psum + p[:, g * Q_TILE:(g + 1) * Q_TILE]
        imp = sum(_mm(ovl, t) for t in _split3(psum))
        impf = _forced_importance(imp, blk, qpos_q // SLC_BLOCK, n_slc)
        rank = jnp.zeros_like(impf)
        for k in range(n_slc):
            rk = impf[k:k + 1, :]
            rank = rank + jnp.where(rk > impf, 1.0, 0.0) + jnp.where(rk == impf, jnp.where(blk > k, 1.0, 0.0), 0.0)
        sel = jnp.where(rank < SLC_TOPN, jnp.where(impf >= 0.0, 0.0, NEG_INF), NEG_INF)
        sel_ref[kvh] = jnp.concatenate([sel] * GQA, axis=1)

    def slc_step(t, carry, extra_bias=None):
        k0 = pl.multiple_of(t * K_TILE, K_TILE)
        keys = ks_ref[0, pl.ds(k0, K_TILE), :]
        out = []
        for kvh in heads:
            m, l, acc = carry[kvh]
            sk = _mm(keys, qpad[kvh])
            sk = jnp.concatenate(
                [sk[j * SLC_BLOCK:(j + 1) * SLC_BLOCK, :] + sel_ref[kvh, pl.ds(t * blocks_per_tile + j, 1), :]
                 for j in range(blocks_per_tile)], axis=0)
            if extra_bias is not None:
                sk = sk + extra_bias
            m_new = jnp.maximum(m, jnp.max(sk, axis=0, keepdims=True))
            alpha = jnp.exp2(m - m_new)
            e = jnp.exp2(sk - m_new)
            l_new = alpha * l + jnp.sum(e, axis=0, keepdims=True)
            pv = _mm(vts_ref[0, hd[kvh], pl.ds(k0, K_TILE)], e.astype(BF16))
            out.append((m_new, l_new, alpha * acc + pv))
        return tuple(out)

    init = (jnp.full((1, lanes), NEG_INF, F32), jnp.zeros((1, lanes), F32), jnp.zeros((HEAD_DIM, lanes), F32))
    carry = lax.fori_loop(0, n_steps - 1, slc_step, (init,) * KV_HEADS)
    carry = slc_step(n_steps - 1, carry, causal_bias)

    for kvh in heads:
        _, l_s, acc_s = carry[kvh]
        o_s = acc_s * (1.0 / l_s)

        sw = _mm(kw_ref[0, pl.ds(w0, win_keys), :], qpad[kvh]) + win_bias
        ew = jnp.exp2(sw - jnp.max(sw, axis=0, keepdims=True))
        o_w = _mm(vtw_ref[0, hd[kvh], pl.ds(w0, win_keys)], ew.astype(BF16)) * (
            1.0 / jnp.sum(ew, axis=0, keepdims=True))

        def gate(r):
            return jnp.concatenate([gt[r * N_HEADS + kvh * GQA + g:r * N_HEADS + kvh * GQA + g + 1, :]
                                    for g in range(GQA)], axis=1)
        o_t = gate(0) * o_c[kvh] + gate(1) * o_s + gate(2) * o_w
        for pair in range(GQA // 2):
            two = jnp.concatenate([o_t[:, (2 * pair) * Q_TILE:(2 * pair + 1) * Q_TILE],
                                   o_t[:, (2 * pair + 1) * Q_TILE:(2 * pair + 2) * Q_TILE]], axis=0)
            c0 = kvh * GQA * HEAD_DIM + pair * 2 * HEAD_DIM
            o_ref[0, :, c0:c0 + 2 * HEAD_DIM] = two.T


def _attn_prompt(qt, kc, vct, ks, vts, kw, vtw, gt):
    b, _, s = qt.shape
    nbp = kc.shape[1]
    per_b = lambda shape: pl.BlockSpec((1,) + shape, lambda i, j: (i, 0, 0))
    return pl.pallas_call(
        _attn_prompt_kernel,
        grid=(b, s // Q_TILE),
        in_specs=[pl.BlockSpec((1, D_ATTN, Q_TILE), lambda i, j: (i, 0, j)),
                  per_b((nbp, D_KV)), per_b((D_KV, nbp)),
                  per_b((s, D_KV)), per_b((D_KV, s)), per_b((s, D_KV)), per_b((D_KV, s)),
                  pl.BlockSpec((1, 32, Q_TILE), lambda i, j: (i, 0, j))],
        out_specs=pl.BlockSpec((1, Q_TILE, D_ATTN), lambda i, j: (i, j, 0)),
        out_shape=jax.ShapeDtypeStruct((b, s, D_ATTN), F32),
        scratch_shapes=[pltpu.VMEM((KV_HEADS, s // SLC_BLOCK, GQA * Q_TILE), F32)],
        compiler_params=_cparams("arbitrary", "arbitrary"),
        name="attn_prompt",
    )(qt, kc, vct, ks, vts, kw, vtw, gt)


def _attn_sample_cmp_kernel(q_ref, kc_ref, vc_ref, oc_ref, idx_ref, *, q_pos, n_slc):
    q = q_ref[0].astype(BF16)
    nb = kc_ref.shape[1]
    s = _mm_nt(q, kc_ref[0])
    cpos = lax.broadcasted_iota(jnp.int32, s.shape, 1) * CMP_STRIDE + (CMP_BLOCK - 1)
    p = _softmax_masked(s, cpos <= q_pos, 1)
    oc_ref[0] = _mm(p.astype(BF16), vc_ref[0])

    nsp = idx_ref.shape[2] * 2
    group_sums = [jnp.sum(p[k * GQA:(k + 1) * GQA, :], axis=0, keepdims=True) for k in range(KV_HEADS)]
    psum = jnp.concatenate(group_sums + [jnp.zeros((N_HEADS - KV_HEADS, nb), F32)], axis=0)
    ovl = _overlap(lax.broadcasted_iota(jnp.int32, (nb, nsp), 0), lax.broadcasted_iota(jnp.int32, (nb, nsp), 1))
    ovl = jnp.where(ovl, 1.0, 0.0).astype(BF16)
    imp = sum(_mm(t, ovl) for t in _split3(psum))
    blk = lax.broadcasted_iota(jnp.int32, imp.shape, 1)
    impf = _forced_importance(imp, blk, q_pos // SLC_BLOCK, n_slc)
    rank = jnp.zeros_like(impf)
    for k in range(n_slc):
        ck = impf[:, k:k + 1]
        rank = rank + jnp.where(ck > impf, 1.0, 0.0) + jnp.where(ck == impf, jnp.where(blk > k, 1.0, 0.0), 0.0)
    blk_f = blk.astype(F32)
    slot = lax.broadcasted_iota(jnp.int32, (N_HEADS, idx_ref.shape[2]), 1)
    idx = jnp.zeros((N_HEADS, idx_ref.shape[2]), F32)
    for t in range(SLC_TOPN):
        chosen = jnp.sum(jnp.where(rank == float(t), blk_f, 0.0), axis=1, keepdims=True)
        idx = idx + jnp.where(slot == t, chosen, 0.0)
    idx_ref[0] = idx.astype(jnp.int32)


def _attn_sample_cmp(qpad, kc, vc, *, q_pos, n_slc):
    nb = qpad.shape[0]
    n = kc.shape[1]
    per_b = lambda shape: pl.BlockSpec((1,) + shape, lambda i: (i, 0, 0))
    return pl.pallas_call(
        functools.partial(_attn_sample_cmp_kernel, q_pos=q_pos, n_slc=n_slc),
        grid=(nb,),
        in_specs=[per_b((N_HEADS, D_KV)), per_b((n, D_KV)), per_b((n, D_KV))],
        out_specs=[per_b((N_HEADS, D_KV)), per_b((N_HEADS, 128))],
        out_shape=[jax.ShapeDtypeStruct((nb, N_HEADS, D_KV), F32), jax.ShapeDtypeStruct((nb, N_HEADS, 128), jnp.int32)],
        compiler_params=_cparams("arbitrary"),
        name="attn_sample_cmp",
    )(qpad, kc, vc)


def _attn_sample_kernel(idx_ref, pt_ref, *refs, n_past_slc):
    n_sel = KV_HEADS * SLC_TOPN
    blk_refs = refs[:n_sel]
    q_ref, oc_ref, gt_ref, kvst_ref, kvwt_ref, win_ref, o_ref, nwin_ref = refs[n_sel:]
    b = pl.program_id(0)
    nb = kvst_ref.shape[1]
    q = q_ref[0].astype(BF16)
    head_kvh = lax.broadcasted_iota(jnp.int32, (N_HEADS, 1), 0) // GQA
    mine = lax.broadcasted_iota(jnp.int32, (1, nb), 1) == b
    lane_half = lax.broadcasted_iota(jnp.int32, (1, PAGE_SIZE), 1) // SLC_BLOCK

    def attend(kt, vt, valid):
        s = jnp.where(valid, _mm(q, kt), NEG_INF)
        e = jnp.where(valid, jnp.exp(s - jnp.max(s, axis=1, keepdims=True)), 0.0)
        return _mm_nt(e.astype(BF16), vt) / jnp.sum(e, axis=1, keepdims=True)

    kvst = kvst_ref[...]
    kvwt = kvwt_ref[...]
    o_s = jnp.zeros((N_HEADS, HEAD_DIM), F32)
    o_w = jnp.zeros((N_HEADS, HEAD_DIM), F32)
    win = win_ref[0]
    for kvh in range(KV_HEADS):
        k_rows = slice(kvh * HEAD_DIM, (kvh + 1) * HEAD_DIM)
        v_rows = slice(D_KV + kvh * HEAD_DIM, D_KV + (kvh + 1) * HEAD_DIM)
        valid = []
        n_new = jnp.int32(0)
        for t in range(SLC_TOPN):
            j = idx_ref[(b * KV_HEADS + kvh) * SLC_TOPN + t]
            past = j < n_past_slc
            half = jnp.minimum(j, n_past_slc - 1) % (PAGE_SIZE // SLC_BLOCK)
            valid.append((lane_half == half) & past)
            n_new = n_new + jnp.where(past, 0, 1)
        valid.append(mine & (n_new > 0))
        kt = jnp.concatenate([blk_refs[kvh * SLC_TOPN + t][0] for t in range(SLC_TOPN)] + [kvst[k_rows, :]], axis=1)
        vt = jnp.concatenate([blk_refs[kvh * SLC_TOPN + t][1] for t in range(SLC_TOPN)] + [kvst[v_rows, :]], axis=1)
        o_k = attend(kt.astype(BF16), vt.astype(BF16), jnp.concatenate(valid, axis=1))
        o_s = jnp.where(head_kvh == kvh, o_k, o_s)
        kt = jnp.concatenate([win[k_rows, :], kvwt[k_rows, :]], axis=1)
        vt = jnp.concatenate([win[v_rows, :], kvwt[v_rows, :]], axis=1)
        valid_w = jnp.concatenate([jnp.full((1, win.shape[1]), True), mine], axis=1)
        o_k = attend(kt.astype(BF16), vt.astype(BF16), valid_w)
        o_w = jnp.where(head_kvh == kvh, o_k, o_w)

    oc = oc_ref[0]
    o_c = jnp.where(head_kvh == 0, oc[:, :HEAD_DIM], oc[:, HEAD_DIM:])
    gates = jnp.sum(jnp.where(mine, gt_ref[...], 0.0), axis=1, keepdims=True)
    o_ref[0] = (gates[0:N_HEADS] * o_c + gates[N_HEADS:2 * N_HEADS] * o_s + gates[2 * N_HEADS:3 * N_HEADS] * o_w)

    new_col = jnp.sum(jnp.where(mine, kvwt, 0.0), axis=1, keepdims=True)
    last = lax.broadcasted_iota(jnp.int32, win.shape, 1) == win.shape[1] - 1
    nwin_ref[0] = jnp.where(last, new_col, pltpu.roll(win, win.shape[1] - 1, 1))


def _attn_sample(idx, page_table, cache_t, q, o_c, gt, kvst, kvwt, win_t, *, n_past_slc):
    nb = q.shape[0]
    sub = PAGE_SIZE // SLC_BLOCK
    win_buf = win_t.shape[2]
    n_pages = page_table.shape[1]

    def blk_spec(kvh, t):
        def index(i, idx_ref, pt_ref):
            j = jnp.minimum(idx_ref[(i * KV_HEADS + kvh) * SLC_TOPN + t], n_past_slc - 1)
            return (pt_ref[i * n_pages + j // sub], 0, kvh, 0, 0)
        return pl.BlockSpec((None, 2, None, HEAD_DIM, PAGE_SIZE), index)

    per_b = lambda shape: pl.BlockSpec((1,) + shape, lambda i, a, c: (i,) + (0,) * len(shape))
    const = lambda a: pl.BlockSpec(a.shape, lambda i, x, c: (0,) * a.ndim)
    n_sel = KV_HEADS * SLC_TOPN
    return pl.pallas_call(
        functools.partial(_attn_sample_kernel, n_past_slc=n_past_slc),
        grid_spec=pltpu.PrefetchScalarGridSpec(
            num_scalar_prefetch=2,
            grid=(nb,),
            in_specs=[blk_spec(k, t) for k in range(KV_HEADS) for t in range(SLC_TOPN)] + [
                per_b((N_HEADS, HEAD_DIM)), per_b((N_HEADS, D_KV)), const(gt), const(kvst), const(kvwt),
                per_b((ROW_W, win_buf))],
            out_specs=[per_b((N_HEADS, HEAD_DIM)), per_b((ROW_W, win_buf))],
        ),
        out_shape=[jax.ShapeDtypeStruct((nb, N_HEADS, HEAD_DIM), F32), jax.ShapeDtypeStruct(win_t.shape, F32)],
        compiler_params=_cparams("arbitrary"),
        name="attn_sample",
    )(idx.reshape(-1), page_table.reshape(-1), *([cache_t] * n_sel), q, o_c, gt, kvst, kvwt, win_t)


def _conv_tail(y, b_ref, lg_ref, lb_ref):
    y = y + b_ref[...]
    yc = y - jnp.mean(y, axis=-1, keepdims=True)
    yn = yc * lax.rsqrt(jnp.mean(yc * yc, axis=-1, keepdims=True) + EPS)
    return _silu(yn * lg_ref[...] + lb_ref[...])


def _conv_prompt_kernel(u_ref, halo_ref, w_ref, b_ref, lg_ref, lb_ref, o_ref, buf_ref, shift_ref, *, halo):
    tm = u_ref.shape[1]
    j = pl.program_id(1)
    buf_ref[0:halo, :] = jnp.where(j > 0, halo_ref[0], 0.0)
    buf_ref[halo:halo + tm, :] = u_ref[0]
    buf_ref[halo + tm:, :] = jnp.zeros((8, buf_ref.shape[1]), F32)
    lead = halo - (CONV_WIDTH - 1)
    for ph in range(8):
        shift_ref[ph] = buf_ref[ph:ph + tm + halo, :]

    def chunk(c, carry):
        r0 = pl.multiple_of(c * CONV_ROWS, CONV_ROWS)
        y = None
        for w in range(CONV_WIDTH):
            o = lead + w
            tap = shift_ref[o % 8, pl.ds(r0 + o // 8 * 8, CONV_ROWS), :] * w_ref[w:w + 1, :]
            y = tap if y is None else y + tap
        o_ref[0, pl.ds(r0, CONV_ROWS), :] = _conv_tail(y, b_ref, lg_ref, lb_ref)
        return carry

    lax.fori_loop(0, tm // CONV_ROWS, chunk, 0, unroll=4)


def _conv_prompt(u, w_dw, b_dw, ln_g, ln_b, *, tm):
    b, s, dc = u.shape
    halo = 32
    per = tm // halo
    const = lambda a: pl.BlockSpec(a.shape, lambda i, j: (0,) * a.ndim)
    return pl.pallas_call(
        functools.partial(_conv_prompt_kernel, halo=halo),
        grid=(b, s // tm),
        in_specs=[pl.BlockSpec((1, tm, dc), lambda i, j: (i, j, 0)),
                  pl.BlockSpec((1, halo, dc), lambda i, j: (i, jnp.maximum(j * per - 1, 0), 0)),
                  const(w_dw), const(b_dw), const(ln_g), const(ln_b)],
        out_specs=pl.BlockSpec((1, tm, dc), lambda i, j: (i, j, 0)),
        out_shape=jax.ShapeDtypeStruct((b, s, dc), F32),
        scratch_shapes=[pltpu.VMEM((tm + halo + 8, dc), F32), pltpu.VMEM((8, tm + halo, dc), F32)],
        compiler_params=_cparams("arbitrary", "arbitrary"),
        name="conv_prompt",
    )(u, u, w_dw, b_dw, ln_g, ln_b)


def _conv_sample_kernel(st_ref, u_ref, w_ref, b_ref, lg_ref, lb_ref, o_ref):
    hist = CONV_WIDTH - 1
    y = u_ref[...] * w_ref[hist:hist + 1, :]
    for w in range(hist):
        y = y + st_ref[w] * w_ref[w:w + 1, :]
    o_ref[...] = _conv_tail(y, b_ref, lg_ref, lb_ref)


def _conv_sample(state_t, u, w_dw, b_dw, ln_g, ln_b, *, sb):
    nb, dc = u.shape
    hist = state_t.shape[0]
    const = lambda a: pl.BlockSpec(a.shape, lambda i: (0,) * a.ndim)
    return pl.pallas_call(
        _conv_sample_kernel,
        grid=(nb // sb,),
        in_specs=[pl.BlockSpec((hist, sb, dc), lambda i: (0, i, 0)), pl.BlockSpec((sb, dc), lambda i: (i, 0)),
                  const(w_dw), const(b_dw), const(ln_g), const(ln_b)],
        out_specs=pl.BlockSpec((sb, dc), lambda i: (i, 0)),
        out_shape=jax.ShapeDtypeStruct((nb, dc), F32),
        compiler_params=_cparams("arbitrary"),
        name="conv_sample",
    )(state_t, u, w_dw, b_dw, ln_g, ln_b)


def _finish_kernel(x_ref, oa_ref, ocv_ref, gate1_ref, shift2_ref, scale2_ref, ga_ref, gc_ref, wout_ref, g2_ref,
                   wr_ref, br_ref, x1_ref, h2_ref, comb_ref, route_ref, rows_ref, count_ref):
    mixed = jnp.concatenate([_rms(oa_ref[0], ga_ref[...]), _rms(ocv_ref[0], gc_ref[...])], axis=1)
    mix = _mm(mixed.astype(BF16), wout_ref[...])
    x1 = x_ref[0] + gate1_ref[0] * mix
    x1_ref[0] = x1
    h2 = _rms(x1, g2_ref[...]) * (1.0 + scale2_ref[0]) + shift2_ref[0]
    h2_ref[0] = h2.astype(BF16)
    rows_ref[0] = h2

    h_hi, h_mid, _ = _split3(h2)
    w_hi, w_mid, _ = _split3(wr_ref[...])
    lg = (_mm(h_hi, w_hi) + _mm(h_hi, w_mid)
          + _mm(h_mid, w_hi)) + br_ref[...]
    lane = lax.broadcasted_iota(jnp.int32, lg.shape, 1)
    lane_f = lane.astype(F32)
    is_group = (lane >= N_EXPERTS) & (lane < N_EXPERTS + N_GROUPS)
    gl = jnp.where(is_group, lg, NEG_INF)
    g_max = jnp.max(gl, axis=1, keepdims=True)
    p_top = 1.0 / jnp.sum(jnp.where(is_group, jnp.exp(gl - g_max), 0.0), axis=1, keepdims=True)
    g_lane = jnp.min(jnp.where(gl == g_max, lane_f, 1e9), axis=1, keepdims=True)
    in_group = (lane < N_EXPERTS) & ((lane // EPG).astype(F32) == g_lane - N_EXPERTS)
    el = jnp.where(in_group, lg, NEG_INF)
    l1 = jnp.max(el, axis=1, keepdims=True)
    i1 = jnp.min(jnp.where(el == l1, lane_f, 1e9), axis=1, keepdims=True)
    el2 = jnp.where(lane_f == i1, NEG_INF, el)
    l2 = jnp.max(el2, axis=1, keepdims=True)
    i2 = jnp.min(jnp.where(el2 == l2, lane_f, 1e9), axis=1, keepdims=True)
    r = jnp.exp(l2 - l1)
    w1 = p_top / (1.0 + r)
    w2 = p_top * r / (1.0 + r)
    comb_ref[0] = jnp.where(lane_f == i1, w1, jnp.where(lane_f == i2, w2, 0.0))
    route_ref[0] = jnp.where(lane == 0, i1, jnp.where(lane == 1, i2, jnp.where(lane == 2, w1, jnp.where(
        lane == 3, w2, 0.0))))

    @pl.when((pl.program_id(0) == 0) & (pl.program_id(1) == 0))
    def _():
        count_ref[...] = jnp.zeros_like(count_ref)

    picks = jnp.where(lane_f == i1, 1.0, 0.0) + jnp.where(lane_f == i2, 1.0, 0.0)
    count_ref[...] += jnp.sum(picks, axis=0, keepdims=True)


def _finish(x, o_attn, o_conv, gate1, shift2, scale2, ga, gc, w_out, g2, w_route, b_route, *, tm):
    b, s, d = x.shape
    r = gate1.shape[1]
    mod_block = (1, 1, d) if r == 1 else (1, tm, d)
    mod_map = (lambda i, j: (i, 0, 0)) if r == 1 else (lambda i, j: (i, j, 0))
    mod = pl.BlockSpec(mod_block, mod_map)
    row = lambda w: pl.BlockSpec((1, tm, w), lambda i, j: (i, j, 0))
    const = lambda a: pl.BlockSpec(a.shape, lambda i, j: (0,) * a.ndim)
    return pl.pallas_call(
        _finish_kernel,
        grid=(b, s // tm),
        in_specs=[row(d), row(D_ATTN), row(D_CONV), mod, mod, mod, const(ga), const(gc), const(w_out), const(g2),
                  const(w_route), const(b_route)],
        out_specs=[row(d), row(d), row(128), row(128), row(d), pl.BlockSpec((8, 128), lambda i, j: (0, 0))],
        out_shape=[jax.ShapeDtypeStruct((b, s, d), F32), jax.ShapeDtypeStruct((b, s, d), BF16),
                   jax.ShapeDtypeStruct((b, s, 128), F32), jax.ShapeDtypeStruct((b, s, 128), F32),
                   jax.ShapeDtypeStruct((b, s, d), F32), jax.ShapeDtypeStruct((8, 128), F32)],
        compiler_params=_cparams("arbitrary", "arbitrary"),
        name="finish",
    )(x, o_attn, o_conv, gate1, shift2, scale2, ga, gc, w_out, g2, w_route, b_route)


def _moe_kernel(h_ref, comb_ref, x1_ref, gate2_ref, wg_ref, wu_ref, wd_ref, gf_ref, y_ref, acc_ref):
    e = pl.program_id(2)

    @pl.when(e == 0)
    def _():
        acc_ref[...] = jnp.zeros_like(acc_ref)

    h = h_ref[0]
    a = _mm(h, wg_ref[0].astype(BF16))
    u = _mm(h, wu_ref[0].astype(BF16))
    comb = comb_ref[0]
    lane = lax.broadcasted_iota(jnp.int32, comb.shape, 1)
    cw = jnp.sum(jnp.where(lane == e, comb, 0.0), axis=1, keepdims=True)
    hid = (_silu(a) * u * cw).astype(BF16)
    acc_ref[...] += _mm(hid, wd_ref[0].astype(BF16))

    @pl.when(e == pl.num_programs(2) - 1)
    def _():
        y = x1_ref[0] + gate2_ref[0] * acc_ref[...]
        y_ref[0] = _rms(y, gf_ref[...])


def _moe(h2, comb, x1, gate2, w_gate, w_up, w_down, final_g, *, tm):
    b, s, d = x1.shape
    r = gate2.shape[1]
    mod_block = (1, 1, d) if r == 1 else (1, tm, d)
    mod_map = (lambda i, j, e: (i, 0, 0)) if r == 1 else (lambda i, j, e: (i, j, 0))
    row = lambda w: pl.BlockSpec((1, tm, w), lambda i, j, e: (i, j, 0))
    ne, _, de = w_gate.shape
    return pl.pallas_call(
        _moe_kernel,
        grid=(b, s // tm, ne),
        in_specs=[row(d), row(128), row(d), pl.BlockSpec(mod_block, mod_map),
                  pl.BlockSpec((1, d, de), lambda i, j, e: (e, 0, 0)),
                  pl.BlockSpec((1, d, de), lambda i, j, e: (e, 0, 0)),
                  pl.BlockSpec((1, de, d), lambda i, j, e: (e, 0, 0)),
                  pl.BlockSpec((1, d), lambda i, j, e: (0, 0))],
        out_specs=row(d),
        out_shape=jax.ShapeDtypeStruct((b, s, d), F32),
        scratch_shapes=[pltpu.VMEM((tm, d), F32)],
        compiler_params=_cparams("arbitrary", "arbitrary", "arbitrary"),
        name="moe",
    )(h2, comb, x1, gate2, w_gate, w_up, w_down, final_g)


def _moe_slots_kernel(route_ref, base_ref, slot_ref, run_ref):
    @pl.when(pl.program_id(0) == 0)
    def _():
        run_ref[...] = jnp.zeros_like(run_ref)

    route = route_ref[...]
    tm = route.shape[0]
    lane = lax.broadcasted_iota(jnp.int32, route.shape, 1).astype(F32)
    first = jnp.where(lane == route[:, 0:1], 1.0, 0.0)
    second = jnp.where(lane == route[:, 1:2], 1.0, 0.0)
    picks = first + second
    earlier = lax.broadcasted_iota(jnp.int32, (tm, tm), 1) < lax.broadcasted_iota(jnp.int32, (tm, tm), 0)
    seen = _mm(jnp.where(earlier, 1.0, 0.0).astype(BF16), picks.astype(BF16)) + (run_ref[...] + base_ref[...])
    slot0 = jnp.sum(first * seen, axis=1, keepdims=True)
    slot1 = jnp.sum(second * seen, axis=1, keepdims=True)
    slot_ref[...] = jnp.where(lane == 0.0, slot0, jnp.where(lane == 1.0, slot1, 0.0)).astype(jnp.int32)
    run_ref[...] += jnp.sum(picks, axis=0, keepdims=True)


def _moe_slots(route, base, *, tm):
    n = route.shape[0]
    return pl.pallas_call(
        _moe_slots_kernel,
        grid=(n // tm,),
        in_specs=[pl.BlockSpec((tm, 128), lambda j: (j, 0)), pl.BlockSpec((1, 128), lambda j: (0, 0))],
        out_specs=pl.BlockSpec((tm, 128), lambda j: (j, 0)),
        out_shape=jax.ShapeDtypeStruct((n, 128), jnp.int32),
        scratch_shapes=[pltpu.VMEM((1, 128), F32)],
        compiler_params=_cparams("arbitrary"),
        name="moe_slots",
    )(route, base)


def _row_copy(src_ref, src_row, dst_ref, dst_row, sem):
    return pltpu.make_async_copy(src_ref.at[pl.ds(src_row, 1), :], dst_ref.at[pl.ds(dst_row, 1), :], sem)


def _moe_scatter_kernel(s0_ref, s1_ref, h_ref, init_ref, sorted_ref, sem):
    del init_ref
    j = pl.program_id(0)
    tm = h_ref.shape[0]

    def start(r, carry):
        _row_copy(h_ref, r, sorted_ref, s0_ref[j * tm + r], sem).start()
        _row_copy(h_ref, r, sorted_ref, s1_ref[j * tm + r], sem).start()
        return carry

    def wait(r, carry):
        _row_copy(h_ref, 0, sorted_ref, 0, sem).wait()
        _row_copy(h_ref, 0, sorted_ref, 0, sem).wait()
        return carry

    lax.fori_loop(0, tm, start, 0, unroll=8)
    lax.fori_loop(0, tm, wait, 0, unroll=8)


def _moe_scatter(slot0, slot1, rows, n_slots, *, tm):
    n, w = rows.shape
    return pl.pallas_call(
        _moe_scatter_kernel,
        grid_spec=pltpu.PrefetchScalarGridSpec(
            num_scalar_prefetch=2,
            grid=(n // tm,),
            in_specs=[pl.BlockSpec((tm, w), lambda j, a, b: (j, 0)), pl.BlockSpec(memory_space=pl.ANY)],
            out_specs=pl.BlockSpec(memory_space=pl.ANY),
            scratch_shapes=[pltpu.SemaphoreType.DMA(())],
        ),
        out_shape=jax.ShapeDtypeStruct((n_slots, w), rows.dtype),
        input_output_aliases={3: 0},
        compiler_params=pltpu.CompilerParams(dimension_semantics=("arbitrary",), vmem_limit_bytes=VMEM_LIMIT,
                                             disable_bounds_checks=True),
        name="moe_scatter",
    )(slot0, slot1, rows, jnp.zeros((n_slots, w), rows.dtype))


def _moe_experts_kernel(te_ref, na_ref, x_ref, wg_ref, wu_ref, wd_ref, o_ref):
    t = pl.program_id(0)

    @pl.when(t < na_ref[0])
    def _():
        xb = x_ref[...].astype(BF16)
        a = _mm(xb, wg_ref[0].astype(BF16))
        u = _mm(xb, wu_ref[0].astype(BF16))
        o_ref[...] = _mm((_silu(a) * u).astype(BF16), wd_ref[0].astype(BF16))

    @pl.when(t >= na_ref[0])
    def _():
        o_ref[...] = jnp.zeros_like(o_ref)


def _moe_experts(tile_expert, n_active, sorted_x, w_gate, w_up, w_down):
    n_slots, w = sorted_x.shape
    ne, d, de = w_gate.shape
    x_map = lambda t, te, na: (jnp.minimum(t, na[0] - 1), 0)
    w_map = lambda t, te, na: (te[t], 0, 0)
    return pl.pallas_call(
        _moe_experts_kernel,
        grid_spec=pltpu.PrefetchScalarGridSpec(
            num_scalar_prefetch=2,
            grid=(n_slots // MOE_TILE,),
            in_specs=[pl.BlockSpec((MOE_TILE, w), x_map), pl.BlockSpec((1, d, de), w_map),
                      pl.BlockSpec((1, d, de), w_map), pl.BlockSpec((1, de, d), w_map)],
            out_specs=pl.BlockSpec((MOE_TILE, d), lambda t, te, na: (t, 0)),
        ),
        out_shape=jax.ShapeDtypeStruct((n_slots, d), F32),
        compiler_params=_cparams("arbitrary"),
        name="moe_experts",
    )(tile_expert, n_active, sorted_x, w_gate, w_up, w_down)


def _moe_combine_kernel(s0_ref, s1_ref, y_hbm, route_ref, x1_ref, gate2_ref, gf_ref, o_ref, a_ref, b_ref, sem):
    j = pl.program_id(0)
    tm = x1_ref.shape[0]

    def issue(step, slot):
        def body(r, carry):
            _row_copy(y_hbm, s0_ref[step * tm + r], a_ref.at[slot], r, sem.at[slot]).start()
            _row_copy(y_hbm, s1_ref[step * tm + r], b_ref.at[slot], r, sem.at[slot]).start()
            return carry
        lax.fori_loop(0, tm, body, 0, unroll=8)

    @pl.when(j == 0)
    def _():
        issue(0, 0)

    @pl.when(j + 1 < pl.num_programs(0))
    def _():
        issue(j + 1, (j + 1) % 2)

    slot = j % 2

    def wait(r, carry):
        _row_copy(y_hbm, 0, a_ref.at[slot], 0, sem.at[slot]).wait()
        _row_copy(y_hbm, 0, b_ref.at[slot], 0, sem.at[slot]).wait()
        return carry
    lax.fori_loop(0, tm, wait, 0, unroll=8)

    route = route_ref[...]
    moe = route[:, 2:3] * a_ref[slot] + route[:, 3:4] * b_ref[slot]
    o_ref[...] = _rms(x1_ref[...] + gate2_ref[0] * moe, gf_ref[...])


def _moe_combine(slot0, slot1, y_sorted, route, x1, gate2, final_g, *, tm, rows_per_mod):
    n, d = x1.shape
    return pl.pallas_call(
        _moe_combine_kernel,
        grid_spec=pltpu.PrefetchScalarGridSpec(
            num_scalar_prefetch=2,
            grid=(n // tm,),
            in_specs=[pl.BlockSpec(memory_space=pl.ANY), pl.BlockSpec((tm, 128), lambda j, a, b: (j, 0)),
                      pl.BlockSpec((tm, d), lambda j, a, b: (j, 0)),
                      pl.BlockSpec((1, 1, d), lambda j, a, b: (j * tm // rows_per_mod, 0, 0)),
                      pl.BlockSpec((1, d), lambda j, a, b: (0, 0))],
            out_specs=pl.BlockSpec((tm, d), lambda j, a, b: (j, 0)),
            scratch_shapes=[pltpu.VMEM((2, tm, d), F32), pltpu.VMEM((2, tm, d), F32), pltpu.SemaphoreType.DMA((2,))],
        ),
        out_shape=jax.ShapeDtypeStruct((n, d), F32),
        compiler_params=pltpu.CompilerParams(dimension_semantics=("arbitrary",), vmem_limit_bytes=VMEM_LIMIT,
                                             disable_bounds_checks=True),
        name="moe_combine",
    )(slot0, slot1, y_sorted, route, x1, gate2, final_g)


def _moe_sorted(route, counts, rows, x1, gate2, w_gate, w_up, w_down, final_g):
    b, s, d = x1.shape
    n = b * s
    ne = w_gate.shape[0]
    n_tiles = 2 * n // MOE_TILE + ne
    cnt = counts[0, :ne].astype(jnp.int32)
    padded = (cnt + MOE_TILE - 1) // MOE_TILE * MOE_TILE
    ends = jnp.sum(jnp.where(jnp.arange(ne)[:, None] <= jnp.arange(ne)[None, :], padded[:, None], 0), axis=0)
    base = jnp.pad((ends - padded).astype(F32), (0, 128 - ne)).reshape(1, 128)
    n_active = (ends[-1] // MOE_TILE).reshape(1)
    tile_start = jnp.arange(n_tiles, dtype=jnp.int32) * MOE_TILE
    tile_expert = jnp.minimum(jnp.sum((ends[None, :] <= tile_start[:, None]).astype(jnp.int32), axis=1), ne - 1)

    route2 = route.reshape(n, 128)
    slots = _moe_slots(route2, base, tm=512)
    slot0, slot1 = slots[:, 0], slots[:, 1]
    sorted_x = _moe_scatter(slot0, slot1, rows.reshape(n, d), n_tiles * MOE_TILE, tm=512)
    y_sorted = _moe_experts(tile_expert, n_active, sorted_x, w_gate, w_up, w_down)
    y = _moe_combine(slot0, slot1, y_sorted, route2, x1.reshape(n, d), gate2, final_g, tm=256, rows_per_mod=s)
    return y.reshape(b, s, d)


def _prep_w_in(w_in):
    o_gl = D_ATTN + 3 * ROW_W
    wqkv = w_in[:, :o_gl].astype(BF16)
    wgl = w_in[:, o_gl:o_gl + 3 * N_HEADS]
    pad = jnp.zeros((w_in.shape[0], 128 - 3 * N_HEADS), w_in.dtype)
    wgl = jnp.concatenate([wgl.reshape(-1, N_HEADS, 3).transpose(0, 2, 1).reshape(-1, 3 * N_HEADS), pad], axis=1)
    wu = w_in[:, o_gl + 3 * N_HEADS:].astype(BF16)
    return wqkv, wgl.astype(BF16), wu


def _prep_compress(w_cmp1, pos_cmp, w_cmp2):
    ratio = CMP_BLOCK // CMP_STRIDE
    eye = jnp.eye(KV_HEADS, dtype=w_cmp1.dtype)
    w1 = w_cmp1.reshape(2, ratio, CMP_STRIDE, HEAD_DIM, HEAD_DIM)
    w1big = jnp.einsum('crsdf,kj->cskdrjf', w1, eye).reshape(2, CMP_STRIDE * D_KV, ratio * D_KV).astype(BF16)
    w2big = jnp.einsum('cfd,kj->ckfjd', w_cmp2, eye).reshape(2, D_KV, D_KV).astype(BF16)
    pos = pos_cmp.reshape(2, ratio, CMP_STRIDE, 1, HEAD_DIM)
    pos = jnp.broadcast_to(pos, (2, ratio, CMP_STRIDE, KV_HEADS, HEAD_DIM)).reshape(2, ratio, CMP_STRIDE * D_KV)
    posrows = jnp.concatenate([pos, jnp.zeros((2, 8 - ratio, CMP_STRIDE * D_KV), pos.dtype)], axis=1).astype(BF16)
    return w1big, w2big, posrows


def _prep_router(w_group, b_group, w_router, b_router):
    d = w_group.shape[0]
    pad = 128 - N_EXPERTS - N_GROUPS
    w = jnp.concatenate([w_router, w_group, jnp.zeros((d, pad), w_group.dtype)], axis=1)
    b = jnp.concatenate([b_router, b_group, jnp.zeros((pad,), b_group.dtype)]).reshape(1, 128)
    return w, b


def kernel(x_prompt, x_sample, cache_cmp_kv, cache_slc_kv, state_win_kv, state_conv, page_table, c_prompt, c_sample,
           norm1_g, w_ada, b_ada, w_in, w_cmp1, pos_cmp, w_cmp2, w_dw, b_dw, conv_ln_g, conv_ln_b, g_attn_out,
           g_conv_out, w_out, norm2_g, w_group, b_group, w_router, b_router, w_gate, w_up, w_down, final_g):
    depth = norm1_g.shape[0]
    assert depth == 1 and x_sample.shape[1] == 1
    bp, seq, d = x_prompt.shape
    nb = x_sample.shape[0]
    n_pages = page_table.shape[1]
    past_len = n_pages * PAGE_SIZE
    n_past_slc = past_len // SLC_BLOCK
    win_buf = state_win_kv.shape[2]
    assert win_buf == WINDOW and seq % K_TILE == 0 and seq >= WINDOW + Q_TILE
    l = 0
    row2 = lambda a: a.reshape(1, -1)

    mods = _adaln(jnp.concatenate([c_prompt, c_sample], axis=0), w_ada[l], b_ada[l]).reshape(bp + nb, 6, d)
    mods_p = [mods[:bp, i][:, None, :] for i in range(6)]
    mods_s = [mods[bp:, i][None, :, :] for i in range(6)]

    wqkv, wgl, wu = _prep_w_in(w_in[l])
    w1big, w2big, posrows = _prep_compress(w_cmp1[l], pos_cmp[l], w_cmp2[l])
    w_route, b_route = _prep_router(w_group[l], b_group[l], w_router[l], b_router[l])
    w_out_b = w_out[l].astype(BF16)
    conv_args = (w_dw[l], row2(b_dw[l]), row2(conv_ln_g[l]), row2(conv_ln_b[l]))

    def kv_rows_from_t(a_t):
        n, _, t = a_t.shape
        return a_t.reshape(n, 2, KV_HEADS, HEAD_DIM, t).transpose(0, 4, 1, 2, 3)

    def kv_rows_to_t(a):
        return a.transpose(0, 2, 3, 4, 1)

    qt, kvc, kvct, kvst, kvwt, ks, vts, kw, vtw, gt, u = _mixer(
        x_prompt, mods_p[0], mods_p[1], row2(norm1_g[l]), wqkv, wgl, wu, tm=512, prompt=True)
    kc, vct = _compress_prompt(kvc, w1big, w2big, posrows)
    o_attn = _attn_prompt(qt, kc, vct, ks, vts, kw, vtw, gt)
    o_conv = _conv_prompt(u, *conv_args, tm=512)
    x1, _, _, route, moe_rows, counts = _finish(x_prompt, o_attn, o_conv, mods_p[2], mods_p[3], mods_p[4],
                                             row2(g_attn_out[l]), row2(g_conv_out[l]), w_out_b, row2(norm2_g[l]),
                                             w_route, b_route, tm=512)
    y_prompt = _moe_sorted(route, counts, moe_rows, x1, mods_p[5], w_gate[l], w_up[l], w_down[l], row2(final_g))

    new_cmp_prompt = kv_rows_from_t(kvct)[None]
    new_slc_prompt = kv_rows_from_t(kvst)[None]
    new_win_prompt = kv_rows_from_t(kvwt[:, :, seq - WINDOW:])[None]
    new_conv_prompt = u[:, seq - (CONV_WIDTH - 1):][None]

    xs = x_sample.reshape(1, nb, d)
    q_s, kvct_s, kvst_s, kvwt_s, gt_s, u_s = _mixer(
        xs, mods_s[0], mods_s[1], row2(norm1_g[l]), wqkv, wgl, wu, tm=nb, prompt=False)
    cmp_t = kv_rows_to_t(cache_cmp_kv[l])
    kc_s, vc_s = _compress_sample(cmp_t.reshape(cmp_t.shape[0], ROW_W, PAGE_SIZE), page_table, w1big, w2big, posrows)

    q4 = q_s.reshape(nb, KV_HEADS, GQA, HEAD_DIM)
    zq = jnp.zeros_like(q4)
    kvh_id = jnp.arange(KV_HEADS).reshape(1, KV_HEADS, 1, 1)
    qpad = jnp.concatenate([jnp.where(kvh_id == 0, q4, zq), jnp.where(kvh_id == 1, q4, zq)],
                           axis=-1).reshape(nb, N_HEADS, D_KV)
    o_c, idx = _attn_sample_cmp(qpad, kc_s, vc_s, q_pos=past_len, n_slc=n_past_slc + 1)
    idx = idx[:, :KV_HEADS, :SLC_TOPN]
    win_t = kv_rows_to_t(state_win_kv[l]).reshape(nb, ROW_W, win_buf)
    o_heads, new_win_t = _attn_sample(idx, page_table, kv_rows_to_t(cache_slc_kv[l]), q_s.reshape(nb, N_HEADS, HEAD_DIM),
                                      o_c, gt_s[0], kvst_s[0], kvwt_s[0], win_t, n_past_slc=n_past_slc)
    o_attn_s = o_heads.reshape(1, nb, D_ATTN)
    state_t = state_conv[l].transpose(1, 0, 2)
    u_rows = u_s.reshape(nb, D_CONV)
    o_conv_s = _conv_sample(state_t, u_rows, *conv_args, sb=8).reshape(1, nb, D_CONV)
    x1_s, h2_s, comb_s, _, _, _ = _finish(xs, o_attn_s, o_conv_s, mods_s[2], mods_s[3], mods_s[4],
                                          row2(g_attn_out[l]), row2(g_conv_out[l]), w_out_b, row2(norm2_g[l]),
                                          w_route, b_route, tm=nb)
    y_sample = _moe(h2_s, comb_s, x1_s, mods_s[5], w_gate[l], w_up[l], w_down[l], row2(final_g), tm=nb)

    row_shape = (1, nb, 1, 2, KV_HEADS, HEAD_DIM)
    new_cmp_sample = kvct_s[0].T.reshape(row_shape)
    new_slc_sample = kvst_s[0].T.reshape(row_shape)
    new_win_sample = kv_rows_from_t(new_win_t)[None]
    new_conv_sample = jnp.concatenate([state_t[1:], u_rows[None]], axis=0).transpose(1, 0, 2)[None]

    return (y_prompt, y_sample.reshape(nb, 1, d), new_cmp_prompt, new_slc_prompt, new_win_prompt, new_conv_prompt,
            new_cmp_sample, new_slc_sample, new_win_sample, new_conv_sample)
```

```python
import functools

import jax
import jax.numpy as jnp
from jax import lax
from jax.experimental import pallas as pl
from jax.experimental.pallas import tpu as pltpu

F32 = jnp.float32
BF16 = jnp.bfloat16

D_MODEL = 1024
N_HEADS = 8
HEAD_DIM = 64
KV_HEADS = 2
GQA = N_HEADS // KV_HEADS
D_ATTN = N_HEADS * HEAD_DIM
D_CONV = D_MODEL - D_ATTN
D_KV = KV_HEADS * HEAD_DIM
CMP_BLOCK = 32
CMP_STRIDE = 16
SLC_BLOCK = 64
SLC_TOPN = 16
N_LOCAL_BLOCKS = 2
WINDOW = 512
FORCED_SCORE = 1e4
CONV_WIDTH = 31
N_GROUPS = 4
EPG = 8
N_EXPERTS = N_GROUPS * EPG
D_EXPERT = 256
PAGE_SIZE = 128
EPS = 1e-6
NEG_INF = -1e30
SCALE = HEAD_DIM ** -0.5
LOG2E = 1.4426950408889634
ROW_W = 2 * D_KV
CHUNK_W = CMP_STRIDE * ROW_W
CHUNK_PITCH = 20
MOE_TILE = 256
CONV_ROWS = 32
Q_TILE = 128
K_TILE = 512
VMEM_LIMIT = 48 * 1024 * 1024


def _cparams(*sem):
    return pltpu.CompilerParams(dimension_semantics=sem, vmem_limit_bytes=VMEM_LIMIT)


def _rms(x, g):
    return x * lax.rsqrt(jnp.mean(x * x, axis=-1, keepdims=True) + EPS) * g


def _silu(x):
    return x * jax.nn.sigmoid(x)


def _mm(a, b):
    return jnp.dot(a, b, preferred_element_type=F32)


def _mm_nt(a, b):
    return lax.dot_general(a, b, (((1,), (1,)), ((), ())), preferred_element_type=F32)


def _split3(x):
    hi = x.astype(BF16)
    r = x - hi.astype(F32)
    mid = r.astype(BF16)
    lo = (r - mid.astype(F32)).astype(BF16)
    return hi, mid, lo


def _softmax_masked(s, mask, axis):
    s = jnp.where(mask, s, NEG_INF)
    m = jnp.max(s, axis=axis, keepdims=True)
    e = jnp.exp(s - m)
    p = e / jnp.sum(e, axis=axis, keepdims=True)
    return jnp.where(mask, p, 0.0)


def _adaln_kernel(c_ref, w_ref, b_ref, o_ref):
    s = _silu(c_ref[...]).astype(BF16)
    o_ref[...] = _mm(s, w_ref[...].astype(BF16)) + b_ref[...]


def _adaln(c_all, w_ada, b_ada):
    n, d = c_all.shape
    nout = w_ada.shape[1]
    tn = 1024
    return pl.pallas_call(
        _adaln_kernel,
        grid=(nout // tn,),
        in_specs=[pl.BlockSpec((n, d), lambda j: (0, 0)),
                  pl.BlockSpec((d, tn), lambda j: (0, j)),
                  pl.BlockSpec((1, tn), lambda j: (0, j))],
        out_specs=pl.BlockSpec((n, tn), lambda j: (0, j)),
        out_shape=jax.ShapeDtypeStruct((n, nout), F32),
        compiler_params=_cparams("arbitrary"),
        name="adaln",
    )(c_all, w_ada, b_ada.reshape(1, nout))


def _mixer_kernel(x_ref, shift_ref, scale_ref, g_ref, wqkv_ref, wgl_ref, wu_ref, *outs, prompt):
    x = x_ref[0]
    h = _rms(x, g_ref[...]) * (1.0 + scale_ref[0]) + shift_ref[0]
    hb = h.astype(BF16)
    p = _mm(hb, wqkv_ref[...])
    gl = _mm(hb, wgl_ref[...])
    pu = _mm(hb, wu_ref[...])
    u = pu[:, :D_CONV] * jax.nn.sigmoid(pu[:, D_CONV:])
    q = p[:, :D_ATTN] * (SCALE * LOG2E if prompt else SCALE)
    o = D_ATTN
    kvc = p[:, o:o + ROW_W]
    kvs = p[:, o + ROW_W:o + 2 * ROW_W]
    kvw = p[:, o + 2 * ROW_W:o + 3 * ROW_W]
    kvs_t = kvs.T
    kvw_t = kvw.T
    if prompt:
        q_ref, kvc_ref, kvct_ref, kvst_ref, kvwt_ref, ks_ref, vts_ref, kw_ref, vtw_ref, gt_ref, u_ref = outs
        q_ref[0] = q.T.astype(BF16)
        kvc_ref[0] = kvc
        ks_ref[0] = kvs[:, :D_KV].astype(BF16)
        vts_ref[0] = kvs_t[D_KV:, :].astype(BF16)
        kw_ref[0] = kvw[:, :D_KV].astype(BF16)
        vtw_ref[0] = kvw_t[D_KV:, :].astype(BF16)
    else:
        q_ref, kvct_ref, kvst_ref, kvwt_ref, gt_ref, u_ref = outs
        q_ref[0] = q
    kvct_ref[0] = kvc.T
    kvst_ref[0] = kvs_t
    kvwt_ref[0] = kvw_t
    gt_ref[0] = jax.nn.sigmoid(gl).T[:32, :]
    u_ref[0] = u


def _mixer(x, shift, scale, g, wqkv, wgl, wu, *, tm, prompt):
    b, s, d = x.shape
    r = shift.shape[1]
    mod_block = (1, 1, d) if r == 1 else (1, tm, d)
    mod_map = (lambda i, j: (i, 0, 0)) if r == 1 else (lambda i, j: (i, j, 0))
    row = lambda w: pl.BlockSpec((1, tm, w), lambda i, j: (i, j, 0))
    col = lambda w: pl.BlockSpec((1, w, tm), lambda i, j: (i, 0, j))
    const = lambda a: pl.BlockSpec(a.shape, lambda i, j: (0,) * a.ndim)
    rows = lambda w, dt: jax.ShapeDtypeStruct((b, s, w), dt)
    cols = lambda w, dt: jax.ShapeDtypeStruct((b, w, s), dt)
    if prompt:
        out_specs = [col(D_ATTN), row(ROW_W), col(ROW_W), col(ROW_W), col(ROW_W), row(D_KV), col(D_KV), row(D_KV),
                     col(D_KV), col(32), row(D_CONV)]
        out_shape = [cols(D_ATTN, BF16), rows(ROW_W, F32), cols(ROW_W, F32), cols(ROW_W, F32), cols(ROW_W, F32),
                     rows(D_KV, BF16), cols(D_KV, BF16), rows(D_KV, BF16), cols(D_KV, BF16), cols(32, F32),
                     rows(D_CONV, F32)]
    else:
        out_specs = [row(D_ATTN), col(ROW_W), col(ROW_W), col(ROW_W), col(32), row(D_CONV)]
        out_shape = [rows(D_ATTN, F32), cols(ROW_W, F32), cols(ROW_W, F32), cols(ROW_W, F32), cols(32, F32),
                     rows(D_CONV, F32)]
    return pl.pallas_call(
        functools.partial(_mixer_kernel, prompt=prompt),
        grid=(b, s // tm),
        in_specs=[row(d), pl.BlockSpec(mod_block, mod_map), pl.BlockSpec(mod_block, mod_map),
                  const(g), const(wqkv), const(wgl), const(wu)],
        out_specs=out_specs,
        out_shape=out_shape,
        compiler_params=_cparams("arbitrary", "arbitrary"),
        name="mixer_prompt" if prompt else "mixer_sample",
    )(x, shift, scale, g, wqkv, wgl, wu)


def _chunk_part(load_offset, pos_rows, w1_c):
    xc = jnp.concatenate([load_offset(s) for s in range(CMP_STRIDE)], axis=1).astype(BF16)
    n = xc.shape[0]
    both = _mm(jnp.concatenate([xc, pos_rows], axis=0), w1_c)
    return both[:n], both[n:]


def _compress_finish(part, posb, w2_c):
    n = part.shape[0]
    nxt = pltpu.roll(part[:, D_KV:], n - 1, 0)
    pre = part[:, :D_KV] + nxt + posb[0:1, :D_KV] + posb[1:2, D_KV:]
    return _mm(_silu(pre).astype(BF16), w2_c)


def _compress_prompt_kernel(x_ref, w1_ref, w2_ref, pos_ref, kc_ref, vct_ref):
    for c in range(2):
        part, posb = _chunk_part(lambda s: x_ref[0, :, s * ROW_W + c * D_KV:s * ROW_W + (c + 1) * D_KV], pos_ref[c],
                                 w1_ref[c])
        out = _compress_finish(part, posb, w2_ref[c])
        if c == 0:
            kc_ref[0] = out.astype(BF16)
        else:
            vct_ref[0] = out.T.astype(BF16)


def _compress_prompt(kvc, w1big, w2big, posrows):
    b, s, _ = kvc.shape
    n = s // CMP_STRIDE
    x = kvc.reshape(b, n, CHUNK_W)
    const = lambda a: pl.BlockSpec(a.shape, lambda i: (0,) * a.ndim)
    return pl.pallas_call(
        _compress_prompt_kernel,
        grid=(b,),
        in_specs=[pl.BlockSpec((1, n, CHUNK_W), lambda i: (i, 0, 0)), const(w1big), const(w2big), const(posrows)],
        out_specs=[pl.BlockSpec((1, n, D_KV), lambda i: (i, 0, 0)), pl.BlockSpec((1, D_KV, n), lambda i: (i, 0, 0))],
        out_shape=[jax.ShapeDtypeStruct((b, n, D_KV), BF16), jax.ShapeDtypeStruct((b, D_KV, n), BF16)],
        compiler_params=_cparams("arbitrary"),
        name="compress_prompt",
    )(x, w1big, w2big, posrows)


def _compress_sample_kernel(pt_ref, cache_ref, q_ref, w1_ref, w2_ref, pos_ref, oc_ref, idx_ref, pages_ref, rows0_ref,
                            rows1_ref, sem, *, pages, q_pos, n_slc):
    i = pl.program_id(0)
    row_refs = (rows0_ref, rows1_ref)
    cpp = PAGE_SIZE // CMP_STRIDE
    n = pages * cpp

    def page_copy(seq, k, buf):
        return pltpu.make_async_copy(cache_ref.at[pt_ref[seq * pages + k]], pages_ref.at[buf, k], sem.at[buf])

    def fetch(seq, buf):
        for k in range(pages):
            page_copy(seq, k, buf).start()

    @pl.when(i == 0)
    def _():
        fetch(0, 0)

    @pl.when(i + 1 < pl.num_programs(0))
    def _():
        fetch(i + 1, (i + 1) % 2)

    buf = i % 2
    for k in range(pages):
        page_copy(i, k, buf).wait()

    kv = []
    for c in range(2):
        for k in range(pages):
            rows = pages_ref[buf, k, c * D_KV:(c + 1) * D_KV, :].astype(BF16).T.astype(F32)
            for j in range(cpp):
                r0 = (k * cpp + j) * CHUNK_PITCH
                row_refs[c][r0:r0 + CMP_STRIDE, :] = rows[j * CMP_STRIDE:(j + 1) * CMP_STRIDE, :]
    for c in range(2):
        part, posb = _chunk_part(lambda s: row_refs[c][pl.ds(s, n, stride=CHUNK_PITCH), :], pos_ref[c], w1_ref[c])
        kv.append(_compress_finish(part, posb, w2_ref[c]).astype(BF16))
    _sample_cmp_attend(q_ref[0].astype(BF16), kv[0], kv[1], oc_ref, idx_ref, q_pos=q_pos, n_slc=n_slc)


def _compress_sample(cache_t, page_table, qpad, w1big, w2big, posrows, *, q_pos, n_slc):
    nb, n_pages = page_table.shape
    n = n_pages * (PAGE_SIZE // CMP_STRIDE)
    const = lambda a: pl.BlockSpec(a.shape, lambda i, pt: (0,) * a.ndim)
    per_b = lambda w: pl.BlockSpec((1, N_HEADS, w), lambda i, pt: (i, 0, 0))
    return pl.pallas_call(
        functools.partial(_compress_sample_kernel, pages=n_pages, q_pos=q_pos, n_slc=n_slc),
        grid_spec=pltpu.PrefetchScalarGridSpec(
            num_scalar_prefetch=1,
            grid=(nb,),
            in_specs=[pl.BlockSpec(memory_space=pl.ANY), per_b(D_KV), const(w1big), const(w2big), const(posrows)],
            out_specs=[per_b(D_KV), per_b(128)],
            scratch_shapes=[pltpu.VMEM((2, n_pages, ROW_W, PAGE_SIZE), F32),
                            pltpu.VMEM((n * CHUNK_PITCH, D_KV), F32), pltpu.VMEM((n * CHUNK_PITCH, D_KV), F32),
                            pltpu.SemaphoreType.DMA((2,))],
        ),
        out_shape=[jax.ShapeDtypeStruct((nb, N_HEADS, D_KV), F32), jax.ShapeDtypeStruct((nb, N_HEADS, 128), jnp.int32)],
        compiler_params=pltpu.CompilerParams(dimension_semantics=("arbitrary",), vmem_limit_bytes=VMEM_LIMIT,
                                             disable_bounds_checks=True),
        name="compress_sample",
    )(page_table.reshape(-1), cache_t, qpad, w1big, w2big, posrows)


def _overlap(cmp_idx, slc_idx):
    lo = cmp_idx * CMP_STRIDE
    so = slc_idx * SLC_BLOCK
    return (lo <= so + SLC_BLOCK - 1) & (lo + CMP_BLOCK - 1 >= so)


def _forced_importance(imp, blk, cur, n_slc):
    valid = (blk <= cur) & (blk < n_slc)
    forced = (blk == 0) | ((cur - blk >= 0) & (cur - blk < N_LOCAL_BLOCKS))
    return jnp.where(valid & forced, FORCED_SCORE, jnp.where(valid, imp, -1.0))


def _attn_prompt_kernel(qt_ref, kc_ref, vct_ref, ks_ref, vts_ref, kw_ref, vtw_ref, gt_ref, o_ref, sel_ref):
    i = pl.program_id(1)
    seq = ks_ref.shape[1]
    nbp = kc_ref.shape[1]
    n_slc = seq // SLC_BLOCK
    lanes = GQA * Q_TILE
    lane = lax.broadcasted_iota(jnp.int32, (1, lanes), 1)
    qpos = i * Q_TILE + (lane & (Q_TILE - 1))
    qpos_q = qpos[:, :Q_TILE]
    qt = qt_ref[0]
    gt = gt_ref[0]
    win_keys = min(WINDOW + Q_TILE, seq)
    blocks_per_tile = K_TILE // SLC_BLOCK

    ovl = _overlap(lax.broadcasted_iota(jnp.int32, (n_slc, nbp), 1), lax.broadcasted_iota(jnp.int32, (n_slc, nbp), 0))
    ovl = jnp.where(ovl, 1.0, 0.0).astype(BF16)
    blk = lax.broadcasted_iota(jnp.int32, (n_slc, Q_TILE), 0)

    cpos = lax.broadcasted_iota(jnp.int32, (nbp, lanes), 0) * CMP_STRIDE + (CMP_BLOCK - 1)
    cmp_bias = jnp.where(cpos <= qpos, 0.0, NEG_INF)
    any_cmp = jnp.where(qpos >= CMP_BLOCK - 1, 1.0, 0.0)
    w0 = pl.multiple_of(jnp.clip(i * Q_TILE - WINDOW, 0, seq - win_keys), Q_TILE)
    wpos = w0 + lax.broadcasted_iota(jnp.int32, (win_keys, lanes), 0)
    win_bias = jnp.where(wpos <= qpos, jnp.where(wpos >= qpos - WINDOW, 0.0, NEG_INF), NEG_INF)
    n_steps = ((i + 1) * Q_TILE + K_TILE - 1) // K_TILE
    last0 = pl.multiple_of((n_steps - 1) * K_TILE, K_TILE)
    causal_bias = jnp.where(last0 + lax.broadcasted_iota(jnp.int32, (K_TILE, lanes), 0) <= qpos, 0.0, NEG_INF)

    heads = range(KV_HEADS)
    hd = [slice(kvh * HEAD_DIM, (kvh + 1) * HEAD_DIM) for kvh in heads]
    qpad, o_c = [], []
    for kvh in heads:
        qk = jnp.concatenate([qt[(kvh * GQA + g) * HEAD_DIM:(kvh * GQA + g + 1) * HEAD_DIM, :] for g in range(GQA)],
                             axis=1)
        zero = jnp.zeros_like(qk)
        qpad.append(jnp.concatenate([qk, zero] if kvh == 0 else [zero, qk], axis=0))

        s = _mm(kc_ref[0], qpad[kvh]) + cmp_bias
        e = jnp.exp2(s - jnp.max(s, axis=0, keepdims=True))
        p = e * (any_cmp / jnp.sum(e, axis=0, keepdims=True))
        o_c.append(_mm(vct_ref[0, hd[kvh], :], p.astype(BF16)))

        psum = p[:, 0:Q_TILE]
        for g in range(1, GQA):
            psum = psum + p[:, g * Q_TILE:(g + 1) * Q_TILE]
        imp = sum(_mm(ovl, t) for t in _split3(psum))
        impf = _forced_importance(imp, blk, qpos_q // SLC_BLOCK, n_slc)
        rank = jnp.zeros_like(impf)
        for k in range(n_slc):
            rk = impf[k:k + 1, :]
            rank = rank + jnp.where(rk > impf, 1.0, 0.0) + jnp.where(rk == impf, jnp.where(blk > k, 1.0, 0.0), 0.0)
        sel = jnp.where(rank < SLC_TOPN, jnp.where(impf >= 0.0, 0.0, NEG_INF), NEG_INF)
        sel_ref[kvh] = jnp.concatenate([sel] * GQA, axis=1)

    def slc_step(t, carry, extra_bias=None):
        k0 = pl.multiple_of(t * K_TILE, K_TILE)
        keys = ks_ref[0, pl.ds(k0, K_TILE), :]
        out = []
        for kvh in heads:
            m, l, acc = carry[kvh]
            sk = _mm(keys, qpad[kvh])
            sk = jnp.concatenate(
                [sk[j * SLC_BLOCK:(j + 1) * SLC_BLOCK, :] + sel_ref[kvh, pl.ds(t * blocks_per_tile + j, 1), :]
                 for j in range(blocks_per_tile)], axis=0)
            if extra_bias is not None:
                sk = sk + extra_bias
            m_new = jnp.maximum(m, jnp.max(sk, axis=0, keepdims=True))
            alpha = jnp.exp2(m - m_new)
            e = jnp.exp2(sk - m_new)
            l_new = alpha * l + jnp.sum(e, axis=0, keepdims=True)
            pv = _mm(vts_ref[0, hd[kvh], pl.ds(k0, K_TILE)], e.astype(BF16))
            out.append((m_new, l_new, alpha * acc + pv))
        return tuple(out)

    init = (jnp.full((1, lanes), NEG_INF, F32), jnp.zeros((1, lanes), F32), jnp.zeros((HEAD_DIM, lanes), F32))
    carry = lax.fori_loop(0, n_steps - 1, slc_step, (init,) * KV_HEADS)
    carry = slc_step(n_steps - 1, carry, causal_bias)

    for kvh in heads:
        _, l_s, acc_s = carry[kvh]
        o_s = acc_s * (1.0 / l_s)

        sw = _mm(kw_ref[0, pl.ds(w0, win_keys), :], qpad[kvh]) + win_bias
        ew = jnp.exp2(sw - jnp.max(sw, axis=0, keepdims=True))
        o_w = _mm(vtw_ref[0, hd[kvh], pl.ds(w0, win_keys)], ew.astype(BF16)) * (
            1.0 / jnp.sum(ew, axis=0, keepdims=True))

        def gate(r):
            return jnp.concatenate([gt[r * N_HEADS + kvh * GQA + g:r * N_HEADS + kvh * GQA + g + 1, :]
                                    for g in range(GQA)], axis=1)
        o_t = gate(0) * o_c[kvh] + gate(1) * o_s + gate(2) * o_w
        for pair in range(GQA // 2):
            two = jnp.concatenate([o_t[:, (2 * pair) * Q_TILE:(2 * pair + 1) * Q_TILE],
                                   o_t[:, (2 * pair + 1) * Q_TILE:(2 * pair + 2) * Q_TILE]], axis=0)
            c0 = kvh * GQA * HEAD_DIM + pair * 2 * HEAD_DIM
            o_ref[0, :, c0:c0 + 2 * HEAD_DIM] = two.T


def _attn_prompt(qt, kc, vct, ks, vts, kw, vtw, gt):
    b, _, s = qt.shape
    nbp = kc.shape[1]
    per_b = lambda shape: pl.BlockSpec((1,) + shape, lambda i, j: (i, 0, 0))
    return pl.pallas_call(
        _attn_prompt_kernel,
        grid=(b, s // Q_TILE),
        in_specs=[pl.BlockSpec((1, D_ATTN, Q_TILE), lambda i, j: (i, 0, j)),
                  per_b((nbp, D_KV)), per_b((D_KV, nbp)),
                  per_b((s, D_KV)), per_b((D_KV, s)), per_b((s, D_KV)), per_b((D_KV, s)),
                  pl.BlockSpec((1, 32, Q_TILE), lambda i, j: (i, 0, j))],
        out_specs=pl.BlockSpec((1, Q_TILE, D_ATTN), lambda i, j: (i, j, 0)),
        out_shape=jax.ShapeDtypeStruct((b, s, D_ATTN), F32),
        scratch_shapes=[pltpu.VMEM((KV_HEADS, s // SLC_BLOCK, GQA * Q_TILE), F32)],
        compiler_params=_cparams("arbitrary", "arbitrary"),
        name="attn_prompt",
    )(qt, kc, vct, ks, vts, kw, vtw, gt)


def _sample_cmp_attend(q, kc, vc, oc_ref, idx_ref, *, q_pos, n_slc):
    nb = kc.shape[0]
    s = _mm_nt(q, kc)
    cpos = lax.broadcasted_iota(jnp.int32, s.shape, 1) * CMP_STRIDE + (CMP_BLOCK - 1)
    p = _softmax_masked(s, cpos <= q_pos, 1)
    oc_ref[0] = _mm(p.astype(BF16), vc)

    nsp = idx_ref.shape[2] * 2
    group_sums = [jnp.sum(p[k * GQA:(k + 1) * GQA, :], axis=0, keepdims=True) for k in range(KV_HEADS)]
    psum = jnp.concatenate(group_sums + [jnp.zeros((N_HEADS - KV_HEADS, nb), F32)], axis=0)
    ovl = _overlap(lax.broadcasted_iota(jnp.int32, (nb, nsp), 0), lax.broadcasted_iota(jnp.int32, (nb, nsp), 1))
    ovl = jnp.where(ovl, 1.0, 0.0).astype(BF16)
    imp = sum(_mm(t, ovl) for t in _split3(psum))
    blk = lax.broadcasted_iota(jnp.int32, imp.shape, 1)
    impf = _forced_importance(imp, blk, q_pos // SLC_BLOCK, n_slc)
    eye = jnp.where(lax.broadcasted_iota(jnp.int32, (nsp, nsp), 0) == lax.broadcasted_iota(jnp.int32, (nsp, nsp), 1),
                    1.0, 0.0).astype(BF16)
    imp_t = sum(_mm_nt(eye, t) for t in _split3(impf))
    k_idx = lax.broadcasted_iota(jnp.int32, (nsp, nsp), 0)
    j_idx = lax.broadcasted_iota(jnp.int32, (nsp, nsp), 1)
    ranks = []
    for r in range(KV_HEADS):
        col, row = imp_t[:, r:r + 1], impf[r:r + 1, :]
        beats = jnp.where(col > row, 1.0, jnp.where(col == row, jnp.where(k_idx < j_idx, 1.0, 0.0), 0.0))
        ranks.append(jnp.sum(beats, axis=0, keepdims=True))
    rank = jnp.concatenate(ranks + [jnp.full((N_HEADS - KV_HEADS, nsp), float(nsp), F32)], axis=0)
    blk_f = blk.astype(F32)
    slot = lax.broadcasted_iota(jnp.int32, (N_HEADS, idx_ref.shape[2]), 1)
    idx = jnp.zeros((N_HEADS, idx_ref.shape[2]), F32)
    for t in range(SLC_TOPN):
        chosen = jnp.sum(jnp.where(rank == float(t), blk_f, 0.0), axis=1, keepdims=True)
        idx = idx + jnp.where(slot == t, chosen, 0.0)
    idx_ref[0] = idx.astype(jnp.int32)


def _attn_sample_kernel(idx_ref, pt_ref, cache_ref, q_ref, oc_ref, gt_ref, kvst_ref, kvwt_ref, win_ref, o_ref, nwin_ref,
                        blk_ref, sem, *, n_past_slc, n_pages):
    n_sel = KV_HEADS * SLC_TOPN
    b = pl.program_id(0)
    nb = kvst_ref.shape[1]
    sub = PAGE_SIZE // SLC_BLOCK

    def block_copy(seq, s, buf):
        j = jnp.minimum(idx_ref[seq * n_sel + s], n_past_slc - 1)
        page = pt_ref[seq * n_pages + j // sub]
        return pltpu.make_async_copy(cache_ref.at[page, :, s // SLC_TOPN], blk_ref.at[buf, s], sem.at[buf])

    def fetch(seq, buf):
        for s in range(n_sel):
            block_copy(seq, s, buf).start()

    @pl.when(b == 0)
    def _():
        fetch(0, 0)

    @pl.when(b + 1 < pl.num_programs(0))
    def _():
        fetch(b + 1, (b + 1) % 2)

    buf = b % 2
    for s in range(n_sel):
        block_copy(b, s, buf).wait()

    q = q_ref[0].astype(BF16)
    head_kvh = lax.broadcasted_iota(jnp.int32, (N_HEADS, 1), 0) // GQA
    mine = lax.broadcasted_iota(jnp.int32, (1, nb), 1) == b
    lane_half = lax.broadcasted_iota(jnp.int32, (1, PAGE_SIZE), 1) // SLC_BLOCK

    def attend(kt, vt, valid):
        s = jnp.where(valid, _mm(q, kt), NEG_INF)
        e = jnp.where(valid, jnp.exp(s - jnp.max(s, axis=1, keepdims=True)), 0.0)
        return _mm_nt(e.astype(BF16), vt) / jnp.sum(e, axis=1, keepdims=True)

    kvst = kvst_ref[...]
    kvwt = kvwt_ref[...]
    o_s = jnp.zeros((N_HEADS, HEAD_DIM), F32)
    o_w = jnp.zeros((N_HEADS, HEAD_DIM), F32)
    win = win_ref[0]
    for kvh in range(KV_HEADS):
        k_rows = slice(kvh * HEAD_DIM, (kvh + 1) * HEAD_DIM)
        v_rows = slice(D_KV + kvh * HEAD_DIM, D_KV + (kvh + 1) * HEAD_DIM)
        valid = []
        n_new = jnp.int32(0)
        for t in range(SLC_TOPN):
            j = idx_ref[(b * KV_HEADS + kvh) * SLC_TOPN + t]
            past = j < n_past_slc
            half = jnp.minimum(j, n_past_slc - 1) % (PAGE_SIZE // SLC_BLOCK)
            valid.append((lane_half == half) & past)
            n_new = n_new + jnp.where(past, 0, 1)
        valid.append(mine & (n_new > 0))
        kt = jnp.concatenate([blk_ref[buf, kvh * SLC_TOPN + t, 0] for t in range(SLC_TOPN)] + [kvst[k_rows, :]],
                             axis=1)
        vt = jnp.concatenate([blk_ref[buf, kvh * SLC_TOPN + t, 1] for t in range(SLC_TOPN)] + [kvst[v_rows, :]],
                             axis=1)
        o_k = attend(kt.astype(BF16), vt.astype(BF16), jnp.concatenate(valid, axis=1))
        o_s = jnp.where(head_kvh == kvh, o_k, o_s)
        kt = jnp.concatenate([win[k_rows, :], kvwt[k_rows, :]], axis=1)
        vt = jnp.concatenate([win[v_rows, :], kvwt[v_rows, :]], axis=1)
        valid_w = jnp.concatenate([jnp.full((1, win.shape[1]), True), mine], axis=1)
        o_k = attend(kt.astype(BF16), vt.astype(BF16), valid_w)
        o_w = jnp.where(head_kvh == kvh, o_k, o_w)

    oc = oc_ref[0]
    o_c = jnp.where(head_kvh == 0, oc[:, :HEAD_DIM], oc[:, HEAD_DIM:])
    gates = jnp.sum(jnp.where(mine, gt_ref[...], 0.0), axis=1, keepdims=True)
    o_ref[0] = (gates[0:N_HEADS] * o_c + gates[N_HEADS:2 * N_HEADS] * o_s + gates[2 * N_HEADS:3 * N_HEADS] * o_w)

    new_col = jnp.sum(jnp.where(mine, kvwt, 0.0), axis=1, keepdims=True)
    last = lax.broadcasted_iota(jnp.int32, win.shape, 1) == win.shape[1] - 1
    nwin_ref[0] = jnp.where(last, new_col, pltpu.roll(win, win.shape[1] - 1, 1))


def _attn_sample(idx, page_table, cache_t, q, o_c, gt, kvst, kvwt, win_t, *, n_past_slc):
    nb = q.shape[0]
    win_buf = win_t.shape[2]
    n_pages = page_table.shape[1]
    per_b = lambda shape: pl.BlockSpec((1,) + shape, lambda i, a, c: (i,) + (0,) * len(shape))
    const = lambda a: pl.BlockSpec(a.shape, lambda i, x, c: (0,) * a.ndim)
    n_sel = KV_HEADS * SLC_TOPN
    return pl.pallas_call(
        functools.partial(_attn_sample_kernel, n_past_slc=n_past_slc, n_pages=n_pages),
        grid_spec=pltpu.PrefetchScalarGridSpec(
            num_scalar_prefetch=2,
            grid=(nb,),
            in_specs=[pl.BlockSpec(memory_space=pl.ANY), per_b((N_HEADS, HEAD_DIM)), per_b((N_HEADS, D_KV)),
                      const(gt), const(kvst), const(kvwt), per_b((ROW_W, win_buf))],
            out_specs=[per_b((N_HEADS, HEAD_DIM)), per_b((ROW_W, win_buf))],
            scratch_shapes=[pltpu.VMEM((2, n_sel, 2, HEAD_DIM, PAGE_SIZE), F32), pltpu.SemaphoreType.DMA((2,))],
        ),
        out_shape=[jax.ShapeDtypeStruct((nb, N_HEADS, HEAD_DIM), F32), jax.ShapeDtypeStruct(win_t.shape, F32)],
        compiler_params=pltpu.CompilerParams(dimension_semantics=("arbitrary",), vmem_limit_bytes=VMEM_LIMIT,
                                             disable_bounds_checks=True),
        name="attn_sample",
    )(idx.reshape(-1), page_table.reshape(-1), cache_t, q, o_c, gt, kvst, kvwt, win_t)


def _conv_tail(y, b_ref, lg_ref, lb_ref):
    y = y + b_ref[...]
    yc = y - jnp.mean(y, axis=-1, keepdims=True)
    yn = yc * lax.rsqrt(jnp.mean(yc * yc, axis=-1, keepdims=True) + EPS)
    return _silu(yn * lg_ref[...] + lb_ref[...])


def _conv_prompt_kernel(u_ref, halo_ref, w_ref, b_ref, lg_ref, lb_ref, o_ref, buf_ref, shift_ref, *, halo):
    tm = u_ref.shape[1]
    j = pl.program_id(1)
    buf_ref[0:halo, :] = jnp.where(j > 0, halo_ref[0], 0.0)
    buf_ref[halo:halo + tm, :] = u_ref[0]
    buf_ref[halo + tm:, :] = jnp.zeros((8, buf_ref.shape[1]), F32)
    lead = halo - (CONV_WIDTH - 1)
    for ph in range(8):
        shift_ref[ph] = buf_ref[ph:ph + tm + halo, :]

    def chunk(c, carry):
        r0 = pl.multiple_of(c * CONV_ROWS, CONV_ROWS)
        y = None
        for w in range(CONV_WIDTH):
            o = lead + w
            tap = shift_ref[o % 8, pl.ds(r0 + o // 8 * 8, CONV_ROWS), :] * w_ref[w:w + 1, :]
            y = tap if y is None else y + tap
        o_ref[0, pl.ds(r0, CONV_ROWS), :] = _conv_tail(y, b_ref, lg_ref, lb_ref)
        return carry

    lax.fori_loop(0, tm // CONV_ROWS, chunk, 0, unroll=4)


def _conv_prompt(u, w_dw, b_dw, ln_g, ln_b, *, tm):
    b, s, dc = u.shape
    halo = 32
    per = tm // halo
    const = lambda a: pl.BlockSpec(a.shape, lambda i, j: (0,) * a.ndim)
    return pl.pallas_call(
        functools.partial(_conv_prompt_kernel, halo=halo),
        grid=(b, s // tm),
        in_specs=[pl.BlockSpec((1, tm, dc), lambda i, j: (i, j, 0)),
                  pl.BlockSpec((1, halo, dc), lambda i, j: (i, jnp.maximum(j * per - 1, 0), 0)),
                  const(w_dw), const(b_dw), const(ln_g), const(ln_b)],
        out_specs=pl.BlockSpec((1, tm, dc), lambda i, j: (i, j, 0)),
        out_shape=jax.ShapeDtypeStruct((b, s, dc), F32),
        scratch_shapes=[pltpu.VMEM((tm + halo + 8, dc), F32), pltpu.VMEM((8, tm + halo, dc), F32)],
        compiler_params=_cparams("arbitrary", "arbitrary"),
        name="conv_prompt",
    )(u, u, w_dw, b_dw, ln_g, ln_b)


def _conv_sample_kernel(st_ref, u_ref, w_ref, b_ref, lg_ref, lb_ref, o_ref):
    hist = CONV_WIDTH - 1
    y = u_ref[...] * w_ref[hist:hist + 1, :]
    for w in range(hist):
        y = y + st_ref[w] * w_ref[w:w + 1, :]
    o_ref[...] = _conv_tail(y, b_ref, lg_ref, lb_ref)


def _conv_sample(state_t, u, w_dw, b_dw, ln_g, ln_b, *, sb):
    nb, dc = u.shape
    hist = state_t.shape[0]
    const = lambda a: pl.BlockSpec(a.shape, lambda i: (0,) * a.ndim)
    return pl.pallas_call(
        _conv_sample_kernel,
        grid=(nb // sb,),
        in_specs=[pl.BlockSpec((hist, sb, dc), lambda i: (0, i, 0)), pl.BlockSpec((sb, dc), lambda i: (i, 0)),
                  const(w_dw), const(b_dw), const(ln_g), const(ln_b)],
        out_specs=pl.BlockSpec((sb, dc), lambda i: (i, 0)),
        out_shape=jax.ShapeDtypeStruct((nb, dc), F32),
        compiler_params=_cparams("arbitrary"),
        name="conv_sample",
    )(state_t, u, w_dw, b_dw, ln_g, ln_b)


def _finish_kernel(x_ref, oa_ref, ocv_ref, gate1_ref, shift2_ref, scale2_ref, ga_ref, gc_ref, wout_ref, g2_ref,
                   wr_ref, br_ref, x1_ref, h2_ref, comb_ref, route_ref, rows_ref, count_ref):
    mixed = jnp.concatenate([_rms(oa_ref[0], ga_ref[...]), _rms(ocv_ref[0], gc_ref[...])], axis=1)
    mix = _mm(mixed.astype(BF16), wout_ref[...])
    x1 = x_ref[0] + gate1_ref[0] * mix
    x1_ref[0] = x1
    h2 = _rms(x1, g2_ref[...]) * (1.0 + scale2_ref[0]) + shift2_ref[0]
    h2_ref[0] = h2.astype(BF16)
    rows_ref[0] = h2

    h_hi, h_mid, _ = _split3(h2)
    w_hi, w_mid, _ = _split3(wr_ref[...])
    lg = (_mm(h_hi, w_hi) + _mm(h_hi, w_mid)
          + _mm(h_mid, w_hi)) + br_ref[...]
    lane = lax.broadcasted_iota(jnp.int32, lg.shape, 1)
    lane_f = lane.astype(F32)
    is_group = (lane >= N_EXPERTS) & (lane < N_EXPERTS + N_GROUPS)
    gl = jnp.where(is_group, lg, NEG_INF)
    g_max = jnp.max(gl, axis=1, keepdims=True)
    p_top = 1.0 / jnp.sum(jnp.where(is_group, jnp.exp(gl - g_max), 0.0), axis=1, keepdims=True)
    g_lane = jnp.min(jnp.where(gl == g_max, lane_f, 1e9), axis=1, keepdims=True)
    in_group = (lane < N_EXPERTS) & ((lane // EPG).astype(F32) == g_lane - N_EXPERTS)
    el = jnp.where(in_group, lg, NEG_INF)
    l1 = jnp.max(el, axis=1, keepdims=True)
    i1 = jnp.min(jnp.where(el == l1, lane_f, 1e9), axis=1, keepdims=True)
    el2 = jnp.where(lane_f == i1, NEG_INF, el)
    l2 = jnp.max(el2, axis=1, keepdims=True)
    i2 = jnp.min(jnp.where(el2 == l2, lane_f, 1e9), axis=1, keepdims=True)
    r = jnp.exp(l2 - l1)
    w1 = p_top / (1.0 + r)
    w2 = p_top * r / (1.0 + r)
    comb_ref[0] = jnp.where(lane_f == i1, w1, jnp.where(lane_f == i2, w2, 0.0))
    route_ref[0] = jnp.where(lane == 0, i1, jnp.where(lane == 1, i2, jnp.where(lane == 2, w1, jnp.where(
        lane == 3, w2, 0.0))))

    @pl.when((pl.program_id(0) == 0) & (pl.program_id(1) == 0))
    def _():
        count_ref[...] = jnp.zeros_like(count_ref)

    picks = jnp.where(lane_f == i1, 1.0, 0.0) + jnp.where(lane_f == i2, 1.0, 0.0)
    count_ref[...] += jnp.sum(picks, axis=0, keepdims=True)


def _finish(x, o_attn, o_conv, gate1, shift2, scale2, ga, gc, w_out, g2, w_route, b_route, *, tm):
    b, s, d = x.shape
    r = gate1.shape[1]
    mod_block = (1, 1, d) if r == 1 else (1, tm, d)
    mod_map = (lambda i, j: (i, 0, 0)) if r == 1 else (lambda i, j: (i, j, 0))
    mod = pl.BlockSpec(mod_block, mod_map)
    row = lambda w: pl.BlockSpec((1, tm, w), lambda i, j: (i, j, 0))
    const = lambda a: pl.BlockSpec(a.shape, lambda i, j: (0,) * a.ndim)
    return pl.pallas_call(
        _finish_kernel,
        grid=(b, s // tm),
        in_specs=[row(d), row(D_ATTN), row(D_CONV), mod, mod, mod, const(ga), const(gc), const(w_out), const(g2),
                  const(w_route), const(b_route)],
        out_specs=[row(d), row(d), row(128), row(128), row(d), pl.BlockSpec((8, 128), lambda i, j: (0, 0))],
        out_shape=[jax.ShapeDtypeStruct((b, s, d), F32), jax.ShapeDtypeStruct((b, s, d), BF16),
                   jax.ShapeDtypeStruct((b, s, 128), F32), jax.ShapeDtypeStruct((b, s, 128), F32),
                   jax.ShapeDtypeStruct((b, s, d), F32), jax.ShapeDtypeStruct((8, 128), F32)],
        compiler_params=_cparams("arbitrary", "arbitrary"),
        name="finish",
    )(x, o_attn, o_conv, gate1, shift2, scale2, ga, gc, w_out, g2, w_route, b_route)


def _moe_kernel(h_ref, comb_ref, x1_ref, gate2_ref, wg_ref, wu_ref, wd_ref, gf_ref, y_ref, acc_ref):
    e = pl.program_id(2)

    @pl.when(e == 0)
    def _():
        acc_ref[...] = jnp.zeros_like(acc_ref)

    h = h_ref[0]
    a = _mm(h, wg_ref[0].astype(BF16))
    u = _mm(h, wu_ref[0].astype(BF16))
    comb = comb_ref[0]
    lane = lax.broadcasted_iota(jnp.int32, comb.shape, 1)
    cw = jnp.sum(jnp.where(lane == e, comb, 0.0), axis=1, keepdims=True)
    hid = (_silu(a) * u * cw).astype(BF16)
    acc_ref[...] += _mm(hid, wd_ref[0].astype(BF16))

    @pl.when(e == pl.num_programs(2) - 1)
    def _():
        y = x1_ref[0] + gate2_ref[0] * acc_ref[...]
        y_ref[0] = _rms(y, gf_ref[...])


def _moe(h2, comb, x1, gate2, w_gate, w_up, w_down, final_g, *, tm):
    b, s, d = x1.shape
    r = gate2.shape[1]
    mod_block = (1, 1, d) if r == 1 else (1, tm, d)
    mod_map = (lambda i, j, e: (i, 0, 0)) if r == 1 else (lambda i, j, e: (i, j, 0))
    row = lambda w: pl.BlockSpec((1, tm, w), lambda i, j, e: (i, j, 0))
    ne, _, de = w_gate.shape
    return pl.pallas_call(
        _moe_kernel,
        grid=(b, s // tm, ne),
        in_specs=[row(d), row(128), row(d), pl.BlockSpec(mod_block, mod_map),
                  pl.BlockSpec((1, d, de), lambda i, j, e: (e, 0, 0)),
                  pl.BlockSpec((1, d, de), lambda i, j, e: (e, 0, 0)),
                  pl.BlockSpec((1, de, d), lambda i, j, e: (e, 0, 0)),
                  pl.BlockSpec((1, d), lambda i, j, e: (0, 0))],
        out_specs=row(d),
        out_shape=jax.ShapeDtypeStruct((b, s, d), F32),
        scratch_shapes=[pltpu.VMEM((tm, d), F32)],
        compiler_params=_cparams("arbitrary", "arbitrary", "arbitrary"),
        name="moe",
    )(h2, comb, x1, gate2, w_gate, w_up, w_down, final_g)


def _moe_slots_kernel(route_ref, base_ref, slot_ref, run_ref):
    @pl.when(pl.program_id(0) == 0)
    def _():
        run_ref[...] = jnp.zeros_like(run_ref)

    route = route_ref[...]
    tm = route.shape[0]
    lane = lax.broadcasted_iota(jnp.int32, route.shape, 1).astype(F32)
    first = jnp.where(lane == route[:, 0:1], 1.0, 0.0)
    second = jnp.where(lane == route[:, 1:2], 1.0, 0.0)
    picks = first + second
    earlier = lax.broadcasted_iota(jnp.int32, (tm, tm), 1) < lax.broadcasted_iota(jnp.int32, (tm, tm), 0)
    seen = _mm(jnp.where(earlier, 1.0, 0.0).astype(BF16), picks.astype(BF16)) + (run_ref[...] + base_ref[...])
    slot0 = jnp.sum(first * seen, axis=1, keepdims=True)
    slot1 = jnp.sum(second * seen, axis=1, keepdims=True)
    slot_ref[...] = jnp.where(lane == 0.0, slot0, jnp.where(lane == 1.0, slot1, 0.0)).astype(jnp.int32)
    run_ref[...] += jnp.sum(picks, axis=0, keepdims=True)


def _moe_slots(route, base, *, tm):
    n = route.shape[0]
    return pl.pallas_call(
        _moe_slots_kernel,
        grid=(n // tm,),
        in_specs=[pl.BlockSpec((tm, 128), lambda j: (j, 0)), pl.BlockSpec((1, 128), lambda j: (0, 0))],
        out_specs=pl.BlockSpec((tm, 128), lambda j: (j, 0)),
        out_shape=jax.ShapeDtypeStruct((n, 128), jnp.int32),
        scratch_shapes=[pltpu.VMEM((1, 128), F32)],
        compiler_params=_cparams("arbitrary"),
        name="moe_slots",
    )(route, base)


def _row_copy(src_ref, src_row, dst_ref, dst_row, sem):
    return pltpu.make_async_copy(src_ref.at[pl.ds(src_row, 1), :], dst_ref.at[pl.ds(dst_row, 1), :], sem)


def _moe_scatter_kernel(s0_ref, s1_ref, h_ref, init_ref, sorted_ref, sem):
    del init_ref
    j = pl.program_id(0)
    tm = h_ref.shape[0]

    def start(r, carry):
        _row_copy(h_ref, r, sorted_ref, s0_ref[j * tm + r], sem).start()
        _row_copy(h_ref, r, sorted_ref, s1_ref[j * tm + r], sem).start()
        return carry

    def wait(r, carry):
        _row_copy(h_ref, 0, sorted_ref, 0, sem).wait()
        _row_copy(h_ref, 0, sorted_ref, 0, sem).wait()
        return carry

    lax.fori_loop(0, tm, start, 0, unroll=8)
    lax.fori_loop(0, tm, wait, 0, unroll=8)


def _moe_scatter(slot0, slot1, rows, n_slots, *, tm):
    n, w = rows.shape
    return pl.pallas_call(
        _moe_scatter_kernel,
        grid_spec=pltpu.PrefetchScalarGridSpec(
            num_scalar_prefetch=2,
            grid=(n // tm,),
            in_specs=[pl.BlockSpec((tm, w), lambda j, a, b: (j, 0)), pl.BlockSpec(memory_space=pl.ANY)],
            out_specs=pl.BlockSpec(memory_space=pl.ANY),
            scratch_shapes=[pltpu.SemaphoreType.DMA(())],
        ),
        out_shape=jax.ShapeDtypeStruct((n_slots, w), rows.dtype),
        input_output_aliases={3: 0},
        compiler_params=pltpu.CompilerParams(dimension_semantics=("arbitrary",), vmem_limit_bytes=VMEM_LIMIT,
                                             disable_bounds_checks=True),
        name="moe_scatter",
    )(slot0, slot1, rows, jnp.zeros((n_slots, w), rows.dtype))


def _moe_experts_kernel(te_ref, na_ref, x_ref, wg_ref, wu_ref, wd_ref, o_ref):
    t = pl.program_id(0)

    @pl.when(t < na_ref[0])
    def _():
        xb = x_ref[...].astype(BF16)
        a = _mm(xb, wg_ref[0].astype(BF16))
        u = _mm(xb, wu_ref[0].astype(BF16))
        o_ref[...] = _mm((_silu(a) * u).astype(BF16), wd_ref[0].astype(BF16))

    @pl.when(t >= na_ref[0])
    def _():
        o_ref[...] = jnp.zeros_like(o_ref)


def _moe_experts(tile_expert, n_active, sorted_x, w_gate, w_up, w_down):
    n_slots, w = sorted_x.shape
    ne, d, de = w_gate.shape
    x_map = lambda t, te, na: (jnp.minimum(t, na[0] - 1), 0)
    w_map = lambda t, te, na: (te[t], 0, 0)
    return pl.pallas_call(
        _moe_experts_kernel,
        grid_spec=pltpu.PrefetchScalarGridSpec(
            num_scalar_prefetch=2,
            grid=(n_slots // MOE_TILE,),
            in_specs=[pl.BlockSpec((MOE_TILE, w), x_map), pl.BlockSpec((1, d, de), w_map),
                      pl.BlockSpec((1, d, de), w_map), pl.BlockSpec((1, de, d), w_map)],
            out_specs=pl.BlockSpec((MOE_TILE, d), lambda t, te, na: (t, 0)),
        ),
        out_shape=jax.ShapeDtypeStruct((n_slots, d), F32),
        compiler_params=_cparams("arbitrary"),
        name="moe_experts",
    )(tile_expert, n_active, sorted_x, w_gate, w_up, w_down)


def _moe_combine_kernel(s0_ref, s1_ref, y_hbm, route_ref, x1_ref, gate2_ref, gf_ref, o_ref, a_ref, b_ref, sem):
    j = pl.program_id(0)
    tm = x1_ref.shape[0]

    def issue(step, slot):
        def body(r, carry):
            _row_copy(y_hbm, s0_ref[step * tm + r], a_ref.at[slot], r, sem.at[slot]).start()
            _row_copy(y_hbm, s1_ref[step * tm + r], b_ref.at[slot], r, sem.at[slot]).start()
            return carry
        lax.fori_loop(0, tm, body, 0, unroll=8)

    @pl.when(j == 0)
    def _():
        issue(0, 0)

    @pl.when(j + 1 < pl.num_programs(0))
    def _():
        issue(j + 1, (j + 1) % 2)

    slot = j % 2

    def wait(r, carry):
        _row_copy(y_hbm, 0, a_ref.at[slot], 0, sem.at[slot]).wait()
        _row_copy(y_hbm, 0, b_ref.at[slot], 0, sem.at[slot]).wait()
        return carry
    lax.fori_loop(0, tm, wait, 0, unroll=8)

    route = route_ref[...]
    moe = route[:, 2:3] * a_ref[slot] + route[:, 3:4] * b_ref[slot]
    o_ref[...] = _rms(x1_ref[...] + gate2_ref[0] * moe, gf_ref[...])


def _moe_combine(slot0, slot1, y_sorted, route, x1, gate2, final_g, *, tm, rows_per_mod):
    n, d = x1.shape
    return pl.pallas_call(
        _moe_combine_kernel,
        grid_spec=pltpu.PrefetchScalarGridSpec(
            num_scalar_prefetch=2,
            grid=(n // tm,),
            in_specs=[pl.BlockSpec(memory_space=pl.ANY), pl.BlockSpec((tm, 128), lambda j, a, b: (j, 0)),
                      pl.BlockSpec((tm, d), lambda j, a, b: (j, 0)),
                      pl.BlockSpec((1, 1, d), lambda j, a, b: (j * tm // rows_per_mod, 0, 0)),
                      pl.BlockSpec((1, d), lambda j, a, b: (0, 0))],
            out_specs=pl.BlockSpec((tm, d), lambda j, a, b: (j, 0)),
            scratch_shapes=[pltpu.VMEM((2, tm, d), F32), pltpu.VMEM((2, tm, d), F32), pltpu.SemaphoreType.DMA((2,))],
        ),
        out_shape=jax.ShapeDtypeStruct((n, d), F32),
        compiler_params=pltpu.CompilerParams(dimension_semantics=("arbitrary",), vmem_limit_bytes=VMEM_LIMIT,
                                             disable_bounds_checks=True),
        name="moe_combine",
    )(slot0, slot1, y_sorted, route, x1, gate2, final_g)


def _moe_sorted(route, counts, rows, x1, gate2, w_gate, w_up, w_down, final_g):
    b, s, d = x1.shape
    n = b * s
    ne = w_gate.shape[0]
    n_tiles = 2 * n // MOE_TILE + ne
    cnt = counts[0, :ne].astype(jnp.int32)
    padded = (cnt + MOE_TILE - 1) // MOE_TILE * MOE_TILE
    ends = jnp.sum(jnp.where(jnp.arange(ne)[:, None] <= jnp.arange(ne)[None, :], padded[:, None], 0), axis=0)
    base = jnp.pad((ends - padded).astype(F32), (0, 128 - ne)).reshape(1, 128)
    n_active = (ends[-1] // MOE_TILE).reshape(1)
    tile_start = jnp.arange(n_tiles, dtype=jnp.int32) * MOE_TILE
    tile_expert = jnp.minimum(jnp.sum((ends[None, :] <= tile_start[:, None]).astype(jnp.int32), axis=1), ne - 1)

    route2 = route.reshape(n, 128)
    slots = _moe_slots(route2, base, tm=512)
    slot0, slot1 = slots[:, 0], slots[:, 1]
    sorted_x = _moe_scatter(slot0, slot1, rows.reshape(n, d), n_tiles * MOE_TILE, tm=512)
    y_sorted = _moe_experts(tile_expert, n_active, sorted_x, w_gate, w_up, w_down)
    y = _moe_combine(slot0, slot1, y_sorted, route2, x1.reshape(n, d), gate2, final_g, tm=256, rows_per_mod=s)
    return y.reshape(b, s, d)


def _prep_w_in(w_in):
    o_gl = D_ATTN + 3 * ROW_W
    wqkv = w_in[:, :o_gl].astype(BF16)
    wgl = w_in[:, o_gl:o_gl + 3 * N_HEADS]
    pad = jnp.zeros((w_in.shape[0], 128 - 3 * N_HEADS), w_in.dtype)
    wgl = jnp.concatenate([wgl.reshape(-1, N_HEADS, 3).transpose(0, 2, 1).reshape(-1, 3 * N_HEADS), pad], axis=1)
    wu = w_in[:, o_gl + 3 * N_HEADS:].astype(BF16)
    return wqkv, wgl.astype(BF16), wu


def _prep_compress(w_cmp1, pos_cmp, w_cmp2):
    ratio = CMP_BLOCK // CMP_STRIDE
    eye = jnp.eye(KV_HEADS, dtype=w_cmp1.dtype)
    w1 = w_cmp1.reshape(2, ratio, CMP_STRIDE, HEAD_DIM, HEAD_DIM)
    w1big = jnp.einsum('crsdf,kj->cskdrjf', w1, eye).reshape(2, CMP_STRIDE * D_KV, ratio * D_KV).astype(BF16)
    w2big = jnp.einsum('cfd,kj->ckfjd', w_cmp2, eye).reshape(2, D_KV, D_KV).astype(BF16)
    pos = pos_cmp.reshape(2, ratio, CMP_STRIDE, 1, HEAD_DIM)
    pos = jnp.broadcast_to(pos, (2, ratio, CMP_STRIDE, KV_HEADS, HEAD_DIM)).reshape(2, ratio, CMP_STRIDE * D_KV)
    posrows = jnp.concatenate([pos, jnp.zeros((2, 8 - ratio, CMP_STRIDE * D_KV), pos.dtype)], axis=1).astype(BF16)
    return w1big, w2big, posrows


def _prep_router(w_group, b_group, w_router, b_router):
    d = w_group.shape[0]
    pad = 128 - N_EXPERTS - N_GROUPS
    w = jnp.concatenate([w_router, w_group, jnp.zeros((d, pad), w_group.dtype)], axis=1)
    b = jnp.concatenate([b_router, b_group, jnp.zeros((pad,), b_group.dtype)]).reshape(1, 128)
    return w, b


def kernel(x_prompt, x_sample, cache_cmp_kv, cache_slc_kv, state_win_kv, state_conv, page_table, c_prompt, c_sample,
           norm1_g, w_ada, b_ada, w_in, w_cmp1, pos_cmp, w_cmp2, w_dw, b_dw, conv_ln_g, conv_ln_b, g_attn_out,
           g_conv_out, w_out, norm2_g, w_group, b_group, w_router, b_router, w_gate, w_up, w_down, final_g):
    depth = norm1_g.shape[0]
    assert depth == 1 and x_sample.shape[1] == 1
    bp, seq, d = x_prompt.shape
    nb = x_sample.shape[0]
    n_pages = page_table.shape[1]
    past_len = n_pages * PAGE_SIZE
    n_past_slc = past_len // SLC_BLOCK
    win_buf = state_win_kv.shape[2]
    assert win_buf == WINDOW and seq % K_TILE == 0 and seq >= WINDOW + Q_TILE
    l = 0
    row2 = lambda a: a.reshape(1, -1)

    mods = _adaln(jnp.concatenate([c_prompt, c_sample], axis=0), w_ada[l], b_ada[l]).reshape(bp + nb, 6, d)
    mods_p = [mods[:bp, i][:, None, :] for i in range(6)]
    mods_s = [mods[bp:, i][None, :, :] for i in range(6)]

    wqkv, wgl, wu = _prep_w_in(w_in[l])
    w1big, w2big, posrows = _prep_compress(w_cmp1[l], pos_cmp[l], w_cmp2[l])
    w_route, b_route = _prep_router(w_group[l], b_group[l], w_router[l], b_router[l])
    w_out_b = w_out[l].astype(BF16)
    conv_args = (w_dw[l], row2(b_dw[l]), row2(conv_ln_g[l]), row2(conv_ln_b[l]))

    def kv_rows_from_t(a_t):
        n, _, t = a_t.shape
        return a_t.reshape(n, 2, KV_HEADS, HEAD_DIM, t).transpose(0, 4, 1, 2, 3)

    def kv_rows_to_t(a):
        return a.transpose(0, 2, 3, 4, 1)

    qt, kvc, kvct, kvst, kvwt, ks, vts, kw, vtw, gt, u = _mixer(
        x_prompt, mods_p[0], mods_p[1], row2(norm1_g[l]), wqkv, wgl, wu, tm=512, prompt=True)
    kc, vct = _compress_prompt(kvc, w1big, w2big, posrows)
    o_attn = _attn_prompt(qt, kc, vct, ks, vts, kw, vtw, gt)
    o_conv = _conv_prompt(u, *conv_args, tm=512)
    x1, _, _, route, moe_rows, counts = _finish(x_prompt, o_attn, o_conv, mods_p[2], mods_p[3], mods_p[4],
                                             row2(g_attn_out[l]), row2(g_conv_out[l]), w_out_b, row2(norm2_g[l]),
                                             w_route, b_route, tm=512)
    y_prompt = _moe_sorted(route, counts, moe_rows, x1, mods_p[5], w_gate[l], w_up[l], w_down[l], row2(final_g))

    new_cmp_prompt = kv_rows_from_t(kvct)[None]
    new_slc_prompt = kv_rows_from_t(kvst)[None]
    new_win_prompt = kv_rows_from_t(kvwt[:, :, seq - WINDOW:])[None]
    new_conv_prompt = u[:, seq - (CONV_WIDTH - 1):][None]

    xs = x_sample.reshape(1, nb, d)
    q_s, kvct_s, kvst_s, kvwt_s, gt_s, u_s = _mixer(
        xs, mods_s[0], mods_s[1], row2(norm1_g[l]), wqkv, wgl, wu, tm=nb, prompt=False)
    cmp_t = kv_rows_to_t(cache_cmp_kv[l])
    q4 = q_s.reshape(nb, KV_HEADS, GQA, HEAD_DIM)
    zq = jnp.zeros_like(q4)
    kvh_id = jnp.arange(KV_HEADS).reshape(1, KV_HEADS, 1, 1)
    qpad = jnp.concatenate([jnp.where(kvh_id == 0, q4, zq), jnp.where(kvh_id == 1, q4, zq)],
                           axis=-1).reshape(nb, N_HEADS, D_KV)
    o_c, idx = _compress_sample(cmp_t.reshape(cmp_t.shape[0], ROW_W, PAGE_SIZE), page_table, qpad, w1big, w2big,
                                posrows, q_pos=past_len, n_slc=n_past_slc + 1)
    idx = idx[:, :KV_HEADS, :SLC_TOPN]
    win_t = kv_rows_to_t(state_win_kv[l]).reshape(nb, ROW_W, win_buf)
    o_heads, new_win_t = _attn_sample(idx, page_table, kv_rows_to_t(cache_slc_kv[l]), q_s.reshape(nb, N_HEADS, HEAD_DIM),
                                      o_c, gt_s[0], kvst_s[0], kvwt_s[0], win_t, n_past_slc=n_past_slc)
    o_attn_s = o_heads.reshape(1, nb, D_ATTN)
    state_t = state_conv[l].transpose(1, 0, 2)
    u_rows = u_s.reshape(nb, D_CONV)
    o_conv_s = _conv_sample(state_t, u_rows, *conv_args, sb=8).reshape(1, nb, D_CONV)
    x1_s, h2_s, comb_s, _, _, _ = _finish(xs, o_attn_s, o_conv_s, mods_s[2], mods_s[3], mods_s[4],
                                          row2(g_attn_out[l]), row2(g_conv_out[l]), w_out_b, row2(norm2_g[l]),
                                          w_route, b_route, tm=nb)
    y_sample = _moe(h2_s, comb_s, x1_s, mods_s[5], w_gate[l], w_up[l], w_down[l], row2(final_g), tm=nb)

    row_shape = (1, nb, 1, 2, KV_HEADS, HEAD_DIM)
    new_cmp_sample = kvct_s[0].T.reshape(row_shape)
    new_slc_sample = kvst_s[0].T.reshape(row_shape)
    new_win_sample = kv_rows_from_t(new_win_t)[None]
    new_conv_sample = jnp.concatenate([state_t[1:], u_rows[None]], axis=0).transpose(1, 0, 2)[None]

    return (y_prompt, y_sample.reshape(nb, 1, d), new_cmp_prompt, new_slc_prompt, new_win_prompt, new_conv_prompt,
            new_cmp_sample, new_slc_sample, new_win_sample, new_conv_sample)
```

```python
import functools

import jax
import jax.numpy as jnp
from jax import lax
from jax.experimental import pallas as pl
from jax.experimental.pallas import tpu as pltpu

F32 = jnp.float32
BF16 = jnp.bfloat16

D_MODEL = 1024
N_HEADS = 8
HEAD_DIM = 64
KV_HEADS = 2
GQA = N_HEADS // KV_HEADS
D_ATTN = N_HEADS * HEAD_DIM
D_CONV = D_MODEL - D_ATTN
D_KV = KV_HEADS * HEAD_DIM
CMP_BLOCK = 32
CMP_STRIDE = 16
SLC_BLOCK = 64
SLC_TOPN = 16
N_LOCAL_BLOCKS = 2
WINDOW = 512
FORCED_SCORE = 1e4
CONV_WIDTH = 31
N_GROUPS = 4
EPG = 8
N_EXPERTS = N_GROUPS * EPG
D_EXPERT = 256
PAGE_SIZE = 128
EPS = 1e-6
NEG_INF = -1e30
SCALE = HEAD_DIM ** -0.5
LOG2E = 1.4426950408889634
ROW_W = 2 * D_KV
CHUNK_W = CMP_STRIDE * ROW_W
CHUNK_PITCH = 20
MOE_TILE = 256
TOKEN_TILE_ROWS = D_MODEL // 128
CONV_ROWS = 32
Q_TILE = 128
K_TILE = 512
VMEM_LIMIT = 48 * 1024 * 1024


def _cparams(*sem):
    return pltpu.CompilerParams(dimension_semantics=sem, vmem_limit_bytes=VMEM_LIMIT)


def _rms(x, g):
    return x * lax.rsqrt(jnp.mean(x * x, axis=-1, keepdims=True) + EPS) * g


def _silu(x):
    return x * jax.nn.sigmoid(x)


def _mm(a, b):
    return jnp.dot(a, b, preferred_element_type=F32)


def _mm_nt(a, b):
    return lax.dot_general(a, b, (((1,), (1,)), ((), ())), preferred_element_type=F32)


def _split3(x):
    hi = x.astype(BF16)
    r = x - hi.astype(F32)
    mid = r.astype(BF16)
    lo = (r - mid.astype(F32)).astype(BF16)
    return hi, mid, lo


def _softmax_masked(s, mask, axis):
    s = jnp.where(mask, s, NEG_INF)
    m = jnp.max(s, axis=axis, keepdims=True)
    e = jnp.exp(s - m)
    p = e / jnp.sum(e, axis=axis, keepdims=True)
    return jnp.where(mask, p, 0.0)


def _adaln_kernel(c_ref, w_ref, b_ref, o_ref):
    s = _silu(c_ref[...]).astype(BF16)
    o_ref[...] = _mm(s, w_ref[...].astype(BF16)) + b_ref[...]


def _adaln(c_all, w_ada, b_ada):
    n, d = c_all.shape
    nout = w_ada.shape[1]
    tn = 1024
    return pl.pallas_call(
        _adaln_kernel,
        grid=(nout // tn,),
        in_specs=[pl.BlockSpec((n, d), lambda j: (0, 0)),
                  pl.BlockSpec((d, tn), lambda j: (0, j)),
                  pl.BlockSpec((1, tn), lambda j: (0, j))],
        out_specs=pl.BlockSpec((n, tn), lambda j: (0, j)),
        out_shape=jax.ShapeDtypeStruct((n, nout), F32),
        compiler_params=_cparams("arbitrary"),
        name="adaln",
    )(c_all, w_ada, b_ada.reshape(1, nout))


def _mixer_kernel(x_ref, shift_ref, scale_ref, g_ref, wqkv_ref, wgl_ref, wu_ref, *outs, prompt):
    x = x_ref[0]
    h = _rms(x, g_ref[...]) * (1.0 + scale_ref[0]) + shift_ref[0]
    hb = h.astype(BF16)
    p = _mm(hb, wqkv_ref[...])
    gl = _mm(hb, wgl_ref[...])
    pu = _mm(hb, wu_ref[...])
    u = pu[:, :D_CONV] * jax.nn.sigmoid(pu[:, D_CONV:])
    q = p[:, :D_ATTN] * (SCALE * LOG2E if prompt else SCALE)
    o = D_ATTN
    kvc = p[:, o:o + ROW_W]
    kvs = p[:, o + ROW_W:o + 2 * ROW_W]
    kvw = p[:, o + 2 * ROW_W:o + 3 * ROW_W]
    kvs_t = kvs.T
    kvw_t = kvw.T
    if prompt:
        q_ref, kvc_ref, kvct_ref, kvst_ref, kvwt_ref, ks_ref, vts_ref, kw_ref, vtw_ref, gt_ref, u_ref = outs
        q_ref[0] = q.T.astype(BF16)
        kvc_ref[0] = kvc
        ks_ref[0] = kvs[:, :D_KV].astype(BF16)
        vts_ref[0] = kvs_t[D_KV:, :].astype(BF16)
        kw_ref[0] = kvw[:, :D_KV].astype(BF16)
        vtw_ref[0] = kvw_t[D_KV:, :].astype(BF16)
    else:
        q_ref, kvct_ref, kvst_ref, kvwt_ref, gt_ref, u_ref = outs
        q_ref[0] = q
    kvct_ref[0] = kvc.T
    kvst_ref[0] = kvs_t
    kvwt_ref[0] = kvw_t
    gt_ref[0] = jax.nn.sigmoid(gl).T[:32, :]
    u_ref[0] = u


def _mixer(x, shift, scale, g, wqkv, wgl, wu, *, tm, prompt):
    b, s, d = x.shape
    r = shift.shape[1]
    mod_block = (1, 1, d) if r == 1 else (1, tm, d)
    mod_map = (lambda i, j: (i, 0, 0)) if r == 1 else (lambda i, j: (i, j, 0))
    row = lambda w: pl.BlockSpec((1, tm, w), lambda i, j: (i, j, 0))
    col = lambda w: pl.BlockSpec((1, w, tm), lambda i, j: (i, 0, j))
    const = lambda a: pl.BlockSpec(a.shape, lambda i, j: (0,) * a.ndim)
    rows = lambda w, dt: jax.ShapeDtypeStruct((b, s, w), dt)
    cols = lambda w, dt: jax.ShapeDtypeStruct((b, w, s), dt)
    if prompt:
        out_specs = [col(D_ATTN), row(ROW_W), col(ROW_W), col(ROW_W), col(ROW_W), row(D_KV), col(D_KV), row(D_KV),
                     col(D_KV), col(32), row(D_CONV)]
        out_shape = [cols(D_ATTN, BF16), rows(ROW_W, F32), cols(ROW_W, F32), cols(ROW_W, F32), cols(ROW_W, F32),
                     rows(D_KV, BF16), cols(D_KV, BF16), rows(D_KV, BF16), cols(D_KV, BF16), cols(32, F32),
                     rows(D_CONV, F32)]
    else:
        out_specs = [row(D_ATTN), col(ROW_W), col(ROW_W), col(ROW_W), col(32), row(D_CONV)]
        out_shape = [rows(D_ATTN, F32), cols(ROW_W, F32), cols(ROW_W, F32), cols(ROW_W, F32), cols(32, F32),
                     rows(D_CONV, F32)]
    return pl.pallas_call(
        functools.partial(_mixer_kernel, prompt=prompt),
        grid=(b, s // tm),
        in_specs=[row(d), pl.BlockSpec(mod_block, mod_map), pl.BlockSpec(mod_block, mod_map),
                  const(g), const(wqkv), const(wgl), const(wu)],
        out_specs=out_specs,
        out_shape=out_shape,
        compiler_params=_cparams("arbitrary", "arbitrary"),
        name="mixer_prompt" if prompt else "mixer_sample",
    )(x, shift, scale, g, wqkv, wgl, wu)


def _chunk_part(load_offset, pos_rows, w1_c):
    xc = jnp.concatenate([load_offset(s) for s in range(CMP_STRIDE)], axis=1).astype(BF16)
    n = xc.shape[0]
    both = _mm(jnp.concatenate([xc, pos_rows], axis=0), w1_c)
    return both[:n], both[n:]


def _compress_finish(part, posb, w2_c):
    n = part.shape[0]
    nxt = pltpu.roll(part[:, D_KV:], n - 1, 0)
    pre = part[:, :D_KV] + nxt + posb[0:1, :D_KV] + posb[1:2, D_KV:]
    return _mm(_silu(pre).astype(BF16), w2_c)


def _compress_prompt_kernel(x_ref, w1_ref, w2_ref, pos_ref, kc_ref, vct_ref):
    for c in range(2):
        part, posb = _chunk_part(lambda s: x_ref[0, :, s * ROW_W + c * D_KV:s * ROW_W + (c + 1) * D_KV], pos_ref[c],
                                 w1_ref[c])
        out = _compress_finish(part, posb, w2_ref[c])
        if c == 0:
            kc_ref[0] = out.astype(BF16)
        else:
            vct_ref[0] = out.T.astype(BF16)


def _compress_prompt(kvc, w1big, w2big, posrows):
    b, s, _ = kvc.shape
    n = s // CMP_STRIDE
    x = kvc.reshape(b, n, CHUNK_W)
    const = lambda a: pl.BlockSpec(a.shape, lambda i: (0,) * a.ndim)
    return pl.pallas_call(
        _compress_prompt_kernel,
        grid=(b,),
        in_specs=[pl.BlockSpec((1, n, CHUNK_W), lambda i: (i, 0, 0)), const(w1big), const(w2big), const(posrows)],
        out_specs=[pl.BlockSpec((1, n, D_KV), lambda i: (i, 0, 0)), pl.BlockSpec((1, D_KV, n), lambda i: (i, 0, 0))],
        out_shape=[jax.ShapeDtypeStruct((b, n, D_KV), BF16), jax.ShapeDtypeStruct((b, D_KV, n), BF16)],
        compiler_params=_cparams("arbitrary"),
        name="compress_prompt",
    )(x, w1big, w2big, posrows)


def _compress_sample_kernel(pt_ref, cache_ref, q_ref, w1_ref, w2_ref, pos_ref, oc_ref, idx_ref, pages_ref, rows0_ref,
                            rows1_ref, sem, *, pages, q_pos, n_slc):
    i = pl.program_id(0)
    row_refs = (rows0_ref, rows1_ref)
    cpp = PAGE_SIZE // CMP_STRIDE
    n = pages * cpp

    def page_copy(seq, k, buf):
        return pltpu.make_async_copy(cache_ref.at[pt_ref[seq * pages + k]], pages_ref.at[buf, k], sem.at[buf])

    def fetch(seq, buf):
        for k in range(pages):
            page_copy(seq, k, buf).start()

    @pl.when(i == 0)
    def _():
        fetch(0, 0)

    @pl.when(i + 1 < pl.num_programs(0))
    def _():
        fetch(i + 1, (i + 1) % 2)

    buf = i % 2
    for k in range(pages):
        page_copy(i, k, buf).wait()

    kv = []
    for c in range(2):
        for k in range(pages):
            rows = pages_ref[buf, k, c * D_KV:(c + 1) * D_KV, :].astype(BF16).T.astype(F32)
            for j in range(cpp):
                r0 = (k * cpp + j) * CHUNK_PITCH
                row_refs[c][r0:r0 + CMP_STRIDE, :] = rows[j * CMP_STRIDE:(j + 1) * CMP_STRIDE, :]
    for c in range(2):
        part, posb = _chunk_part(lambda s: row_refs[c][pl.ds(s, n, stride=CHUNK_PITCH), :], pos_ref[c], w1_ref[c])
        kv.append(_compress_finish(part, posb, w2_ref[c]).astype(BF16))
    _sample_cmp_attend(q_ref[0].astype(BF16), kv[0], kv[1], oc_ref, idx_ref, q_pos=q_pos, n_slc=n_slc)


def _compress_sample(cache_t, page_table, qpad, w1big, w2big, posrows, *, q_pos, n_slc):
    nb, n_pages = page_table.shape
    n = n_pages * (PAGE_SIZE // CMP_STRIDE)
    const = lambda a: pl.BlockSpec(a.shape, lambda i, pt: (0,) * a.ndim)
    per_b = lambda w: pl.BlockSpec((1, N_HEADS, w), lambda i, pt: (i, 0, 0))
    return pl.pallas_call(
        functools.partial(_compress_sample_kernel, pages=n_pages, q_pos=q_pos, n_slc=n_slc),
        grid_spec=pltpu.PrefetchScalarGridSpec(
            num_scalar_prefetch=1,
            grid=(nb,),
            in_specs=[pl.BlockSpec(memory_space=pl.ANY), per_b(D_KV), const(w1big), const(w2big), const(posrows)],
            out_specs=[per_b(D_KV), per_b(128)],
            scratch_shapes=[pltpu.VMEM((2, n_pages, ROW_W, PAGE_SIZE), F32),
                            pltpu.VMEM((n * CHUNK_PITCH, D_KV), F32), pltpu.VMEM((n * CHUNK_PITCH, D_KV), F32),
                            pltpu.SemaphoreType.DMA((2,))],
        ),
        out_shape=[jax.ShapeDtypeStruct((nb, N_HEADS, D_KV), F32), jax.ShapeDtypeStruct((nb, N_HEADS, 128), jnp.int32)],
        compiler_params=pltpu.CompilerParams(dimension_semantics=("arbitrary",), vmem_limit_bytes=VMEM_LIMIT,
                                             disable_bounds_checks=True),
        name="compress_sample",
    )(page_table.reshape(-1), cache_t, qpad, w1big, w2big, posrows)


def _overlap(cmp_idx, slc_idx):
    lo = cmp_idx * CMP_STRIDE
    so = slc_idx * SLC_BLOCK
    return (lo <= so + SLC_BLOCK - 1) & (lo + CMP_BLOCK - 1 >= so)


def _forced_importance(imp, blk, cur, n_slc):
    valid = (blk <= cur) & (blk < n_slc)
    forced = (blk == 0) | ((cur - blk >= 0) & (cur - blk < N_LOCAL_BLOCKS))
    return jnp.where(valid & forced, FORCED_SCORE, jnp.where(valid, imp, -1.0))


def _attn_prompt_kernel(qt_ref, kc_ref, vct_ref, ks_ref, vts_ref, kw_ref, vtw_ref, gt_ref, o_ref, sel_ref):
    i = pl.program_id(1)
    seq = ks_ref.shape[1]
    nbp = kc_ref.shape[1]
    n_slc = seq // SLC_BLOCK
    lanes = GQA * Q_TILE
    lane = lax.broadcasted_iota(jnp.int32, (1, lanes), 1)
    qpos = i * Q_TILE + (lane & (Q_TILE - 1))
    qpos_q = qpos[:, :Q_TILE]
    qt = qt_ref[0]
    gt = gt_ref[0]
    win_keys = min(WINDOW + Q_TILE, seq)
    blocks_per_tile = K_TILE // SLC_BLOCK

    ovl = _overlap(lax.broadcasted_iota(jnp.int32, (n_slc, nbp), 1), lax.broadcasted_iota(jnp.int32, (n_slc, nbp), 0))
    ovl = jnp.where(ovl, 1.0, 0.0).astype(BF16)
    blk = lax.broadcasted_iota(jnp.int32, (n_slc, Q_TILE), 0)

    cpos = lax.broadcasted_iota(jnp.int32, (nbp, lanes), 0) * CMP_STRIDE + (CMP_BLOCK - 1)
    cmp_bias = jnp.where(cpos <= qpos, 0.0, NEG_INF)
    any_cmp = jnp.where(qpos >= CMP_BLOCK - 1, 1.0, 0.0)
    w0 = pl.multiple_of(jnp.clip(i * Q_TILE - WINDOW, 0, seq - win_keys), Q_TILE)
    wpos = w0 + lax.broadcasted_iota(jnp.int32, (win_keys, lanes), 0)
    win_bias = jnp.where(wpos <= qpos, jnp.where(wpos >= qpos - WINDOW, 0.0, NEG_INF), NEG_INF)
    n_steps = ((i + 1) * Q_TILE + K_TILE - 1) // K_TILE
    last0 = pl.multiple_of((n_steps - 1) * K_TILE, K_TILE)
    causal_bias = jnp.where(last0 + lax.broadcasted_iota(jnp.int32, (K_TILE, lanes), 0) <= qpos, 0.0, NEG_INF)

    heads = range(KV_HEADS)
    hd = [slice(kvh * HEAD_DIM, (kvh + 1) * HEAD_DIM) for kvh in heads]
    qpad, o_c = [], []
    for kvh in heads:
        qk = jnp.concatenate([qt[(kvh * GQA + g) * HEAD_DIM:(kvh * GQA + g + 1) * HEAD_DIM, :] for g in range(GQA)],
                             axis=1)
        zero = jnp.zeros_like(qk)
        qpad.append(jnp.concatenate([qk, zero] if kvh == 0 else [zero, qk], axis=0))

        s = _mm(kc_ref[0], qpad[kvh]) + cmp_bias
        e = jnp.exp2(s - jnp.max(s, axis=0, keepdims=True))
        p = e * (any_cmp / jnp.sum(e, axis=0, keepdims=True))
        o_c.append(_mm(vct_ref[0, hd[kvh], :], p.astype(BF16)))

        psum = p[:, 0:Q_TILE]
        for g in range(1, GQA):
            psum = psum + p[:, g * Q_TILE:(g + 1) * Q_TILE]
        imp = sum(_mm(ovl, t) for t in _split3(psum))
        impf = _forced_importance(imp, blk, qpos_q // SLC_BLOCK, n_slc)
        groups = [impf[r:r + 8, :] for r in range(0, n_slc, 8)]
        ranks = [jnp.zeros_like(grp) for grp in groups]
        for k in range(n_slc):
            rk = impf[k:k + 1, :]
            for gi, grp in enumerate(groups):
                if gi * 8 > k:
                    ranks[gi] = ranks[gi] + jnp.where(rk >= grp, 1.0, 0.0)
                elif gi * 8 + 7 < k:
                    ranks[gi] = ranks[gi] + jnp.where(rk > grp, 1.0, 0.0)
                else:
                    later = blk[0:8, :] + gi * 8 > k
                    ranks[gi] = ranks[gi] + jnp.where(later, jnp.where(rk >= grp, 1.0, 0.0),
                                                      jnp.where(rk > grp, 1.0, 0.0))
        rank = jnp.concatenate(ranks, axis=0)
        sel = jnp.where(rank < SLC_TOPN, jnp.where(impf >= 0.0, 0.0, NEG_INF), NEG_INF)
        sel_ref[kvh] = jnp.concatenate([sel] * GQA, axis=1)

    def slc_step(t, carry, extra_bias=None):
        k0 = pl.multiple_of(t * K_TILE, K_TILE)
        keys = ks_ref[0, pl.ds(k0, K_TILE), :]
        out = []
        for kvh in heads:
            m, l, acc = carry[kvh]
            sk = _mm(keys, qpad[kvh])
            sk = jnp.concatenate(
                [sk[j * SLC_BLOCK:(j + 1) * SLC_BLOCK, :] + sel_ref[kvh, pl.ds(t * blocks_per_tile + j, 1), :]
                 for j in range(blocks_per_tile)], axis=0)
            if extra_bias is not None:
                sk = sk + extra_bias
            m_new = jnp.maximum(m, jnp.max(sk, axis=0, keepdims=True))
            alpha = jnp.exp2(m - m_new)
            e = jnp.exp2(sk - m_new)
            l_new = alpha * l + jnp.sum(e, axis=0, keepdims=True)
            pv = _mm(vts_ref[0, hd[kvh], pl.ds(k0, K_TILE)], e.astype(BF16))
            out.append((m_new, l_new, alpha * acc + pv))
        return tuple(out)

    init = (jnp.full((1, lanes), NEG_INF, F32), jnp.zeros((1, lanes), F32), jnp.zeros((HEAD_DIM, lanes), F32))
    carry = lax.fori_loop(0, n_steps - 1, slc_step, (init,) * KV_HEADS)
    carry = slc_step(n_steps - 1, carry, causal_bias)

    for kvh in heads:
        _, l_s, acc_s = carry[kvh]
        o_s = acc_s * (1.0 / l_s)

        sw = _mm(kw_ref[0, pl.ds(w0, win_keys), :], qpad[kvh]) + win_bias
        ew = jnp.exp2(sw - jnp.max(sw, axis=0, keepdims=True))
        o_w = _mm(vtw_ref[0, hd[kvh], pl.ds(w0, win_keys)], ew.astype(BF16)) * (
            1.0 / jnp.sum(ew, axis=0, keepdims=True))

        def gate(r):
            return jnp.concatenate([gt[r * N_HEADS + kvh * GQA + g:r * N_HEADS + kvh * GQA + g + 1, :]
                                    for g in range(GQA)], axis=1)
        o_t = gate(0) * o_c[kvh] + gate(1) * o_s + gate(2) * o_w
        for pair in range(GQA // 2):
            two = jnp.concatenate([o_t[:, (2 * pair) * Q_TILE:(2 * pair + 1) * Q_TILE],
                                   o_t[:, (2 * pair + 1) * Q_TILE:(2 * pair + 2) * Q_TILE]], axis=0)
            c0 = kvh * GQA * HEAD_DIM + pair * 2 * HEAD_DIM
            o_ref[0, :, c0:c0 + 2 * HEAD_DIM] = two.T


def _attn_prompt(qt, kc, vct, ks, vts, kw, vtw, gt):
    b, _, s = qt.shape
    nbp = kc.shape[1]
    per_b = lambda shape: pl.BlockSpec((1,) + shape, lambda i, j: (i, 0, 0))
    return pl.pallas_call(
        _attn_prompt_kernel,
        grid=(b, s // Q_TILE),
        in_specs=[pl.BlockSpec((1, D_ATTN, Q_TILE), lambda i, j: (i, 0, j)),
                  per_b((nbp, D_KV)), per_b((D_KV, nbp)),
                  per_b((s, D_KV)), per_b((D_KV, s)), per_b((s, D_KV)), per_b((D_KV, s)),
                  pl.BlockSpec((1, 32, Q_TILE), lambda i, j: (i, 0, j))],
        out_specs=pl.BlockSpec((1, Q_TILE, D_ATTN), lambda i, j: (i, j, 0)),
        out_shape=jax.ShapeDtypeStruct((b, s, D_ATTN), F32),
        scratch_shapes=[pltpu.VMEM((KV_HEADS, s // SLC_BLOCK, GQA * Q_TILE), F32)],
        compiler_params=_cparams("arbitrary", "arbitrary"),
        name="attn_prompt",
    )(qt, kc, vct, ks, vts, kw, vtw, gt)


def _sample_cmp_attend(q, kc, vc, oc_ref, idx_ref, *, q_pos, n_slc):
    nb = kc.shape[0]
    s = _mm_nt(q, kc)
    cpos = lax.broadcasted_iota(jnp.int32, s.shape, 1) * CMP_STRIDE + (CMP_BLOCK - 1)
    p = _softmax_masked(s, cpos <= q_pos, 1)
    oc_ref[0] = _mm(p.astype(BF16), vc)

    nsp = idx_ref.shape[2] * 2
    group_sums = [jnp.sum(p[k * GQA:(k + 1) * GQA, :], axis=0, keepdims=True) for k in range(KV_HEADS)]
    psum = jnp.concatenate(group_sums + [jnp.zeros((N_HEADS - KV_HEADS, nb), F32)], axis=0)
    ovl = _overlap(lax.broadcasted_iota(jnp.int32, (nb, nsp), 0), lax.broadcasted_iota(jnp.int32, (nb, nsp), 1))
    ovl = jnp.where(ovl, 1.0, 0.0).astype(BF16)
    imp = sum(_mm(t, ovl) for t in _split3(psum))
    blk = lax.broadcasted_iota(jnp.int32, imp.shape, 1)
    impf = _forced_importance(imp, blk, q_pos // SLC_BLOCK, n_slc)
    eye = jnp.where(lax.broadcasted_iota(jnp.int32, (nsp, nsp), 0) == lax.broadcasted_iota(jnp.int32, (nsp, nsp), 1),
                    1.0, 0.0).astype(BF16)
    imp_t = sum(_mm_nt(eye, t) for t in _split3(impf))
    k_idx = lax.broadcasted_iota(jnp.int32, (nsp, nsp), 0)
    j_idx = lax.broadcasted_iota(jnp.int32, (nsp, nsp), 1)
    ranks = []
    for r in range(KV_HEADS):
        col, row = imp_t[:, r:r + 1], impf[r:r + 1, :]
        beats = jnp.where(col > row, 1.0, jnp.where(col == row, jnp.where(k_idx < j_idx, 1.0, 0.0), 0.0))
        ranks.append(jnp.sum(beats, axis=0, keepdims=True))
    rank = jnp.concatenate(ranks + [jnp.full((N_HEADS - KV_HEADS, nsp), float(nsp), F32)], axis=0)
    blk_f = blk.astype(F32)
    slot = lax.broadcasted_iota(jnp.int32, (N_HEADS, idx_ref.shape[2]), 1)
    idx = jnp.zeros((N_HEADS, idx_ref.shape[2]), F32)
    for t in range(SLC_TOPN):
        chosen = jnp.sum(jnp.where(rank == float(t), blk_f, 0.0), axis=1, keepdims=True)
        idx = idx + jnp.where(slot == t, chosen, 0.0)
    idx_ref[0] = idx.astype(jnp.int32)


def _attn_sample_kernel(idx_ref, pt_ref, cache_ref, q_ref, oc_ref, gt_ref, kvst_ref, kvwt_ref, win_ref, o_ref, nwin_ref,
                        blk_ref, sem, *, n_past_slc, n_pages):
    n_sel = KV_HEADS * SLC_TOPN
    b = pl.program_id(0)
    nb = kvst_ref.shape[1]
    sub = PAGE_SIZE // SLC_BLOCK

    def block_copy(seq, s, buf):
        j = jnp.minimum(idx_ref[seq * n_sel + s], n_past_slc - 1)
        page = pt_ref[seq * n_pages + j // sub]
        return pltpu.make_async_copy(cache_ref.at[page, :, s // SLC_TOPN], blk_ref.at[buf, s], sem.at[buf])

    def fetch(seq, buf):
        for s in range(n_sel):
            block_copy(seq, s, buf).start()

    @pl.when(b == 0)
    def _():
        fetch(0, 0)

    @pl.when(b + 1 < pl.num_programs(0))
    def _():
        fetch(b + 1, (b + 1) % 2)

    buf = b % 2
    for s in range(n_sel):
        block_copy(b, s, buf).wait()

    q = q_ref[0].astype(BF16)
    head_kvh = lax.broadcasted_iota(jnp.int32, (N_HEADS, 1), 0) // GQA
    mine = lax.broadcasted_iota(jnp.int32, (1, nb), 1) == b
    lane_half = lax.broadcasted_iota(jnp.int32, (1, PAGE_SIZE), 1) // SLC_BLOCK

    def attend(kt, vt, valid):
        s = jnp.where(valid, _mm(q, kt), NEG_INF)
        e = jnp.where(valid, jnp.exp(s - jnp.max(s, axis=1, keepdims=True)), 0.0)
        return _mm_nt(e.astype(BF16), vt) / jnp.sum(e, axis=1, keepdims=True)

    kvst = kvst_ref[...]
    kvwt = kvwt_ref[...]
    o_s = jnp.zeros((N_HEADS, HEAD_DIM), F32)
    o_w = jnp.zeros((N_HEADS, HEAD_DIM), F32)
    win = win_ref[0]
    for kvh in range(KV_HEADS):
        k_rows = slice(kvh * HEAD_DIM, (kvh + 1) * HEAD_DIM)
        v_rows = slice(D_KV + kvh * HEAD_DIM, D_KV + (kvh + 1) * HEAD_DIM)
        valid = []
        n_new = jnp.int32(0)
        for t in range(SLC_TOPN):
            j = idx_ref[(b * KV_HEADS + kvh) * SLC_TOPN + t]
            past = j < n_past_slc
            half = jnp.minimum(j, n_past_slc - 1) % (PAGE_SIZE // SLC_BLOCK)
            valid.append((lane_half == half) & past)
            n_new = n_new + jnp.where(past, 0, 1)
        valid.append(mine & (n_new > 0))
        kt = jnp.concatenate([blk_ref[buf, kvh * SLC_TOPN + t, 0] for t in range(SLC_TOPN)] + [kvst[k_rows, :]],
                             axis=1)
        vt = jnp.concatenate([blk_ref[buf, kvh * SLC_TOPN + t, 1] for t in range(SLC_TOPN)] + [kvst[v_rows, :]],
                             axis=1)
        o_k = attend(kt.astype(BF16), vt.astype(BF16), jnp.concatenate(valid, axis=1))
        o_s = jnp.where(head_kvh == kvh, o_k, o_s)
        kt = jnp.concatenate([win[k_rows, :], kvwt[k_rows, :]], axis=1)
        vt = jnp.concatenate([win[v_rows, :], kvwt[v_rows, :]], axis=1)
        valid_w = jnp.concatenate([jnp.full((1, win.shape[1]), True), mine], axis=1)
        o_k = attend(kt.astype(BF16), vt.astype(BF16), valid_w)
        o_w = jnp.where(head_kvh == kvh, o_k, o_w)

    oc = oc_ref[0]
    o_c = jnp.where(head_kvh == 0, oc[:, :HEAD_DIM], oc[:, HEAD_DIM:])
    gates = jnp.sum(jnp.where(mine, gt_ref[...], 0.0), axis=1, keepdims=True)
    o_ref[0] = (gates[0:N_HEADS] * o_c + gates[N_HEADS:2 * N_HEADS] * o_s + gates[2 * N_HEADS:3 * N_HEADS] * o_w)

    new_col = jnp.sum(jnp.where(mine, kvwt, 0.0), axis=1, keepdims=True)
    last = lax.broadcasted_iota(jnp.int32, win.shape, 1) == win.shape[1] - 1
    nwin_ref[0] = jnp.where(last, new_col, pltpu.roll(win, win.shape[1] - 1, 1))


def _attn_sample(idx, page_table, cache_t, q, o_c, gt, kvst, kvwt, win_t, *, n_past_slc):
    nb = q.shape[0]
    win_buf = win_t.shape[2]
    n_pages = page_table.shape[1]
    per_b = lambda shape: pl.BlockSpec((1,) + shape, lambda i, a, c: (i,) + (0,) * len(shape))
    const = lambda a: pl.BlockSpec(a.shape, lambda i, x, c: (0,) * a.ndim)
    n_sel = KV_HEADS * SLC_TOPN
    return pl.pallas_call(
        functools.partial(_attn_sample_kernel, n_past_slc=n_past_slc, n_pages=n_pages),
        grid_spec=pltpu.PrefetchScalarGridSpec(
            num_scalar_prefetch=2,
            grid=(nb,),
            in_specs=[pl.BlockSpec(memory_space=pl.ANY), per_b((N_HEADS, HEAD_DIM)), per_b((N_HEADS, D_KV)),
                      const(gt), const(kvst), const(kvwt), per_b((ROW_W, win_buf))],
            out_specs=[per_b((N_HEADS, HEAD_DIM)), per_b((ROW_W, win_buf))],
            scratch_shapes=[pltpu.VMEM((2, n_sel, 2, HEAD_DIM, PAGE_SIZE), F32), pltpu.SemaphoreType.DMA((2,))],
        ),
        out_shape=[jax.ShapeDtypeStruct((nb, N_HEADS, HEAD_DIM), F32), jax.ShapeDtypeStruct(win_t.shape, F32)],
        compiler_params=pltpu.CompilerParams(dimension_semantics=("arbitrary",), vmem_limit_bytes=VMEM_LIMIT,
                                             disable_bounds_checks=True),
        name="attn_sample",
    )(idx.reshape(-1), page_table.reshape(-1), cache_t, q, o_c, gt, kvst, kvwt, win_t)


def _conv_tail(y, b_ref, lg_ref, lb_ref):
    y = y + b_ref[...]
    yc = y - jnp.mean(y, axis=-1, keepdims=True)
    yn = yc * lax.rsqrt(jnp.mean(yc * yc, axis=-1, keepdims=True) + EPS)
    return _silu(yn * lg_ref[...] + lb_ref[...])


def _conv_prompt_kernel(u_ref, halo_ref, w_ref, b_ref, lg_ref, lb_ref, o_ref, buf_ref, shift_ref, *, halo):
    tm = u_ref.shape[1]
    j = pl.program_id(1)
    buf_ref[0:halo, :] = jnp.where(j > 0, halo_ref[0], 0.0)
    buf_ref[halo:halo + tm, :] = u_ref[0]
    buf_ref[halo + tm:, :] = jnp.zeros((8, buf_ref.shape[1]), F32)
    lead = halo - (CONV_WIDTH - 1)
    for ph in range(8):
        shift_ref[ph] = buf_ref[ph:ph + tm + halo, :]

    def chunk(c, carry):
        r0 = pl.multiple_of(c * CONV_ROWS, CONV_ROWS)
        y = None
        for w in range(CONV_WIDTH):
            o = lead + w
            tap = shift_ref[o % 8, pl.ds(r0 + o // 8 * 8, CONV_ROWS), :] * w_ref[w:w + 1, :]
            y = tap if y is None else y + tap
        o_ref[0, pl.ds(r0, CONV_ROWS), :] = _conv_tail(y, b_ref, lg_ref, lb_ref)
        return carry

    lax.fori_loop(0, tm // CONV_ROWS, chunk, 0, unroll=4)


def _conv_prompt(u, w_dw, b_dw, ln_g, ln_b, *, tm):
    b, s, dc = u.shape
    halo = 32
    per = tm // halo
    const = lambda a: pl.BlockSpec(a.shape, lambda i, j: (0,) * a.ndim)
    return pl.pallas_call(
        functools.partial(_conv_prompt_kernel, halo=halo),
        grid=(b, s // tm),
        in_specs=[pl.BlockSpec((1, tm, dc), lambda i, j: (i, j, 0)),
                  pl.BlockSpec((1, halo, dc), lambda i, j: (i, jnp.maximum(j * per - 1, 0), 0)),
                  const(w_dw), const(b_dw), const(ln_g), const(ln_b)],
        out_specs=pl.BlockSpec((1, tm, dc), lambda i, j: (i, j, 0)),
        out_shape=jax.ShapeDtypeStruct((b, s, dc), F32),
        scratch_shapes=[pltpu.VMEM((tm + halo + 8, dc), F32), pltpu.VMEM((8, tm + halo, dc), F32)],
        compiler_params=_cparams("arbitrary", "arbitrary"),
        name="conv_prompt",
    )(u, u, w_dw, b_dw, ln_g, ln_b)


def _conv_sample_kernel(st_ref, u_ref, w_ref, b_ref, lg_ref, lb_ref, o_ref):
    hist = CONV_WIDTH - 1
    y = u_ref[...] * w_ref[hist:hist + 1, :]
    for w in range(hist):
        y = y + st_ref[w] * w_ref[w:w + 1, :]
    o_ref[...] = _conv_tail(y, b_ref, lg_ref, lb_ref)


def _conv_sample(state_t, u, w_dw, b_dw, ln_g, ln_b, *, sb):
    nb, dc = u.shape
    hist = state_t.shape[0]
    const = lambda a: pl.BlockSpec(a.shape, lambda i: (0,) * a.ndim)
    return pl.pallas_call(
        _conv_sample_kernel,
        grid=(nb // sb,),
        in_specs=[pl.BlockSpec((hist, sb, dc), lambda i: (0, i, 0)), pl.BlockSpec((sb, dc), lambda i: (i, 0)),
                  const(w_dw), const(b_dw), const(ln_g), const(ln_b)],
        out_specs=pl.BlockSpec((sb, dc), lambda i: (i, 0)),
        out_shape=jax.ShapeDtypeStruct((nb, dc), F32),
        compiler_params=_cparams("arbitrary"),
        name="conv_sample",
    )(state_t, u, w_dw, b_dw, ln_g, ln_b)


def _finish_kernel(x_ref, oa_ref, ocv_ref, gate1_ref, shift2_ref, scale2_ref, ga_ref, gc_ref, wout_ref, g2_ref,
                   wr_ref, br_ref, x1_ref, *outs, sorted_moe):
    mixed = jnp.concatenate([_rms(oa_ref[0], ga_ref[...]), _rms(ocv_ref[0], gc_ref[...])], axis=1)
    mix = _mm(mixed.astype(BF16), wout_ref[...])
    x1 = x_ref[0] + gate1_ref[0] * mix
    x1_ref[0] = x1
    h2 = _rms(x1, g2_ref[...]) * (1.0 + scale2_ref[0]) + shift2_ref[0]

    h_hi, h_mid, _ = _split3(h2)
    w_hi, w_mid, _ = _split3(wr_ref[...])
    lg = (_mm(h_hi, w_hi) + _mm(h_hi, w_mid)
          + _mm(h_mid, w_hi)) + br_ref[...]
    lane = lax.broadcasted_iota(jnp.int32, lg.shape, 1)
    lane_f = lane.astype(F32)
    is_group = (lane >= N_EXPERTS) & (lane < N_EXPERTS + N_GROUPS)
    gl = jnp.where(is_group, lg, NEG_INF)
    g_max = jnp.max(gl, axis=1, keepdims=True)
    p_top = 1.0 / jnp.sum(jnp.where(is_group, jnp.exp(gl - g_max), 0.0), axis=1, keepdims=True)
    g_lane = jnp.min(jnp.where(gl == g_max, lane_f, 1e9), axis=1, keepdims=True)
    in_group = (lane < N_EXPERTS) & ((lane // EPG).astype(F32) == g_lane - N_EXPERTS)
    el = jnp.where(in_group, lg, NEG_INF)
    l1 = jnp.max(el, axis=1, keepdims=True)
    i1 = jnp.min(jnp.where(el == l1, lane_f, 1e9), axis=1, keepdims=True)
    el2 = jnp.where(lane_f == i1, NEG_INF, el)
    l2 = jnp.max(el2, axis=1, keepdims=True)
    i2 = jnp.min(jnp.where(el2 == l2, lane_f, 1e9), axis=1, keepdims=True)
    r = jnp.exp(l2 - l1)
    w1 = p_top / (1.0 + r)
    w2 = p_top * r / (1.0 + r)
    if not sorted_moe:
        h2_ref, comb_ref = outs
        h2_ref[0] = h2.astype(BF16)
        comb_ref[0] = jnp.where(lane_f == i1, w1, jnp.where(lane_f == i2, w2, 0.0))
        return
    route_ref, rows_ref, count_ref = outs
    for k in range(TOKEN_TILE_ROWS):
        rows_ref[0, pl.ds(k, h2.shape[0], stride=TOKEN_TILE_ROWS), :] = h2[:, k * 128:(k + 1) * 128]
    route_ref[0] = jnp.where(lane == 0, i1, jnp.where(lane == 1, i2, jnp.where(lane == 2, w1, jnp.where(
        lane == 3, w2, 0.0))))

    @pl.when((pl.program_id(0) == 0) & (pl.program_id(1) == 0))
    def _():
        count_ref[...] = jnp.zeros_like(count_ref)

    picks = jnp.where(lane_f == i1, 1.0, 0.0) + jnp.where(lane_f == i2, 1.0, 0.0)
    count_ref[...] += jnp.sum(picks, axis=0, keepdims=True)


def _finish(x, o_attn, o_conv, gate1, shift2, scale2, ga, gc, w_out, g2, w_route, b_route, *, tm, sorted_moe):
    b, s, d = x.shape
    r = gate1.shape[1]
    mod_block = (1, 1, d) if r == 1 else (1, tm, d)
    mod_map = (lambda i, j: (i, 0, 0)) if r == 1 else (lambda i, j: (i, j, 0))
    mod = pl.BlockSpec(mod_block, mod_map)
    row = lambda w: pl.BlockSpec((1, tm, w), lambda i, j: (i, j, 0))
    const = lambda a: pl.BlockSpec(a.shape, lambda i, j: (0,) * a.ndim)
    shape = lambda w, dt: jax.ShapeDtypeStruct((b, s, w), dt)
    if sorted_moe:
        tiles = pl.BlockSpec((1, tm * TOKEN_TILE_ROWS, 128), lambda i, j: (i, j, 0))
        out_specs = [row(d), row(128), tiles, pl.BlockSpec((8, 128), lambda i, j: (0, 0))]
        out_shape = [shape(d, F32), shape(128, F32), jax.ShapeDtypeStruct((b, s * TOKEN_TILE_ROWS, 128), F32),
                     jax.ShapeDtypeStruct((8, 128), F32)]
    else:
        out_specs = [row(d), row(d), row(128)]
        out_shape = [shape(d, F32), shape(d, BF16), shape(128, F32)]
    return pl.pallas_call(
        functools.partial(_finish_kernel, sorted_moe=sorted_moe),
        grid=(b, s // tm),
        in_specs=[row(d), row(D_ATTN), row(D_CONV), mod, mod, mod, const(ga), const(gc), const(w_out), const(g2),
                  const(w_route), const(b_route)],
        out_specs=out_specs,
        out_shape=out_shape,
        compiler_params=_cparams("arbitrary", "arbitrary"),
        name="finish_sorted" if sorted_moe else "finish",
    )(x, o_attn, o_conv, gate1, shift2, scale2, ga, gc, w_out, g2, w_route, b_route)


def _moe_kernel(h_ref, comb_ref, x1_ref, gate2_ref, wg_ref, wu_ref, wd_ref, gf_ref, y_ref, acc_ref):
    e = pl.program_id(2)

    @pl.when(e == 0)
    def _():
        acc_ref[...] = jnp.zeros_like(acc_ref)

    h = h_ref[0]
    a = _mm(h, wg_ref[0].astype(BF16))
    u = _mm(h, wu_ref[0].astype(BF16))
    comb = comb_ref[0]
    lane = lax.broadcasted_iota(jnp.int32, comb.shape, 1)
    cw = jnp.sum(jnp.where(lane == e, comb, 0.0), axis=1, keepdims=True)
    hid = (_silu(a) * u * cw).astype(BF16)
    acc_ref[...] += _mm(hid, wd_ref[0].astype(BF16))

    @pl.when(e == pl.num_programs(2) - 1)
    def _():
        y = x1_ref[0] + gate2_ref[0] * acc_ref[...]
        y_ref[0] = _rms(y, gf_ref[...])


def _moe(h2, comb, x1, gate2, w_gate, w_up, w_down, final_g, *, tm):
    b, s, d = x1.shape
    r = gate2.shape[1]
    mod_block = (1, 1, d) if r == 1 else (1, tm, d)
    mod_map = (lambda i, j, e: (i, 0, 0)) if r == 1 else (lambda i, j, e: (i, j, 0))
    row = lambda w: pl.BlockSpec((1, tm, w), lambda i, j, e: (i, j, 0))
    ne, _, de = w_gate.shape
    return pl.pallas_call(
        _moe_kernel,
        grid=(b, s // tm, ne),
        in_specs=[row(d), row(128), row(d), pl.BlockSpec(mod_block, mod_map),
                  pl.BlockSpec((1, d, de), lambda i, j, e: (e, 0, 0)),
                  pl.BlockSpec((1, d, de), lambda i, j, e: (e, 0, 0)),
                  pl.BlockSpec((1, de, d), lambda i, j, e: (e, 0, 0)),
                  pl.BlockSpec((1, d), lambda i, j, e: (0, 0))],
        out_specs=row(d),
        out_shape=jax.ShapeDtypeStruct((b, s, d), F32),
        scratch_shapes=[pltpu.VMEM((tm, d), F32)],
        compiler_params=_cparams("arbitrary", "arbitrary", "arbitrary"),
        name="moe",
    )(h2, comb, x1, gate2, w_gate, w_up, w_down, final_g)


def _moe_slots_kernel(route_ref, base_ref, slot_ref, run_ref):
    @pl.when(pl.program_id(0) == 0)
    def _():
        run_ref[...] = jnp.zeros_like(run_ref)

    route = route_ref[...]
    tm = route.shape[0]
    lane = lax.broadcasted_iota(jnp.int32, route.shape, 1).astype(F32)
    first = jnp.where(lane == route[:, 0:1], 1.0, 0.0)
    second = jnp.where(lane == route[:, 1:2], 1.0, 0.0)
    picks = first + second
    earlier = lax.broadcasted_iota(jnp.int32, (tm, tm), 1) < lax.broadcasted_iota(jnp.int32, (tm, tm), 0)
    seen = _mm(jnp.where(earlier, 1.0, 0.0).astype(BF16), picks.astype(BF16)) + (run_ref[...] + base_ref[...])
    slot0 = jnp.sum(first * seen, axis=1, keepdims=True)
    slot1 = jnp.sum(second * seen, axis=1, keepdims=True)
    slot_ref[...] = jnp.where(lane == 0.0, slot0, jnp.where(lane == 1.0, slot1, 0.0)).astype(jnp.int32)
    run_ref[...] += jnp.sum(picks, axis=0, keepdims=True)


def _moe_slots(route, base, *, tm):
    n = route.shape[0]
    return pl.pallas_call(
        _moe_slots_kernel,
        grid=(n // tm,),
        in_specs=[pl.BlockSpec((tm, 128), lambda j: (j, 0)), pl.BlockSpec((1, 128), lambda j: (0, 0))],
        out_specs=pl.BlockSpec((tm, 128), lambda j: (j, 0)),
        out_shape=jax.ShapeDtypeStruct((n, 128), jnp.int32),
        scratch_shapes=[pltpu.VMEM((1, 128), F32)],
        compiler_params=_cparams("arbitrary"),
        name="moe_slots",
    )(route, base)


def _row_copy(src_ref, src_row, dst_ref, dst_row, sem):
    src = pl.ds(pl.multiple_of(src_row * TOKEN_TILE_ROWS, TOKEN_TILE_ROWS), TOKEN_TILE_ROWS)
    dst = pl.ds(pl.multiple_of(dst_row * TOKEN_TILE_ROWS, TOKEN_TILE_ROWS), TOKEN_TILE_ROWS)
    return pltpu.make_async_copy(src_ref.at[src, :], dst_ref.at[dst, :], sem)


def _token_rows(ref, n):
    return jnp.concatenate([ref[pl.ds(k, n, stride=TOKEN_TILE_ROWS), :] for k in range(TOKEN_TILE_ROWS)], axis=1)


def _moe_scatter_kernel(s0_ref, s1_ref, h_ref, init_ref, sorted_ref, sem):
    del init_ref
    j = pl.program_id(0)
    tm = h_ref.shape[0] // TOKEN_TILE_ROWS

    def start(r, carry):
        _row_copy(h_ref, r, sorted_ref, s0_ref[j * tm + r], sem).start()
        _row_copy(h_ref, r, sorted_ref, s1_ref[j * tm + r], sem).start()
        return carry

    def wait(r, carry):
        _row_copy(h_ref, 0, sorted_ref, 0, sem).wait()
        _row_copy(h_ref, 0, sorted_ref, 0, sem).wait()
        return carry

    lax.fori_loop(0, tm, start, 0, unroll=8)
    lax.fori_loop(0, tm, wait, 0, unroll=8)


def _moe_scatter(slot0, slot1, rows, n_slots, *, tm):
    w = rows.shape[1]
    n = rows.shape[0] // TOKEN_TILE_ROWS
    tm, n_slots = tm * TOKEN_TILE_ROWS, n_slots * TOKEN_TILE_ROWS
    n = n * TOKEN_TILE_ROWS
    return pl.pallas_call(
        _moe_scatter_kernel,
        grid_spec=pltpu.PrefetchScalarGridSpec(
            num_scalar_prefetch=2,
            grid=(n // tm,),
            in_specs=[pl.BlockSpec((tm, w), lambda j, a, b: (j, 0)), pl.BlockSpec(memory_space=pl.ANY)],
            out_specs=pl.BlockSpec(memory_space=pl.ANY),
            scratch_shapes=[pltpu.SemaphoreType.DMA(())],
        ),
        out_shape=jax.ShapeDtypeStruct((n_slots, w), rows.dtype),
        input_output_aliases={3: 0},
        compiler_params=pltpu.CompilerParams(dimension_semantics=("arbitrary",), vmem_limit_bytes=VMEM_LIMIT,
                                             disable_bounds_checks=True),
        name="moe_scatter",
    )(slot0, slot1, rows, jnp.zeros((n_slots, w), rows.dtype))


def _moe_experts_kernel(te_ref, na_ref, x_ref, wg_ref, wu_ref, wd_ref, o_ref):
    t = pl.program_id(0)

    @pl.when(t < na_ref[0])
    def _():
        xb = _token_rows(x_ref, MOE_TILE).astype(BF16)
        a = _mm(xb, wg_ref[0].astype(BF16))
        u = _mm(xb, wu_ref[0].astype(BF16))
        y = _mm((_silu(a) * u).astype(BF16), wd_ref[0].astype(BF16))
        for k in range(TOKEN_TILE_ROWS):
            o_ref[pl.ds(k, MOE_TILE, stride=TOKEN_TILE_ROWS), :] = y[:, k * 128:(k + 1) * 128]

    @pl.when(t >= na_ref[0])
    def _():
        o_ref[...] = jnp.zeros_like(o_ref)


def _moe_experts(tile_expert, n_active, sorted_x, w_gate, w_up, w_down):
    w = sorted_x.shape[1]
    n_slots = sorted_x.shape[0] // TOKEN_TILE_ROWS
    ne, d, de = w_gate.shape
    x_map = lambda t, te, na: (jnp.minimum(t, na[0] - 1), 0)
    w_map = lambda t, te, na: (te[t], 0, 0)
    tile_rows = MOE_TILE * TOKEN_TILE_ROWS
    return pl.pallas_call(
        _moe_experts_kernel,
        grid_spec=pltpu.PrefetchScalarGridSpec(
            num_scalar_prefetch=2,
            grid=(n_slots // MOE_TILE,),
            in_specs=[pl.BlockSpec((tile_rows, w), x_map), pl.BlockSpec((1, d, de), w_map),
                      pl.BlockSpec((1, d, de), w_map), pl.BlockSpec((1, de, d), w_map)],
            out_specs=pl.BlockSpec((tile_rows, w), lambda t, te, na: (t, 0)),
        ),
        out_shape=jax.ShapeDtypeStruct(sorted_x.shape, F32),
        compiler_params=_cparams("arbitrary"),
        name="moe_experts",
    )(tile_expert, n_active, sorted_x, w_gate, w_up, w_down)


def _moe_combine_kernel(s0_ref, s1_ref, y_hbm, route_ref, x1_ref, gate2_ref, gf_ref, o_ref, a0_ref, b0_ref, a1_ref,
                        b1_ref, sem):
    j = pl.program_id(0)
    tm = x1_ref.shape[0]
    bufs = ((a0_ref, b0_ref), (a1_ref, b1_ref))

    def issue(step, slot):
        a_ref, b_ref = bufs[slot]

        def body(r, carry):
            _row_copy(y_hbm, s0_ref[step * tm + r], a_ref, r, sem.at[slot]).start()
            _row_copy(y_hbm, s1_ref[step * tm + r], b_ref, r, sem.at[slot]).start()
            return carry
        lax.fori_loop(0, tm, body, 0, unroll=8)

    def finish(slot):
        a_ref, b_ref = bufs[slot]

        def wait(r, carry):
            _row_copy(y_hbm, 0, a_ref, 0, sem.at[slot]).wait()
            _row_copy(y_hbm, 0, b_ref, 0, sem.at[slot]).wait()
            return carry
        lax.fori_loop(0, tm, wait, 0, unroll=8)
        route = route_ref[...]
        moe = route[:, 2:3] * _token_rows(a_ref, tm) + route[:, 3:4] * _token_rows(b_ref, tm)
        o_ref[...] = _rms(x1_ref[...] + gate2_ref[0] * moe, gf_ref[...])

    @pl.when(j == 0)
    def _():
        issue(0, 0)

    for slot in range(2):
        @pl.when((j % 2 == slot) & (j + 1 < pl.num_programs(0)))
        def _():
            issue(j + 1, 1 - slot)

        @pl.when(j % 2 == slot)
        def _():
            finish(slot)


def _moe_combine(slot0, slot1, y_sorted, route, x1, gate2, final_g, *, tm, rows_per_mod):
    n, d = x1.shape
    tiles = pltpu.VMEM((tm * TOKEN_TILE_ROWS, 128), F32)
    return pl.pallas_call(
        _moe_combine_kernel,
        grid_spec=pltpu.PrefetchScalarGridSpec(
            num_scalar_prefetch=2,
            grid=(n // tm,),
            in_specs=[pl.BlockSpec(memory_space=pl.ANY), pl.BlockSpec((tm, 128), lambda j, a, b: (j, 0)),
                      pl.BlockSpec((tm, d), lambda j, a, b: (j, 0)),
                      pl.BlockSpec((1, 1, d), lambda j, a, b: (j * tm // rows_per_mod, 0, 0)),
                      pl.BlockSpec((1, d), lambda j, a, b: (0, 0))],
            out_specs=pl.BlockSpec((tm, d), lambda j, a, b: (j, 0)),
            scratch_shapes=[tiles, tiles, tiles, tiles, pltpu.SemaphoreType.DMA((2,))],
        ),
        out_shape=jax.ShapeDtypeStruct((n, d), F32),
        compiler_params=pltpu.CompilerParams(dimension_semantics=("arbitrary",), vmem_limit_bytes=VMEM_LIMIT,
                                             disable_bounds_checks=True),
        name="moe_combine",
    )(slot0, slot1, y_sorted, route, x1, gate2, final_g)


def _moe_sorted(route, counts, rows, x1, gate2, w_gate, w_up, w_down, final_g):
    b, s, d = x1.shape
    n = b * s
    ne = w_gate.shape[0]
    n_tiles = 2 * n // MOE_TILE + ne
    cnt = counts[0, :ne].astype(jnp.int32)
    padded = (cnt + MOE_TILE - 1) // MOE_TILE * MOE_TILE
    ends = jnp.sum(jnp.where(jnp.arange(ne)[:, None] <= jnp.arange(ne)[None, :], padded[:, None], 0), axis=0)
    base = jnp.pad((ends - padded).astype(F32), (0, 128 - ne)).reshape(1, 128)
    n_active = (ends[-1] // MOE_TILE).reshape(1)
    tile_start = jnp.arange(n_tiles, dtype=jnp.int32) * MOE_TILE
    tile_expert = jnp.minimum(jnp.sum((ends[None, :] <= tile_start[:, None]).astype(jnp.int32), axis=1), ne - 1)

    route2 = route.reshape(n, 128)
    slots = _moe_slots(route2, base, tm=512)
    slot0, slot1 = slots[:, 0], slots[:, 1]
    sorted_x = _moe_scatter(slot0, slot1, rows.reshape(n * TOKEN_TILE_ROWS, 128), n_tiles * MOE_TILE, tm=512)
    y_sorted = _moe_experts(tile_expert, n_active, sorted_x, w_gate, w_up, w_down)
    y = _moe_combine(slot0, slot1, y_sorted, route2, x1.reshape(n, d), gate2, final_g, tm=256, rows_per_mod=s)
    return y.reshape(b, s, d)


def _prep_w_in(w_in):
    o_gl = D_ATTN + 3 * ROW_W
    wqkv = w_in[:, :o_gl].astype(BF16)
    wgl = w_in[:, o_gl:o_gl + 3 * N_HEADS]
    pad = jnp.zeros((w_in.shape[0], 128 - 3 * N_HEADS), w_in.dtype)
    wgl = jnp.concatenate([wgl.reshape(-1, N_HEADS, 3).transpose(0, 2, 1).reshape(-1, 3 * N_HEADS), pad], axis=1)
    wu = w_in[:, o_gl + 3 * N_HEADS:].astype(BF16)
    return wqkv, wgl.astype(BF16), wu


def _prep_compress(w_cmp1, pos_cmp, w_cmp2):
    ratio = CMP_BLOCK // CMP_STRIDE
    eye = jnp.eye(KV_HEADS, dtype=w_cmp1.dtype)
    w1 = w_cmp1.reshape(2, ratio, CMP_STRIDE, HEAD_DIM, HEAD_DIM)
    w1big = jnp.einsum('crsdf,kj->cskdrjf', w1, eye).reshape(2, CMP_STRIDE * D_KV, ratio * D_KV).astype(BF16)
    w2big = jnp.einsum('cfd,kj->ckfjd', w_cmp2, eye).reshape(2, D_KV, D_KV).astype(BF16)
    pos = pos_cmp.reshape(2, ratio, CMP_STRIDE, 1, HEAD_DIM)
    pos = jnp.broadcast_to(pos, (2, ratio, CMP_STRIDE, KV_HEADS, HEAD_DIM)).reshape(2, ratio, CMP_STRIDE * D_KV)
    posrows = jnp.concatenate([pos, jnp.zeros((2, 8 - ratio, CMP_STRIDE * D_KV), pos.dtype)], axis=1).astype(BF16)
    return w1big, w2big, posrows


def _prep_router(w_group, b_group, w_router, b_router):
    d = w_group.shape[0]
    pad = 128 - N_EXPERTS - N_GROUPS
    w = jnp.concatenate([w_router, w_group, jnp.zeros((d, pad), w_group.dtype)], axis=1)
    b = jnp.concatenate([b_router, b_group, jnp.zeros((pad,), b_group.dtype)]).reshape(1, 128)
    return w, b


def kernel(x_prompt, x_sample, cache_cmp_kv, cache_slc_kv, state_win_kv, state_conv, page_table, c_prompt, c_sample,
           norm1_g, w_ada, b_ada, w_in, w_cmp1, pos_cmp, w_cmp2, w_dw, b_dw, conv_ln_g, conv_ln_b, g_attn_out,
           g_conv_out, w_out, norm2_g, w_group, b_group, w_router, b_router, w_gate, w_up, w_down, final_g):
    depth = norm1_g.shape[0]
    assert depth == 1 and x_sample.shape[1] == 1
    bp, seq, d = x_prompt.shape
    nb = x_sample.shape[0]
    n_pages = page_table.shape[1]
    past_len = n_pages * PAGE_SIZE
    n_past_slc = past_len // SLC_BLOCK
    win_buf = state_win_kv.shape[2]
    assert win_buf == WINDOW and seq % K_TILE == 0 and seq >= WINDOW + Q_TILE
    l = 0
    row2 = lambda a: a.reshape(1, -1)

    mods = _adaln(jnp.concatenate([c_prompt, c_sample], axis=0), w_ada[l], b_ada[l]).reshape(bp + nb, 6, d)
    mods_p = [mods[:bp, i][:, None, :] for i in range(6)]
    mods_s = [mods[bp:, i][None, :, :] for i in range(6)]

    wqkv, wgl, wu = _prep_w_in(w_in[l])
    w1big, w2big, posrows = _prep_compress(w_cmp1[l], pos_cmp[l], w_cmp2[l])
    w_route, b_route = _prep_router(w_group[l], b_group[l], w_router[l], b_router[l])
    w_out_b = w_out[l].astype(BF16)
    conv_args = (w_dw[l], row2(b_dw[l]), row2(conv_ln_g[l]), row2(conv_ln_b[l]))

    def kv_rows_from_t(a_t):
        n, _, t = a_t.shape
        return a_t.reshape(n, 2, KV_HEADS, HEAD_DIM, t).transpose(0, 4, 1, 2, 3)

    def kv_rows_to_t(a):
        return a.transpose(0, 2, 3, 4, 1)

    qt, kvc, kvct, kvst, kvwt, ks, vts, kw, vtw, gt, u = _mixer(
        x_prompt, mods_p[0], mods_p[1], row2(norm1_g[l]), wqkv, wgl, wu, tm=512, prompt=True)
    kc, vct = _compress_prompt(kvc, w1big, w2big, posrows)
    o_attn = _attn_prompt(qt, kc, vct, ks, vts, kw, vtw, gt)
    o_conv = _conv_prompt(u, *conv_args, tm=512)
    x1, route, moe_rows, counts = _finish(x_prompt, o_attn, o_conv, mods_p[2], mods_p[3], mods_p[4],
                                          row2(g_attn_out[l]), row2(g_conv_out[l]), w_out_b, row2(norm2_g[l]),
                                          w_route, b_route, tm=512, sorted_moe=True)
    y_prompt = _moe_sorted(route, counts, moe_rows, x1, mods_p[5], w_gate[l], w_up[l], w_down[l], row2(final_g))

    new_cmp_prompt = kv_rows_from_t(kvct)[None]
    new_slc_prompt = kv_rows_from_t(kvst)[None]
    new_win_prompt = kv_rows_from_t(kvwt[:, :, seq - WINDOW:])[None]
    new_conv_prompt = u[:, seq - (CONV_WIDTH - 1):][None]

    xs = x_sample.reshape(1, nb, d)
    q_s, kvct_s, kvst_s, kvwt_s, gt_s, u_s = _mixer(
        xs, mods_s[0], mods_s[1], row2(norm1_g[l]), wqkv, wgl, wu, tm=nb, prompt=False)
    cmp_t = kv_rows_to_t(cache_cmp_kv[l])
    q4 = q_s.reshape(nb, KV_HEADS, GQA, HEAD_DIM)
    zq = jnp.zeros_like(q4)
    kvh_id = jnp.arange(KV_HEADS).reshape(1, KV_HEADS, 1, 1)
    qpad = jnp.concatenate([jnp.where(kvh_id == 0, q4, zq), jnp.where(kvh_id == 1, q4, zq)],
                           axis=-1).reshape(nb, N_HEADS, D_KV)
    o_c, idx = _compress_sample(cmp_t.reshape(cmp_t.shape[0], ROW_W, PAGE_SIZE), page_table, qpad, w1big, w2big,
                                posrows, q_pos=past_len, n_slc=n_past_slc + 1)
    idx = idx[:, :KV_HEADS, :SLC_TOPN]
    win_t = kv_rows_to_t(state_win_kv[l]).reshape(nb, ROW_W, win_buf)
    o_heads, new_win_t = _attn_sample(idx, page_table, kv_rows_to_t(cache_slc_kv[l]), q_s.reshape(nb, N_HEADS, HEAD_DIM),
                                      o_c, gt_s[0], kvst_s[0], kvwt_s[0], win_t, n_past_slc=n_past_slc)
    o_attn_s = o_heads.reshape(1, nb, D_ATTN)
    state_t = state_conv[l].transpose(1, 0, 2)
    u_rows = u_s.reshape(nb, D_CONV)
    o_conv_s = _conv_sample(state_t, u_rows, *conv_args, sb=8).reshape(1, nb, D_CONV)
    x1_s, h2_s, comb_s = _finish(xs, o_attn_s, o_conv_s, mods_s[2], mods_s[3], mods_s[4], row2(g_attn_out[l]),
                                 row2(g_conv_out[l]), w_out_b, row2(norm2_g[l]), w_route, b_route, tm=nb,
                                 sorted_moe=False)
    y_sample = _moe(h2_s, comb_s, x1_s, mods_s[5], w_gate[l], w_up[l], w_down[l], row2(final_g), tm=nb)

    row_shape = (1, nb, 1, 2, KV_HEADS, HEAD_DIM)
    new_cmp_sample = kvct_s[0].T.reshape(row_shape)
    new_slc_sample = kvst_s[0].T.reshape(row_shape)
    new_win_sample = kv_rows_from_t(new_win_t)[None]
    new_conv_sample = jnp.concatenate([state_t[1:], u_rows[None]], axis=0).transpose(1, 0, 2)[None]

    return (y_prompt, y_sample.reshape(nb, 1, d), new_cmp_prompt, new_slc_prompt, new_win_prompt, new_conv_prompt,
            new_cmp_sample, new_slc_sample, new_win_sample, new_conv_sample)
```

```python
import functools

import jax
import jax.numpy as jnp
from jax import lax
from jax.experimental import pallas as pl
from jax.experimental.pallas import tpu as pltpu

F32 = jnp.float32
BF16 = jnp.bfloat16

D_MODEL = 1024
N_HEADS = 8
HEAD_DIM = 64
KV_HEADS = 2
GQA = N_HEADS // KV_HEADS
D_ATTN = N_HEADS * HEAD_DIM
D_CONV = D_MODEL - D_ATTN
D_KV = KV_HEADS * HEAD_DIM
CMP_BLOCK = 32
CMP_STRIDE = 16
SLC_BLOCK = 64
SLC_TOPN = 16
N_LOCAL_BLOCKS = 2
WINDOW = 512
FORCED_SCORE = 1e4
CONV_WIDTH = 31
N_GROUPS = 4
EPG = 8
N_EXPERTS = N_GROUPS * EPG
D_EXPERT = 256
PAGE_SIZE = 128
EPS = 1e-6
NEG_INF = -1e30
SCALE = HEAD_DIM ** -0.5
LOG2E = 1.4426950408889634
ROW_W = 2 * D_KV
CHUNK_W = CMP_STRIDE * ROW_W
CHUNK_PITCH = 20
MOE_TILE = 256
TOKEN_TILE_ROWS = D_MODEL // 128
CONV_ROWS = 32
Q_TILE = 128
K_TILE = 512
VMEM_LIMIT = 48 * 1024 * 1024


def _cparams(*sem):
    return pltpu.CompilerParams(dimension_semantics=sem, vmem_limit_bytes=VMEM_LIMIT)


def _rms(x, g):
    return x * lax.rsqrt(jnp.mean(x * x, axis=-1, keepdims=True) + EPS) * g


def _silu(x):
    return x * jax.nn.sigmoid(x)


def _mm(a, b):
    return jnp.dot(a, b, preferred_element_type=F32)


def _mm_nt(a, b):
    return lax.dot_general(a, b, (((1,), (1,)), ((), ())), preferred_element_type=F32)


def _split3(x):
    hi = x.astype(BF16)
    r = x - hi.astype(F32)
    mid = r.astype(BF16)
    lo = (r - mid.astype(F32)).astype(BF16)
    return hi, mid, lo


def _softmax_masked(s, mask, axis):
    s = jnp.where(mask, s, NEG_INF)
    m = jnp.max(s, axis=axis, keepdims=True)
    e = jnp.exp(s - m)
    p = e / jnp.sum(e, axis=axis, keepdims=True)
    return jnp.where(mask, p, 0.0)


def _adaln_kernel(c_ref, w_ref, b_ref, o_ref):
    s = _silu(c_ref[...]).astype(BF16)
    o_ref[...] = _mm(s, w_ref[...].astype(BF16)) + b_ref[...]


def _adaln(c_all, w_ada, b_ada):
    n, d = c_all.shape
    nout = w_ada.shape[1]
    tn = 1024
    return pl.pallas_call(
        _adaln_kernel,
        grid=(nout // tn,),
        in_specs=[pl.BlockSpec((n, d), lambda j: (0, 0)),
                  pl.BlockSpec((d, tn), lambda j: (0, j)),
                  pl.BlockSpec((1, tn), lambda j: (0, j))],
        out_specs=pl.BlockSpec((n, tn), lambda j: (0, j)),
        out_shape=jax.ShapeDtypeStruct((n, nout), F32),
        compiler_params=_cparams("arbitrary"),
        name="adaln",
    )(c_all, w_ada, b_ada.reshape(1, nout))


def _mixer_kernel(x_ref, shift_ref, scale_ref, g_ref, wqkv_ref, wgl_ref, wu_ref, *outs, prompt):
    x = x_ref[0]
    h = _rms(x, g_ref[...]) * (1.0 + scale_ref[0]) + shift_ref[0]
    hb = h.astype(BF16)
    p = _mm(hb, wqkv_ref[...])
    gl = _mm(hb, wgl_ref[...])
    pu = _mm(hb, wu_ref[...])
    u = pu[:, :D_CONV] * jax.nn.sigmoid(pu[:, D_CONV:])
    q = p[:, :D_ATTN] * (SCALE * LOG2E if prompt else SCALE)
    o = D_ATTN
    kvc = p[:, o:o + ROW_W]
    kvs = p[:, o + ROW_W:o + 2 * ROW_W]
    kvw = p[:, o + 2 * ROW_W:o + 3 * ROW_W]
    kvs_t = kvs.T
    kvw_t = kvw.T
    if prompt:
        q_ref, kvc_ref, kvct_ref, kvst_ref, kvwt_ref, ks_ref, vts_ref, kw_ref, vtw_ref, gt_ref, u_ref = outs
        q_ref[0] = q.T.astype(BF16)
        kvc_ref[0] = kvc
        ks_ref[0] = kvs[:, :D_KV].astype(BF16)
        vts_ref[0] = kvs_t[D_KV:, :].astype(BF16)
        kw_ref[0] = kvw[:, :D_KV].astype(BF16)
        vtw_ref[0] = kvw_t[D_KV:, :].astype(BF16)
    else:
        q_ref, kvct_ref, kvst_ref, kvwt_ref, gt_ref, u_ref = outs
        q_ref[0] = q
    kvct_ref[0] = kvc.T
    kvst_ref[0] = kvs_t
    kvwt_ref[0] = kvw_t
    gt_ref[0] = jax.nn.sigmoid(gl).T[:32, :]
    u_ref[0] = u


def _mixer(x, shift, scale, g, wqkv, wgl, wu, *, tm, prompt):
    b, s, d = x.shape
    r = shift.shape[1]
    mod_block = (1, 1, d) if r == 1 else (1, tm, d)
    mod_map = (lambda i, j: (i, 0, 0)) if r == 1 else (lambda i, j: (i, j, 0))
    row = lambda w: pl.BlockSpec((1, tm, w), lambda i, j: (i, j, 0))
    col = lambda w: pl.BlockSpec((1, w, tm), lambda i, j: (i, 0, j))
    const = lambda a: pl.BlockSpec(a.shape, lambda i, j: (0,) * a.ndim)
    rows = lambda w, dt: jax.ShapeDtypeStruct((b, s, w), dt)
    cols = lambda w, dt: jax.ShapeDtypeStruct((b, w, s), dt)
    if prompt:
        out_specs = [col(D_ATTN), row(ROW_W), col(ROW_W), col(ROW_W), col(ROW_W), row(D_KV), col(D_KV), row(D_KV),
                     col(D_KV), col(32), row(D_CONV)]
        out_shape = [cols(D_ATTN, BF16), rows(ROW_W, F32), cols(ROW_W, F32), cols(ROW_W, F32), cols(ROW_W, F32),
                     rows(D_KV, BF16), cols(D_KV, BF16), rows(D_KV, BF16), cols(D_KV, BF16), cols(32, F32),
                     rows(D_CONV, F32)]
    else:
        out_specs = [row(D_ATTN), col(ROW_W), col(ROW_W), col(ROW_W), col(32), row(D_CONV)]
        out_shape = [rows(D_ATTN, F32), cols(ROW_W, F32), cols(ROW_W, F32), cols(ROW_W, F32), cols(32, F32),
                     rows(D_CONV, F32)]
    return pl.pallas_call(
        functools.partial(_mixer_kernel, prompt=prompt),
        grid=(b, s // tm),
        in_specs=[row(d), pl.BlockSpec(mod_block, mod_map), pl.BlockSpec(mod_block, mod_map),
                  const(g), const(wqkv), const(wgl), const(wu)],
        out_specs=out_specs,
        out_shape=out_shape,
        compiler_params=_cparams("arbitrary", "arbitrary"),
        name="mixer_prompt" if prompt else "mixer_sample",
    )(x, shift, scale, g, wqkv, wgl, wu)


def _chunk_part(load_offset, pos_rows, w1_c):
    xc = jnp.concatenate([load_offset(s) for s in range(CMP_STRIDE)], axis=1).astype(BF16)
    n = xc.shape[0]
    both = _mm(jnp.concatenate([xc, pos_rows], axis=0), w1_c)
    return both[:n], both[n:]


def _compress_finish(part, posb, w2_c):
    n = part.shape[0]
    nxt = pltpu.roll(part[:, D_KV:], n - 1, 0)
    pre = part[:, :D_KV] + nxt + posb[0:1, :D_KV] + posb[1:2, D_KV:]
    return _mm(_silu(pre).astype(BF16), w2_c)


def _compress_prompt_kernel(x_ref, w1_ref, w2_ref, pos_ref, kc_ref, vct_ref):
    for c in range(2):
        part, posb = _chunk_part(lambda s: x_ref[0, :, s * ROW_W + c * D_KV:s * ROW_W + (c + 1) * D_KV], pos_ref[c],
                                 w1_ref[c])
        out = _compress_finish(part, posb, w2_ref[c])
        if c == 0:
            kc_ref[0] = out.astype(BF16)
        else:
            vct_ref[0] = out.T.astype(BF16)


def _compress_prompt(kvc, w1big, w2big, posrows):
    b, s, _ = kvc.shape
    n = s // CMP_STRIDE
    x = kvc.reshape(b, n, CHUNK_W)
    const = lambda a: pl.BlockSpec(a.shape, lambda i: (0,) * a.ndim)
    return pl.pallas_call(
        _compress_prompt_kernel,
        grid=(b,),
        in_specs=[pl.BlockSpec((1, n, CHUNK_W), lambda i: (i, 0, 0)), const(w1big), const(w2big), const(posrows)],
        out_specs=[pl.BlockSpec((1, n, D_KV), lambda i: (i, 0, 0)), pl.BlockSpec((1, D_KV, n), lambda i: (i, 0, 0))],
        out_shape=[jax.ShapeDtypeStruct((b, n, D_KV), BF16), jax.ShapeDtypeStruct((b, D_KV, n), BF16)],
        compiler_params=_cparams("arbitrary"),
        name="compress_prompt",
    )(x, w1big, w2big, posrows)


def _compress_sample_kernel(pt_ref, cache_ref, q_ref, w1_ref, w2_ref, pos_ref, oc_ref, idx_ref, pages_ref, rows0_ref,
                            rows1_ref, sem, *, pages, q_pos, n_slc):
    i = pl.program_id(0)
    row_refs = (rows0_ref, rows1_ref)
    cpp = PAGE_SIZE // CMP_STRIDE
    n = pages * cpp

    def page_copy(seq, k, buf):
        return pltpu.make_async_copy(cache_ref.at[pt_ref[seq * pages + k]], pages_ref.at[buf, k], sem.at[buf])

    def fetch(seq, buf):
        for k in range(pages):
            page_copy(seq, k, buf).start()

    @pl.when(i == 0)
    def _():
        fetch(0, 0)

    @pl.when(i + 1 < pl.num_programs(0))
    def _():
        fetch(i + 1, (i + 1) % 2)

    buf = i % 2
    for k in range(pages):
        page_copy(i, k, buf).wait()

    kv = []
    for c in range(2):
        for k in range(pages):
            rows = pages_ref[buf, k, c * D_KV:(c + 1) * D_KV, :].astype(BF16).T.astype(F32)
            for j in range(cpp):
                r0 = (k * cpp + j) * CHUNK_PITCH
                row_refs[c][r0:r0 + CMP_STRIDE, :] = rows[j * CMP_STRIDE:(j + 1) * CMP_STRIDE, :]
    for c in range(2):
        part, posb = _chunk_part(lambda s: row_refs[c][pl.ds(s, n, stride=CHUNK_PITCH), :], pos_ref[c], w1_ref[c])
        kv.append(_compress_finish(part, posb, w2_ref[c]).astype(BF16))
    _sample_cmp_attend(q_ref[0].astype(BF16), kv[0], kv[1], oc_ref, idx_ref, q_pos=q_pos, n_slc=n_slc)


def _compress_sample(cache_t, page_table, qpad, w1big, w2big, posrows, *, q_pos, n_slc):
    nb, n_pages = page_table.shape
    n = n_pages * (PAGE_SIZE // CMP_STRIDE)
    const = lambda a: pl.BlockSpec(a.shape, lambda i, pt: (0,) * a.ndim)
    per_b = lambda w: pl.BlockSpec((1, N_HEADS, w), lambda i, pt: (i, 0, 0))
    return pl.pallas_call(
        functools.partial(_compress_sample_kernel, pages=n_pages, q_pos=q_pos, n_slc=n_slc),
        grid_spec=pltpu.PrefetchScalarGridSpec(
            num_scalar_prefetch=1,
            grid=(nb,),
            in_specs=[pl.BlockSpec(memory_space=pl.ANY), per_b(D_KV), const(w1big), const(w2big), const(posrows)],
            out_specs=[per_b(D_KV), per_b(128)],
            scratch_shapes=[pltpu.VMEM((2, n_pages, ROW_W, PAGE_SIZE), F32),
                            pltpu.VMEM((n * CHUNK_PITCH, D_KV), F32), pltpu.VMEM((n * CHUNK_PITCH, D_KV), F32),
                            pltpu.SemaphoreType.DMA((2,))],
        ),
        out_shape=[jax.ShapeDtypeStruct((nb, N_HEADS, D_KV), F32), jax.ShapeDtypeStruct((nb, N_HEADS, 128), jnp.int32)],
        compiler_params=pltpu.CompilerParams(dimension_semantics=("arbitrary",), vmem_limit_bytes=VMEM_LIMIT,
                                             disable_bounds_checks=True),
        name="compress_sample",
    )(page_table.reshape(-1), cache_t, qpad, w1big, w2big, posrows)


def _overlap(cmp_idx, slc_idx):
    lo = cmp_idx * CMP_STRIDE
    so = slc_idx * SLC_BLOCK
    return (lo <= so + SLC_BLOCK - 1) & (lo + CMP_BLOCK - 1 >= so)


def _forced_importance(imp, blk, cur, n_slc):
    valid = (blk <= cur) & (blk < n_slc)
    forced = (blk == 0) | ((cur - blk >= 0) & (cur - blk < N_LOCAL_BLOCKS))
    return jnp.where(valid & forced, FORCED_SCORE, jnp.where(valid, imp, -1.0))


def _attn_prompt_kernel(qt_ref, kc_ref, vct_ref, ks_ref, vts_ref, kw_ref, vtw_ref, gt_ref, o_ref, sel_ref):
    i = pl.program_id(1)
    seq = ks_ref.shape[1]
    nbp = kc_ref.shape[1]
    n_slc = seq // SLC_BLOCK
    lanes = GQA * Q_TILE
    lane = lax.broadcasted_iota(jnp.int32, (1, lanes), 1)
    qpos = i * Q_TILE + (lane & (Q_TILE - 1))
    qpos_q = qpos[:, :Q_TILE]
    qt = qt_ref[0]
    gt = gt_ref[0]
    win_keys = min(WINDOW + Q_TILE, seq)
    blocks_per_tile = K_TILE // SLC_BLOCK

    ovl = _overlap(lax.broadcasted_iota(jnp.int32, (n_slc, nbp), 1), lax.broadcasted_iota(jnp.int32, (n_slc, nbp), 0))
    ovl = jnp.where(ovl, 1.0, 0.0).astype(BF16)
    blk = lax.broadcasted_iota(jnp.int32, (n_slc, Q_TILE), 0)

    cpos = lax.broadcasted_iota(jnp.int32, (nbp, lanes), 0) * CMP_STRIDE + (CMP_BLOCK - 1)
    cmp_bias = jnp.where(cpos <= qpos, 0.0, NEG_INF)
    any_cmp = jnp.where(qpos >= CMP_BLOCK - 1, 1.0, 0.0)
    w0 = pl.multiple_of(jnp.clip(i * Q_TILE - WINDOW, 0, seq - win_keys), Q_TILE)
    wpos = w0 + lax.broadcasted_iota(jnp.int32, (win_keys, lanes), 0)
    win_bias = jnp.where(wpos <= qpos, jnp.where(wpos >= qpos - WINDOW, 0.0, NEG_INF), NEG_INF)
    n_steps = ((i + 1) * Q_TILE + K_TILE - 1) // K_TILE
    last0 = pl.multiple_of((n_steps - 1) * K_TILE, K_TILE)
    causal_bias = jnp.where(last0 + lax.broadcasted_iota(jnp.int32, (K_TILE, lanes), 0) <= qpos, 0.0, NEG_INF)

    heads = range(KV_HEADS)
    hd = [slice(kvh * HEAD_DIM, (kvh + 1) * HEAD_DIM) for kvh in heads]
    qpad, o_c = [], []
    for kvh in heads:
        qk = jnp.concatenate([qt[(kvh * GQA + g) * HEAD_DIM:(kvh * GQA + g + 1) * HEAD_DIM, :] for g in range(GQA)],
                             axis=1)
        zero = jnp.zeros_like(qk)
        qpad.append(jnp.concatenate([qk, zero] if kvh == 0 else [zero, qk], axis=0))

        s = _mm(kc_ref[0], qpad[kvh]) + cmp_bias
        e = jnp.exp2(s - jnp.max(s, axis=0, keepdims=True))
        p = e * (any_cmp / jnp.sum(e, axis=0, keepdims=True))
        o_c.append(_mm(vct_ref[0, hd[kvh], :], p.astype(BF16)))

        psum = p[:, 0:Q_TILE]
        for g in range(1, GQA):
            psum = psum + p[:, g * Q_TILE:(g + 1) * Q_TILE]
        imp = sum(_mm(ovl, t) for t in _split3(psum))
        impf = _forced_importance(imp, blk, qpos_q // SLC_BLOCK, n_slc)
        groups = [impf[r:r + 8, :] for r in range(0, n_slc, 8)]
        ranks = [jnp.zeros_like(grp) for grp in groups]
        for k in range(n_slc):
            rk = impf[k:k + 1, :]
            for gi, grp in enumerate(groups):
                if gi * 8 > k:
                    ranks[gi] = ranks[gi] + jnp.where(rk >= grp, 1.0, 0.0)
                elif gi * 8 + 7 < k:
                    ranks[gi] = ranks[gi] + jnp.where(rk > grp, 1.0, 0.0)
                else:
                    later = blk[0:8, :] + gi * 8 > k
                    ranks[gi] = ranks[gi] + jnp.where(later, jnp.where(rk >= grp, 1.0, 0.0),
                                                      jnp.where(rk > grp, 1.0, 0.0))
        rank = jnp.concatenate(ranks, axis=0)
        sel = jnp.where(rank < SLC_TOPN, jnp.where(impf >= 0.0, 0.0, NEG_INF), NEG_INF)
        sel_ref[kvh] = jnp.concatenate([sel] * GQA, axis=1)

    def slc_step(t, carry, extra_bias=None):
        k0 = pl.multiple_of(t * K_TILE, K_TILE)
        keys = ks_ref[0, pl.ds(k0, K_TILE), :]
        out = []
        for kvh in heads:
            m, l, acc = carry[kvh]
            sk = _mm(keys, qpad[kvh])
            sk = jnp.concatenate(
                [sk[j * SLC_BLOCK:(j + 1) * SLC_BLOCK, :] + sel_ref[kvh, pl.ds(t * blocks_per_tile + j, 1), :]
                 for j in range(blocks_per_tile)], axis=0)
            if extra_bias is not None:
                sk = sk + extra_bias
            m_new = jnp.maximum(m, jnp.max(sk, axis=0, keepdims=True))
            alpha = jnp.exp2(m - m_new)
            e = jnp.exp2(sk - m_new)
            l_new = alpha * l + jnp.sum(e, axis=0, keepdims=True)
            pv = _mm(vts_ref[0, hd[kvh], pl.ds(k0, K_TILE)], e.astype(BF16))
            out.append((m_new, l_new, alpha * acc + pv))
        return tuple(out)

    init = (jnp.full((1, lanes), NEG_INF, F32), jnp.zeros((1, lanes), F32), jnp.zeros((HEAD_DIM, lanes), F32))
    carry = lax.fori_loop(0, n_steps - 1, slc_step, (init,) * KV_HEADS)
    carry = slc_step(n_steps - 1, carry, causal_bias)

    for kvh in heads:
        _, l_s, acc_s = carry[kvh]
        o_s = acc_s * (1.0 / l_s)

        sw = _mm(kw_ref[0, pl.ds(w0, win_keys), :], qpad[kvh]) + win_bias
        ew = jnp.exp2(sw - jnp.max(sw, axis=0, keepdims=True))
        o_w = _mm(vtw_ref[0, hd[kvh], pl.ds(w0, win_keys)], ew.astype(BF16)) * (
            1.0 / jnp.sum(ew, axis=0, keepdims=True))

        def gate(r):
            return jnp.concatenate([gt[r * N_HEADS + kvh * GQA + g:r * N_HEADS + kvh * GQA + g + 1, :]
                                    for g in range(GQA)], axis=1)
        o_t = gate(0) * o_c[kvh] + gate(1) * o_s + gate(2) * o_w
        for pair in range(GQA // 2):
            two = jnp.concatenate([o_t[:, (2 * pair) * Q_TILE:(2 * pair + 1) * Q_TILE],
                                   o_t[:, (2 * pair + 1) * Q_TILE:(2 * pair + 2) * Q_TILE]], axis=0)
            c0 = kvh * GQA * HEAD_DIM + pair * 2 * HEAD_DIM
            o_ref[0, :, c0:c0 + 2 * HEAD_DIM] = two.T


def _attn_prompt(qt, kc, vct, ks, vts, kw, vtw, gt):
    b, _, s = qt.shape
    nbp = kc.shape[1]
    per_b = lambda shape: pl.BlockSpec((1,) + shape, lambda i, j: (i, 0, 0))
    return pl.pallas_call(
        _attn_prompt_kernel,
        grid=(b, s // Q_TILE),
        in_specs=[pl.BlockSpec((1, D_ATTN, Q_TILE), lambda i, j: (i, 0, j)),
                  per_b((nbp, D_KV)), per_b((D_KV, nbp)),
                  per_b((s, D_KV)), per_b((D_KV, s)), per_b((s, D_KV)), per_b((D_KV, s)),
                  pl.BlockSpec((1, 32, Q_TILE), lambda i, j: (i, 0, j))],
        out_specs=pl.BlockSpec((1, Q_TILE, D_ATTN), lambda i, j: (i, j, 0)),
        out_shape=jax.ShapeDtypeStruct((b, s, D_ATTN), F32),
        scratch_shapes=[pltpu.VMEM((KV_HEADS, s // SLC_BLOCK, GQA * Q_TILE), F32)],
        compiler_params=_cparams("arbitrary", "arbitrary"),
        name="attn_prompt",
    )(qt, kc, vct, ks, vts, kw, vtw, gt)


def _sample_cmp_attend(q, kc, vc, oc_ref, idx_ref, *, q_pos, n_slc):
    nb = kc.shape[0]
    s = _mm_nt(q, kc)
    cpos = lax.broadcasted_iota(jnp.int32, s.shape, 1) * CMP_STRIDE + (CMP_BLOCK - 1)
    p = _softmax_masked(s, cpos <= q_pos, 1)
    oc_ref[0] = _mm(p.astype(BF16), vc)

    nsp = idx_ref.shape[2] * 2
    group_sums = [jnp.sum(p[k * GQA:(k + 1) * GQA, :], axis=0, keepdims=True) for k in range(KV_HEADS)]
    psum = jnp.concatenate(group_sums + [jnp.zeros((N_HEADS - KV_HEADS, nb), F32)], axis=0)
    ovl = _overlap(lax.broadcasted_iota(jnp.int32, (nb, nsp), 0), lax.broadcasted_iota(jnp.int32, (nb, nsp), 1))
    ovl = jnp.where(ovl, 1.0, 0.0).astype(BF16)
    imp = sum(_mm(t, ovl) for t in _split3(psum))
    blk = lax.broadcasted_iota(jnp.int32, imp.shape, 1)
    impf = _forced_importance(imp, blk, q_pos // SLC_BLOCK, n_slc)
    eye = jnp.where(lax.broadcasted_iota(jnp.int32, (nsp, nsp), 0) == lax.broadcasted_iota(jnp.int32, (nsp, nsp), 1),
                    1.0, 0.0).astype(BF16)
    imp_t = sum(_mm_nt(eye, t) for t in _split3(impf))
    k_idx = lax.broadcasted_iota(jnp.int32, (nsp, nsp), 0)
    j_idx = lax.broadcasted_iota(jnp.int32, (nsp, nsp), 1)
    ranks = []
    for r in range(KV_HEADS):
        col, row = imp_t[:, r:r + 1], impf[r:r + 1, :]
        beats = jnp.where(col > row, 1.0, jnp.where(col == row, jnp.where(k_idx < j_idx, 1.0, 0.0), 0.0))
        ranks.append(jnp.sum(beats, axis=0, keepdims=True))
    rank = jnp.concatenate(ranks + [jnp.full((N_HEADS - KV_HEADS, nsp), float(nsp), F32)], axis=0)
    blk_f = blk.astype(F32)
    slot = lax.broadcasted_iota(jnp.int32, (N_HEADS, idx_ref.shape[2]), 1)
    idx = jnp.zeros((N_HEADS, idx_ref.shape[2]), F32)
    for t in range(SLC_TOPN):
        chosen = jnp.sum(jnp.where(rank == float(t), blk_f, 0.0), axis=1, keepdims=True)
        idx = idx + jnp.where(slot == t, chosen, 0.0)
    idx_ref[0] = idx.astype(jnp.int32)


def _attn_sample_kernel(idx_ref, pt_ref, cache_ref, q_ref, oc_ref, gt_ref, kvst_ref, kvwt_ref, win_ref, o_ref, nwin_ref,
                        blk_ref, sem, *, n_past_slc, n_pages):
    n_sel = KV_HEADS * SLC_TOPN
    b = pl.program_id(0)
    nb = kvst_ref.shape[1]
    sub = PAGE_SIZE // SLC_BLOCK

    def block_copy(seq, s, buf):
        j = jnp.minimum(idx_ref[seq * n_sel + s], n_past_slc - 1)
        page = pt_ref[seq * n_pages + j // sub]
        return pltpu.make_async_copy(cache_ref.at[page, :, s // SLC_TOPN], blk_ref.at[buf, s], sem.at[buf])

    def fetch(seq, buf):
        for s in range(n_sel):
            block_copy(seq, s, buf).start()

    @pl.when(b == 0)
    def _():
        fetch(0, 0)

    @pl.when(b + 1 < pl.num_programs(0))
    def _():
        fetch(b + 1, (b + 1) % 2)

    buf = b % 2
    for s in range(n_sel):
        block_copy(b, s, buf).wait()

    q = q_ref[0].astype(BF16)
    head_kvh = lax.broadcasted_iota(jnp.int32, (N_HEADS, 1), 0) // GQA
    mine = lax.broadcasted_iota(jnp.int32, (1, nb), 1) == b
    lane_half = lax.broadcasted_iota(jnp.int32, (1, PAGE_SIZE), 1) // SLC_BLOCK

    def attend(kt, vt, valid):
        s = jnp.where(valid, _mm(q, kt), NEG_INF)
        e = jnp.where(valid, jnp.exp(s - jnp.max(s, axis=1, keepdims=True)), 0.0)
        return _mm_nt(e.astype(BF16), vt) / jnp.sum(e, axis=1, keepdims=True)

    kvst = kvst_ref[...]
    kvwt = kvwt_ref[...]
    o_s = jnp.zeros((N_HEADS, HEAD_DIM), F32)
    o_w = jnp.zeros((N_HEADS, HEAD_DIM), F32)
    win = win_ref[0]
    for kvh in range(KV_HEADS):
        k_rows = slice(kvh * HEAD_DIM, (kvh + 1) * HEAD_DIM)
        v_rows = slice(D_KV + kvh * HEAD_DIM, D_KV + (kvh + 1) * HEAD_DIM)
        valid = []
        n_new = jnp.int32(0)
        for t in range(SLC_TOPN):
            j = idx_ref[(b * KV_HEADS + kvh) * SLC_TOPN + t]
            past = j < n_past_slc
            half = jnp.minimum(j, n_past_slc - 1) % (PAGE_SIZE // SLC_BLOCK)
            valid.append((lane_half == half) & past)
            n_new = n_new + jnp.where(past, 0, 1)
        valid.append(mine & (n_new > 0))
        kt = jnp.concatenate([blk_ref[buf, kvh * SLC_TOPN + t, 0] for t in range(SLC_TOPN)] + [kvst[k_rows, :]],
                             axis=1)
        vt = jnp.concatenate([blk_ref[buf, kvh * SLC_TOPN + t, 1] for t in range(SLC_TOPN)] + [kvst[v_rows, :]],
                             axis=1)
        o_k = attend(kt.astype(BF16), vt.astype(BF16), jnp.concatenate(valid, axis=1))
        o_s = jnp.where(head_kvh == kvh, o_k, o_s)
        kt = jnp.concatenate([win[k_rows, :], kvwt[k_rows, :]], axis=1)
        vt = jnp.concatenate([win[v_rows, :], kvwt[v_rows, :]], axis=1)
        valid_w = jnp.concatenate([jnp.full((1, win.shape[1]), True), mine], axis=1)
        o_k = attend(kt.astype(BF16), vt.astype(BF16), valid_w)
        o_w = jnp.where(head_kvh == kvh, o_k, o_w)

    oc = oc_ref[0]
    o_c = jnp.where(head_kvh == 0, oc[:, :HEAD_DIM], oc[:, HEAD_DIM:])
    gates = jnp.sum(jnp.where(mine, gt_ref[...], 0.0), axis=1, keepdims=True)
    o_ref[0] = (gates[0:N_HEADS] * o_c + gates[N_HEADS:2 * N_HEADS] * o_s + gates[2 * N_HEADS:3 * N_HEADS] * o_w)

    new_col = jnp.sum(jnp.where(mine, kvwt, 0.0), axis=1, keepdims=True)
    last = lax.broadcasted_iota(jnp.int32, win.shape, 1) == win.shape[1] - 1
    nwin_ref[0] = jnp.where(last, new_col, pltpu.roll(win, win.shape[1] - 1, 1))


def _attn_sample(idx, page_table, cache_t, q, o_c, gt, kvst, kvwt, win_t, *, n_past_slc):
    nb = q.shape[0]
    win_buf = win_t.shape[2]
    n_pages = page_table.shape[1]
    per_b = lambda shape: pl.BlockSpec((1,) + shape, lambda i, a, c: (i,) + (0,) * len(shape))
    const = lambda a: pl.BlockSpec(a.shape, lambda i, x, c: (0,) * a.ndim)
    n_sel = KV_HEADS * SLC_TOPN
    return pl.pallas_call(
        functools.partial(_attn_sample_kernel, n_past_slc=n_past_slc, n_pages=n_pages),
        grid_spec=pltpu.PrefetchScalarGridSpec(
            num_scalar_prefetch=2,
            grid=(nb,),
            in_specs=[pl.BlockSpec(memory_space=pl.ANY), per_b((N_HEADS, HEAD_DIM)), per_b((N_HEADS, D_KV)),
                      const(gt), const(kvst), const(kvwt), per_b((ROW_W, win_buf))],
            out_specs=[per_b((N_HEADS, HEAD_DIM)), per_b((ROW_W, win_buf))],
            scratch_shapes=[pltpu.VMEM((2, n_sel, 2, HEAD_DIM, PAGE_SIZE), F32), pltpu.SemaphoreType.DMA((2,))],
        ),
        out_shape=[jax.ShapeDtypeStruct((nb, N_HEADS, HEAD_DIM), F32), jax.ShapeDtypeStruct(win_t.shape, F32)],
        compiler_params=pltpu.CompilerParams(dimension_semantics=("arbitrary",), vmem_limit_bytes=VMEM_LIMIT,
                                             disable_bounds_checks=True),
        name="attn_sample",
    )(idx.reshape(-1), page_table.reshape(-1), cache_t, q, o_c, gt, kvst, kvwt, win_t)


def _conv_tail(y, b_ref, lg_ref, lb_ref):
    y = y + b_ref[...]
    yc = y - jnp.mean(y, axis=-1, keepdims=True)
    yn = yc * lax.rsqrt(jnp.mean(yc * yc, axis=-1, keepdims=True) + EPS)
    return _silu(yn * lg_ref[...] + lb_ref[...])


def _conv_prompt_kernel(u_ref, halo_ref, w_ref, b_ref, lg_ref, lb_ref, o_ref, buf_ref, shift_ref, *, halo):
    tm = u_ref.shape[1]
    j = pl.program_id(1)
    buf_ref[0:halo, :] = jnp.where(j > 0, halo_ref[0], 0.0)
    buf_ref[halo:halo + tm, :] = u_ref[0]
    buf_ref[halo + tm:, :] = jnp.zeros((8, buf_ref.shape[1]), F32)
    lead = halo - (CONV_WIDTH - 1)
    for ph in range(8):
        shift_ref[ph] = buf_ref[ph:ph + tm + halo, :]

    def chunk(c, carry):
        r0 = pl.multiple_of(c * CONV_ROWS, CONV_ROWS)
        y = None
        for w in range(CONV_WIDTH):
            o = lead + w
            tap = shift_ref[o % 8, pl.ds(r0 + o // 8 * 8, CONV_ROWS), :] * w_ref[w:w + 1, :]
            y = tap if y is None else y + tap
        o_ref[0, pl.ds(r0, CONV_ROWS), :] = _conv_tail(y, b_ref, lg_ref, lb_ref)
        return carry

    lax.fori_loop(0, tm // CONV_ROWS, chunk, 0, unroll=4)


def _conv_prompt(u, w_dw, b_dw, ln_g, ln_b, *, tm):
    b, s, dc = u.shape
    halo = 32
    per = tm // halo
    const = lambda a: pl.BlockSpec(a.shape, lambda i, j: (0,) * a.ndim)
    return pl.pallas_call(
        functools.partial(_conv_prompt_kernel, halo=halo),
        grid=(b, s // tm),
        in_specs=[pl.BlockSpec((1, tm, dc), lambda i, j: (i, j, 0)),
                  pl.BlockSpec((1, halo, dc), lambda i, j: (i, jnp.maximum(j * per - 1, 0), 0)),
                  const(w_dw), const(b_dw), const(ln_g), const(ln_b)],
        out_specs=pl.BlockSpec((1, tm, dc), lambda i, j: (i, j, 0)),
        out_shape=jax.ShapeDtypeStruct((b, s, dc), F32),
        scratch_shapes=[pltpu.VMEM((tm + halo + 8, dc), F32), pltpu.VMEM((8, tm + halo, dc), F32)],
        compiler_params=_cparams("arbitrary", "arbitrary"),
        name="conv_prompt",
    )(u, u, w_dw, b_dw, ln_g, ln_b)


def _conv_sample_kernel(st_ref, u_ref, w_ref, b_ref, lg_ref, lb_ref, o_ref):
    hist = CONV_WIDTH - 1
    y = u_ref[...] * w_ref[hist:hist + 1, :]
    for w in range(hist):
        y = y + st_ref[w] * w_ref[w:w + 1, :]
    o_ref[...] = _conv_tail(y, b_ref, lg_ref, lb_ref)


def _conv_sample(state_t, u, w_dw, b_dw, ln_g, ln_b, *, sb):
    nb, dc = u.shape
    hist = state_t.shape[0]
    const = lambda a: pl.BlockSpec(a.shape, lambda i: (0,) * a.ndim)
    return pl.pallas_call(
        _conv_sample_kernel,
        grid=(nb // sb,),
        in_specs=[pl.BlockSpec((hist, sb, dc), lambda i: (0, i, 0)), pl.BlockSpec((sb, dc), lambda i: (i, 0)),
                  const(w_dw), const(b_dw), const(ln_g), const(ln_b)],
        out_specs=pl.BlockSpec((sb, dc), lambda i: (i, 0)),
        out_shape=jax.ShapeDtypeStruct((nb, dc), F32),
        compiler_params=_cparams("arbitrary"),
        name="conv_sample",
    )(state_t, u, w_dw, b_dw, ln_g, ln_b)


def _finish_kernel(x_ref, oa_ref, ocv_ref, gate1_ref, shift2_ref, scale2_ref, ga_ref, gc_ref, wout_ref, g2_ref,
                   wr_ref, br_ref, x1_ref, *outs, sorted_moe):
    mixed = jnp.concatenate([_rms(oa_ref[0], ga_ref[...]), _rms(ocv_ref[0], gc_ref[...])], axis=1)
    mix = _mm(mixed.astype(BF16), wout_ref[...])
    x1 = x_ref[0] + gate1_ref[0] * mix
    x1_ref[0] = x1
    h2 = _rms(x1, g2_ref[...]) * (1.0 + scale2_ref[0]) + shift2_ref[0]

    h_hi, h_mid, _ = _split3(h2)
    w_hi, w_mid, _ = _split3(wr_ref[...])
    lg = (_mm(h_hi, w_hi) + _mm(h_hi, w_mid)
          + _mm(h_mid, w_hi)) + br_ref[...]
    lane = lax.broadcasted_iota(jnp.int32, lg.shape, 1)
    lane_f = lane.astype(F32)
    is_group = (lane >= N_EXPERTS) & (lane < N_EXPERTS + N_GROUPS)
    gl = jnp.where(is_group, lg, NEG_INF)
    g_max = jnp.max(gl, axis=1, keepdims=True)
    p_top = 1.0 / jnp.sum(jnp.where(is_group, jnp.exp(gl - g_max), 0.0), axis=1, keepdims=True)
    g_lane = jnp.min(jnp.where(gl == g_max, lane_f, 1e9), axis=1, keepdims=True)
    in_group = (lane < N_EXPERTS) & ((lane // EPG).astype(F32) == g_lane - N_EXPERTS)
    el = jnp.where(in_group, lg, NEG_INF)
    l1 = jnp.max(el, axis=1, keepdims=True)
    i1 = jnp.min(jnp.where(el == l1, lane_f, 1e9), axis=1, keepdims=True)
    el2 = jnp.where(lane_f == i1, NEG_INF, el)
    l2 = jnp.max(el2, axis=1, keepdims=True)
    i2 = jnp.min(jnp.where(el2 == l2, lane_f, 1e9), axis=1, keepdims=True)
    r = jnp.exp(l2 - l1)
    w1 = p_top / (1.0 + r)
    w2 = p_top * r / (1.0 + r)
    if not sorted_moe:
        h2_ref, comb_ref = outs
        h2_ref[0] = h2.astype(BF16)
        comb_ref[0] = jnp.where(lane_f == i1, w1, jnp.where(lane_f == i2, w2, 0.0))
        return
    route_ref, rows_ref, count_ref = outs
    for k in range(TOKEN_TILE_ROWS):
        rows_ref[0, pl.ds(k, h2.shape[0], stride=TOKEN_TILE_ROWS), :] = h2[:, k * 128:(k + 1) * 128]
    route_ref[0] = jnp.where(lane == 0, i1, jnp.where(lane == 1, i2, jnp.where(lane == 2, w1, jnp.where(
        lane == 3, w2, 0.0))))

    @pl.when((pl.program_id(0) == 0) & (pl.program_id(1) == 0))
    def _():
        count_ref[...] = jnp.zeros_like(count_ref)

    picks = jnp.where(lane_f == i1, 1.0, 0.0) + jnp.where(lane_f == i2, 1.0, 0.0)
    count_ref[...] += jnp.sum(picks, axis=0, keepdims=True)


def _finish(x, o_attn, o_conv, gate1, shift2, scale2, ga, gc, w_out, g2, w_route, b_route, *, tm, sorted_moe):
    b, s, d = x.shape
    r = gate1.shape[1]
    mod_block = (1, 1, d) if r == 1 else (1, tm, d)
    mod_map = (lambda i, j: (i, 0, 0)) if r == 1 else (lambda i, j: (i, j, 0))
    mod = pl.BlockSpec(mod_block, mod_map)
    row = lambda w: pl.BlockSpec((1, tm, w), lambda i, j: (i, j, 0))
    const = lambda a: pl.BlockSpec(a.shape, lambda i, j: (0,) * a.ndim)
    shape = lambda w, dt: jax.ShapeDtypeStruct((b, s, w), dt)
    if sorted_moe:
        tiles = pl.BlockSpec((1, tm * TOKEN_TILE_ROWS, 128), lambda i, j: (i, j, 0))
        out_specs = [row(d), row(128), tiles, pl.BlockSpec((8, 128), lambda i, j: (0, 0))]
        out_shape = [shape(d, F32), shape(128, F32), jax.ShapeDtypeStruct((b, s * TOKEN_TILE_ROWS, 128), F32),
                     jax.ShapeDtypeStruct((8, 128), F32)]
    else:
        out_specs = [row(d), row(d), row(128)]
        out_shape = [shape(d, F32), shape(d, BF16), shape(128, F32)]
    return pl.pallas_call(
        functools.partial(_finish_kernel, sorted_moe=sorted_moe),
        grid=(b, s // tm),
        in_specs=[row(d), row(D_ATTN), row(D_CONV), mod, mod, mod, const(ga), const(gc), const(w_out), const(g2),
                  const(w_route), const(b_route)],
        out_specs=out_specs,
        out_shape=out_shape,
        compiler_params=_cparams("arbitrary", "arbitrary"),
        name="finish_sorted" if sorted_moe else "finish",
    )(x, o_attn, o_conv, gate1, shift2, scale2, ga, gc, w_out, g2, w_route, b_route)


def _moe_kernel(h_ref, comb_ref, x1_ref, gate2_ref, wg_ref, wu_ref, wd_ref, gf_ref, y_ref, acc_ref):
    e = pl.program_id(2)

    @pl.when(e == 0)
    def _():
        acc_ref[...] = jnp.zeros_like(acc_ref)

    h = h_ref[0]
    a = _mm(h, wg_ref[0].astype(BF16))
    u = _mm(h, wu_ref[0].astype(BF16))
    comb = comb_ref[0]
    lane = lax.broadcasted_iota(jnp.int32, comb.shape, 1)
    cw = jnp.sum(jnp.where(lane == e, comb, 0.0), axis=1, keepdims=True)
    hid = (_silu(a) * u * cw).astype(BF16)
    acc_ref[...] += _mm(hid, wd_ref[0].astype(BF16))

    @pl.when(e == pl.num_programs(2) - 1)
    def _():
        y = x1_ref[0] + gate2_ref[0] * acc_ref[...]
        y_ref[0] = _rms(y, gf_ref[...])


def _moe(h2, comb, x1, gate2, w_gate, w_up, w_down, final_g, *, tm):
    b, s, d = x1.shape
    r = gate2.shape[1]
    mod_block = (1, 1, d) if r == 1 else (1, tm, d)
    mod_map = (lambda i, j, e: (i, 0, 0)) if r == 1 else (lambda i, j, e: (i, j, 0))
    row = lambda w: pl.BlockSpec((1, tm, w), lambda i, j, e: (i, j, 0))
    ne, _, de = w_gate.shape
    return pl.pallas_call(
        _moe_kernel,
        grid=(b, s // tm, ne),
        in_specs=[row(d), row(128), row(d), pl.BlockSpec(mod_block, mod_map),
                  pl.BlockSpec((1, d, de), lambda i, j, e: (e, 0, 0)),
                  pl.BlockSpec((1, d, de), lambda i, j, e: (e, 0, 0)),
                  pl.BlockSpec((1, de, d), lambda i, j, e: (e, 0, 0)),
                  pl.BlockSpec((1, d), lambda i, j, e: (0, 0))],
        out_specs=row(d),
        out_shape=jax.ShapeDtypeStruct((b, s, d), F32),
        scratch_shapes=[pltpu.VMEM((tm, d), F32)],
        compiler_params=_cparams("arbitrary", "arbitrary", "arbitrary"),
        name="moe",
    )(h2, comb, x1, gate2, w_gate, w_up, w_down, final_g)


def _moe_slots_kernel(route_ref, base_ref, slot_ref, run_ref):
    @pl.when(pl.program_id(0) == 0)
    def _():
        run_ref[...] = jnp.zeros_like(run_ref)

    route = route_ref[...]
    tm = route.shape[0]
    lane = lax.broadcasted_iota(jnp.int32, route.shape, 1).astype(F32)
    first = jnp.where(lane == route[:, 0:1], 1.0, 0.0)
    second = jnp.where(lane == route[:, 1:2], 1.0, 0.0)
    picks = first + second
    earlier = lax.broadcasted_iota(jnp.int32, (tm, tm), 1) < lax.broadcasted_iota(jnp.int32, (tm, tm), 0)
    seen = _mm(jnp.where(earlier, 1.0, 0.0).astype(BF16), picks.astype(BF16)) + (run_ref[...] + base_ref[...])
    slot0 = jnp.sum(first * seen, axis=1, keepdims=True)
    slot1 = jnp.sum(second * seen, axis=1, keepdims=True)
    slot_ref[...] = jnp.where(lane == 0.0, slot0, jnp.where(lane == 1.0, slot1, 0.0)).astype(jnp.int32)
    run_ref[...] += jnp.sum(picks, axis=0, keepdims=True)


def _moe_slots(route, base, *, tm):
    n = route.shape[0]
    return pl.pallas_call(
        _moe_slots_kernel,
        grid=(n // tm,),
        in_specs=[pl.BlockSpec((tm, 128), lambda j: (j, 0)), pl.BlockSpec((1, 128), lambda j: (0, 0))],
        out_specs=pl.BlockSpec((tm, 128), lambda j: (j, 0)),
        out_shape=jax.ShapeDtypeStruct((n, 128), jnp.int32),
        scratch_shapes=[pltpu.VMEM((1, 128), F32)],
        compiler_params=_cparams("arbitrary"),
        name="moe_slots",
    )(route, base)


def _row_copy(src_ref, src_row, dst_ref, dst_row, sem):
    src = pl.ds(pl.multiple_of(src_row * TOKEN_TILE_ROWS, TOKEN_TILE_ROWS), TOKEN_TILE_ROWS)
    dst = pl.ds(pl.multiple_of(dst_row * TOKEN_TILE_ROWS, TOKEN_TILE_ROWS), TOKEN_TILE_ROWS)
    return pltpu.make_async_copy(src_ref.at[src, :], dst_ref.at[dst, :], sem)


def _token_rows(ref, n):
    return jnp.concatenate([ref[pl.ds(k, n, stride=TOKEN_TILE_ROWS), :] for k in range(TOKEN_TILE_ROWS)], axis=1)


def _moe_scatter_kernel(s0_ref, s1_ref, h_ref, sorted_ref, sem):
    j = pl.program_id(0)
    tm = h_ref.shape[0] // TOKEN_TILE_ROWS

    def start(r, carry):
        _row_copy(h_ref, r, sorted_ref, s0_ref[j * tm + r], sem).start()
        _row_copy(h_ref, r, sorted_ref, s1_ref[j * tm + r], sem).start()
        return carry

    def wait(r, carry):
        _row_copy(h_ref, 0, sorted_ref, 0, sem).wait()
        _row_copy(h_ref, 0, sorted_ref, 0, sem).wait()
        return carry

    lax.fori_loop(0, tm, start, 0, unroll=8)
    lax.fori_loop(0, tm, wait, 0, unroll=8)


def _moe_scatter(slot0, slot1, rows, n_slots, *, tm):
    w = rows.shape[1]
    n = rows.shape[0] // TOKEN_TILE_ROWS
    tm, n_slots = tm * TOKEN_TILE_ROWS, n_slots * TOKEN_TILE_ROWS
    n = n * TOKEN_TILE_ROWS
    return pl.pallas_call(
        _moe_scatter_kernel,
        grid_spec=pltpu.PrefetchScalarGridSpec(
            num_scalar_prefetch=2,
            grid=(n // tm,),
            in_specs=[pl.BlockSpec((tm, w), lambda j, a, b: (j, 0))],
            out_specs=pl.BlockSpec(memory_space=pl.ANY),
            scratch_shapes=[pltpu.SemaphoreType.DMA(())],
        ),
        out_shape=jax.ShapeDtypeStruct((n_slots, w), rows.dtype),
        compiler_params=pltpu.CompilerParams(dimension_semantics=("arbitrary",), vmem_limit_bytes=VMEM_LIMIT,
                                             disable_bounds_checks=True),
        name="moe_scatter",
    )(slot0, slot1, rows)


def _moe_experts_kernel(vt_ref, ve_ref, lo_ref, hi_ref, first_ref, nv_ref, x_ref, wg_ref, wu_ref, wd_ref, o_ref):
    v = pl.program_id(0)

    @pl.when(v < nv_ref[0])
    def _():
        xb = _token_rows(x_ref, MOE_TILE).astype(BF16)
        a = _mm(xb, wg_ref[0].astype(BF16))
        u = _mm(xb, wu_ref[0].astype(BF16))
        y = _mm((_silu(a) * u).astype(BF16), wd_ref[0].astype(BF16))
        row = lax.broadcasted_iota(jnp.int32, (MOE_TILE, 1), 0)
        mine = (row >= lo_ref[v]) & (row < hi_ref[v])

        @pl.when(first_ref[v] == 1)
        def _():
            for k in range(TOKEN_TILE_ROWS):
                o_ref[pl.ds(k, MOE_TILE, stride=TOKEN_TILE_ROWS), :] = jnp.where(mine, y[:, k * 128:(k + 1) * 128], 0.0)

        @pl.when(first_ref[v] == 0)
        def _():
            for k in range(TOKEN_TILE_ROWS):
                rows = pl.ds(k, MOE_TILE, stride=TOKEN_TILE_ROWS)
                o_ref[rows, :] = jnp.where(mine, y[:, k * 128:(k + 1) * 128], o_ref[rows, :])


def _moe_experts(visits, sorted_x, w_gate, w_up, w_down):
    w = sorted_x.shape[1]
    ne, d, de = w_gate.shape
    x_map = lambda v, vt, ve, lo, hi, first, nv: (vt[v], 0)
    w_map = lambda v, vt, ve, lo, hi, first, nv: (ve[v], 0, 0)
    tile_rows = MOE_TILE * TOKEN_TILE_ROWS
    return pl.pallas_call(
        _moe_experts_kernel,
        grid_spec=pltpu.PrefetchScalarGridSpec(
            num_scalar_prefetch=6,
            grid=(visits[0].shape[0],),
            in_specs=[pl.BlockSpec((tile_rows, w), x_map), pl.BlockSpec((1, d, de), w_map),
                      pl.BlockSpec((1, d, de), w_map), pl.BlockSpec((1, de, d), w_map)],
            out_specs=pl.BlockSpec((tile_rows, w), x_map),
        ),
        out_shape=jax.ShapeDtypeStruct(sorted_x.shape, F32),
        compiler_params=_cparams("arbitrary"),
        name="moe_experts",
    )(*visits, sorted_x, w_gate, w_up, w_down)


def _moe_combine_kernel(s0_ref, s1_ref, y_hbm, route_ref, x1_ref, gate2_ref, gf_ref, o_ref, a0_ref, b0_ref, a1_ref,
                        b1_ref, sem):
    j = pl.program_id(0)
    tm = x1_ref.shape[0]
    bufs = ((a0_ref, b0_ref), (a1_ref, b1_ref))

    def issue(step, slot):
        a_ref, b_ref = bufs[slot]

        def body(r, carry):
            _row_copy(y_hbm, s0_ref[step * tm + r], a_ref, r, sem.at[slot]).start()
            _row_copy(y_hbm, s1_ref[step * tm + r], b_ref, r, sem.at[slot]).start()
            return carry
        lax.fori_loop(0, tm, body, 0, unroll=8)

    def finish(slot):
        a_ref, b_ref = bufs[slot]

        def wait(r, carry):
            _row_copy(y_hbm, 0, a_ref, 0, sem.at[slot]).wait()
            _row_copy(y_hbm, 0, b_ref, 0, sem.at[slot]).wait()
            return carry
        lax.fori_loop(0, tm, wait, 0, unroll=8)
        route = route_ref[...]
        moe = route[:, 2:3] * _token_rows(a_ref, tm) + route[:, 3:4] * _token_rows(b_ref, tm)
        o_ref[...] = _rms(x1_ref[...] + gate2_ref[0] * moe, gf_ref[...])

    @pl.when(j == 0)
    def _():
        issue(0, 0)

    for slot in range(2):
        @pl.when((j % 2 == slot) & (j + 1 < pl.num_programs(0)))
        def _():
            issue(j + 1, 1 - slot)

        @pl.when(j % 2 == slot)
        def _():
            finish(slot)


def _moe_combine(slot0, slot1, y_sorted, route, x1, gate2, final_g, *, tm, rows_per_mod):
    n, d = x1.shape
    tiles = pltpu.VMEM((tm * TOKEN_TILE_ROWS, 128), F32)
    return pl.pallas_call(
        _moe_combine_kernel,
        grid_spec=pltpu.PrefetchScalarGridSpec(
            num_scalar_prefetch=2,
            grid=(n // tm,),
            in_specs=[pl.BlockSpec(memory_space=pl.ANY), pl.BlockSpec((tm, 128), lambda j, a, b: (j, 0)),
                      pl.BlockSpec((tm, d), lambda j, a, b: (j, 0)),
                      pl.BlockSpec((1, 1, d), lambda j, a, b: (j * tm // rows_per_mod, 0, 0)),
                      pl.BlockSpec((1, d), lambda j, a, b: (0, 0))],
            out_specs=pl.BlockSpec((tm, d), lambda j, a, b: (j, 0)),
            scratch_shapes=[tiles, tiles, tiles, tiles, pltpu.SemaphoreType.DMA((2,))],
        ),
        out_shape=jax.ShapeDtypeStruct((n, d), F32),
        compiler_params=pltpu.CompilerParams(dimension_semantics=("arbitrary",), vmem_limit_bytes=VMEM_LIMIT,
                                             disable_bounds_checks=True),
        name="moe_combine",
    )(slot0, slot1, y_sorted, route, x1, gate2, final_g)


def _moe_sorted(route, counts, rows, x1, gate2, w_gate, w_up, w_down, final_g):
    b, s, d = x1.shape
    n = b * s
    ne = w_gate.shape[0]
    n_tiles = 2 * n // MOE_TILE
    cnt = counts[0, :ne].astype(jnp.int32)
    upto = jnp.arange(ne)[:, None] <= jnp.arange(ne)[None, :]
    ends = jnp.sum(jnp.where(upto, cnt[:, None], 0), axis=0)
    starts = ends - cnt
    base = jnp.pad(starts.astype(F32), (0, 128 - ne)).reshape(1, 128)
    first_tile = starts // MOE_TILE
    n_vis_e = jnp.where(cnt > 0, (ends - 1) // MOE_TILE - first_tile + 1, 0)
    vis_end = jnp.sum(jnp.where(upto, n_vis_e[:, None], 0), axis=0)
    n_vis = vis_end[-1]
    v = jnp.minimum(jnp.arange(n_tiles + ne, dtype=jnp.int32), n_vis - 1)
    v_expert = jnp.sum((vis_end[None, :] <= v[:, None]).astype(jnp.int32), axis=1)
    v_tile = first_tile[v_expert] + v - (vis_end - n_vis_e)[v_expert]
    v_lo = jnp.maximum(starts[v_expert] - v_tile * MOE_TILE, 0)
    v_hi = jnp.minimum(ends[v_expert] - v_tile * MOE_TILE, MOE_TILE)
    v_first = jnp.concatenate([jnp.ones((1,), jnp.int32), (v_tile[1:] != v_tile[:-1]).astype(jnp.int32)])
    visits = tuple(a.astype(jnp.int32) for a in (v_tile, v_expert, v_lo, v_hi, v_first, n_vis.reshape(1)))

    route2 = route.reshape(n, 128)
    slots = _moe_slots(route2, base, tm=512)
    slot0, slot1 = slots[:, 0], slots[:, 1]
    sorted_x = _moe_scatter(slot0, slot1, rows.reshape(n * TOKEN_TILE_ROWS, 128), n_tiles * MOE_TILE, tm=512)
    y_sorted = _moe_experts(visits, sorted_x, w_gate, w_up, w_down)
    y = _moe_combine(slot0, slot1, y_sorted, route2, x1.reshape(n, d), gate2, final_g, tm=256, rows_per_mod=s)
    return y.reshape(b, s, d)


def _prep_w_in(w_in):
    o_gl = D_ATTN + 3 * ROW_W
    wqkv = w_in[:, :o_gl].astype(BF16)
    wgl = w_in[:, o_gl:o_gl + 3 * N_HEADS]
    pad = jnp.zeros((w_in.shape[0], 128 - 3 * N_HEADS), w_in.dtype)
    wgl = jnp.concatenate([wgl.reshape(-1, N_HEADS, 3).transpose(0, 2, 1).reshape(-1, 3 * N_HEADS), pad], axis=1)
    wu = w_in[:, o_gl + 3 * N_HEADS:].astype(BF16)
    return wqkv, wgl.astype(BF16), wu


def _prep_compress(w_cmp1, pos_cmp, w_cmp2):
    ratio = CMP_BLOCK // CMP_STRIDE
    eye = jnp.eye(KV_HEADS, dtype=w_cmp1.dtype)
    w1 = w_cmp1.reshape(2, ratio, CMP_STRIDE, HEAD_DIM, HEAD_DIM)
    w1big = jnp.einsum('crsdf,kj->cskdrjf', w1, eye).reshape(2, CMP_STRIDE * D_KV, ratio * D_KV).astype(BF16)
    w2big = jnp.einsum('cfd,kj->ckfjd', w_cmp2, eye).reshape(2, D_KV, D_KV).astype(BF16)
    pos = pos_cmp.reshape(2, ratio, CMP_STRIDE, 1, HEAD_DIM)
    pos = jnp.broadcast_to(pos, (2, ratio, CMP_STRIDE, KV_HEADS, HEAD_DIM)).reshape(2, ratio, CMP_STRIDE * D_KV)
    posrows = jnp.concatenate([pos, jnp.zeros((2, 8 - ratio, CMP_STRIDE * D_KV), pos.dtype)], axis=1).astype(BF16)
    return w1big, w2big, posrows


def _prep_router(w_group, b_group, w_router, b_router):
    d = w_group.shape[0]
    pad = 128 - N_EXPERTS - N_GROUPS
    w = jnp.concatenate([w_router, w_group, jnp.zeros((d, pad), w_group.dtype)], axis=1)
    b = jnp.concatenate([b_router, b_group, jnp.zeros((pad,), b_group.dtype)]).reshape(1, 128)
    return w, b


def kernel(x_prompt, x_sample, cache_cmp_kv, cache_slc_kv, state_win_kv, state_conv, page_table, c_prompt, c_sample,
           norm1_g, w_ada, b_ada, w_in, w_cmp1, pos_cmp, w_cmp2, w_dw, b_dw, conv_ln_g, conv_ln_b, g_attn_out,
           g_conv_out, w_out, norm2_g, w_group, b_group, w_router, b_router, w_gate, w_up, w_down, final_g):
    depth = norm1_g.shape[0]
    assert depth == 1 and x_sample.shape[1] == 1
    bp, seq, d = x_prompt.shape
    nb = x_sample.shape[0]
    n_pages = page_table.shape[1]
    past_len = n_pages * PAGE_SIZE
    n_past_slc = past_len // SLC_BLOCK
    win_buf = state_win_kv.shape[2]
    assert win_buf == WINDOW and seq % K_TILE == 0 and seq >= WINDOW + Q_TILE
    l = 0
    row2 = lambda a: a.reshape(1, -1)

    mods = _adaln(jnp.concatenate([c_prompt, c_sample], axis=0), w_ada[l], b_ada[l]).reshape(bp + nb, 6, d)
    mods_p = [mods[:bp, i][:, None, :] for i in range(6)]
    mods_s = [mods[bp:, i][None, :, :] for i in range(6)]

    wqkv, wgl, wu = _prep_w_in(w_in[l])
    w1big, w2big, posrows = _prep_compress(w_cmp1[l], pos_cmp[l], w_cmp2[l])
    w_route, b_route = _prep_router(w_group[l], b_group[l], w_router[l], b_router[l])
    w_out_b = w_out[l].astype(BF16)
    conv_args = (w_dw[l], row2(b_dw[l]), row2(conv_ln_g[l]), row2(conv_ln_b[l]))

    def kv_rows_from_t(a_t):
        n, _, t = a_t.shape
        return a_t.reshape(n, 2, KV_HEADS, HEAD_DIM, t).transpose(0, 4, 1, 2, 3)

    def kv_rows_to_t(a):
        return a.transpose(0, 2, 3, 4, 1)

    qt, kvc, kvct, kvst, kvwt, ks, vts, kw, vtw, gt, u = _mixer(
        x_prompt, mods_p[0], mods_p[1], row2(norm1_g[l]), wqkv, wgl, wu, tm=512, prompt=True)
    kc, vct = _compress_prompt(kvc, w1big, w2big, posrows)
    o_attn = _attn_prompt(qt, kc, vct, ks, vts, kw, vtw, gt)
    o_conv = _conv_prompt(u, *conv_args, tm=512)
    x1, route, moe_rows, counts = _finish(x_prompt, o_attn, o_conv, mods_p[2], mods_p[3], mods_p[4],
                                          row2(g_attn_out[l]), row2(g_conv_out[l]), w_out_b, row2(norm2_g[l]),
                                          w_route, b_route, tm=512, sorted_moe=True)
    y_prompt = _moe_sorted(route, counts, moe_rows, x1, mods_p[5], w_gate[l], w_up[l], w_down[l], row2(final_g))

    new_cmp_prompt = kv_rows_from_t(kvct)[None]
    new_slc_prompt = kv_rows_from_t(kvst)[None]
    new_win_prompt = kv_rows_from_t(kvwt[:, :, seq - WINDOW:])[None]
    new_conv_prompt = u[:, seq - (CONV_WIDTH - 1):][None]

    xs = x_sample.reshape(1, nb, d)
    q_s, kvct_s, kvst_s, kvwt_s, gt_s, u_s = _mixer(
        xs, mods_s[0], mods_s[1], row2(norm1_g[l]), wqkv, wgl, wu, tm=nb, prompt=False)
    cmp_t = kv_rows_to_t(cache_cmp_kv[l])
    q4 = q_s.reshape(nb, KV_HEADS, GQA, HEAD_DIM)
    zq = jnp.zeros_like(q4)
    kvh_id = jnp.arange(KV_HEADS).reshape(1, KV_HEADS, 1, 1)
    qpad = jnp.concatenate([jnp.where(kvh_id == 0, q4, zq), jnp.where(kvh_id == 1, q4, zq)],
                           axis=-1).reshape(nb, N_HEADS, D_KV)
    o_c, idx = _compress_sample(cmp_t.reshape(cmp_t.shape[0], ROW_W, PAGE_SIZE), page_table, qpad, w1big, w2big,
                                posrows, q_pos=past_len, n_slc=n_past_slc + 1)
    idx = idx[:, :KV_HEADS, :SLC_TOPN]
    win_t = kv_rows_to_t(state_win_kv[l]).reshape(nb, ROW_W, win_buf)
    o_heads, new_win_t = _attn_sample(idx, page_table, kv_rows_to_t(cache_slc_kv[l]), q_s.reshape(nb, N_HEADS, HEAD_DIM),
                                      o_c, gt_s[0], kvst_s[0], kvwt_s[0], win_t, n_past_slc=n_past_slc)
    o_attn_s = o_heads.reshape(1, nb, D_ATTN)
    state_t = state_conv[l].transpose(1, 0, 2)
    u_rows = u_s.reshape(nb, D_CONV)
    o_conv_s = _conv_sample(state_t, u_rows, *conv_args, sb=8).reshape(1, nb, D_CONV)
    x1_s, h2_s, comb_s = _finish(xs, o_attn_s, o_conv_s, mods_s[2], mods_s[3], mods_s[4], row2(g_attn_out[l]),
                                 row2(g_conv_out[l]), w_out_b, row2(norm2_g[l]), w_route, b_route, tm=nb,
                                 sorted_moe=False)
    y_sample = _moe(h2_s, comb_s, x1_s, mods_s[5], w_gate[l], w_up[l], w_down[l], row2(final_g), tm=nb)

    row_shape = (1, nb, 1, 2, KV_HEADS, HEAD_DIM)
    new_cmp_sample = kvct_s[0].T.reshape(row_shape)
    new_slc_sample = kvst_s[0].T.reshape(row_shape)
    new_win_sample = kv_rows_from_t(new_win_t)[None]
    new_conv_sample = jnp.concatenate([state_t[1:], u_rows[None]], axis=0).transpose(1, 0, 2)[None]

    return (y_prompt, y_sample.reshape(nb, 1, d), new_cmp_prompt, new_slc_prompt, new_win_prompt, new_conv_prompt,
            new_cmp_sample, new_slc_sample, new_win_sample, new_conv_sample)
```

```python
import functools

import jax
import jax.numpy as jnp
from jax import lax
from jax.experimental import pallas as pl
from jax.experimental.pallas import tpu as pltpu

F32 = jnp.float32
BF16 = jnp.bfloat16

D_MODEL = 1024
N_HEADS = 8
HEAD_DIM = 64
KV_HEADS = 2
GQA = N_HEADS // KV_HEADS
D_ATTN = N_HEADS * HEAD_DIM
D_CONV = D_MODEL - D_ATTN
D_KV = KV_HEADS * HEAD_DIM
CMP_BLOCK = 32
CMP_STRIDE = 16
SLC_BLOCK = 64
SLC_TOPN = 16
N_LOCAL_BLOCKS = 2
WINDOW = 512
FORCED_SCORE = 1e4
CONV_WIDTH = 31
N_GROUPS = 4
EPG = 8
N_EXPERTS = N_GROUPS * EPG
D_EXPERT = 256
PAGE_SIZE = 128
EPS = 1e-6
NEG_INF = -1e30
SCALE = HEAD_DIM ** -0.5
LOG2E = 1.4426950408889634
ROW_W = 2 * D_KV
CHUNK_W = CMP_STRIDE * ROW_W
CHUNK_PITCH = 20
MOE_TILE = 256
TOKEN_TILE_ROWS = D_MODEL // 128
CONV_ROWS = 32
Q_TILE = 128
K_TILE = 1024
VMEM_LIMIT = 48 * 1024 * 1024


def _cparams(*sem):
    return pltpu.CompilerParams(dimension_semantics=sem, vmem_limit_bytes=VMEM_LIMIT)


def _rms(x, g):
    return x * lax.rsqrt(jnp.mean(x * x, axis=-1, keepdims=True) + EPS) * g


def _silu(x):
    return x * jax.nn.sigmoid(x)


def _mm(a, b):
    return jnp.dot(a, b, preferred_element_type=F32)


def _mm_nt(a, b):
    return lax.dot_general(a, b, (((1,), (1,)), ((), ())), preferred_element_type=F32)


def _split3(x):
    hi = x.astype(BF16)
    r = x - hi.astype(F32)
    mid = r.astype(BF16)
    lo = (r - mid.astype(F32)).astype(BF16)
    return hi, mid, lo


def _softmax_masked(s, mask, axis):
    s = jnp.where(mask, s, NEG_INF)
    m = jnp.max(s, axis=axis, keepdims=True)
    e = jnp.exp(s - m)
    p = e / jnp.sum(e, axis=axis, keepdims=True)
    return jnp.where(mask, p, 0.0)


def _adaln_kernel(c_ref, w_ref, b_ref, o_ref):
    s = _silu(c_ref[...]).astype(BF16)
    o_ref[...] = _mm(s, w_ref[...].astype(BF16)) + b_ref[...]


def _adaln(c_all, w_ada, b_ada):
    n, d = c_all.shape
    nout = w_ada.shape[1]
    tn = 1024
    return pl.pallas_call(
        _adaln_kernel,
        grid=(nout // tn,),
        in_specs=[pl.BlockSpec((n, d), lambda j: (0, 0)),
                  pl.BlockSpec((d, tn), lambda j: (0, j)),
                  pl.BlockSpec((1, tn), lambda j: (0, j))],
        out_specs=pl.BlockSpec((n, tn), lambda j: (0, j)),
        out_shape=jax.ShapeDtypeStruct((n, nout), F32),
        compiler_params=_cparams("arbitrary"),
        name="adaln",
    )(c_all, w_ada, b_ada.reshape(1, nout))


def _mixer_kernel(x_ref, shift_ref, scale_ref, g_ref, wqkv_ref, wgl_ref, wu_ref, *outs, prompt):
    x = x_ref[0]
    h = _rms(x, g_ref[...]) * (1.0 + scale_ref[0]) + shift_ref[0]
    hb = h.astype(BF16)
    p = _mm(hb, wqkv_ref[...])
    gl = _mm(hb, wgl_ref[...])
    pu = _mm(hb, wu_ref[...])
    u = pu[:, :D_CONV] * jax.nn.sigmoid(pu[:, D_CONV:])
    q = p[:, :D_ATTN] * (SCALE * LOG2E if prompt else SCALE)
    o = D_ATTN
    kvc = p[:, o:o + ROW_W]
    kvs = p[:, o + ROW_W:o + 2 * ROW_W]
    kvw = p[:, o + 2 * ROW_W:o + 3 * ROW_W]
    kvs_t = kvs.T
    kvw_t = kvw.T
    if prompt:
        q_ref, kvc_ref, kvct_ref, kvst_ref, kvwt_ref, ks_ref, vts_ref, kw_ref, vtw_ref, gt_ref, u_ref = outs
        q_ref[0] = q.T.astype(BF16)
        kvc_ref[0] = kvc
        ks_ref[0] = kvs[:, :D_KV].astype(BF16)
        vts_ref[0] = kvs_t[D_KV:, :].astype(BF16)
        kw_ref[0] = kvw[:, :D_KV].astype(BF16)
        vtw_ref[0] = kvw_t[D_KV:, :].astype(BF16)
    else:
        q_ref, kvct_ref, kvst_ref, kvwt_ref, gt_ref, u_ref = outs
        q_ref[0] = q
    kvct_ref[0] = kvc.T
    kvst_ref[0] = kvs_t
    kvwt_ref[0] = kvw_t
    gt_ref[0] = jax.nn.sigmoid(gl).T[:32, :]
    u_ref[0] = u


def _mixer(x, shift, scale, g, wqkv, wgl, wu, *, tm, prompt):
    b, s, d = x.shape
    r = shift.shape[1]
    mod_block = (1, 1, d) if r == 1 else (1, tm, d)
    mod_map = (lambda i, j: (i, 0, 0)) if r == 1 else (lambda i, j: (i, j, 0))
    row = lambda w: pl.BlockSpec((1, tm, w), lambda i, j: (i, j, 0))
    col = lambda w: pl.BlockSpec((1, w, tm), lambda i, j: (i, 0, j))
    const = lambda a: pl.BlockSpec(a.shape, lambda i, j: (0,) * a.ndim)
    rows = lambda w, dt: jax.ShapeDtypeStruct((b, s, w), dt)
    cols = lambda w, dt: jax.ShapeDtypeStruct((b, w, s), dt)
    if prompt:
        out_specs = [col(D_ATTN), row(ROW_W), col(ROW_W), col(ROW_W), col(ROW_W), row(D_KV), col(D_KV), row(D_KV),
                     col(D_KV), col(32), row(D_CONV)]
        out_shape = [cols(D_ATTN, BF16), rows(ROW_W, F32), cols(ROW_W, F32), cols(ROW_W, F32), cols(ROW_W, F32),
                     rows(D_KV, BF16), cols(D_KV, BF16), rows(D_KV, BF16), cols(D_KV, BF16), cols(32, F32),
                     rows(D_CONV, F32)]
    else:
        out_specs = [row(D_ATTN), col(ROW_W), col(ROW_W), col(ROW_W), col(32), row(D_CONV)]
        out_shape = [rows(D_ATTN, F32), cols(ROW_W, F32), cols(ROW_W, F32), cols(ROW_W, F32), cols(32, F32),
                     rows(D_CONV, F32)]
    return pl.pallas_call(
        functools.partial(_mixer_kernel, prompt=prompt),
        grid=(b, s // tm),
        in_specs=[row(d), pl.BlockSpec(mod_block, mod_map), pl.BlockSpec(mod_block, mod_map),
                  const(g), const(wqkv), const(wgl), const(wu)],
        out_specs=out_specs,
        out_shape=out_shape,
        compiler_params=_cparams("arbitrary", "arbitrary"),
        name="mixer_prompt" if prompt else "mixer_sample",
    )(x, shift, scale, g, wqkv, wgl, wu)


def _chunk_part(load_offset, pos_rows, w1_c):
    xc = jnp.concatenate([load_offset(s) for s in range(CMP_STRIDE)], axis=1).astype(BF16)
    n = xc.shape[0]
    both = _mm(jnp.concatenate([xc, pos_rows], axis=0), w1_c)
    return both[:n], both[n:]


def _compress_finish(part, posb, w2_c):
    n = part.shape[0]
    nxt = pltpu.roll(part[:, D_KV:], n - 1, 0)
    pre = part[:, :D_KV] + nxt + posb[0:1, :D_KV] + posb[1:2, D_KV:]
    return _mm(_silu(pre).astype(BF16), w2_c)


def _compress_prompt_kernel(x_ref, w1_ref, w2_ref, pos_ref, kc_ref, vct_ref):
    for c in range(2):
        part, posb = _chunk_part(lambda s: x_ref[0, :, s * ROW_W + c * D_KV:s * ROW_W + (c + 1) * D_KV], pos_ref[c],
                                 w1_ref[c])
        out = _compress_finish(part, posb, w2_ref[c])
        if c == 0:
            kc_ref[0] = out.astype(BF16)
        else:
            vct_ref[0] = out.T.astype(BF16)


def _compress_prompt(kvc, w1big, w2big, posrows):
    b, s, _ = kvc.shape
    n = s // CMP_STRIDE
    x = kvc.reshape(b, n, CHUNK_W)
    const = lambda a: pl.BlockSpec(a.shape, lambda i: (0,) * a.ndim)
    return pl.pallas_call(
        _compress_prompt_kernel,
        grid=(b,),
        in_specs=[pl.BlockSpec((1, n, CHUNK_W), lambda i: (i, 0, 0)), const(w1big), const(w2big), const(posrows)],
        out_specs=[pl.BlockSpec((1, n, D_KV), lambda i: (i, 0, 0)), pl.BlockSpec((1, D_KV, n), lambda i: (i, 0, 0))],
        out_shape=[jax.ShapeDtypeStruct((b, n, D_KV), BF16), jax.ShapeDtypeStruct((b, D_KV, n), BF16)],
        compiler_params=_cparams("arbitrary"),
        name="compress_prompt",
    )(x, w1big, w2big, posrows)


def _compress_sample_kernel(pt_ref, cache_ref, q_ref, w1_ref, w2_ref, pos_ref, oc_ref, idx_ref, pages_ref, rows0_ref,
                            rows1_ref, sem, *, pages, q_pos, n_slc):
    i = pl.program_id(0)
    row_refs = (rows0_ref, rows1_ref)
    cpp = PAGE_SIZE // CMP_STRIDE
    n = pages * cpp

    def page_copy(seq, k, buf):
        return pltpu.make_async_copy(cache_ref.at[pt_ref[seq * pages + k]], pages_ref.at[buf, k], sem.at[buf])

    def fetch(seq, buf):
        for k in range(pages):
            page_copy(seq, k, buf).start()

    @pl.when(i == 0)
    def _():
        fetch(0, 0)

    @pl.when(i + 1 < pl.num_programs(0))
    def _():
        fetch(i + 1, (i + 1) % 2)

    buf = i % 2
    for k in range(pages):
        page_copy(i, k, buf).wait()

    kv = []
    for c in range(2):
        for k in range(pages):
            rows = pages_ref[buf, k, c * D_KV:(c + 1) * D_KV, :].astype(BF16).T.astype(F32)
            for j in range(cpp):
                r0 = (k * cpp + j) * CHUNK_PITCH
                row_refs[c][r0:r0 + CMP_STRIDE, :] = rows[j * CMP_STRIDE:(j + 1) * CMP_STRIDE, :]
    for c in range(2):
        part, posb = _chunk_part(lambda s: row_refs[c][pl.ds(s, n, stride=CHUNK_PITCH), :], pos_ref[c], w1_ref[c])
        kv.append(_compress_finish(part, posb, w2_ref[c]).astype(BF16))
    _sample_cmp_attend(q_ref[0].astype(BF16), kv[0], kv[1], oc_ref, idx_ref, q_pos=q_pos, n_slc=n_slc)


def _compress_sample(cache_t, page_table, qpad, w1big, w2big, posrows, *, q_pos, n_slc):
    nb, n_pages = page_table.shape
    n = n_pages * (PAGE_SIZE // CMP_STRIDE)
    const = lambda a: pl.BlockSpec(a.shape, lambda i, pt: (0,) * a.ndim)
    per_b = lambda w: pl.BlockSpec((1, N_HEADS, w), lambda i, pt: (i, 0, 0))
    return pl.pallas_call(
        functools.partial(_compress_sample_kernel, pages=n_pages, q_pos=q_pos, n_slc=n_slc),
        grid_spec=pltpu.PrefetchScalarGridSpec(
            num_scalar_prefetch=1,
            grid=(nb,),
            in_specs=[pl.BlockSpec(memory_space=pl.ANY), per_b(D_KV), const(w1big), const(w2big), const(posrows)],
            out_specs=[per_b(D_KV), per_b(128)],
            scratch_shapes=[pltpu.VMEM((2, n_pages, ROW_W, PAGE_SIZE), F32),
                            pltpu.VMEM((n * CHUNK_PITCH, D_KV), F32), pltpu.VMEM((n * CHUNK_PITCH, D_KV), F32),
                            pltpu.SemaphoreType.DMA((2,))],
        ),
        out_shape=[jax.ShapeDtypeStruct((nb, N_HEADS, D_KV), F32), jax.ShapeDtypeStruct((nb, N_HEADS, 128), jnp.int32)],
        compiler_params=pltpu.CompilerParams(dimension_semantics=("arbitrary",), vmem_limit_bytes=VMEM_LIMIT,
                                             disable_bounds_checks=True),
        name="compress_sample",
    )(page_table.reshape(-1), cache_t, qpad, w1big, w2big, posrows)


def _overlap(cmp_idx, slc_idx):
    lo = cmp_idx * CMP_STRIDE
    so = slc_idx * SLC_BLOCK
    return (lo <= so + SLC_BLOCK - 1) & (lo + CMP_BLOCK - 1 >= so)


def _forced_importance(imp, blk, cur, n_slc):
    valid = (blk <= cur) & (blk < n_slc)
    forced = (blk == 0) | ((cur - blk >= 0) & (cur - blk < N_LOCAL_BLOCKS))
    return jnp.where(valid & forced, FORCED_SCORE, jnp.where(valid, imp, -1.0))


def _attn_prompt_kernel(qt_ref, kc_ref, vct_ref, ks_ref, vts_ref, kw_ref, vtw_ref, gt_ref, o_ref, sel_ref):
    i = pl.program_id(1)
    seq = ks_ref.shape[1]
    nbp = kc_ref.shape[1]
    n_slc = seq // SLC_BLOCK
    lanes = GQA * Q_TILE
    lane = lax.broadcasted_iota(jnp.int32, (1, lanes), 1)
    qpos = i * Q_TILE + (lane & (Q_TILE - 1))
    qpos_q = qpos[:, :Q_TILE]
    qt = qt_ref[0]
    gt = gt_ref[0]
    win_keys = min(WINDOW + Q_TILE, seq)
    blocks_per_tile = K_TILE // SLC_BLOCK

    ovl = _overlap(lax.broadcasted_iota(jnp.int32, (n_slc, nbp), 1), lax.broadcasted_iota(jnp.int32, (n_slc, nbp), 0))
    ovl = jnp.where(ovl, 1.0, 0.0).astype(BF16)
    blk = lax.broadcasted_iota(jnp.int32, (n_slc, Q_TILE), 0)

    cpos = lax.broadcasted_iota(jnp.int32, (nbp, lanes), 0) * CMP_STRIDE + (CMP_BLOCK - 1)
    cmp_bias = jnp.where(cpos <= qpos, 0.0, NEG_INF)
    any_cmp = jnp.where(qpos >= CMP_BLOCK - 1, 1.0, 0.0)
    w0 = pl.multiple_of(jnp.clip(i * Q_TILE - WINDOW, 0, seq - win_keys), Q_TILE)
    wpos = w0 + lax.broadcasted_iota(jnp.int32, (win_keys, lanes), 0)
    win_bias = jnp.where(wpos <= qpos, jnp.where(wpos >= qpos - WINDOW, 0.0, NEG_INF), NEG_INF)
    n_steps = ((i + 1) * Q_TILE + K_TILE - 1) // K_TILE
    last0 = pl.multiple_of((n_steps - 1) * K_TILE, K_TILE)
    causal_bias = jnp.where(last0 + lax.broadcasted_iota(jnp.int32, (K_TILE, lanes), 0) <= qpos, 0.0, NEG_INF)

    heads = range(KV_HEADS)
    hd = [slice(kvh * HEAD_DIM, (kvh + 1) * HEAD_DIM) for kvh in heads]
    qpad, o_c = [], []
    for kvh in heads:
        qk = jnp.concatenate([qt[(kvh * GQA + g) * HEAD_DIM:(kvh * GQA + g + 1) * HEAD_DIM, :] for g in range(GQA)],
                             axis=1)
        zero = jnp.zeros_like(qk)
        qpad.append(jnp.concatenate([qk, zero] if kvh == 0 else [zero, qk], axis=0))

        s = _mm(kc_ref[0], qpad[kvh]) + cmp_bias
        e = jnp.exp2(s - jnp.max(s, axis=0, keepdims=True))
        p = e * (any_cmp / jnp.sum(e, axis=0, keepdims=True))
        o_c.append(_mm(vct_ref[0, hd[kvh], :], p.astype(BF16)))

        psum = p[:, 0:Q_TILE]
        for g in range(1, GQA):
            psum = psum + p[:, g * Q_TILE:(g + 1) * Q_TILE]
        imp = sum(_mm(ovl, t) for t in _split3(psum))
        impf = _forced_importance(imp, blk, qpos_q // SLC_BLOCK, n_slc)
        groups = [impf[r:r + 8, :] for r in range(0, n_slc, 8)]
        ranks = [jnp.zeros_like(grp) for grp in groups]
        for k in range(n_slc):
            rk = impf[k:k + 1, :]
            for gi, grp in enumerate(groups):
                if gi * 8 > k:
                    ranks[gi] = ranks[gi] + jnp.where(rk >= grp, 1.0, 0.0)
                elif gi * 8 + 7 < k:
                    ranks[gi] = ranks[gi] + jnp.where(rk > grp, 1.0, 0.0)
                else:
                    later = blk[0:8, :] + gi * 8 > k
                    ranks[gi] = ranks[gi] + jnp.where(later, jnp.where(rk >= grp, 1.0, 0.0),
                                                      jnp.where(rk > grp, 1.0, 0.0))
        rank = jnp.concatenate(ranks, axis=0)
        sel = jnp.where(rank < SLC_TOPN, jnp.where(impf >= 0.0, 0.0, NEG_INF), NEG_INF)
        sel_ref[kvh] = jnp.concatenate([sel] * GQA, axis=1)

    def slc_step(t, carry, extra_bias=None):
        k0 = pl.multiple_of(t * K_TILE, K_TILE)
        keys = ks_ref[0, pl.ds(k0, K_TILE), :]
        out = []
        for kvh in heads:
            m, l, acc = carry[kvh]
            sk = _mm(keys, qpad[kvh])
            sk = jnp.concatenate(
                [sk[j * SLC_BLOCK:(j + 1) * SLC_BLOCK, :] + sel_ref[kvh, pl.ds(t * blocks_per_tile + j, 1), :]
                 for j in range(blocks_per_tile)], axis=0)
            if extra_bias is not None:
                sk = sk + extra_bias
            m_new = jnp.maximum(m, jnp.max(sk, axis=0, keepdims=True))
            alpha = jnp.exp2(m - m_new)
            e = jnp.exp2(sk - m_new)
            l_new = alpha * l + jnp.sum(e, axis=0, keepdims=True)
            pv = _mm(vts_ref[0, hd[kvh], pl.ds(k0, K_TILE)], e.astype(BF16))
            out.append((m_new, l_new, alpha * acc + pv))
        return tuple(out)

    init = (jnp.full((1, lanes), NEG_INF, F32), jnp.zeros((1, lanes), F32), jnp.zeros((HEAD_DIM, lanes), F32))
    carry = lax.fori_loop(0, n_steps - 1, slc_step, (init,) * KV_HEADS)
    carry = slc_step(n_steps - 1, carry, causal_bias)

    for kvh in heads:
        _, l_s, acc_s = carry[kvh]
        o_s = acc_s * (1.0 / l_s)

        sw = _mm(kw_ref[0, pl.ds(w0, win_keys), :], qpad[kvh]) + win_bias
        ew = jnp.exp2(sw - jnp.max(sw, axis=0, keepdims=True))
        o_w = _mm(vtw_ref[0, hd[kvh], pl.ds(w0, win_keys)], ew.astype(BF16)) * (
            1.0 / jnp.sum(ew, axis=0, keepdims=True))

        def gate(r):
            return jnp.concatenate([gt[r * N_HEADS + kvh * GQA + g:r * N_HEADS + kvh * GQA + g + 1, :]
                                    for g in range(GQA)], axis=1)
        o_t = gate(0) * o_c[kvh] + gate(1) * o_s + gate(2) * o_w
        for pair in range(GQA // 2):
            two = jnp.concatenate([o_t[:, (2 * pair) * Q_TILE:(2 * pair + 1) * Q_TILE],
                                   o_t[:, (2 * pair + 1) * Q_TILE:(2 * pair + 2) * Q_TILE]], axis=0)
            c0 = kvh * GQA * HEAD_DIM + pair * 2 * HEAD_DIM
            o_ref[0, :, c0:c0 + 2 * HEAD_DIM] = two.T


def _attn_prompt(qt, kc, vct, ks, vts, kw, vtw, gt):
    b, _, s = qt.shape
    nbp = kc.shape[1]
    per_b = lambda shape: pl.BlockSpec((1,) + shape, lambda i, j: (i, 0, 0))
    return pl.pallas_call(
        _attn_prompt_kernel,
        grid=(b, s // Q_TILE),
        in_specs=[pl.BlockSpec((1, D_ATTN, Q_TILE), lambda i, j: (i, 0, j)),
                  per_b((nbp, D_KV)), per_b((D_KV, nbp)),
                  per_b((s, D_KV)), per_b((D_KV, s)), per_b((s, D_KV)), per_b((D_KV, s)),
                  pl.BlockSpec((1, 32, Q_TILE), lambda i, j: (i, 0, j))],
        out_specs=pl.BlockSpec((1, Q_TILE, D_ATTN), lambda i, j: (i, j, 0)),
        out_shape=jax.ShapeDtypeStruct((b, s, D_ATTN), F32),
        scratch_shapes=[pltpu.VMEM((KV_HEADS, s // SLC_BLOCK, GQA * Q_TILE), F32)],
        compiler_params=_cparams("arbitrary", "arbitrary"),
        name="attn_prompt",
    )(qt, kc, vct, ks, vts, kw, vtw, gt)


def _sample_cmp_attend(q, kc, vc, oc_ref, idx_ref, *, q_pos, n_slc):
    nb = kc.shape[0]
    s = _mm_nt(q, kc)
    cpos = lax.broadcasted_iota(jnp.int32, s.shape, 1) * CMP_STRIDE + (CMP_BLOCK - 1)
    p = _softmax_masked(s, cpos <= q_pos, 1)
    oc_ref[0] = _mm(p.astype(BF16), vc)

    nsp = idx_ref.shape[2] * 2
    group_sums = [jnp.sum(p[k * GQA:(k + 1) * GQA, :], axis=0, keepdims=True) for k in range(KV_HEADS)]
    psum = jnp.concatenate(group_sums + [jnp.zeros((N_HEADS - KV_HEADS, nb), F32)], axis=0)
    ovl = _overlap(lax.broadcasted_iota(jnp.int32, (nb, nsp), 0), lax.broadcasted_iota(jnp.int32, (nb, nsp), 1))
    ovl = jnp.where(ovl, 1.0, 0.0).astype(BF16)
    imp = sum(_mm(t, ovl) for t in _split3(psum))
    blk = lax.broadcasted_iota(jnp.int32, imp.shape, 1)
    impf = _forced_importance(imp, blk, q_pos // SLC_BLOCK, n_slc)
    eye = jnp.where(lax.broadcasted_iota(jnp.int32, (nsp, nsp), 0) == lax.broadcasted_iota(jnp.int32, (nsp, nsp), 1),
                    1.0, 0.0).astype(BF16)
    imp_t = sum(_mm_nt(eye, t) for t in _split3(impf))
    k_idx = lax.broadcasted_iota(jnp.int32, (nsp, nsp), 0)
    j_idx = lax.broadcasted_iota(jnp.int32, (nsp, nsp), 1)
    ranks = []
    for r in range(KV_HEADS):
        col, row = imp_t[:, r:r + 1], impf[r:r + 1, :]
        beats = jnp.where(col > row, 1.0, jnp.where(col == row, jnp.where(k_idx < j_idx, 1.0, 0.0), 0.0))
        ranks.append(jnp.sum(beats, axis=0, keepdims=True))
    rank = jnp.concatenate(ranks + [jnp.full((N_HEADS - KV_HEADS, nsp), float(nsp), F32)], axis=0)
    blk_f = blk.astype(F32)
    slot = lax.broadcasted_iota(jnp.int32, (N_HEADS, idx_ref.shape[2]), 1)
    idx = jnp.zeros((N_HEADS, idx_ref.shape[2]), F32)
    for t in range(SLC_TOPN):
        chosen = jnp.sum(jnp.where(rank == float(t), blk_f, 0.0), axis=1, keepdims=True)
        idx = idx + jnp.where(slot == t, chosen, 0.0)
    idx_ref[0] = idx.astype(jnp.int32)


def _attn_sample_kernel(idx_ref, pt_ref, cache_ref, q_ref, oc_ref, gt_ref, kvst_ref, kvwt_ref, win_ref, o_ref, nwin_ref,
                        blk_ref, sem, *, n_past_slc, n_pages):
    n_sel = KV_HEADS * SLC_TOPN
    b = pl.program_id(0)
    nb = kvst_ref.shape[1]
    sub = PAGE_SIZE // SLC_BLOCK

    def block_copy(seq, s, buf):
        j = jnp.minimum(idx_ref[seq * n_sel + s], n_past_slc - 1)
        page = pt_ref[seq * n_pages + j // sub]
        return pltpu.make_async_copy(cache_ref.at[page, :, s // SLC_TOPN], blk_ref.at[buf, s], sem.at[buf])

    def fetch(seq, buf):
        for s in range(n_sel):
            block_copy(seq, s, buf).start()

    @pl.when(b == 0)
    def _():
        fetch(0, 0)

    @pl.when(b + 1 < pl.num_programs(0))
    def _():
        fetch(b + 1, (b + 1) % 2)

    buf = b % 2
    for s in range(n_sel):
        block_copy(b, s, buf).wait()

    q = q_ref[0].astype(BF16)
    head_kvh = lax.broadcasted_iota(jnp.int32, (N_HEADS, 1), 0) // GQA
    mine = lax.broadcasted_iota(jnp.int32, (1, nb), 1) == b
    lane_half = lax.broadcasted_iota(jnp.int32, (1, PAGE_SIZE), 1) // SLC_BLOCK

    def attend(kt, vt, valid):
        s = jnp.where(valid, _mm(q, kt), NEG_INF)
        e = jnp.where(valid, jnp.exp(s - jnp.max(s, axis=1, keepdims=True)), 0.0)
        return _mm_nt(e.astype(BF16), vt) / jnp.sum(e, axis=1, keepdims=True)

    kvst = kvst_ref[...]
    kvwt = kvwt_ref[...]
    o_s = jnp.zeros((N_HEADS, HEAD_DIM), F32)
    o_w = jnp.zeros((N_HEADS, HEAD_DIM), F32)
    win = win_ref[0]
    for kvh in range(KV_HEADS):
        k_rows = slice(kvh * HEAD_DIM, (kvh + 1) * HEAD_DIM)
        v_rows = slice(D_KV + kvh * HEAD_DIM, D_KV + (kvh + 1) * HEAD_DIM)
        valid = []
        n_new = jnp.int32(0)
        for t in range(SLC_TOPN):
            j = idx_ref[(b * KV_HEADS + kvh) * SLC_TOPN + t]
            past = j < n_past_slc
            half = jnp.minimum(j, n_past_slc - 1) % (PAGE_SIZE // SLC_BLOCK)
            valid.append((lane_half == half) & past)
            n_new = n_new + jnp.where(past, 0, 1)
        valid.append(mine & (n_new > 0))
        kt = jnp.concatenate([blk_ref[buf, kvh * SLC_TOPN + t, 0] for t in range(SLC_TOPN)] + [kvst[k_rows, :]],
                             axis=1)
        vt = jnp.concatenate([blk_ref[buf, kvh * SLC_TOPN + t, 1] for t in range(SLC_TOPN)] + [kvst[v_rows, :]],
                             axis=1)
        o_k = attend(kt.astype(BF16), vt.astype(BF16), jnp.concatenate(valid, axis=1))
        o_s = jnp.where(head_kvh == kvh, o_k, o_s)
        kt = jnp.concatenate([win[k_rows, :], kvwt[k_rows, :]], axis=1)
        vt = jnp.concatenate([win[v_rows, :], kvwt[v_rows, :]], axis=1)
        valid_w = jnp.concatenate([jnp.full((1, win.shape[1]), True), mine], axis=1)
        o_k = attend(kt.astype(BF16), vt.astype(BF16), valid_w)
        o_w = jnp.where(head_kvh == kvh, o_k, o_w)

    oc = oc_ref[0]
    o_c = jnp.where(head_kvh == 0, oc[:, :HEAD_DIM], oc[:, HEAD_DIM:])
    gates = jnp.sum(jnp.where(mine, gt_ref[...], 0.0), axis=1, keepdims=True)
    o_ref[0] = (gates[0:N_HEADS] * o_c + gates[N_HEADS:2 * N_HEADS] * o_s + gates[2 * N_HEADS:3 * N_HEADS] * o_w)

    new_col = jnp.sum(jnp.where(mine, kvwt, 0.0), axis=1, keepdims=True)
    last = lax.broadcasted_iota(jnp.int32, win.shape, 1) == win.shape[1] - 1
    nwin_ref[0] = jnp.where(last, new_col, pltpu.roll(win, win.shape[1] - 1, 1))


def _attn_sample(idx, page_table, cache_t, q, o_c, gt, kvst, kvwt, win_t, *, n_past_slc):
    nb = q.shape[0]
    win_buf = win_t.shape[2]
    n_pages = page_table.shape[1]
    per_b = lambda shape: pl.BlockSpec((1,) + shape, lambda i, a, c: (i,) + (0,) * len(shape))
    const = lambda a: pl.BlockSpec(a.shape, lambda i, x, c: (0,) * a.ndim)
    n_sel = KV_HEADS * SLC_TOPN
    return pl.pallas_call(
        functools.partial(_attn_sample_kernel, n_past_slc=n_past_slc, n_pages=n_pages),
        grid_spec=pltpu.PrefetchScalarGridSpec(
            num_scalar_prefetch=2,
            grid=(nb,),
            in_specs=[pl.BlockSpec(memory_space=pl.ANY), per_b((N_HEADS, HEAD_DIM)), per_b((N_HEADS, D_KV)),
                      const(gt), const(kvst), const(kvwt), per_b((ROW_W, win_buf))],
            out_specs=[per_b((N_HEADS, HEAD_DIM)), per_b((ROW_W, win_buf))],
            scratch_shapes=[pltpu.VMEM((2, n_sel, 2, HEAD_DIM, PAGE_SIZE), F32), pltpu.SemaphoreType.DMA((2,))],
        ),
        out_shape=[jax.ShapeDtypeStruct((nb, N_HEADS, HEAD_DIM), F32), jax.ShapeDtypeStruct(win_t.shape, F32)],
        compiler_params=pltpu.CompilerParams(dimension_semantics=("arbitrary",), vmem_limit_bytes=VMEM_LIMIT,
                                             disable_bounds_checks=True),
        name="attn_sample",
    )(idx.reshape(-1), page_table.reshape(-1), cache_t, q, o_c, gt, kvst, kvwt, win_t)


def _conv_tail(y, b_ref, lg_ref, lb_ref):
    y = y + b_ref[...]
    yc = y - jnp.mean(y, axis=-1, keepdims=True)
    yn = yc * lax.rsqrt(jnp.mean(yc * yc, axis=-1, keepdims=True) + EPS)
    return _silu(yn * lg_ref[...] + lb_ref[...])


def _conv_prompt_kernel(u_ref, halo_ref, w_ref, b_ref, lg_ref, lb_ref, o_ref, buf_ref, shift_ref, *, halo):
    tm = u_ref.shape[1]
    j = pl.program_id(1)
    buf_ref[0:halo, :] = jnp.where(j > 0, halo_ref[0], 0.0)
    buf_ref[halo:halo + tm, :] = u_ref[0]
    buf_ref[halo + tm:, :] = jnp.zeros((8, buf_ref.shape[1]), F32)
    lead = halo - (CONV_WIDTH - 1)
    for ph in range(8):
        shift_ref[ph] = buf_ref[ph:ph + tm + halo, :]

    def chunk(c, carry):
        r0 = pl.multiple_of(c * CONV_ROWS, CONV_ROWS)
        y = None
        for w in range(CONV_WIDTH):
            o = lead + w
            tap = shift_ref[o % 8, pl.ds(r0 + o // 8 * 8, CONV_ROWS), :] * w_ref[w:w + 1, :]
            y = tap if y is None else y + tap
        o_ref[0, pl.ds(r0, CONV_ROWS), :] = _conv_tail(y, b_ref, lg_ref, lb_ref)
        return carry

    lax.fori_loop(0, tm // CONV_ROWS, chunk, 0, unroll=4)


def _conv_prompt(u, w_dw, b_dw, ln_g, ln_b, *, tm):
    b, s, dc = u.shape
    halo = 32
    per = tm // halo
    const = lambda a: pl.BlockSpec(a.shape, lambda i, j: (0,) * a.ndim)
    return pl.pallas_call(
        functools.partial(_conv_prompt_kernel, halo=halo),
        grid=(b, s // tm),
        in_specs=[pl.BlockSpec((1, tm, dc), lambda i, j: (i, j, 0)),
                  pl.BlockSpec((1, halo, dc), lambda i, j: (i, jnp.maximum(j * per - 1, 0), 0)),
                  const(w_dw), const(b_dw), const(ln_g), const(ln_b)],
        out_specs=pl.BlockSpec((1, tm, dc), lambda i, j: (i, j, 0)),
        out_shape=jax.ShapeDtypeStruct((b, s, dc), F32),
        scratch_shapes=[pltpu.VMEM((tm + halo + 8, dc), F32), pltpu.VMEM((8, tm + halo, dc), F32)],
        compiler_params=_cparams("arbitrary", "arbitrary"),
        name="conv_prompt",
    )(u, u, w_dw, b_dw, ln_g, ln_b)


def _conv_sample_kernel(st_ref, u_ref, w_ref, b_ref, lg_ref, lb_ref, o_ref):
    hist = CONV_WIDTH - 1
    y = u_ref[...] * w_ref[hist:hist + 1, :]
    for w in range(hist):
        y = y + st_ref[w] * w_ref[w:w + 1, :]
    o_ref[...] = _conv_tail(y, b_ref, lg_ref, lb_ref)


def _conv_sample(state_t, u, w_dw, b_dw, ln_g, ln_b, *, sb):
    nb, dc = u.shape
    hist = state_t.shape[0]
    const = lambda a: pl.BlockSpec(a.shape, lambda i: (0,) * a.ndim)
    return pl.pallas_call(
        _conv_sample_kernel,
        grid=(nb // sb,),
        in_specs=[pl.BlockSpec((hist, sb, dc), lambda i: (0, i, 0)), pl.BlockSpec((sb, dc), lambda i: (i, 0)),
                  const(w_dw), const(b_dw), const(ln_g), const(ln_b)],
        out_specs=pl.BlockSpec((sb, dc), lambda i: (i, 0)),
        out_shape=jax.ShapeDtypeStruct((nb, dc), F32),
        compiler_params=_cparams("arbitrary"),
        name="conv_sample",
    )(state_t, u, w_dw, b_dw, ln_g, ln_b)


def _finish_kernel(x_ref, oa_ref, ocv_ref, gate1_ref, shift2_ref, scale2_ref, ga_ref, gc_ref, wout_ref, g2_ref,
                   wr_ref, br_ref, x1_ref, *outs, sorted_moe):
    mixed = jnp.concatenate([_rms(oa_ref[0], ga_ref[...]), _rms(ocv_ref[0], gc_ref[...])], axis=1)
    mix = _mm(mixed.astype(BF16), wout_ref[...])
    x1 = x_ref[0] + gate1_ref[0] * mix
    x1_ref[0] = x1
    h2 = _rms(x1, g2_ref[...]) * (1.0 + scale2_ref[0]) + shift2_ref[0]

    h_hi, h_mid, _ = _split3(h2)
    w_hi, w_mid, _ = _split3(wr_ref[...])
    lg = (_mm(h_hi, w_hi) + _mm(h_hi, w_mid)
          + _mm(h_mid, w_hi)) + br_ref[...]
    lane = lax.broadcasted_iota(jnp.int32, lg.shape, 1)
    lane_f = lane.astype(F32)
    is_group = (lane >= N_EXPERTS) & (lane < N_EXPERTS + N_GROUPS)
    gl = jnp.where(is_group, lg, NEG_INF)
    g_max = jnp.max(gl, axis=1, keepdims=True)
    p_top = 1.0 / jnp.sum(jnp.where(is_group, jnp.exp(gl - g_max), 0.0), axis=1, keepdims=True)
    g_lane = jnp.min(jnp.where(gl == g_max, lane_f, 1e9), axis=1, keepdims=True)
    in_group = (lane < N_EXPERTS) & ((lane // EPG).astype(F32) == g_lane - N_EXPERTS)
    el = jnp.where(in_group, lg, NEG_INF)
    l1 = jnp.max(el, axis=1, keepdims=True)
    i1 = jnp.min(jnp.where(el == l1, lane_f, 1e9), axis=1, keepdims=True)
    el2 = jnp.where(lane_f == i1, NEG_INF, el)
    l2 = jnp.max(el2, axis=1, keepdims=True)
    i2 = jnp.min(jnp.where(el2 == l2, lane_f, 1e9), axis=1, keepdims=True)
    r = jnp.exp(l2 - l1)
    w1 = p_top / (1.0 + r)
    w2 = p_top * r / (1.0 + r)
    if not sorted_moe:
        h2_ref, comb_ref = outs
        h2_ref[0] = h2.astype(BF16)
        comb_ref[0] = jnp.where(lane_f == i1, w1, jnp.where(lane_f == i2, w2, 0.0))
        return
    route_ref, rows_ref, count_ref = outs
    for k in range(TOKEN_TILE_ROWS):
        rows_ref[0, pl.ds(k, h2.shape[0], stride=TOKEN_TILE_ROWS), :] = h2[:, k * 128:(k + 1) * 128]
    route_ref[0] = jnp.where(lane == 0, i1, jnp.where(lane == 1, i2, jnp.where(lane == 2, w1, jnp.where(
        lane == 3, w2, 0.0))))

    @pl.when((pl.program_id(0) == 0) & (pl.program_id(1) == 0))
    def _():
        count_ref[...] = jnp.zeros_like(count_ref)

    picks = jnp.where(lane_f == i1, 1.0, 0.0) + jnp.where(lane_f == i2, 1.0, 0.0)
    count_ref[...] += jnp.sum(picks, axis=0, keepdims=True)


def _finish(x, o_attn, o_conv, gate1, shift2, scale2, ga, gc, w_out, g2, w_route, b_route, *, tm, sorted_moe):
    b, s, d = x.shape
    r = gate1.shape[1]
    mod_block = (1, 1, d) if r == 1 else (1, tm, d)
    mod_map = (lambda i, j: (i, 0, 0)) if r == 1 else (lambda i, j: (i, j, 0))
    mod = pl.BlockSpec(mod_block, mod_map)
    row = lambda w: pl.BlockSpec((1, tm, w), lambda i, j: (i, j, 0))
    const = lambda a: pl.BlockSpec(a.shape, lambda i, j: (0,) * a.ndim)
    shape = lambda w, dt: jax.ShapeDtypeStruct((b, s, w), dt)
    if sorted_moe:
        tiles = pl.BlockSpec((1, tm * TOKEN_TILE_ROWS, 128), lambda i, j: (i, j, 0))
        out_specs = [row(d), row(128), tiles, pl.BlockSpec((8, 128), lambda i, j: (0, 0))]
        out_shape = [shape(d, F32), shape(128, F32), jax.ShapeDtypeStruct((b, s * TOKEN_TILE_ROWS, 128), F32),
                     jax.ShapeDtypeStruct((8, 128), F32)]
    else:
        out_specs = [row(d), row(d), row(128)]
        out_shape = [shape(d, F32), shape(d, BF16), shape(128, F32)]
    return pl.pallas_call(
        functools.partial(_finish_kernel, sorted_moe=sorted_moe),
        grid=(b, s // tm),
        in_specs=[row(d), row(D_ATTN), row(D_CONV), mod, mod, mod, const(ga), const(gc), const(w_out), const(g2),
                  const(w_route), const(b_route)],
        out_specs=out_specs,
        out_shape=out_shape,
        compiler_params=_cparams("arbitrary", "arbitrary"),
        name="finish_sorted" if sorted_moe else "finish",
    )(x, o_attn, o_conv, gate1, shift2, scale2, ga, gc, w_out, g2, w_route, b_route)


def _moe_kernel(h_ref, comb_ref, x1_ref, gate2_ref, wg_ref, wu_ref, wd_ref, gf_ref, y_ref, acc_ref):
    e = pl.program_id(2)

    @pl.when(e == 0)
    def _():
        acc_ref[...] = jnp.zeros_like(acc_ref)

    h = h_ref[0]
    a = _mm(h, wg_ref[0].astype(BF16))
    u = _mm(h, wu_ref[0].astype(BF16))
    comb = comb_ref[0]
    lane = lax.broadcasted_iota(jnp.int32, comb.shape, 1)
    cw = jnp.sum(jnp.where(lane == e, comb, 0.0), axis=1, keepdims=True)
    hid = (_silu(a) * u * cw).astype(BF16)
    acc_ref[...] += _mm(hid, wd_ref[0].astype(BF16))

    @pl.when(e == pl.num_programs(2) - 1)
    def _():
        y = x1_ref[0] + gate2_ref[0] * acc_ref[...]
        y_ref[0] = _rms(y, gf_ref[...])


def _moe(h2, comb, x1, gate2, w_gate, w_up, w_down, final_g, *, tm):
    b, s, d = x1.shape
    r = gate2.shape[1]
    mod_block = (1, 1, d) if r == 1 else (1, tm, d)
    mod_map = (lambda i, j, e: (i, 0, 0)) if r == 1 else (lambda i, j, e: (i, j, 0))
    row = lambda w: pl.BlockSpec((1, tm, w), lambda i, j, e: (i, j, 0))
    ne, _, de = w_gate.shape
    return pl.pallas_call(
        _moe_kernel,
        grid=(b, s // tm, ne),
        in_specs=[row(d), row(128), row(d), pl.BlockSpec(mod_block, mod_map),
                  pl.BlockSpec((1, d, de), lambda i, j, e: (e, 0, 0)),
                  pl.BlockSpec((1, d, de), lambda i, j, e: (e, 0, 0)),
                  pl.BlockSpec((1, de, d), lambda i, j, e: (e, 0, 0)),
                  pl.BlockSpec((1, d), lambda i, j, e: (0, 0))],
        out_specs=row(d),
        out_shape=jax.ShapeDtypeStruct((b, s, d), F32),
        scratch_shapes=[pltpu.VMEM((tm, d), F32)],
        compiler_params=_cparams("arbitrary", "arbitrary", "arbitrary"),
        name="moe",
    )(h2, comb, x1, gate2, w_gate, w_up, w_down, final_g)


def _moe_slots_kernel(route_ref, base_ref, slot_ref, run_ref):
    @pl.when(pl.program_id(0) == 0)
    def _():
        run_ref[...] = jnp.zeros_like(run_ref)

    route = route_ref[...]
    tm = route.shape[0]
    lane = lax.broadcasted_iota(jnp.int32, route.shape, 1).astype(F32)
    first = jnp.where(lane == route[:, 0:1], 1.0, 0.0)
    second = jnp.where(lane == route[:, 1:2], 1.0, 0.0)
    picks = first + second
    earlier = lax.broadcasted_iota(jnp.int32, (tm, tm), 1) < lax.broadcasted_iota(jnp.int32, (tm, tm), 0)
    seen = _mm(jnp.where(earlier, 1.0, 0.0).astype(BF16), picks.astype(BF16)) + (run_ref[...] + base_ref[...])
    slot0 = jnp.sum(first * seen, axis=1, keepdims=True)
    slot1 = jnp.sum(second * seen, axis=1, keepdims=True)
    slot_ref[...] = jnp.where(lane == 0.0, slot0, jnp.where(lane == 1.0, slot1, 0.0)).astype(jnp.int32)
    run_ref[...] += jnp.sum(picks, axis=0, keepdims=True)


def _moe_slots(route, base, *, tm):
    n = route.shape[0]
    return pl.pallas_call(
        _moe_slots_kernel,
        grid=(n // tm,),
        in_specs=[pl.BlockSpec((tm, 128), lambda j: (j, 0)), pl.BlockSpec((1, 128), lambda j: (0, 0))],
        out_specs=pl.BlockSpec((tm, 128), lambda j: (j, 0)),
        out_shape=jax.ShapeDtypeStruct((n, 128), jnp.int32),
        scratch_shapes=[pltpu.VMEM((1, 128), F32)],
        compiler_params=_cparams("arbitrary"),
        name="moe_slots",
    )(route, base)


def _row_copy(src_ref, src_row, dst_ref, dst_row, sem):
    src = pl.ds(pl.multiple_of(src_row * TOKEN_TILE_ROWS, TOKEN_TILE_ROWS), TOKEN_TILE_ROWS)
    dst = pl.ds(pl.multiple_of(dst_row * TOKEN_TILE_ROWS, TOKEN_TILE_ROWS), TOKEN_TILE_ROWS)
    return pltpu.make_async_copy(src_ref.at[src, :], dst_ref.at[dst, :], sem)


def _token_rows(ref, n):
    return jnp.concatenate([ref[pl.ds(k, n, stride=TOKEN_TILE_ROWS), :] for k in range(TOKEN_TILE_ROWS)], axis=1)


def _moe_scatter_kernel(s0_ref, s1_ref, h_ref, init_ref, sorted_ref, sem):
    del init_ref
    j = pl.program_id(0)
    tm = h_ref.shape[0] // TOKEN_TILE_ROWS

    def start(r, carry):
        _row_copy(h_ref, r, sorted_ref, s0_ref[j * tm + r], sem).start()
        _row_copy(h_ref, r, sorted_ref, s1_ref[j * tm + r], sem).start()
        return carry

    def wait(r, carry):
        _row_copy(h_ref, 0, sorted_ref, 0, sem).wait()
        _row_copy(h_ref, 0, sorted_ref, 0, sem).wait()
        return carry

    lax.fori_loop(0, tm, start, 0, unroll=8)
    lax.fori_loop(0, tm, wait, 0, unroll=8)


def _moe_scatter(slot0, slot1, rows, n_slots, *, tm):
    w = rows.shape[1]
    n = rows.shape[0] // TOKEN_TILE_ROWS
    tm, n_slots = tm * TOKEN_TILE_ROWS, n_slots * TOKEN_TILE_ROWS
    n = n * TOKEN_TILE_ROWS
    return pl.pallas_call(
        _moe_scatter_kernel,
        grid_spec=pltpu.PrefetchScalarGridSpec(
            num_scalar_prefetch=2,
            grid=(n // tm,),
            in_specs=[pl.BlockSpec((tm, w), lambda j, a, b: (j, 0)), pl.BlockSpec(memory_space=pl.ANY)],
            out_specs=pl.BlockSpec(memory_space=pl.ANY),
            scratch_shapes=[pltpu.SemaphoreType.DMA(())],
        ),
        out_shape=jax.ShapeDtypeStruct((n_slots, w), rows.dtype),
        input_output_aliases={3: 0},
        compiler_params=pltpu.CompilerParams(dimension_semantics=("arbitrary",), vmem_limit_bytes=VMEM_LIMIT,
                                             disable_bounds_checks=True),
        name="moe_scatter",
    )(slot0, slot1, rows, jnp.zeros((n_slots, w), rows.dtype))


def _moe_experts_kernel(te_ref, na_ref, x_ref, wg_ref, wu_ref, wd_ref, o_ref):
    t = pl.program_id(0)

    @pl.when(t < na_ref[0])
    def _():
        xb = _token_rows(x_ref, MOE_TILE).astype(BF16)
        a = _mm(xb, wg_ref[0].astype(BF16))
        u = _mm(xb, wu_ref[0].astype(BF16))
        y = _mm((_silu(a) * u).astype(BF16), wd_ref[0].astype(BF16))
        for k in range(TOKEN_TILE_ROWS):
            o_ref[pl.ds(k, MOE_TILE, stride=TOKEN_TILE_ROWS), :] = y[:, k * 128:(k + 1) * 128]

    @pl.when(t >= na_ref[0])
    def _():
        o_ref[...] = jnp.zeros_like(o_ref)


def _moe_experts(tile_expert, n_active, sorted_x, w_gate, w_up, w_down):
    w = sorted_x.shape[1]
    n_slots = sorted_x.shape[0] // TOKEN_TILE_ROWS
    ne, d, de = w_gate.shape
    x_map = lambda t, te, na: (jnp.minimum(t, na[0] - 1), 0)
    w_map = lambda t, te, na: (te[t], 0, 0)
    tile_rows = MOE_TILE * TOKEN_TILE_ROWS
    return pl.pallas_call(
        _moe_experts_kernel,
        grid_spec=pltpu.PrefetchScalarGridSpec(
            num_scalar_prefetch=2,
            grid=(n_slots // MOE_TILE,),
            in_specs=[pl.BlockSpec((tile_rows, w), x_map), pl.BlockSpec((1, d, de), w_map),
                      pl.BlockSpec((1, d, de), w_map), pl.BlockSpec((1, de, d), w_map)],
            out_specs=pl.BlockSpec((tile_rows, w), lambda t, te, na: (t, 0)),
        ),
        out_shape=jax.ShapeDtypeStruct(sorted_x.shape, F32),
        compiler_params=_cparams("arbitrary"),
        name="moe_experts",
    )(tile_expert, n_active, sorted_x, w_gate, w_up, w_down)


def _moe_combine_kernel(s0_ref, s1_ref, y_hbm, route_ref, x1_ref, gate2_ref, gf_ref, o_ref, a0_ref, b0_ref, a1_ref,
                        b1_ref, sem):
    j = pl.program_id(0)
    tm = x1_ref.shape[0]
    bufs = ((a0_ref, b0_ref), (a1_ref, b1_ref))

    def issue(step, slot):
        a_ref, b_ref = bufs[slot]

        def body(r, carry):
            _row_copy(y_hbm, s0_ref[step * tm + r], a_ref, r, sem.at[slot]).start()
            _row_copy(y_hbm, s1_ref[step * tm + r], b_ref, r, sem.at[slot]).start()
            return carry
        lax.fori_loop(0, tm, body, 0, unroll=8)

    def finish(slot):
        a_ref, b_ref = bufs[slot]

        def wait(r, carry):
            _row_copy(y_hbm, 0, a_ref, 0, sem.at[slot]).wait()
            _row_copy(y_hbm, 0, b_ref, 0, sem.at[slot]).wait()
            return carry
        lax.fori_loop(0, tm, wait, 0, unroll=8)
        route = route_ref[...]
        moe = route[:, 2:3] * _token_rows(a_ref, tm) + route[:, 3:4] * _token_rows(b_ref, tm)
        o_ref[...] = _rms(x1_ref[...] + gate2_ref[0] * moe, gf_ref[...])

    @pl.when(j == 0)
    def _():
        issue(0, 0)

    for slot in range(2):
        @pl.when((j % 2 == slot) & (j + 1 < pl.num_programs(0)))
        def _():
            issue(j + 1, 1 - slot)

        @pl.when(j % 2 == slot)
        def _():
            finish(slot)


def _moe_combine(slot0, slot1, y_sorted, route, x1, gate2, final_g, *, tm, rows_per_mod):
    n, d = x1.shape
    tiles = pltpu.VMEM((tm * TOKEN_TILE_ROWS, 128), F32)
    return pl.pallas_call(
        _moe_combine_kernel,
        grid_spec=pltpu.PrefetchScalarGridSpec(
            num_scalar_prefetch=2,
            grid=(n // tm,),
            in_specs=[pl.BlockSpec(memory_space=pl.ANY), pl.BlockSpec((tm, 128), lambda j, a, b: (j, 0)),
                      pl.BlockSpec((tm, d), lambda j, a, b: (j, 0)),
                      pl.BlockSpec((1, 1, d), lambda j, a, b: (j * tm // rows_per_mod, 0, 0)),
                      pl.BlockSpec((1, d), lambda j, a, b: (0, 0))],
            out_specs=pl.BlockSpec((tm, d), lambda j, a, b: (j, 0)),
            scratch_shapes=[tiles, tiles, tiles, tiles, pltpu.SemaphoreType.DMA((2,))],
        ),
        out_shape=jax.ShapeDtypeStruct((n, d), F32),
        compiler_params=pltpu.CompilerParams(dimension_semantics=("arbitrary",), vmem_limit_bytes=VMEM_LIMIT,
                                             disable_bounds_checks=True),
        name="moe_combine",
    )(slot0, slot1, y_sorted, route, x1, gate2, final_g)


def _moe_sorted(route, counts, rows, x1, gate2, w_gate, w_up, w_down, final_g):
    b, s, d = x1.shape
    n = b * s
    ne = w_gate.shape[0]
    n_tiles = 2 * n // MOE_TILE + ne
    cnt = counts[0, :ne].astype(jnp.int32)
    padded = (cnt + MOE_TILE - 1) // MOE_TILE * MOE_TILE
    ends = jnp.sum(jnp.where(jnp.arange(ne)[:, None] <= jnp.arange(ne)[None, :], padded[:, None], 0), axis=0)
    base = jnp.pad((ends - padded).astype(F32), (0, 128 - ne)).reshape(1, 128)
    n_active = (ends[-1] // MOE_TILE).reshape(1)
    tile_start = jnp.arange(n_tiles, dtype=jnp.int32) * MOE_TILE
    tile_expert = jnp.minimum(jnp.sum((ends[None, :] <= tile_start[:, None]).astype(jnp.int32), axis=1), ne - 1)

    route2 = route.reshape(n, 128)
    slots = _moe_slots(route2, base, tm=512)
    slot0, slot1 = slots[:, 0], slots[:, 1]
    sorted_x = _moe_scatter(slot0, slot1, rows.reshape(n * TOKEN_TILE_ROWS, 128), n_tiles * MOE_TILE, tm=512)
    y_sorted = _moe_experts(tile_expert, n_active, sorted_x, w_gate, w_up, w_down)
    y = _moe_combine(slot0, slot1, y_sorted, route2, x1.reshape(n, d), gate2, final_g, tm=256, rows_per_mod=s)
    return y.reshape(b, s, d)


def _prep_w_in(w_in):
    o_gl = D_ATTN + 3 * ROW_W
    wqkv = w_in[:, :o_gl].astype(BF16)
    wgl = w_in[:, o_gl:o_gl + 3 * N_HEADS]
    pad = jnp.zeros((w_in.shape[0], 128 - 3 * N_HEADS), w_in.dtype)
    wgl = jnp.concatenate([wgl.reshape(-1, N_HEADS, 3).transpose(0, 2, 1).reshape(-1, 3 * N_HEADS), pad], axis=1)
    wu = w_in[:, o_gl + 3 * N_HEADS:].astype(BF16)
    return wqkv, wgl.astype(BF16), wu


def _prep_compress(w_cmp1, pos_cmp, w_cmp2):
    ratio = CMP_BLOCK // CMP_STRIDE
    eye = jnp.eye(KV_HEADS, dtype=w_cmp1.dtype)
    w1 = w_cmp1.reshape(2, ratio, CMP_STRIDE, HEAD_DIM, HEAD_DIM)
    w1big = jnp.einsum('crsdf,kj->cskdrjf', w1, eye).reshape(2, CMP_STRIDE * D_KV, ratio * D_KV).astype(BF16)
    w2big = jnp.einsum('cfd,kj->ckfjd', w_cmp2, eye).reshape(2, D_KV, D_KV).astype(BF16)
    pos = pos_cmp.reshape(2, ratio, CMP_STRIDE, 1, HEAD_DIM)
    pos = jnp.broadcast_to(pos, (2, ratio, CMP_STRIDE, KV_HEADS, HEAD_DIM)).reshape(2, ratio, CMP_STRIDE * D_KV)
    posrows = jnp.concatenate([pos, jnp.zeros((2, 8 - ratio, CMP_STRIDE * D_KV), pos.dtype)], axis=1).astype(BF16)
    return w1big, w2big, posrows


def _prep_router(w_group, b_group, w_router, b_router):
    d = w_group.shape[0]
    pad = 128 - N_EXPERTS - N_GROUPS
    w = jnp.concatenate([w_router, w_group, jnp.zeros((d, pad), w_group.dtype)], axis=1)
    b = jnp.concatenate([b_router, b_group, jnp.zeros((pad,), b_group.dtype)]).reshape(1, 128)
    return w, b


def kernel(x_prompt, x_sample, cache_cmp_kv, cache_slc_kv, state_win_kv, state_conv, page_table, c_prompt, c_sample,
           norm1_g, w_ada, b_ada, w_in, w_cmp1, pos_cmp, w_cmp2, w_dw, b_dw, conv_ln_g, conv_ln_b, g_attn_out,
           g_conv_out, w_out, norm2_g, w_group, b_group, w_router, b_router, w_gate, w_up, w_down, final_g):
    depth = norm1_g.shape[0]
    assert depth == 1 and x_sample.shape[1] == 1
    bp, seq, d = x_prompt.shape
    nb = x_sample.shape[0]
    n_pages = page_table.shape[1]
    past_len = n_pages * PAGE_SIZE
    n_past_slc = past_len // SLC_BLOCK
    win_buf = state_win_kv.shape[2]
    assert win_buf == WINDOW and seq % K_TILE == 0 and seq >= WINDOW + Q_TILE
    l = 0
    row2 = lambda a: a.reshape(1, -1)

    mods = _adaln(jnp.concatenate([c_prompt, c_sample], axis=0), w_ada[l], b_ada[l]).reshape(bp + nb, 6, d)
    mods_p = [mods[:bp, i][:, None, :] for i in range(6)]
    mods_s = [mods[bp:, i][None, :, :] for i in range(6)]

    wqkv, wgl, wu = _prep_w_in(w_in[l])
    w1big, w2big, posrows = _prep_compress(w_cmp1[l], pos_cmp[l], w_cmp2[l])
    w_route, b_route = _prep_router(w_group[l], b_group[l], w_router[l], b_router[l])
    w_out_b = w_out[l].astype(BF16)
    conv_args = (w_dw[l], row2(b_dw[l]), row2(conv_ln_g[l]), row2(conv_ln_b[l]))

    def kv_rows_from_t(a_t):
        n, _, t = a_t.shape
        return a_t.reshape(n, 2, KV_HEADS, HEAD_DIM, t).transpose(0, 4, 1, 2, 3)

    def kv_rows_to_t(a):
        return a.transpose(0, 2, 3, 4, 1)

    qt, kvc, kvct, kvst, kvwt, ks, vts, kw, vtw, gt, u = _mixer(
        x_prompt, mods_p[0], mods_p[1], row2(norm1_g[l]), wqkv, wgl, wu, tm=512, prompt=True)
    kc, vct = _compress_prompt(kvc, w1big, w2big, posrows)
    o_attn = _attn_prompt(qt, kc, vct, ks, vts, kw, vtw, gt)
    o_conv = _conv_prompt(u, *conv_args, tm=512)
    x1, route, moe_rows, counts = _finish(x_prompt, o_attn, o_conv, mods_p[2], mods_p[3], mods_p[4],
                                          row2(g_attn_out[l]), row2(g_conv_out[l]), w_out_b, row2(norm2_g[l]),
                                          w_route, b_route, tm=512, sorted_moe=True)
    y_prompt = _moe_sorted(route, counts, moe_rows, x1, mods_p[5], w_gate[l], w_up[l], w_down[l], row2(final_g))

    new_cmp_prompt = kv_rows_from_t(kvct)[None]
    new_slc_prompt = kv_rows_from_t(kvst)[None]
    new_win_prompt = kv_rows_from_t(kvwt[:, :, seq - WINDOW:])[None]
    new_conv_prompt = u[:, seq - (CONV_WIDTH - 1):][None]

    xs = x_sample.reshape(1, nb, d)
    q_s, kvct_s, kvst_s, kvwt_s, gt_s, u_s = _mixer(
        xs, mods_s[0], mods_s[1], row2(norm1_g[l]), wqkv, wgl, wu, tm=nb, prompt=False)
    cmp_t = kv_rows_to_t(cache_cmp_kv[l])
    q4 = q_s.reshape(nb, KV_HEADS, GQA, HEAD_DIM)
    zq = jnp.zeros_like(q4)
    kvh_id = jnp.arange(KV_HEADS).reshape(1, KV_HEADS, 1, 1)
    qpad = jnp.concatenate([jnp.where(kvh_id == 0, q4, zq), jnp.where(kvh_id == 1, q4, zq)],
                           axis=-1).reshape(nb, N_HEADS, D_KV)
    o_c, idx = _compress_sample(cmp_t.reshape(cmp_t.shape[0], ROW_W, PAGE_SIZE), page_table, qpad, w1big, w2big,
                                posrows, q_pos=past_len, n_slc=n_past_slc + 1)
    idx = idx[:, :KV_HEADS, :SLC_TOPN]
    win_t = kv_rows_to_t(state_win_kv[l]).reshape(nb, ROW_W, win_buf)
    o_heads, new_win_t = _attn_sample(idx, page_table, kv_rows_to_t(cache_slc_kv[l]), q_s.reshape(nb, N_HEADS, HEAD_DIM),
                                      o_c, gt_s[0], kvst_s[0], kvwt_s[0], win_t, n_past_slc=n_past_slc)
    o_attn_s = o_heads.reshape(1, nb, D_ATTN)
    state_t = state_conv[l].transpose(1, 0, 2)
    u_rows = u_s.reshape(nb, D_CONV)
    o_conv_s = _conv_sample(state_t, u_rows, *conv_args, sb=8).reshape(1, nb, D_CONV)
    x1_s, h2_s, comb_s = _finish(xs, o_attn_s, o_conv_s, mods_s[2], mods_s[3], mods_s[4], row2(g_attn_out[l]),
                                 row2(g_conv_out[l]), w_out_b, row2(norm2_g[l]), w_route, b_route, tm=nb,
                                 sorted_moe=False)
    y_sample = _moe(h2_s, comb_s, x1_s, mods_s[5], w_gate[l], w_up[l], w_down[l], row2(final_g), tm=nb)

    row_shape = (1, nb, 1, 2, KV_HEADS, HEAD_DIM)
    new_cmp_sample = kvct_s[0].T.reshape(row_shape)
    new_slc_sample = kvst_s[0].T.reshape(row_shape)
    new_win_sample = kv_rows_from_t(new_win_t)[None]
    new_conv_sample = jnp.concatenate([state_t[1:], u_rows[None]], axis=0).transpose(1, 0, 2)[None]

    return (y_prompt, y_sample.reshape(nb, 1, d), new_cmp_prompt, new_slc_prompt, new_win_prompt, new_conv_prompt,
            new_cmp_sample, new_slc_sample, new_win_sample, new_conv_sample)
```

```python
import functools

import jax
import jax.numpy as jnp
from jax import lax
from jax.experimental import pallas as pl
from jax.experimental.pallas import tpu as pltpu

F32 = jnp.float32
BF16 = jnp.bfloat16

D_MODEL = 1024
N_HEADS = 8
HEAD_DIM = 64
KV_HEADS = 2
GQA = N_HEADS // KV_HEADS
D_ATTN = N_HEADS * HEAD_DIM
D_CONV = D_MODEL - D_ATTN
D_KV = KV_HEADS * HEAD_DIM
CMP_BLOCK = 32
CMP_STRIDE = 16
SLC_BLOCK = 64
SLC_TOPN = 16
N_LOCAL_BLOCKS = 2
WINDOW = 512
FORCED_SCORE = 1e4
CONV_WIDTH = 31
N_GROUPS = 4
EPG = 8
N_EXPERTS = N_GROUPS * EPG
D_EXPERT = 256
PAGE_SIZE = 128
EPS = 1e-6
NEG_INF = -1e30
SCALE = HEAD_DIM ** -0.5
LOG2E = 1.4426950408889634
ROW_W = 2 * D_KV
CHUNK_W = CMP_STRIDE * ROW_W
CHUNK_PITCH = 20
MOE_TILE = 256
TOKEN_TILE_ROWS = D_MODEL // 128
CONV_ROWS = 32
Q_TILE = 128
K_TILE = 1024
VMEM_LIMIT = 48 * 1024 * 1024


def _cparams(*sem):
    return pltpu.CompilerParams(dimension_semantics=sem, vmem_limit_bytes=VMEM_LIMIT)


def _rms(x, g):
    return x * lax.rsqrt(jnp.mean(x * x, axis=-1, keepdims=True) + EPS) * g


def _silu(x):
    return x * jax.nn.sigmoid(x)


def _mm(a, b):
    return jnp.dot(a, b, preferred_element_type=F32)


def _mm_nt(a, b):
    return lax.dot_general(a, b, (((1,), (1,)), ((), ())), preferred_element_type=F32)


def _split3(x):
    hi = x.astype(BF16)
    r = x - hi.astype(F32)
    mid = r.astype(BF16)
    lo = (r - mid.astype(F32)).astype(BF16)
    return hi, mid, lo


def _softmax_masked(s, mask, axis):
    s = jnp.where(mask, s, NEG_INF)
    m = jnp.max(s, axis=axis, keepdims=True)
    e = jnp.exp(s - m)
    p = e / jnp.sum(e, axis=axis, keepdims=True)
    return jnp.where(mask, p, 0.0)


def _adaln_kernel(c_ref, w_ref, b_ref, o_ref):
    s = _silu(c_ref[...]).astype(BF16)
    o_ref[...] = _mm(s, w_ref[...].astype(BF16)) + b_ref[...]


def _adaln(c_all, w_ada, b_ada):
    n, d = c_all.shape
    nout = w_ada.shape[1]
    tn = 1024
    return pl.pallas_call(
        _adaln_kernel,
        grid=(nout // tn,),
        in_specs=[pl.BlockSpec((n, d), lambda j: (0, 0)),
                  pl.BlockSpec((d, tn), lambda j: (0, j)),
                  pl.BlockSpec((1, tn), lambda j: (0, j))],
        out_specs=pl.BlockSpec((n, tn), lambda j: (0, j)),
        out_shape=jax.ShapeDtypeStruct((n, nout), F32),
        compiler_params=_cparams("arbitrary"),
        name="adaln",
    )(c_all, w_ada, b_ada.reshape(1, nout))


def _mixer_kernel(x_ref, shift_ref, scale_ref, g_ref, wqkv_ref, wgl_ref, wu_ref, *outs, prompt):
    x = x_ref[0]
    h = _rms(x, g_ref[...]) * (1.0 + scale_ref[0]) + shift_ref[0]
    hb = h.astype(BF16)
    p = _mm(hb, wqkv_ref[...])
    gl = _mm(hb, wgl_ref[...])
    pu = _mm(hb, wu_ref[...])
    u = pu[:, :D_CONV] * jax.nn.sigmoid(pu[:, D_CONV:])
    q = p[:, :D_ATTN] * (SCALE * LOG2E if prompt else SCALE)
    o = D_ATTN
    kvc = p[:, o:o + ROW_W]
    kvs = p[:, o + ROW_W:o + 2 * ROW_W]
    kvw = p[:, o + 2 * ROW_W:o + 3 * ROW_W]
    kvs_t = kvs.T
    kvw_t = kvw.T
    if prompt:
        q_ref, kvc_ref, kvct_ref, kvst_ref, kvwt_ref, ks_ref, vts_ref, kw_ref, vtw_ref, gt_ref, u_ref = outs
        q_ref[0] = q.T.astype(BF16)
        kvc_ref[0] = kvc
        ks_ref[0] = kvs[:, :D_KV].astype(BF16)
        vts_ref[0] = kvs_t[D_KV:, :].astype(BF16)
        kw_ref[0] = kvw[:, :D_KV].astype(BF16)
        vtw_ref[0] = kvw_t[D_KV:, :].astype(BF16)
    else:
        q_ref, kvct_ref, kvst_ref, kvwt_ref, gt_ref, u_ref = outs
        q_ref[0] = q
    kvct_ref[0] = kvc.T
    kvst_ref[0] = kvs_t
    kvwt_ref[0] = kvw_t
    gt_ref[0] = jax.nn.sigmoid(gl).T[:32, :]
    u_ref[0] = u


def _mixer(x, shift, scale, g, wqkv, wgl, wu, *, tm, prompt):
    b, s, d = x.shape
    r = shift.shape[1]
    mod_block = (1, 1, d) if r == 1 else (1, tm, d)
    mod_map = (lambda i, j: (i, 0, 0)) if r == 1 else (lambda i, j: (i, j, 0))
    row = lambda w: pl.BlockSpec((1, tm, w), lambda i, j: (i, j, 0))
    col = lambda w: pl.BlockSpec((1, w, tm), lambda i, j: (i, 0, j))
    const = lambda a: pl.BlockSpec(a.shape, lambda i, j: (0,) * a.ndim)
    rows = lambda w, dt: jax.ShapeDtypeStruct((b, s, w), dt)
    cols = lambda w, dt: jax.ShapeDtypeStruct((b, w, s), dt)
    if prompt:
        out_specs = [col(D_ATTN), row(ROW_W), col(ROW_W), col(ROW_W), col(ROW_W), row(D_KV), col(D_KV), row(D_KV),
                     col(D_KV), col(32), row(D_CONV)]
        out_shape = [cols(D_ATTN, BF16), rows(ROW_W, F32), cols(ROW_W, F32), cols(ROW_W, F32), cols(ROW_W, F32),
                     rows(D_KV, BF16), cols(D_KV, BF16), rows(D_KV, BF16), cols(D_KV, BF16), cols(32, F32),
                     rows(D_CONV, F32)]
    else:
        out_specs = [row(D_ATTN), col(ROW_W), col(ROW_W), col(ROW_W), col(32), row(D_CONV)]
        out_shape = [rows(D_ATTN, F32), cols(ROW_W, F32), cols(ROW_W, F32), cols(ROW_W, F32), cols(32, F32),
                     rows(D_CONV, F32)]
    return pl.pallas_call(
        functools.partial(_mixer_kernel, prompt=prompt),
        grid=(b, s // tm),
        in_specs=[row(d), pl.BlockSpec(mod_block, mod_map), pl.BlockSpec(mod_block, mod_map),
                  const(g), const(wqkv), const(wgl), const(wu)],
        out_specs=out_specs,
        out_shape=out_shape,
        compiler_params=_cparams("arbitrary", "arbitrary"),
        name="mixer_prompt" if prompt else "mixer_sample",
    )(x, shift, scale, g, wqkv, wgl, wu)


def _chunk_part(load_offset, pos_rows, w1_c):
    xc = jnp.concatenate([load_offset(s) for s in range(CMP_STRIDE)], axis=1).astype(BF16)
    n = xc.shape[0]
    both = _mm(jnp.concatenate([xc, pos_rows], axis=0), w1_c)
    return both[:n], both[n:]


def _compress_finish(part, posb, w2_c):
    n = part.shape[0]
    nxt = pltpu.roll(part[:, D_KV:], n - 1, 0)
    pre = part[:, :D_KV] + nxt + posb[0:1, :D_KV] + posb[1:2, D_KV:]
    return _mm(_silu(pre).astype(BF16), w2_c)


def _compress_prompt_kernel(x_ref, w1_ref, w2_ref, pos_ref, kc_ref, vct_ref):
    for c in range(2):
        part, posb = _chunk_part(lambda s: x_ref[0, :, s * ROW_W + c * D_KV:s * ROW_W + (c + 1) * D_KV], pos_ref[c],
                                 w1_ref[c])
        out = _compress_finish(part, posb, w2_ref[c])
        if c == 0:
            kc_ref[0] = out.astype(BF16)
        else:
            vct_ref[0] = out.T.astype(BF16)


def _compress_prompt(kvc, w1big, w2big, posrows):
    b, s, _ = kvc.shape
    n = s // CMP_STRIDE
    x = kvc.reshape(b, n, CHUNK_W)
    const = lambda a: pl.BlockSpec(a.shape, lambda i: (0,) * a.ndim)
    return pl.pallas_call(
        _compress_prompt_kernel,
        grid=(b,),
        in_specs=[pl.BlockSpec((1, n, CHUNK_W), lambda i: (i, 0, 0)), const(w1big), const(w2big), const(posrows)],
        out_specs=[pl.BlockSpec((1, n, D_KV), lambda i: (i, 0, 0)), pl.BlockSpec((1, D_KV, n), lambda i: (i, 0, 0))],
        out_shape=[jax.ShapeDtypeStruct((b, n, D_KV), BF16), jax.ShapeDtypeStruct((b, D_KV, n), BF16)],
        compiler_params=_cparams("arbitrary"),
        name="compress_prompt",
    )(x, w1big, w2big, posrows)


def _compress_sample_kernel(pt_ref, cache_ref, q_ref, w1_ref, w2_ref, pos_ref, oc_ref, idx_ref, pages_ref, rows0_ref,
                            rows1_ref, sem, *, pages, q_pos, n_slc):
    i = pl.program_id(0)
    row_refs = (rows0_ref, rows1_ref)
    cpp = PAGE_SIZE // CMP_STRIDE
    n = pages * cpp

    def page_copy(seq, k, buf):
        return pltpu.make_async_copy(cache_ref.at[pt_ref[seq * pages + k]], pages_ref.at[buf, k], sem.at[buf])

    def fetch(seq, buf):
        for k in range(pages):
            page_copy(seq, k, buf).start()

    @pl.when(i == 0)
    def _():
        fetch(0, 0)

    @pl.when(i + 1 < pl.num_programs(0))
    def _():
        fetch(i + 1, (i + 1) % 2)

    buf = i % 2
    for k in range(pages):
        page_copy(i, k, buf).wait()

    kv = []
    for c in range(2):
        for k in range(pages):
            rows = pages_ref[buf, k, c * D_KV:(c + 1) * D_KV, :].astype(BF16).T.astype(F32)
            for j in range(cpp):
                r0 = (k * cpp + j) * CHUNK_PITCH
                row_refs[c][r0:r0 + CMP_STRIDE, :] = rows[j * CMP_STRIDE:(j + 1) * CMP_STRIDE, :]
    for c in range(2):
        part, posb = _chunk_part(lambda s: row_refs[c][pl.ds(s, n, stride=CHUNK_PITCH), :], pos_ref[c], w1_ref[c])
        kv.append(_compress_finish(part, posb, w2_ref[c]).astype(BF16))
    _sample_cmp_attend(q_ref[0].astype(BF16), kv[0], kv[1], oc_ref, idx_ref, q_pos=q_pos, n_slc=n_slc)


def _compress_sample(cache_t, page_table, qpad, w1big, w2big, posrows, *, q_pos, n_slc):
    nb, n_pages = page_table.shape
    n = n_pages * (PAGE_SIZE // CMP_STRIDE)
    const = lambda a: pl.BlockSpec(a.shape, lambda i, pt: (0,) * a.ndim)
    per_b = lambda w: pl.BlockSpec((1, N_HEADS, w), lambda i, pt: (i, 0, 0))
    return pl.pallas_call(
        functools.partial(_compress_sample_kernel, pages=n_pages, q_pos=q_pos, n_slc=n_slc),
        grid_spec=pltpu.PrefetchScalarGridSpec(
            num_scalar_prefetch=1,
            grid=(nb,),
            in_specs=[pl.BlockSpec(memory_space=pl.ANY), per_b(D_KV), const(w1big), const(w2big), const(posrows)],
            out_specs=[per_b(D_KV), per_b(128)],
            scratch_shapes=[pltpu.VMEM((2, n_pages, ROW_W, PAGE_SIZE), F32),
                            pltpu.VMEM((n * CHUNK_PITCH, D_KV), F32), pltpu.VMEM((n * CHUNK_PITCH, D_KV), F32),
                            pltpu.SemaphoreType.DMA((2,))],
        ),
        out_shape=[jax.ShapeDtypeStruct((nb, N_HEADS, D_KV), F32), jax.ShapeDtypeStruct((nb, N_HEADS, 128), jnp.int32)],
        compiler_params=pltpu.CompilerParams(dimension_semantics=("arbitrary",), vmem_limit_bytes=VMEM_LIMIT,
                                             disable_bounds_checks=True),
        name="compress_sample",
    )(page_table.reshape(-1), cache_t, qpad, w1big, w2big, posrows)


def _overlap(cmp_idx, slc_idx):
    lo = cmp_idx * CMP_STRIDE
    so = slc_idx * SLC_BLOCK
    return (lo <= so + SLC_BLOCK - 1) & (lo + CMP_BLOCK - 1 >= so)


def _forced_importance(imp, blk, cur, n_slc):
    valid = (blk <= cur) & (blk < n_slc)
    forced = (blk == 0) | ((cur - blk >= 0) & (cur - blk < N_LOCAL_BLOCKS))
    return jnp.where(valid & forced, FORCED_SCORE, jnp.where(valid, imp, -1.0))


def _attn_prompt_kernel(qt_ref, kc_ref, vct_ref, ks_ref, vts_ref, kw_ref, vtw_ref, gt_ref, o_ref, sel_ref):
    i = pl.program_id(1)
    seq = ks_ref.shape[1]
    nbp = kc_ref.shape[1]
    n_slc = seq // SLC_BLOCK
    lanes = GQA * Q_TILE
    lane = lax.broadcasted_iota(jnp.int32, (1, lanes), 1)
    qpos = i * Q_TILE + (lane & (Q_TILE - 1))
    qpos_q = qpos[:, :Q_TILE]
    qt = qt_ref[0]
    gt = gt_ref[0]
    win_keys = min(WINDOW + Q_TILE, seq)
    blocks_per_tile = K_TILE // SLC_BLOCK

    ovl = _overlap(lax.broadcasted_iota(jnp.int32, (n_slc, nbp), 1), lax.broadcasted_iota(jnp.int32, (n_slc, nbp), 0))
    ovl = jnp.where(ovl, 1.0, 0.0).astype(BF16)
    blk = lax.broadcasted_iota(jnp.int32, (n_slc, Q_TILE), 0)

    cpos = lax.broadcasted_iota(jnp.int32, (nbp, lanes), 0) * CMP_STRIDE + (CMP_BLOCK - 1)
    cmp_bias = jnp.where(cpos <= qpos, 0.0, NEG_INF)
    any_cmp = jnp.where(qpos >= CMP_BLOCK - 1, 1.0, 0.0)
    w0 = pl.multiple_of(jnp.clip(i * Q_TILE - WINDOW, 0, seq - win_keys), Q_TILE)
    wpos = w0 + lax.broadcasted_iota(jnp.int32, (win_keys, lanes), 0)
    win_bias = jnp.where(wpos <= qpos, jnp.where(wpos >= qpos - WINDOW, 0.0, NEG_INF), NEG_INF)
    n_steps = ((i + 1) * Q_TILE + K_TILE - 1) // K_TILE
    last0 = pl.multiple_of((n_steps - 1) * K_TILE, K_TILE)
    causal_bias = jnp.where(last0 + lax.broadcasted_iota(jnp.int32, (K_TILE, lanes), 0) <= qpos, 0.0, NEG_INF)

    heads = range(KV_HEADS)
    hd = [slice(kvh * HEAD_DIM, (kvh + 1) * HEAD_DIM) for kvh in heads]
    qpad, o_c = [], []
    for kvh in heads:
        qk = jnp.concatenate([qt[(kvh * GQA + g) * HEAD_DIM:(kvh * GQA + g + 1) * HEAD_DIM, :] for g in range(GQA)],
                             axis=1)
        zero = jnp.zeros_like(qk)
        qpad.append(jnp.concatenate([qk, zero] if kvh == 0 else [zero, qk], axis=0))

        s = _mm(kc_ref[0], qpad[kvh]) + cmp_bias
        e = jnp.exp2(s - jnp.max(s, axis=0, keepdims=True))
        p = e * (any_cmp / jnp.sum(e, axis=0, keepdims=True))
        o_c.append(_mm(vct_ref[0, hd[kvh], :], p.astype(BF16)))

        psum = p[:, 0:Q_TILE]
        for g in range(1, GQA):
            psum = psum + p[:, g * Q_TILE:(g + 1) * Q_TILE]
        imp = sum(_mm(ovl, t) for t in _split3(psum))
        impf = _forced_importance(imp, blk, qpos_q // SLC_BLOCK, n_slc)
        groups = [impf[r:r + 8, :] for r in range(0, n_slc, 8)]
        ranks = [jnp.zeros_like(grp) for grp in groups]
        for k in range(n_slc):
            rk = impf[k:k + 1, :]
            for gi, grp in enumerate(groups):
                if gi * 8 > k:
                    ranks[gi] = ranks[gi] + jnp.where(rk >= grp, 1.0, 0.0)
                elif gi * 8 + 7 < k:
                    ranks[gi] = ranks[gi] + jnp.where(rk > grp, 1.0, 0.0)
                else:
                    later = blk[0:8, :] + gi * 8 > k
                    ranks[gi] = ranks[gi] + jnp.where(later, jnp.where(rk >= grp, 1.0, 0.0),
                                                      jnp.where(rk > grp, 1.0, 0.0))
        rank = jnp.concatenate(ranks, axis=0)
        sel = jnp.where(rank < SLC_TOPN, jnp.where(impf >= 0.0, 0.0, NEG_INF), NEG_INF)
        sel_ref[kvh] = jnp.concatenate([sel] * GQA, axis=1)

    def slc_step(t, carry, extra_bias=None):
        k0 = pl.multiple_of(t * K_TILE, K_TILE)
        keys = ks_ref[0, pl.ds(k0, K_TILE), :]
        out = []
        for kvh in heads:
            m, l, acc = carry[kvh]
            sk = _mm(keys, qpad[kvh])
            sk = jnp.concatenate(
                [sk[j * SLC_BLOCK:(j + 1) * SLC_BLOCK, :] + sel_ref[kvh, pl.ds(t * blocks_per_tile + j, 1), :]
                 for j in range(blocks_per_tile)], axis=0)
            if extra_bias is not None:
                sk = sk + extra_bias
            m_new = jnp.maximum(m, jnp.max(sk, axis=0, keepdims=True))
            alpha = jnp.exp2(m - m_new)
            e = jnp.exp2(sk - m_new)
            l_new = alpha * l + jnp.sum(e, axis=0, keepdims=True)
            pv = _mm(vts_ref[0, hd[kvh], pl.ds(k0, K_TILE)], e.astype(BF16))
            out.append((m_new, l_new, alpha * acc + pv))
        return tuple(out)

    init = (jnp.full((1, lanes), NEG_INF, F32), jnp.zeros((1, lanes), F32), jnp.zeros((HEAD_DIM, lanes), F32))
    carry = lax.fori_loop(0, n_steps - 1, slc_step, (init,) * KV_HEADS)
    carry = slc_step(n_steps - 1, carry, causal_bias)

    for kvh in heads:
        _, l_s, acc_s = carry[kvh]
        o_s = acc_s * (1.0 / l_s)

        sw = _mm(kw_ref[0, pl.ds(w0, win_keys), :], qpad[kvh]) + win_bias
        ew = jnp.exp2(sw - jnp.max(sw, axis=0, keepdims=True))
        o_w = _mm(vtw_ref[0, hd[kvh], pl.ds(w0, win_keys)], ew.astype(BF16)) * (
            1.0 / jnp.sum(ew, axis=0, keepdims=True))

        def gate(r):
            return jnp.concatenate([gt[r * N_HEADS + kvh * GQA + g:r * N_HEADS + kvh * GQA + g + 1, :]
                                    for g in range(GQA)], axis=1)
        o_t = gate(0) * o_c[kvh] + gate(1) * o_s + gate(2) * o_w
        for pair in range(GQA // 2):
            two = jnp.concatenate([o_t[:, (2 * pair) * Q_TILE:(2 * pair + 1) * Q_TILE],
                                   o_t[:, (2 * pair + 1) * Q_TILE:(2 * pair + 2) * Q_TILE]], axis=0)
            c0 = kvh * GQA * HEAD_DIM + pair * 2 * HEAD_DIM
            o_ref[0, :, c0:c0 + 2 * HEAD_DIM] = two.T


def _attn_prompt(qt, kc, vct, ks, vts, kw, vtw, gt):
    b, _, s = qt.shape
    nbp = kc.shape[1]
    per_b = lambda shape: pl.BlockSpec((1,) + shape, lambda i, j: (i, 0, 0))
    return pl.pallas_call(
        _attn_prompt_kernel,
        grid=(b, s // Q_TILE),
        in_specs=[pl.BlockSpec((1, D_ATTN, Q_TILE), lambda i, j: (i, 0, j)),
                  per_b((nbp, D_KV)), per_b((D_KV, nbp)),
                  per_b((s, D_KV)), per_b((D_KV, s)), per_b((s, D_KV)), per_b((D_KV, s)),
                  pl.BlockSpec((1, 32, Q_TILE), lambda i, j: (i, 0, j))],
        out_specs=pl.BlockSpec((1, Q_TILE, D_ATTN), lambda i, j: (i, j, 0)),
        out_shape=jax.ShapeDtypeStruct((b, s, D_ATTN), F32),
        scratch_shapes=[pltpu.VMEM((KV_HEADS, s // SLC_BLOCK, GQA * Q_TILE), F32)],
        compiler_params=_cparams("arbitrary", "arbitrary"),
        name="attn_prompt",
    )(qt, kc, vct, ks, vts, kw, vtw, gt)


def _sample_cmp_attend(q, kc, vc, oc_ref, idx_ref, *, q_pos, n_slc):
    nb = kc.shape[0]
    s = _mm_nt(q, kc)
    cpos = lax.broadcasted_iota(jnp.int32, s.shape, 1) * CMP_STRIDE + (CMP_BLOCK - 1)
    p = _softmax_masked(s, cpos <= q_pos, 1)
    oc_ref[0] = _mm(p.astype(BF16), vc)

    nsp = idx_ref.shape[2] * 2
    group_sums = [jnp.sum(p[k * GQA:(k + 1) * GQA, :], axis=0, keepdims=True) for k in range(KV_HEADS)]
    psum = jnp.concatenate(group_sums + [jnp.zeros((N_HEADS - KV_HEADS, nb), F32)], axis=0)
    ovl = _overlap(lax.broadcasted_iota(jnp.int32, (nb, nsp), 0), lax.broadcasted_iota(jnp.int32, (nb, nsp), 1))
    ovl = jnp.where(ovl, 1.0, 0.0).astype(BF16)
    imp = sum(_mm(t, ovl) for t in _split3(psum))
    blk = lax.broadcasted_iota(jnp.int32, imp.shape, 1)
    impf = _forced_importance(imp, blk, q_pos // SLC_BLOCK, n_slc)
    eye = jnp.where(lax.broadcasted_iota(jnp.int32, (nsp, nsp), 0) == lax.broadcasted_iota(jnp.int32, (nsp, nsp), 1),
                    1.0, 0.0).astype(BF16)
    imp_t = sum(_mm_nt(eye, t) for t in _split3(impf))
    k_idx = lax.broadcasted_iota(jnp.int32, (nsp, nsp), 0)
    j_idx = lax.broadcasted_iota(jnp.int32, (nsp, nsp), 1)
    ranks = []
    for r in range(KV_HEADS):
        col, row = imp_t[:, r:r + 1], impf[r:r + 1, :]
        beats = jnp.where(col > row, 1.0, jnp.where(col == row, jnp.where(k_idx < j_idx, 1.0, 0.0), 0.0))
        ranks.append(jnp.sum(beats, axis=0, keepdims=True))
    rank = jnp.concatenate(ranks + [jnp.full((N_HEADS - KV_HEADS, nsp), float(nsp), F32)], axis=0)
    blk_f = blk.astype(F32)
    slot = lax.broadcasted_iota(jnp.int32, (N_HEADS, idx_ref.shape[2]), 1)
    idx = jnp.zeros((N_HEADS, idx_ref.shape[2]), F32)
    for t in range(SLC_TOPN):
        chosen = jnp.sum(jnp.where(rank == float(t), blk_f, 0.0), axis=1, keepdims=True)
        idx = idx + jnp.where(slot == t, chosen, 0.0)
    idx_ref[0] = idx.astype(jnp.int32)


def _attn_sample_kernel(idx_ref, pt_ref, cache_ref, q_ref, oc_ref, gt_ref, kvst_ref, kvwt_ref, win_ref, o_ref, nwin_ref,
                        blk_ref, sem, *, n_past_slc, n_pages):
    n_sel = KV_HEADS * SLC_TOPN
    b = pl.program_id(0)
    nb = kvst_ref.shape[1]
    sub = PAGE_SIZE // SLC_BLOCK

    def block_copy(seq, s, buf):
        j = jnp.minimum(idx_ref[seq * n_sel + s], n_past_slc - 1)
        page = pt_ref[seq * n_pages + j // sub]
        return pltpu.make_async_copy(cache_ref.at[page, :, s // SLC_TOPN], blk_ref.at[buf, s], sem.at[buf])

    def fetch(seq, buf):
        for s in range(n_sel):
            block_copy(seq, s, buf).start()

    @pl.when(b == 0)
    def _():
        fetch(0, 0)

    @pl.when(b + 1 < pl.num_programs(0))
    def _():
        fetch(b + 1, (b + 1) % 2)

    buf = b % 2
    for s in range(n_sel):
        block_copy(b, s, buf).wait()

    q = q_ref[0].astype(BF16)
    head_kvh = lax.broadcasted_iota(jnp.int32, (N_HEADS, 1), 0) // GQA
    mine = lax.broadcasted_iota(jnp.int32, (1, nb), 1) == b
    lane_half = lax.broadcasted_iota(jnp.int32, (1, PAGE_SIZE), 1) // SLC_BLOCK

    def attend(kt, vt, valid):
        s = jnp.where(valid, _mm(q, kt), NEG_INF)
        e = jnp.where(valid, jnp.exp(s - jnp.max(s, axis=1, keepdims=True)), 0.0)
        return _mm_nt(e.astype(BF16), vt) / jnp.sum(e, axis=1, keepdims=True)

    kvst = kvst_ref[...]
    kvwt = kvwt_ref[...]
    o_s = jnp.zeros((N_HEADS, HEAD_DIM), F32)
    o_w = jnp.zeros((N_HEADS, HEAD_DIM), F32)
    win = win_ref[0]
    for kvh in range(KV_HEADS):
        k_rows = slice(kvh * HEAD_DIM, (kvh + 1) * HEAD_DIM)
        v_rows = slice(D_KV + kvh * HEAD_DIM, D_KV + (kvh + 1) * HEAD_DIM)
        valid = []
        n_new = jnp.int32(0)
        for t in range(SLC_TOPN):
            j = idx_ref[(b * KV_HEADS + kvh) * SLC_TOPN + t]
            past = j < n_past_slc
            half = jnp.minimum(j, n_past_slc - 1) % (PAGE_SIZE // SLC_BLOCK)
            valid.append((lane_half == half) & past)
            n_new = n_new + jnp.where(past, 0, 1)
        valid.append(mine & (n_new > 0))
        kt = jnp.concatenate([blk_ref[buf, kvh * SLC_TOPN + t, 0] for t in range(SLC_TOPN)] + [kvst[k_rows, :]],
                             axis=1)
        vt = jnp.concatenate([blk_ref[buf, kvh * SLC_TOPN + t, 1] for t in range(SLC_TOPN)] + [kvst[v_rows, :]],
                             axis=1)
        o_k = attend(kt.astype(BF16), vt.astype(BF16), jnp.concatenate(valid, axis=1))
        o_s = jnp.where(head_kvh == kvh, o_k, o_s)
        kt = jnp.concatenate([win[k_rows, :], kvwt[k_rows, :]], axis=1)
        vt = jnp.concatenate([win[v_rows, :], kvwt[v_rows, :]], axis=1)
        valid_w = jnp.concatenate([jnp.full((1, win.shape[1]), True), mine], axis=1)
        o_k = attend(kt.astype(BF16), vt.astype(BF16), valid_w)
        o_w = jnp.where(head_kvh == kvh, o_k, o_w)

    oc = oc_ref[0]
    o_c = jnp.where(head_kvh == 0, oc[:, :HEAD_DIM], oc[:, HEAD_DIM:])
    gates = jnp.sum(jnp.where(mine, gt_ref[...], 0.0), axis=1, keepdims=True)
    o_ref[0] = (gates[0:N_HEADS] * o_c + gates[N_HEADS:2 * N_HEADS] * o_s + gates[2 * N_HEADS:3 * N_HEADS] * o_w)

    new_col = jnp.sum(jnp.where(mine, kvwt, 0.0), axis=1, keepdims=True)
    last = lax.broadcasted_iota(jnp.int32, win.shape, 1) == win.shape[1] - 1
    nwin_ref[0] = jnp.where(last, new_col, pltpu.roll(win, win.shape[1] - 1, 1))


def _attn_sample(idx, page_table, cache_t, q, o_c, gt, kvst, kvwt, win_t, *, n_past_slc):
    nb = q.shape[0]
    win_buf = win_t.shape[2]
    n_pages = page_table.shape[1]
    per_b = lambda shape: pl.BlockSpec((1,) + shape, lambda i, a, c: (i,) + (0,) * len(shape))
    const = lambda a: pl.BlockSpec(a.shape, lambda i, x, c: (0,) * a.ndim)
    n_sel = KV_HEADS * SLC_TOPN
    return pl.pallas_call(
        functools.partial(_attn_sample_kernel, n_past_slc=n_past_slc, n_pages=n_pages),
        grid_spec=pltpu.PrefetchScalarGridSpec(
            num_scalar_prefetch=2,
            grid=(nb,),
            in_specs=[pl.BlockSpec(memory_space=pl.ANY), per_b((N_HEADS, HEAD_DIM)), per_b((N_HEADS, D_KV)),
                      const(gt), const(kvst), const(kvwt), per_b((ROW_W, win_buf))],
            out_specs=[per_b((N_HEADS, HEAD_DIM)), per_b((ROW_W, win_buf))],
            scratch_shapes=[pltpu.VMEM((2, n_sel, 2, HEAD_DIM, PAGE_SIZE), F32), pltpu.SemaphoreType.DMA((2,))],
        ),
        out_shape=[jax.ShapeDtypeStruct((nb, N_HEADS, HEAD_DIM), F32), jax.ShapeDtypeStruct(win_t.shape, F32)],
        compiler_params=pltpu.CompilerParams(dimension_semantics=("arbitrary",), vmem_limit_bytes=VMEM_LIMIT,
                                             disable_bounds_checks=True),
        name="attn_sample",
    )(idx.reshape(-1), page_table.reshape(-1), cache_t, q, o_c, gt, kvst, kvwt, win_t)


def _conv_tail(y, b_ref, lg_ref, lb_ref):
    y = y + b_ref[...]
    yc = y - jnp.mean(y, axis=-1, keepdims=True)
    yn = yc * lax.rsqrt(jnp.mean(yc * yc, axis=-1, keepdims=True) + EPS)
    return _silu(yn * lg_ref[...] + lb_ref[...])


def _conv_prompt_kernel(u_ref, halo_ref, w_ref, b_ref, lg_ref, lb_ref, o_ref, buf_ref, shift_ref, *, halo):
    tm = u_ref.shape[1]
    j = pl.program_id(1)
    buf_ref[0:halo, :] = jnp.where(j > 0, halo_ref[0], 0.0)
    buf_ref[halo:halo + tm, :] = u_ref[0]
    buf_ref[halo + tm:, :] = jnp.zeros((8, buf_ref.shape[1]), F32)
    lead = halo - (CONV_WIDTH - 1)
    for ph in range(8):
        shift_ref[ph] = buf_ref[ph:ph + tm + halo, :]

    def chunk(c, carry):
        r0 = pl.multiple_of(c * CONV_ROWS, CONV_ROWS)
        y = None
        for w in range(CONV_WIDTH):
            o = lead + w
            tap = shift_ref[o % 8, pl.ds(r0 + o // 8 * 8, CONV_ROWS), :] * w_ref[w:w + 1, :]
            y = tap if y is None else y + tap
        o_ref[0, pl.ds(r0, CONV_ROWS), :] = _conv_tail(y, b_ref, lg_ref, lb_ref)
        return carry

    lax.fori_loop(0, tm // CONV_ROWS, chunk, 0, unroll=4)


def _conv_prompt(u, w_dw, b_dw, ln_g, ln_b, *, tm):
    b, s, dc = u.shape
    halo = 32
    per = tm // halo
    const = lambda a: pl.BlockSpec(a.shape, lambda i, j: (0,) * a.ndim)
    return pl.pallas_call(
        functools.partial(_conv_prompt_kernel, halo=halo),
        grid=(b, s // tm),
        in_specs=[pl.BlockSpec((1, tm, dc), lambda i, j: (i, j, 0)),
                  pl.BlockSpec((1, halo, dc), lambda i, j: (i, jnp.maximum(j * per - 1, 0), 0)),
                  const(w_dw), const(b_dw), const(ln_g), const(ln_b)],
        out_specs=pl.BlockSpec((1, tm, dc), lambda i, j: (i, j, 0)),
        out_shape=jax.ShapeDtypeStruct((b, s, dc), F32),
        scratch_shapes=[pltpu.VMEM((tm + halo + 8, dc), F32), pltpu.VMEM((8, tm + halo, dc), F32)],
        compiler_params=_cparams("arbitrary", "arbitrary"),
        name="conv_prompt",
    )(u, u, w_dw, b_dw, ln_g, ln_b)


def _conv_sample_kernel(st_ref, u_ref, w_ref, b_ref, lg_ref, lb_ref, o_ref):
    hist = CONV_WIDTH - 1
    y = u_ref[...] * w_ref[hist:hist + 1, :]
    for w in range(hist):
        y = y + st_ref[w] * w_ref[w:w + 1, :]
    o_ref[...] = _conv_tail(y, b_ref, lg_ref, lb_ref)


def _conv_sample(state_t, u, w_dw, b_dw, ln_g, ln_b, *, sb):
    nb, dc = u.shape
    hist = state_t.shape[0]
    const = lambda a: pl.BlockSpec(a.shape, lambda i: (0,) * a.ndim)
    return pl.pallas_call(
        _conv_sample_kernel,
        grid=(nb // sb,),
        in_specs=[pl.BlockSpec((hist, sb, dc), lambda i: (0, i, 0)), pl.BlockSpec((sb, dc), lambda i: (i, 0)),
                  const(w_dw), const(b_dw), const(ln_g), const(ln_b)],
        out_specs=pl.BlockSpec((sb, dc), lambda i: (i, 0)),
        out_shape=jax.ShapeDtypeStruct((nb, dc), F32),
        compiler_params=_cparams("arbitrary"),
        name="conv_sample",
    )(state_t, u, w_dw, b_dw, ln_g, ln_b)


def _finish_kernel(x_ref, oa_ref, ocv_ref, gate1_ref, shift2_ref, scale2_ref, ga_ref, gc_ref, wout_ref, g2_ref,
                   wr_ref, br_ref, x1_ref, *outs, sorted_moe):
    mixed = jnp.concatenate([_rms(oa_ref[0], ga_ref[...]), _rms(ocv_ref[0], gc_ref[...])], axis=1)
    mix = _mm(mixed.astype(BF16), wout_ref[...])
    x1 = x_ref[0] + gate1_ref[0] * mix
    x1_ref[0] = x1
    h2 = _rms(x1, g2_ref[...]) * (1.0 + scale2_ref[0]) + shift2_ref[0]

    h_hi, h_mid, _ = _split3(h2)
    w_hi, w_mid, _ = _split3(wr_ref[...])
    lg = (_mm(h_hi, w_hi) + _mm(h_hi, w_mid)
          + _mm(h_mid, w_hi)) + br_ref[...]
    lane = lax.broadcasted_iota(jnp.int32, lg.shape, 1)
    lane_f = lane.astype(F32)
    is_group = (lane >= N_EXPERTS) & (lane < N_EXPERTS + N_GROUPS)
    gl = jnp.where(is_group, lg, NEG_INF)
    g_max = jnp.max(gl, axis=1, keepdims=True)
    p_top = 1.0 / jnp.sum(jnp.where(is_group, jnp.exp(gl - g_max), 0.0), axis=1, keepdims=True)
    g_lane = jnp.min(jnp.where(gl == g_max, lane_f, 1e9), axis=1, keepdims=True)
    in_group = (lane < N_EXPERTS) & ((lane // EPG).astype(F32) == g_lane - N_EXPERTS)
    el = jnp.where(in_group, lg, NEG_INF)
    l1 = jnp.max(el, axis=1, keepdims=True)
    i1 = jnp.min(jnp.where(el == l1, lane_f, 1e9), axis=1, keepdims=True)
    el2 = jnp.where(lane_f == i1, NEG_INF, el)
    l2 = jnp.max(el2, axis=1, keepdims=True)
    i2 = jnp.min(jnp.where(el2 == l2, lane_f, 1e9), axis=1, keepdims=True)
    r = jnp.exp(l2 - l1)
    w1 = p_top / (1.0 + r)
    w2 = p_top * r / (1.0 + r)
    if not sorted_moe:
        h2_ref, comb_ref = outs
        h2_ref[0] = h2.astype(BF16)
        comb_ref[0] = jnp.where(lane_f == i1, w1, jnp.where(lane_f == i2, w2, 0.0))
        return
    route_ref, rows_ref, count_ref = outs
    for k in range(TOKEN_TILE_ROWS):
        rows_ref[0, pl.ds(k, h2.shape[0], stride=TOKEN_TILE_ROWS), :] = h2[:, k * 128:(k + 1) * 128]
    route_ref[0] = jnp.where(lane == 0, i1, jnp.where(lane == 1, i2, jnp.where(lane == 2, w1, jnp.where(
        lane == 3, w2, 0.0))))

    @pl.when((pl.program_id(0) == 0) & (pl.program_id(1) == 0))
    def _():
        count_ref[...] = jnp.zeros_like(count_ref)

    picks = jnp.where(lane_f == i1, 1.0, 0.0) + jnp.where(lane_f == i2, 1.0, 0.0)
    count_ref[...] += jnp.sum(picks, axis=0, keepdims=True)


def _finish(x, o_attn, o_conv, gate1, shift2, scale2, ga, gc, w_out, g2, w_route, b_route, *, tm, sorted_moe):
    b, s, d = x.shape
    r = gate1.shape[1]
    mod_block = (1, 1, d) if r == 1 else (1, tm, d)
    mod_map = (lambda i, j: (i, 0, 0)) if r == 1 else (lambda i, j: (i, j, 0))
    mod = pl.BlockSpec(mod_block, mod_map)
    row = lambda w: pl.BlockSpec((1, tm, w), lambda i, j: (i, j, 0))
    const = lambda a: pl.BlockSpec(a.shape, lambda i, j: (0,) * a.ndim)
    shape = lambda w, dt: jax.ShapeDtypeStruct((b, s, w), dt)
    if sorted_moe:
        tiles = pl.BlockSpec((1, tm * TOKEN_TILE_ROWS, 128), lambda i, j: (i, j, 0))
        out_specs = [row(d), row(128), tiles, pl.BlockSpec((8, 128), lambda i, j: (0, 0))]
        out_shape = [shape(d, F32), shape(128, F32), jax.ShapeDtypeStruct((b, s * TOKEN_TILE_ROWS, 128), F32),
                     jax.ShapeDtypeStruct((8, 128), F32)]
    else:
        out_specs = [row(d), row(d), row(128)]
        out_shape = [shape(d, F32), shape(d, BF16), shape(128, F32)]
    return pl.pallas_call(
        functools.partial(_finish_kernel, sorted_moe=sorted_moe),
        grid=(b, s // tm),
        in_specs=[row(d), row(D_ATTN), row(D_CONV), mod, mod, mod, const(ga), const(gc), const(w_out), const(g2),
                  const(w_route), const(b_route)],
        out_specs=out_specs,
        out_shape=out_shape,
        compiler_params=_cparams("arbitrary", "arbitrary"),
        name="finish_sorted" if sorted_moe else "finish",
    )(x, o_attn, o_conv, gate1, shift2, scale2, ga, gc, w_out, g2, w_route, b_route)


def _moe_kernel(h_ref, comb_ref, x1_ref, gate2_ref, wg_ref, wu_ref, wd_ref, gf_ref, y_ref, acc_ref):
    e = pl.program_id(2)

    @pl.when(e == 0)
    def _():
        acc_ref[...] = jnp.zeros_like(acc_ref)

    h = h_ref[0]
    a = _mm(h, wg_ref[0].astype(BF16))
    u = _mm(h, wu_ref[0].astype(BF16))
    comb = comb_ref[0]
    lane = lax.broadcasted_iota(jnp.int32, comb.shape, 1)
    cw = jnp.sum(jnp.where(lane == e, comb, 0.0), axis=1, keepdims=True)
    hid = (_silu(a) * u * cw).astype(BF16)
    acc_ref[...] += _mm(hid, wd_ref[0].astype(BF16))

    @pl.when(e == pl.num_programs(2) - 1)
    def _():
        y = x1_ref[0] + gate2_ref[0] * acc_ref[...]
        y_ref[0] = _rms(y, gf_ref[...])


def _moe(h2, comb, x1, gate2, w_gate, w_up, w_down, final_g, *, tm):
    b, s, d = x1.shape
    r = gate2.shape[1]
    mod_block = (1, 1, d) if r == 1 else (1, tm, d)
    mod_map = (lambda i, j, e: (i, 0, 0)) if r == 1 else (lambda i, j, e: (i, j, 0))
    row = lambda w: pl.BlockSpec((1, tm, w), lambda i, j, e: (i, j, 0))
    ne, _, de = w_gate.shape
    return pl.pallas_call(
        _moe_kernel,
        grid=(b, s // tm, ne),
        in_specs=[row(d), row(128), row(d), pl.BlockSpec(mod_block, mod_map),
                  pl.BlockSpec((1, d, de), lambda i, j, e: (e, 0, 0)),
                  pl.BlockSpec((1, d, de), lambda i, j, e: (e, 0, 0)),
                  pl.BlockSpec((1, de, d), lambda i, j, e: (e, 0, 0)),
                  pl.BlockSpec((1, d), lambda i, j, e: (0, 0))],
        out_specs=row(d),
        out_shape=jax.ShapeDtypeStruct((b, s, d), F32),
        scratch_shapes=[pltpu.VMEM((tm, d), F32)],
        compiler_params=_cparams("arbitrary", "arbitrary", "arbitrary"),
        name="moe",
    )(h2, comb, x1, gate2, w_gate, w_up, w_down, final_g)


def _moe_slots_kernel(route_ref, base_ref, slot_ref, run_ref):
    @pl.when(pl.program_id(0) == 0)
    def _():
        run_ref[...] = jnp.zeros_like(run_ref)

    route = route_ref[...]
    tm = route.shape[0]
    lane = lax.broadcasted_iota(jnp.int32, route.shape, 1).astype(F32)
    first = jnp.where(lane == route[:, 0:1], 1.0, 0.0)
    second = jnp.where(lane == route[:, 1:2], 1.0, 0.0)
    picks = first + second
    earlier = lax.broadcasted_iota(jnp.int32, (tm, tm), 1) < lax.broadcasted_iota(jnp.int32, (tm, tm), 0)
    seen = _mm(jnp.where(earlier, 1.0, 0.0).astype(BF16), picks.astype(BF16)) + (run_ref[...] + base_ref[...])
    slot0 = jnp.sum(first * seen, axis=1, keepdims=True)
    slot1 = jnp.sum(second * seen, axis=1, keepdims=True)
    slot_ref[...] = jnp.where(lane == 0.0, slot0, jnp.where(lane == 1.0, slot1, 0.0)).astype(jnp.int32)
    run_ref[...] += jnp.sum(picks, axis=0, keepdims=True)


def _moe_slots(route, base, *, tm):
    n = route.shape[0]
    return pl.pallas_call(
        _moe_slots_kernel,
        grid=(n // tm,),
        in_specs=[pl.BlockSpec((tm, 128), lambda j: (j, 0)), pl.BlockSpec((1, 128), lambda j: (0, 0))],
        out_specs=pl.BlockSpec((tm, 128), lambda j: (j, 0)),
        out_shape=jax.ShapeDtypeStruct((n, 128), jnp.int32),
        scratch_shapes=[pltpu.VMEM((1, 128), F32)],
        compiler_params=_cparams("arbitrary"),
        name="moe_slots",
    )(route, base)


def _row_copy(src_ref, src_row, dst_ref, dst_row, sem):
    src = pl.ds(pl.multiple_of(src_row * TOKEN_TILE_ROWS, TOKEN_TILE_ROWS), TOKEN_TILE_ROWS)
    dst = pl.ds(pl.multiple_of(dst_row * TOKEN_TILE_ROWS, TOKEN_TILE_ROWS), TOKEN_TILE_ROWS)
    return pltpu.make_async_copy(src_ref.at[src, :], dst_ref.at[dst, :], sem)


def _token_rows(ref, n):
    return jnp.concatenate([ref[pl.ds(k, n, stride=TOKEN_TILE_ROWS), :] for k in range(TOKEN_TILE_ROWS)], axis=1)


def _moe_scatter_kernel(s0_ref, s1_ref, h_ref, init_ref, sorted_ref, sem):
    del init_ref
    j = pl.program_id(0)
    tm = h_ref.shape[0] // TOKEN_TILE_ROWS

    def start(r, carry):
        _row_copy(h_ref, r, sorted_ref, s0_ref[j * tm + r], sem).start()
        _row_copy(h_ref, r, sorted_ref, s1_ref[j * tm + r], sem).start(priority=1)
        return carry

    def wait(r, carry):
        _row_copy(h_ref, 0, sorted_ref, 0, sem).wait()
        _row_copy(h_ref, 0, sorted_ref, 0, sem).wait()
        return carry

    lax.fori_loop(0, tm, start, 0, unroll=8)
    lax.fori_loop(0, tm, wait, 0, unroll=8)


def _moe_scatter(slot0, slot1, rows, n_slots, *, tm):
    w = rows.shape[1]
    n = rows.shape[0] // TOKEN_TILE_ROWS
    tm, n_slots = tm * TOKEN_TILE_ROWS, n_slots * TOKEN_TILE_ROWS
    n = n * TOKEN_TILE_ROWS
    return pl.pallas_call(
        _moe_scatter_kernel,
        grid_spec=pltpu.PrefetchScalarGridSpec(
            num_scalar_prefetch=2,
            grid=(n // tm,),
            in_specs=[pl.BlockSpec((tm, w), lambda j, a, b: (j, 0)), pl.BlockSpec(memory_space=pl.ANY)],
            out_specs=pl.BlockSpec(memory_space=pl.ANY),
            scratch_shapes=[pltpu.SemaphoreType.DMA(())],
        ),
        out_shape=jax.ShapeDtypeStruct((n_slots, w), rows.dtype),
        input_output_aliases={3: 0},
        compiler_params=pltpu.CompilerParams(dimension_semantics=("arbitrary",), vmem_limit_bytes=VMEM_LIMIT,
                                             disable_bounds_checks=True),
        name="moe_scatter",
    )(slot0, slot1, rows, jnp.zeros((n_slots, w), rows.dtype))


def _moe_experts_kernel(te_ref, na_ref, x_ref, wg_ref, wu_ref, wd_ref, o_ref):
    t = pl.program_id(0)

    @pl.when(t < na_ref[0])
    def _():
        xb = _token_rows(x_ref, MOE_TILE).astype(BF16)
        a = _mm(xb, wg_ref[0].astype(BF16))
        u = _mm(xb, wu_ref[0].astype(BF16))
        y = _mm((_silu(a) * u).astype(BF16), wd_ref[0].astype(BF16))
        for k in range(TOKEN_TILE_ROWS):
            o_ref[pl.ds(k, MOE_TILE, stride=TOKEN_TILE_ROWS), :] = y[:, k * 128:(k + 1) * 128]

    @pl.when(t >= na_ref[0])
    def _():
        o_ref[...] = jnp.zeros_like(o_ref)


def _moe_experts(tile_expert, n_active, sorted_x, w_gate, w_up, w_down):
    w = sorted_x.shape[1]
    n_slots = sorted_x.shape[0] // TOKEN_TILE_ROWS
    ne, d, de = w_gate.shape
    x_map = lambda t, te, na: (jnp.minimum(t, na[0] - 1), 0)
    w_map = lambda t, te, na: (te[t], 0, 0)
    tile_rows = MOE_TILE * TOKEN_TILE_ROWS
    return pl.pallas_call(
        _moe_experts_kernel,
        grid_spec=pltpu.PrefetchScalarGridSpec(
            num_scalar_prefetch=2,
            grid=(n_slots // MOE_TILE,),
            in_specs=[pl.BlockSpec((tile_rows, w), x_map), pl.BlockSpec((1, d, de), w_map),
                      pl.BlockSpec((1, d, de), w_map), pl.BlockSpec((1, de, d), w_map)],
            out_specs=pl.BlockSpec((tile_rows, w), lambda t, te, na: (t, 0)),
        ),
        out_shape=jax.ShapeDtypeStruct(sorted_x.shape, F32),
        compiler_params=_cparams("arbitrary"),
        name="moe_experts",
    )(tile_expert, n_active, sorted_x, w_gate, w_up, w_down)


def _moe_combine_kernel(s0_ref, s1_ref, y_hbm, route_ref, x1_ref, gate2_ref, gf_ref, o_ref, a0_ref, b0_ref, a1_ref,
                        b1_ref, sem):
    j = pl.program_id(0)
    tm = x1_ref.shape[0]
    bufs = ((a0_ref, b0_ref), (a1_ref, b1_ref))

    def issue(step, slot):
        a_ref, b_ref = bufs[slot]

        def body(r, carry):
            _row_copy(y_hbm, s0_ref[step * tm + r], a_ref, r, sem.at[slot]).start()
            _row_copy(y_hbm, s1_ref[step * tm + r], b_ref, r, sem.at[slot]).start(priority=1)
            return carry
        lax.fori_loop(0, tm, body, 0, unroll=8)

    def finish(slot):
        a_ref, b_ref = bufs[slot]

        def wait(r, carry):
            _row_copy(y_hbm, 0, a_ref, 0, sem.at[slot]).wait()
            _row_copy(y_hbm, 0, b_ref, 0, sem.at[slot]).wait()
            return carry
        lax.fori_loop(0, tm, wait, 0, unroll=8)
        route = route_ref[...]
        moe = route[:, 2:3] * _token_rows(a_ref, tm) + route[:, 3:4] * _token_rows(b_ref, tm)
        o_ref[...] = _rms(x1_ref[...] + gate2_ref[0] * moe, gf_ref[...])

    @pl.when(j == 0)
    def _():
        issue(0, 0)

    for slot in range(2):
        @pl.when((j % 2 == slot) & (j + 1 < pl.num_programs(0)))
        def _():
            issue(j + 1, 1 - slot)

        @pl.when(j % 2 == slot)
        def _():
            finish(slot)


def _moe_combine(slot0, slot1, y_sorted, route, x1, gate2, final_g, *, tm, rows_per_mod):
    n, d = x1.shape
    tiles = pltpu.VMEM((tm * TOKEN_TILE_ROWS, 128), F32)
    return pl.pallas_call(
        _moe_combine_kernel,
        grid_spec=pltpu.PrefetchScalarGridSpec(
            num_scalar_prefetch=2,
            grid=(n // tm,),
            in_specs=[pl.BlockSpec(memory_space=pl.ANY), pl.BlockSpec((tm, 128), lambda j, a, b: (j, 0)),
                      pl.BlockSpec((tm, d), lambda j, a, b: (j, 0)),
                      pl.BlockSpec((1, 1, d), lambda j, a, b: (j * tm // rows_per_mod, 0, 0)),
                      pl.BlockSpec((1, d), lambda j, a, b: (0, 0))],
            out_specs=pl.BlockSpec((tm, d), lambda j, a, b: (j, 0)),
            scratch_shapes=[tiles, tiles, tiles, tiles, pltpu.SemaphoreType.DMA((2,))],
        ),
        out_shape=jax.ShapeDtypeStruct((n, d), F32),
        compiler_params=pltpu.CompilerParams(dimension_semantics=("arbitrary",), vmem_limit_bytes=VMEM_LIMIT,
                                             disable_bounds_checks=True),
        name="moe_combine",
    )(slot0, slot1, y_sorted, route, x1, gate2, final_g)


def _moe_sorted(route, counts, rows, x1, gate2, w_gate, w_up, w_down, final_g):
    b, s, d = x1.shape
    n = b * s
    ne = w_gate.shape[0]
    n_tiles = 2 * n // MOE_TILE + ne
    cnt = counts[0, :ne].astype(jnp.int32)
    padded = (cnt + MOE_TILE - 1) // MOE_TILE * MOE_TILE
    ends = jnp.sum(jnp.where(jnp.arange(ne)[:, None] <= jnp.arange(ne)[None, :], padded[:, None], 0), axis=0)
    base = jnp.pad((ends - padded).astype(F32), (0, 128 - ne)).reshape(1, 128)
    n_active = (ends[-1] // MOE_TILE).reshape(1)
    tile_start = jnp.arange(n_tiles, dtype=jnp.int32) * MOE_TILE
    tile_expert = jnp.minimum(jnp.sum((ends[None, :] <= tile_start[:, None]).astype(jnp.int32), axis=1), ne - 1)

    route2 = route.reshape(n, 128)
    slots = _moe_slots(route2, base, tm=512)
    slot0, slot1 = slots[:, 0], slots[:, 1]
    sorted_x = _moe_scatter(slot0, slot1, rows.reshape(n * TOKEN_TILE_ROWS, 128), n_tiles * MOE_TILE, tm=512)
    y_sorted = _moe_experts(tile_expert, n_active, sorted_x, w_gate, w_up, w_down)
    y = _moe_combine(slot0, slot1, y_sorted, route2, x1.reshape(n, d), gate2, final_g, tm=256, rows_per_mod=s)
    return y.reshape(b, s, d)


def _prep_w_in(w_in):
    o_gl = D_ATTN + 3 * ROW_W
    wqkv = w_in[:, :o_gl].astype(BF16)
    wgl = w_in[:, o_gl:o_gl + 3 * N_HEADS]
    pad = jnp.zeros((w_in.shape[0], 128 - 3 * N_HEADS), w_in.dtype)
    wgl = jnp.concatenate([wgl.reshape(-1, N_HEADS, 3).transpose(0, 2, 1).reshape(-1, 3 * N_HEADS), pad], axis=1)
    wu = w_in[:, o_gl + 3 * N_HEADS:].astype(BF16)
    return wqkv, wgl.astype(BF16), wu


def _prep_compress(w_cmp1, pos_cmp, w_cmp2):
    ratio = CMP_BLOCK // CMP_STRIDE
    eye = jnp.eye(KV_HEADS, dtype=w_cmp1.dtype)
    w1 = w_cmp1.reshape(2, ratio, CMP_STRIDE, HEAD_DIM, HEAD_DIM)
    w1big = jnp.einsum('crsdf,kj->cskdrjf', w1, eye).reshape(2, CMP_STRIDE * D_KV, ratio * D_KV).astype(BF16)
    w2big = jnp.einsum('cfd,kj->ckfjd', w_cmp2, eye).reshape(2, D_KV, D_KV).astype(BF16)
    pos = pos_cmp.reshape(2, ratio, CMP_STRIDE, 1, HEAD_DIM)
    pos = jnp.broadcast_to(pos, (2, ratio, CMP_STRIDE, KV_HEADS, HEAD_DIM)).reshape(2, ratio, CMP_STRIDE * D_KV)
    posrows = jnp.concatenate([pos, jnp.zeros((2, 8 - ratio, CMP_STRIDE * D_KV), pos.dtype)], axis=1).astype(BF16)
    return w1big, w2big, posrows


def _prep_router(w_group, b_group, w_router, b_router):
    d = w_group.shape[0]
    pad = 128 - N_EXPERTS - N_GROUPS
    w = jnp.concatenate([w_router, w_group, jnp.zeros((d, pad), w_group.dtype)], axis=1)
    b = jnp.concatenate([b_router, b_group, jnp.zeros((pad,), b_group.dtype)]).reshape(1, 128)
    return w, b


def kernel(x_prompt, x_sample, cache_cmp_kv, cache_slc_kv, state_win_kv, state_conv, page_table, c_prompt, c_sample,
           norm1_g, w_ada, b_ada, w_in, w_cmp1, pos_cmp, w_cmp2, w_dw, b_dw, conv_ln_g, conv_ln_b, g_attn_out,
           g_conv_out, w_out, norm2_g, w_group, b_group, w_router, b_router, w_gate, w_up, w_down, final_g):
    depth = norm1_g.shape[0]
    assert depth == 1 and x_sample.shape[1] == 1
    bp, seq, d = x_prompt.shape
    nb = x_sample.shape[0]
    n_pages = page_table.shape[1]
    past_len = n_pages * PAGE_SIZE
    n_past_slc = past_len // SLC_BLOCK
    win_buf = state_win_kv.shape[2]
    assert win_buf == WINDOW and seq % K_TILE == 0 and seq >= WINDOW + Q_TILE
    l = 0
    row2 = lambda a: a.reshape(1, -1)

    mods = _adaln(jnp.concatenate([c_prompt, c_sample], axis=0), w_ada[l], b_ada[l]).reshape(bp + nb, 6, d)
    mods_p = [mods[:bp, i][:, None, :] for i in range(6)]
    mods_s = [mods[bp:, i][None, :, :] for i in range(6)]

    wqkv, wgl, wu = _prep_w_in(w_in[l])
    w1big, w2big, posrows = _prep_compress(w_cmp1[l], pos_cmp[l], w_cmp2[l])
    w_route, b_route = _prep_router(w_group[l], b_group[l], w_router[l], b_router[l])
    w_out_b = w_out[l].astype(BF16)
    conv_args = (w_dw[l], row2(b_dw[l]), row2(conv_ln_g[l]), row2(conv_ln_b[l]))

    def kv_rows_from_t(a_t):
        n, _, t = a_t.shape
        return a_t.reshape(n, 2, KV_HEADS, HEAD_DIM, t).transpose(0, 4, 1, 2, 3)

    def kv_rows_to_t(a):
        return a.transpose(0, 2, 3, 4, 1)

    qt, kvc, kvct, kvst, kvwt, ks, vts, kw, vtw, gt, u = _mixer(
        x_prompt, mods_p[0], mods_p[1], row2(norm1_g[l]), wqkv, wgl, wu, tm=512, prompt=True)
    kc, vct = _compress_prompt(kvc, w1big, w2big, posrows)
    o_attn = _attn_prompt(qt, kc, vct, ks, vts, kw, vtw, gt)
    o_conv = _conv_prompt(u, *conv_args, tm=512)
    x1, route, moe_rows, counts = _finish(x_prompt, o_attn, o_conv, mods_p[2], mods_p[3], mods_p[4],
                                          row2(g_attn_out[l]), row2(g_conv_out[l]), w_out_b, row2(norm2_g[l]),
                                          w_route, b_route, tm=512, sorted_moe=True)
    y_prompt = _moe_sorted(route, counts, moe_rows, x1, mods_p[5], w_gate[l], w_up[l], w_down[l], row2(final_g))

    new_cmp_prompt = kv_rows_from_t(kvct)[None]
    new_slc_prompt = kv_rows_from_t(kvst)[None]
    new_win_prompt = kv_rows_from_t(kvwt[:, :, seq - WINDOW:])[None]
    new_conv_prompt = u[:, seq - (CONV_WIDTH - 1):][None]

    xs = x_sample.reshape(1, nb, d)
    q_s, kvct_s, kvst_s, kvwt_s, gt_s, u_s = _mixer(
        xs, mods_s[0], mods_s[1], row2(norm1_g[l]), wqkv, wgl, wu, tm=nb, prompt=False)
    cmp_t = kv_rows_to_t(cache_cmp_kv[l])
    q4 = q_s.reshape(nb, KV_HEADS, GQA, HEAD_DIM)
    zq = jnp.zeros_like(q4)
    kvh_id = jnp.arange(KV_HEADS).reshape(1, KV_HEADS, 1, 1)
    qpad = jnp.concatenate([jnp.where(kvh_id == 0, q4, zq), jnp.where(kvh_id == 1, q4, zq)],
                           axis=-1).reshape(nb, N_HEADS, D_KV)
    o_c, idx = _compress_sample(cmp_t.reshape(cmp_t.shape[0], ROW_W, PAGE_SIZE), page_table, qpad, w1big, w2big,
                                posrows, q_pos=past_len, n_slc=n_past_slc + 1)
    idx = idx[:, :KV_HEADS, :SLC_TOPN]
    win_t = kv_rows_to_t(state_win_kv[l]).reshape(nb, ROW_W, win_buf)
    o_heads, new_win_t = _attn_sample(idx, page_table, kv_rows_to_t(cache_slc_kv[l]), q_s.reshape(nb, N_HEADS, HEAD_DIM),
                                      o_c, gt_s[0], kvst_s[0], kvwt_s[0], win_t, n_past_slc=n_past_slc)
    o_attn_s = o_heads.reshape(1, nb, D_ATTN)
    state_t = state_conv[l].transpose(1, 0, 2)
    u_rows = u_s.reshape(nb, D_CONV)
    o_conv_s = _conv_sample(state_t, u_rows, *conv_args, sb=8).reshape(1, nb, D_CONV)
    x1_s, h2_s, comb_s = _finish(xs, o_attn_s, o_conv_s, mods_s[2], mods_s[3], mods_s[4], row2(g_attn_out[l]),
                                 row2(g_conv_out[l]), w_out_b, row2(norm2_g[l]), w_route, b_route, tm=nb,
                                 sorted_moe=False)
    y_sample = _moe(h2_s, comb_s, x1_s, mods_s[5], w_gate[l], w_up[l], w_down[l], row2(final_g), tm=nb)

    row_shape = (1, nb, 1, 2, KV_HEADS, HEAD_DIM)
    new_cmp_sample = kvct_s[0].T.reshape(row_shape)
    new_slc_sample = kvst_s[0].T.reshape(row_shape)
    new_win_sample = kv_rows_from_t(new_win_t)[None]
    new_conv_sample = jnp.concatenate([state_t[1:], u_rows[None]], axis=0).transpose(1, 0, 2)[None]

    return (y_prompt, y_sample.reshape(nb, 1, d), new_cmp_prompt, new_slc_prompt, new_win_prompt, new_conv_prompt,
            new_cmp_sample, new_slc_sample, new_win_sample, new_conv_sample)
```
